```python
import jax, jax.numpy as jnp
from jax import lax
import numpy as np

D_MODEL = 1024
BATCH = 8
SEQ = 2048
DEPTH = 1
DEC_BATCH = 128
DEC_SEQ = 8
PAST_LEN = 16384
PAGE_SIZE = 128

CHUNK = 128
SGU_GROUPS = 4
SGU_WIDTH = D_MODEL // 2
SGU_GROUP_DIM = SGU_WIDTH // SGU_GROUPS
POOL_WINDOWS = (2, 4, 8, 16)
POOL_GROUPS = len(POOL_WINDOWS)
POOL_WIDTH = D_MODEL // 4
POOL_GROUP_DIM = POOL_WIDTH // POOL_GROUPS
POOL_STATE = max(POOL_WINDOWS) - 1
N_MEM = 256
X_HEADS = 4
X_WIDTH = D_MODEL // 4
X_HEAD_DIM = X_WIDTH // X_HEADS
N_BRANCH = 3
D_FF = 4 * D_MODEL
D_IN = 2 * SGU_WIDTH + POOL_WIDTH + X_WIDTH + N_BRANCH * D_MODEL
EPS = 1e-6

kernel_name = "hybrid_gmlp_pool_memxattn_decode_step"


def rmsnorm(x, g):
    xf = x.astype(jnp.float32)
    y = xf * lax.rsqrt(jnp.mean(xf * xf, axis=-1, keepdims=True) + EPS)
    return (y * g.astype(jnp.float32)).astype(x.dtype)


def layernorm(x, g, b):
    xf = x.astype(jnp.float32)
    mu = jnp.mean(xf, axis=-1, keepdims=True)
    xc = xf - mu
    y = xc * lax.rsqrt(jnp.mean(xc * xc, axis=-1, keepdims=True) + EPS)
    return (y * g.astype(jnp.float32) + b.astype(jnp.float32)).astype(x.dtype)


def chunk_spatial_gate(u, vhat, w_s, b_s):
    bsz, length, _ = u.shape
    n_chunks = -(-length // CHUNK)
    pad = n_chunks * CHUNK - length
    vp = jnp.pad(vhat, ((0, 0), (0, pad), (0, 0)))
    vp = vp.reshape(bsz, n_chunks, CHUNK, SGU_GROUPS, SGU_GROUP_DIM)
    causal = jnp.tril(jnp.ones((CHUNK, CHUNK), dtype=bool))
    w = jnp.where(causal, w_s, 0).astype(vp.dtype)
    mixed = jnp.einsum('gij,bnjgd->bnigd', w, vp) + b_s.T[None, None, :, :, None]
    mixed = mixed.reshape(bsz, n_chunks * CHUNK, SGU_WIDTH)[:, :length]
    return u * mixed


def multiscale_pool(p_prev, p_new, start_pos, w_pool, pool_scale):
    bsz, length, _ = p_new.shape
    ext = jnp.concatenate([p_prev, p_new], axis=1)
    extf = ext.astype(jnp.float32)
    csum = jnp.pad(jnp.cumsum(extf, axis=1), ((0, 0), (1, 0), (0, 0)))
    pos = start_pos + jnp.arange(length)
    means = []
    for gi, win in enumerate(POOL_WINDOWS):
        sl = slice(gi * POOL_GROUP_DIM, (gi + 1) * POOL_GROUP_DIM)
        hi = csum[:, POOL_STATE + 1:POOL_STATE + 1 + length, sl]
        lo = csum[:, POOL_STATE + 1 - win:POOL_STATE + 1 - win + length, sl]
        cnt = jnp.minimum(pos + 1, win).astype(jnp.float32)[None, :, None]
        means.append((hi - lo) / cnt)
    pooled = jnp.concatenate(means, axis=-1) - extf[:, POOL_STATE:]
    pooled = pooled.astype(p_new.dtype).reshape(bsz, length, POOL_GROUPS, POOL_GROUP_DIM)
    mixed = jnp.einsum('blgd,gde->blge', pooled, w_pool).reshape(bsz, length, POOL_WIDTH)
    return mixed * pool_scale, ext[:, -POOL_STATE:]


def memory_kv(mem, g_mem, w_kv):
    bsz = mem.shape[0]
    kv = rmsnorm(mem, g_mem) @ w_kv
    k, v = jnp.split(kv, 2, axis=-1)
    return (k.reshape(bsz, N_MEM, X_HEADS, X_HEAD_DIM),
            v.reshape(bsz, N_MEM, X_HEADS, X_HEAD_DIM))


def memory_attend(q, mem_k, mem_v):
    bsz, length, _ = q.shape
    qh = q.reshape(bsz, length, X_HEADS, X_HEAD_DIM)
    s = jnp.einsum('blhd,bmhd->bhlm', qh, mem_k,
                   preferred_element_type=jnp.float32) * (X_HEAD_DIM ** -0.5)
    p = jax.nn.softmax(s, axis=-1).astype(mem_v.dtype)
    o = jnp.einsum('bhlm,bmhd->blhd', p, mem_v)
    return o.reshape(bsz, length, X_WIDTH)


def decoder_layer(x, pool_prev, start_pos, mem_k, mem_v,
                  g_mix, w_in, g_v, b_v, w_s, b_s, w_pool, pool_scale,
                  w_out_a, w_out_b, w_out_c, w_o, g_ffn, w_up, w_down):
    length = x.shape[1]
    h = rmsnorm(x, g_mix)
    z = h @ w_in
    cuts = [SGU_WIDTH, 2 * SGU_WIDTH, 2 * SGU_WIDTH + POOL_WIDTH,
            2 * SGU_WIDTH + POOL_WIDTH + X_WIDTH]
    u, v, p, q, gate_logits = jnp.split(z, cuts, axis=-1)
    u = jax.nn.gelu(u)
    vhat = layernorm(jax.nn.gelu(v), g_v, b_v)
    a = chunk_spatial_gate(u, vhat, w_s, b_s) @ w_out_a
    pooled, pool_tail = multiscale_pool(pool_prev, p, start_pos, w_pool, pool_scale)
    b = pooled @ w_out_b
    c = memory_attend(q, mem_k, mem_v) @ w_out_c
    gates = jax.nn.sigmoid(gate_logits.astype(jnp.float32)).astype(x.dtype)
    g_a, g_b, g_c = jnp.split(gates, N_BRANCH, axis=-1)
    x = x + (g_a * a + g_b * b + g_c * c) @ w_o
    h2 = rmsnorm(x, g_ffn)
    x = x + jnp.square(jax.nn.relu(h2 @ w_up)) @ w_down
    open_start = ((length - 1) // CHUNK) * CHUNK
    return x, vhat[:, open_start:], pool_tail


def setup_inputs(seed: int = 0) -> dict:
    key = jax.random.key(seed)
    ks = iter(jax.random.split(key, 32))
    nrm = lambda shape, scale=1.0: jax.random.normal(next(ks), shape, jnp.float32) * scale
    gain = lambda shape: 1.0 + 0.05 * jax.random.normal(next(ks), shape, jnp.float32)
    L = DEPTH
    return {
        "x_prompt": nrm((BATCH, SEQ, D_MODEL)),
        "x_sample": nrm((DEC_BATCH, DEC_SEQ, D_MODEL)),
        "mem_prompt": nrm((BATCH, N_MEM, D_MODEL)),
        "cache_mem_k": nrm((L, DEC_BATCH, N_MEM, X_HEADS, X_HEAD_DIM)),
        "cache_mem_v": nrm((L, DEC_BATCH, N_MEM, X_HEADS, X_HEAD_DIM)),
        "state_pool": nrm((L, DEC_BATCH, POOL_STATE, POOL_WIDTH)),
        "g_mix": gain((L, D_MODEL)),
        "w_in": nrm((L, D_MODEL, D_IN), D_MODEL ** -0.5),
        "g_v": gain((L, SGU_WIDTH)),
        "b_v": nrm((L, SGU_WIDTH), 0.02),
        "w_s": nrm((L, SGU_GROUPS, CHUNK, CHUNK), CHUNK ** -0.5),
        "b_s": gain((L, SGU_GROUPS, CHUNK)),
        "w_pool": nrm((L, POOL_GROUPS, POOL_GROUP_DIM, POOL_GROUP_DIM), POOL_GROUP_DIM ** -0.5),
        "pool_scale": 1.0 + 0.1 * nrm((L, POOL_WIDTH)),
        "g_mem": gain((L, D_MODEL)),
        "w_kv": nrm((L, D_MODEL, 2 * X_WIDTH), D_MODEL ** -0.5),
        "w_out_a": nrm((L, SGU_WIDTH, D_MODEL), SGU_WIDTH ** -0.5),
        "w_out_b": nrm((L, POOL_WIDTH, D_MODEL), POOL_WIDTH ** -0.5),
        "w_out_c": nrm((L, X_WIDTH, D_MODEL), X_WIDTH ** -0.5),
        "w_o": nrm((L, D_MODEL, D_MODEL), D_MODEL ** -0.5),
        "g_ffn": gain((L, D_MODEL)),
        "w_up": nrm((L, D_MODEL, D_FF), D_MODEL ** -0.5),
        "w_down": nrm((L, D_FF, D_MODEL), D_FF ** -0.5),
        "g_final": gain((D_MODEL,)),
    }


def reference(x_prompt, x_sample, mem_prompt, cache_mem_k, cache_mem_v, state_pool,
              g_mix, w_in, g_v, b_v, w_s, b_s, w_pool, pool_scale, g_mem, w_kv,
              w_out_a, w_out_b, w_out_c, w_o, g_ffn, w_up, w_down, g_final):
    yp, ys = x_prompt, x_sample
    zero_prev = jnp.zeros((x_prompt.shape[0], POOL_STATE, POOL_WIDTH), x_prompt.dtype)
    mk_p, mv_p, pool_p, pool_s, cv_p, cv_s = [], [], [], [], [], []
    for l in range(DEPTH):
        lw = (g_mix[l], w_in[l], g_v[l], b_v[l], w_s[l], b_s[l], w_pool[l], pool_scale[l],
              w_out_a[l], w_out_b[l], w_out_c[l], w_o[l], g_ffn[l], w_up[l], w_down[l])
        mem_k, mem_v = memory_kv(mem_prompt, g_mem[l], w_kv[l])
        yp, vrows_p, ptail_p = decoder_layer(yp, zero_prev, 0, mem_k, mem_v, *lw)
        ys, vrows_s, ptail_s = decoder_layer(ys, state_pool[l], PAST_LEN,
                                             cache_mem_k[l], cache_mem_v[l], *lw)
        mk_p.append(mem_k)
        mv_p.append(mem_v)
        pool_p.append(ptail_p)
        pool_s.append(ptail_s)
        cv_p.append(vrows_p)
        cv_s.append(vrows_s)
    y_prompt = rmsnorm(yp, g_final)
    y_sample = rmsnorm(ys, g_final)
    return (y_prompt, y_sample, jnp.stack(mk_p), jnp.stack(mv_p), jnp.stack(pool_p),
            jnp.stack(pool_s), jnp.stack(cv_p), jnp.stack(cv_s))
```

```python
import functools

import jax
import jax.numpy as jnp
from jax import lax
from jax.experimental import pallas as pl
from jax.experimental.pallas import tpu as pltpu

CHUNK = 128
SGU_GROUPS = 4
POOL_WINDOWS = (2, 4, 8, 16)
POOL_STATE = max(POOL_WINDOWS) - 1
X_HEADS = 4
EPS = 1e-6

V7X_VMEM_BYTES = 64 * 1024 * 1024
V7X_LANES = 128
V7X_SUBLANES = 8
VMEM_LIMIT_BYTES = V7X_VMEM_BYTES - 6 * 1024 * 1024

PROMPT_TILE = 256
SAMPLE_TILE_SEQS = 16
FFN_SPLIT = 4
POOL_PAD = POOL_STATE + 1

BF16 = jnp.bfloat16
F32 = jnp.float32


def _dot(a, b):
    return jnp.dot(a, b, preferred_element_type=F32)


def _log2(n):
    assert n > 0 and n & (n - 1) == 0, n
    return n.bit_length() - 1


def _rmsnorm(x, g):
    return x * lax.rsqrt(jnp.mean(x * x, axis=-1, keepdims=True) + EPS) * g


def _layernorm(x, g, b):
    xc = x - jnp.mean(x, axis=-1, keepdims=True)
    return xc * lax.rsqrt(jnp.mean(xc * xc, axis=-1, keepdims=True) + EPS) * g + b


def _const_spec(shape):
    zeros = (0,) * len(shape)
    return pl.BlockSpec(shape, lambda *_: zeros, pipeline_mode=pl.Buffered(1))


def _spatial_gate(h, w_in_ref, g_v, b_v, wmix_ref, bias_ref, w_out_a_ref, chunk_rows):
    t = h.shape[0]
    sgu_w = g_v.shape[-1]
    group_dim = sgu_w // wmix_ref.shape[0]
    u = jax.nn.gelu(_dot(h, w_in_ref[:, 0:sgu_w]))
    v = jax.nn.gelu(_dot(h, w_in_ref[:, sgu_w:2 * sgu_w]))
    vhat = _layernorm(v, g_v, b_v)
    vb = vhat.astype(BF16)
    shift = _log2(chunk_rows)
    row = lax.broadcasted_iota(jnp.int32, (t, t), 0)
    col = lax.broadcasted_iota(jnp.int32, (t, t), 1)
    same_chunk = (row >> shift) == (col >> shift)
    causal = col <= row
    parts = []
    for g in range(wmix_ref.shape[0]):
        w = wmix_ref[g]
        w = jnp.where(same_chunk, jnp.where(causal, w, jnp.zeros_like(w)), jnp.zeros_like(w))
        parts.append(_dot(w, vb[:, g * group_dim:(g + 1) * group_dim]))
    mixed = jnp.concatenate(parts, axis=1) + bias_ref[...]
    a = _dot((u * mixed).astype(BF16), w_out_a_ref[...])
    return a, vhat


def _window_means_minus_self(load_shifted, p, pos):
    group_lanes = p.shape[-1] // len(POOL_WINDOWS)
    outs = []
    for lt in range(p.shape[-1] // V7X_LANES):
        p_lt = p[..., lt * V7X_LANES:(lt + 1) * V7X_LANES]
        lane = lax.broadcasted_iota(jnp.int32, p_lt.shape, p_lt.ndim - 1)
        wins = POOL_WINDOWS[lt * V7X_LANES // group_lanes:(lt + 1) * V7X_LANES // group_lanes]
        acc = p_lt
        sums = {1: p_lt}
        for k in range(1, max(wins)):
            acc = acc + load_shifted(k, lt)
            sums[k + 1] = acc
        win_sum = sums[wins[-1]]
        win = jnp.full(p_lt.shape, wins[-1], jnp.int32)
        for gi in range(len(wins) - 2, -1, -1):
            in_group = lane < (gi + 1) * group_lanes
            win_sum = jnp.where(in_group, sums[wins[gi]], win_sum)
            win = jnp.where(in_group, wins[gi], win)
        cnt = jnp.minimum(pos + 1, win).astype(F32)
        outs.append(win_sum / cnt - p_lt)
    return jnp.concatenate(outs, axis=-1)


def _pool_project(pooled, wpool_ref, pool_scale, w_out_b_ref):
    mixed = _dot(pooled.astype(BF16), wpool_ref[...]) * pool_scale
    return _dot(mixed.astype(BF16), w_out_b_ref[...])


def _merge_ffn_final(x, a, b, c, gate_logits, w_o_ref, g_ffn, w_up_ref, w_down_ref, g_final):
    d = x.shape[-1]
    gates = jax.nn.sigmoid(gate_logits)
    merged = gates[:, 0:d] * a + gates[:, d:2 * d] * b + gates[:, 2 * d:3 * d] * c
    x1 = x + _dot(merged.astype(BF16), w_o_ref[...])
    h2 = _rmsnorm(x1, g_ffn).astype(BF16)
    slab = w_up_ref.shape[1] // FFN_SPLIT
    x2 = x1
    for s in range(FFN_SPLIT):
        up = _dot(h2, w_up_ref[:, s * slab:(s + 1) * slab])
        act = jnp.square(jnp.maximum(up, 0.0)).astype(BF16)
        x2 = x2 + _dot(act, w_down_ref[s * slab:(s + 1) * slab, :])
    return _rmsnorm(x2, g_final)


def _mem_kv_kernel(mem_ref, g_mem_ref, w_kv_ref, k_ref, v_ref, kt_ref, vb_ref):
    xw = k_ref.shape[-1]
    hn = _rmsnorm(mem_ref[0], g_mem_ref[...]).astype(BF16)
    kv = _dot(hn, w_kv_ref[...])
    k = kv[:, 0:xw]
    v = kv[:, xw:2 * xw]
    k_ref[0] = k
    v_ref[0] = v
    kt_ref[0] = k.T.astype(BF16)
    vb_ref[0] = v.astype(BF16)


def _prompt_kernel(x_ref, kt_ref, vb_ref,
                   g_mix_ref, w_in_ref, g_v_ref, b_v_ref, wmix_ref, bias_ref, wpool_ref,
                   pool_scale_ref, w_out_a_ref, w_out_b_ref, w_out_c_ref, w_o_ref,
                   g_ffn_ref, w_up_ref, w_down_ref, g_final_ref,
                   y_ref, cv_ref, ptail_ref, ext_ref):
    j = pl.program_id(1)
    t = x_ref.shape[1]
    sgu_w = g_v_ref.shape[-1]
    pool_w = pool_scale_ref.shape[-1]
    x_w = kt_ref.shape[1]
    x = x_ref[0]
    h = _rmsnorm(x, g_mix_ref[...]).astype(BF16)

    a, vhat = _spatial_gate(h, w_in_ref, g_v_ref[...], b_v_ref[...], wmix_ref, bias_ref,
                            w_out_a_ref, CHUNK)

    @pl.when(j == pl.num_programs(1) - 1)
    def _():
        cv_ref[0] = vhat[t - CHUNK:, :]

    c0 = 2 * sgu_w
    p = _dot(h, w_in_ref[:, c0:c0 + pool_w])

    @pl.when(j == 0)
    def _():
        ext_ref[0:POOL_PAD, :] = jnp.zeros((POOL_PAD, pool_w), F32)

    ext_ref[POOL_PAD:POOL_PAD + t, :] = p

    def load_shifted(k, lt):
        return ext_ref[POOL_PAD - k:POOL_PAD - k + t, lt * V7X_LANES:(lt + 1) * V7X_LANES]

    pos = j * t + lax.broadcasted_iota(jnp.int32, (t, V7X_LANES), 0)
    pooled = _window_means_minus_self(load_shifted, p, pos)
    tail = ext_ref[t:t + POOL_PAD, :]
    ext_ref[0:POOL_PAD, :] = tail

    @pl.when(j == pl.num_programs(1) - 1)
    def _():
        ptail_ref[0] = tail

    b = _pool_project(pooled, wpool_ref, pool_scale_ref[...], w_out_b_ref)

    c1 = c0 + pool_w
    head_dim = x_w // X_HEADS
    q = _dot(h, w_in_ref[:, c1:c1 + x_w]) * (head_dim ** -0.5)
    head = lax.broadcasted_iota(jnp.int32, (t, x_w), 1) >> _log2(head_dim)
    kt = kt_ref[0]
    vb = vb_ref[0]
    o = jnp.zeros((t, x_w), F32)
    for hd in range(X_HEADS):
        in_head = head == hd
        s = _dot(jnp.where(in_head, q, 0.0).astype(BF16), kt)
        e = jnp.exp(s - jnp.max(s, axis=-1, keepdims=True))
        prob = e * (1.0 / jnp.sum(e, axis=-1, keepdims=True))
        o = jnp.where(in_head, _dot(prob.astype(BF16), vb), o)
    c = _dot(o.astype(BF16), w_out_c_ref[...])

    c2 = c1 + x_w
    gate_logits = _dot(h, w_in_ref[:, c2:])
    y_ref[0] = _merge_ffn_final(x, a, b, c, gate_logits, w_o_ref, g_ffn_ref[...],
                                w_up_ref, w_down_ref, g_final_ref[...])


def _sample_kernel(start_pos, x_ref, k_ref, v_ref, st_ref,
                   g_mix_ref, w_in_ref, g_v_ref, b_v_ref, wmix_ref, bias_ref, wpool_ref,
                   pool_scale_ref, w_out_a_ref, w_out_b_ref, w_out_c_ref, w_o_ref,
                   g_ffn_ref, w_up_ref, w_down_ref, g_final_ref,
                   y_ref, cv_ref, ptail_ref, ext_ref, q_ref, o_ref):
    t = x_ref.shape[0]
    n_seq, n_mem, x_w = k_ref.shape
    rows = t // n_seq
    sgu_w = g_v_ref.shape[-1]
    pool_w = pool_scale_ref.shape[-1]
    x = x_ref[...]
    h = _rmsnorm(x, g_mix_ref[...]).astype(BF16)

    a, vhat = _spatial_gate(h, w_in_ref, g_v_ref[...], b_v_ref[...], wmix_ref, bias_ref,
                            w_out_a_ref, rows)
    cv_ref[...] = vhat

    c0 = 2 * sgu_w
    p = _dot(h, w_in_ref[:, c0:c0 + pool_w]).reshape(n_seq, rows, pool_w)
    ext_ref[:, 0:POOL_PAD, :] = st_ref[...]
    ext_ref[:, POOL_PAD:POOL_PAD + rows, :] = p

    def load_shifted(k, lt):
        return ext_ref[:, POOL_PAD - k:POOL_PAD - k + rows, lt * V7X_LANES:(lt + 1) * V7X_LANES]

    pos = start_pos + lax.broadcasted_iota(jnp.int32, (n_seq, rows, V7X_LANES), 1)
    pooled = _window_means_minus_self(load_shifted, p, pos).reshape(t, pool_w)
    ptail_ref[...] = ext_ref[:, rows:rows + POOL_PAD, :]
    b = _pool_project(pooled, wpool_ref, pool_scale_ref[...], w_out_b_ref)

    c1 = c0 + pool_w
    head_dim = x_w // X_HEADS
    q_ref[...] = _dot(h, w_in_ref[:, c1:c1 + x_w]) * (head_dim ** -0.5)
    stack = X_HEADS * rows
    lane_head = lax.broadcasted_iota(jnp.int32, (stack, x_w), 1) >> _log2(head_dim)
    row_head = lax.broadcasted_iota(jnp.int32, (stack, x_w), 0) >> _log2(rows)
    own_head = lane_head == row_head

    def attend(i, carry):
        r0 = pl.multiple_of(i * rows, rows)
        qs = jnp.concatenate([q_ref[pl.ds(r0, rows), :]] * X_HEADS, axis=0)
        qm = jnp.where(own_head, qs, 0.0).astype(BF16)
        s = lax.dot_general(qm, k_ref[i].astype(BF16), (((1,), (1,)), ((), ())),
                            preferred_element_type=F32)
        e = jnp.exp(s - jnp.max(s, axis=-1, keepdims=True))
        prob = e * (1.0 / jnp.sum(e, axis=-1, keepdims=True))
        of = jnp.where(own_head, _dot(prob.astype(BF16), v_ref[i].astype(BF16)), 0.0)
        out = of[0:rows]
        for hd in range(1, X_HEADS):
            out = out + of[hd * rows:(hd + 1) * rows]
        o_ref[pl.ds(r0, rows), :] = out
        return carry

    lax.fori_loop(0, n_seq, attend, 0)
    c = _dot(o_ref[...].astype(BF16), w_out_c_ref[...])

    c2 = c1 + x_w
    gate_logits = _dot(h, w_in_ref[:, c2:])
    y_ref[...] = _merge_ffn_final(x, a, b, c, gate_logits, w_o_ref, g_ffn_ref[...],
                                  w_up_ref, w_down_ref, g_final_ref[...])


def _layer_weight_operands(lw, tile_rows, chunk_rows):
    (g_mix, w_in, g_v, b_v, w_s, b_s, w_pool, pool_scale,
     w_out_a, w_out_b, w_out_c, w_o, g_ffn, w_up, w_down, g_final) = lw
    reps = tile_rows // chunk_rows
    row2 = lambda vec: vec.reshape(1, -1).astype(F32)
    wmix = jnp.tile(w_s[:, :chunk_rows, :chunk_rows], (1, reps, reps)).astype(BF16)
    group_dim = g_v.shape[-1] // w_s.shape[0]
    bias = jnp.tile(jnp.repeat(b_s[:, :chunk_rows].T, group_dim, axis=1), (reps, 1)).astype(F32)
    wpool = jax.scipy.linalg.block_diag(*[w_pool[i] for i in range(w_pool.shape[0])])
    return [row2(g_mix), w_in.astype(BF16), row2(g_v), row2(b_v), wmix, bias,
            wpool.astype(BF16), row2(pool_scale), w_out_a.astype(BF16), w_out_b.astype(BF16),
            w_out_c.astype(BF16), w_o.astype(BF16), row2(g_ffn), w_up.astype(BF16),
            w_down.astype(BF16), row2(g_final)]


def _mem_kv(mem, g_mem, w_kv):
    bsz, n_mem, d = mem.shape
    xw = w_kv.shape[-1] // 2
    blk = lambda: pl.BlockSpec((1, n_mem, xw), lambda i: (i, 0, 0))
    return pl.pallas_call(
        _mem_kv_kernel,
        grid=(bsz,),
        in_specs=[pl.BlockSpec((1, n_mem, d), lambda i: (i, 0, 0)),
                  _const_spec((1, d)), _const_spec((d, 2 * xw))],
        out_specs=[blk(), blk(), blk(), blk()],
        out_shape=[jax.ShapeDtypeStruct((bsz, n_mem, xw), F32),
                   jax.ShapeDtypeStruct((bsz, n_mem, xw), F32),
                   jax.ShapeDtypeStruct((bsz, xw, n_mem), BF16),
                   jax.ShapeDtypeStruct((bsz, n_mem, xw), BF16)],
        compiler_params=pltpu.CompilerParams(dimension_semantics=("arbitrary",)),
        name="mem_kv",
    )(mem, g_mem.reshape(1, d).astype(F32), w_kv.astype(BF16))


def _prompt_layer(x, kt, vb, lw):
    bsz, seq, d = x.shape
    t = PROMPT_TILE
    ops = _layer_weight_operands(lw, t, CHUNK)
    sgu_w = lw[2].shape[-1]
    pool_w = lw[7].shape[-1]
    n_mem, xw = vb.shape[1:]
    return pl.pallas_call(
        _prompt_kernel,
        grid=(bsz, seq // t),
        in_specs=[pl.BlockSpec((1, t, d), lambda i, j: (i, j, 0)),
                  pl.BlockSpec((1, xw, n_mem), lambda i, j: (i, 0, 0)),
                  pl.BlockSpec((1, n_mem, xw), lambda i, j: (i, 0, 0))]
                 + [_const_spec(o.shape) for o in ops],
        out_specs=[pl.BlockSpec((1, t, d), lambda i, j: (i, j, 0)),
                   pl.BlockSpec((1, CHUNK, sgu_w), lambda i, j: (i, 0, 0)),
                   pl.BlockSpec((1, POOL_PAD, pool_w), lambda i, j: (i, 0, 0))],
        out_shape=[jax.ShapeDtypeStruct((bsz, seq, d), F32),
                   jax.ShapeDtypeStruct((bsz, CHUNK, sgu_w), F32),
                   jax.ShapeDtypeStruct((bsz, POOL_PAD, pool_w), F32)],
        scratch_shapes=[pltpu.VMEM((POOL_PAD + t, pool_w), F32)],
        compiler_params=pltpu.CompilerParams(
            dimension_semantics=("arbitrary", "arbitrary"),
            vmem_limit_bytes=VMEM_LIMIT_BYTES),
        name="prompt_layer",
    )(x, kt, vb, *ops)


def _sample_layer(x, mem_k, mem_v, state, start_pos, lw):
    n_seq, rows, d = x.shape
    ts = SAMPLE_TILE_SEQS
    t = ts * rows
    ops = _layer_weight_operands(lw, t, rows)
    sgu_w = lw[2].shape[-1]
    pool_w = lw[7].shape[-1]
    n_mem = mem_k.shape[1]
    xw = mem_k.shape[2] * mem_k.shape[3]
    state_pad = jnp.pad(state, ((0, 0), (POOL_PAD - state.shape[1], 0), (0, 0)))
    y, cv, ptail = pl.pallas_call(
        functools.partial(_sample_kernel, start_pos),
        grid=(n_seq // ts,),
        in_specs=[pl.BlockSpec((t, d), lambda i: (i, 0)),
                  pl.BlockSpec((ts, n_mem, xw), lambda i: (i, 0, 0)),
                  pl.BlockSpec((ts, n_mem, xw), lambda i: (i, 0, 0)),
                  pl.BlockSpec((ts, POOL_PAD, pool_w), lambda i: (i, 0, 0))]
                 + [_const_spec(o.shape) for o in ops],
        out_specs=[pl.BlockSpec((t, d), lambda i: (i, 0)),
                   pl.BlockSpec((t, sgu_w), lambda i: (i, 0)),
                   pl.BlockSpec((ts, POOL_PAD, pool_w), lambda i: (i, 0, 0))],
        out_shape=[jax.ShapeDtypeStruct((n_seq * rows, d), F32),
                   jax.ShapeDtypeStruct((n_seq * rows, sgu_w), F32),
                   jax.ShapeDtypeStruct((n_seq, POOL_PAD, pool_w), F32)],
        scratch_shapes=[pltpu.VMEM((ts, POOL_PAD + rows, pool_w), F32),
                        pltpu.VMEM((t, xw), F32),
                        pltpu.VMEM((t, xw), F32)],
        compiler_params=pltpu.CompilerParams(
            dimension_semantics=("arbitrary",),
            vmem_limit_bytes=VMEM_LIMIT_BYTES),
        name="sample_layer",
    )(x.reshape(n_seq * rows, d), mem_k.reshape(n_seq, n_mem, xw),
      mem_v.reshape(n_seq, n_mem, xw), state_pad, *ops)
    return (y.reshape(n_seq, rows, d), cv.reshape(n_seq, rows, sgu_w),
            ptail[:, POOL_PAD - POOL_STATE:, :])


def kernel(x_prompt, x_sample, mem_prompt, cache_mem_k, cache_mem_v, state_pool, g_mix, w_in, g_v, b_v, w_s, b_s, w_pool, pool_scale, g_mem, w_kv, w_out_a, w_out_b, w_out_c, w_o, g_ffn, w_up, w_down, g_final):
    depth = w_in.shape[0]
    assert depth == 1, "the final rmsnorm is fused into the (single) layer kernel"
    past_len = 16384
    bsz, n_mem = mem_prompt.shape[:2]
    heads, head_dim = cache_mem_k.shape[-2:]
    lw = (g_mix[0], w_in[0], g_v[0], b_v[0], w_s[0], b_s[0], w_pool[0], pool_scale[0],
          w_out_a[0], w_out_b[0], w_out_c[0], w_o[0], g_ffn[0], w_up[0], w_down[0], g_final)
    mem_k, mem_v, kt, vb = _mem_kv(mem_prompt, g_mem[0], w_kv[0])
    y_prompt, cv_p, ptail_p = _prompt_layer(x_prompt, kt, vb, lw)
    y_sample, cv_s, ptail_s = _sample_layer(x_sample, cache_mem_k[0], cache_mem_v[0],
                                            state_pool[0], past_len, lw)
    kv_shape = (1, bsz, n_mem, heads, head_dim)
    return (y_prompt, y_sample, mem_k.reshape(kv_shape), mem_v.reshape(kv_shape),
            ptail_p[None, :, POOL_PAD - POOL_STATE:, :], ptail_s[None], cv_p[None], cv_s[None])
```

```python
import functools

import jax
import jax.numpy as jnp
from jax import lax
from jax.experimental import pallas as pl
from jax.experimental.pallas import tpu as pltpu

PAST_LEN = 16384
CHUNK = 128
POOL_WINDOWS = (2, 4, 8, 16)
POOL_STATE = max(POOL_WINDOWS) - 1
X_HEADS = 4
EPS = 1e-6

V7X_VMEM_BYTES = 64 * 1024 * 1024
V7X_LANES = 128
VMEM_LIMIT_BYTES = V7X_VMEM_BYTES - 6 * 1024 * 1024

PROMPT_TILE = 256
SAMPLE_TILE_SEQS = 32
ATTN_TILE_SEQS = 16
FFN_SPLIT = 4
POOL_PAD = POOL_STATE + 1

BF16 = jnp.bfloat16
F32 = jnp.float32


def _dot(a, b):
    return jnp.dot(a, b, preferred_element_type=F32)


def _log2(n):
    assert n > 0 and n & (n - 1) == 0, n
    return n.bit_length() - 1


def _rmsnorm(x, g):
    return x * lax.rsqrt(jnp.mean(x * x, axis=-1, keepdims=True) + EPS) * g


def _layernorm(x, g, b):
    xc = x - jnp.mean(x, axis=-1, keepdims=True)
    return xc * lax.rsqrt(jnp.mean(xc * xc, axis=-1, keepdims=True) + EPS) * g + b


def _const_spec(shape):
    zeros = (0,) * len(shape)
    return pl.BlockSpec(shape, lambda *_: zeros, pipeline_mode=pl.Buffered(1))


def _spatial_gate(h, w_in_ref, g_v, b_v, wmix_ref, bias_ref, w_out_a_ref, chunk_rows):
    t = h.shape[0]
    sgu_w = g_v.shape[-1]
    group_dim = sgu_w // wmix_ref.shape[0]
    u = jax.nn.gelu(_dot(h, w_in_ref[:, 0:sgu_w]))
    v = jax.nn.gelu(_dot(h, w_in_ref[:, sgu_w:2 * sgu_w]))
    vhat = _layernorm(v, g_v, b_v)
    vb = vhat.astype(BF16)
    shift = _log2(chunk_rows)
    row = lax.broadcasted_iota(jnp.int32, (t, t), 0)
    col = lax.broadcasted_iota(jnp.int32, (t, t), 1)
    same_chunk = (row >> shift) == (col >> shift)
    causal = col <= row
    parts = []
    for g in range(wmix_ref.shape[0]):
        w = wmix_ref[g]
        w = jnp.where(same_chunk, jnp.where(causal, w, jnp.zeros_like(w)), jnp.zeros_like(w))
        parts.append(_dot(w, vb[:, g * group_dim:(g + 1) * group_dim]))
    mixed = jnp.concatenate(parts, axis=1) + bias_ref[...]
    a = _dot((u * mixed).astype(BF16), w_out_a_ref[...])
    return a, vhat


def _window_means_minus_self(load_shifted, p, pos):
    group_lanes = p.shape[-1] // len(POOL_WINDOWS)
    outs = []
    for lt in range(p.shape[-1] // V7X_LANES):
        p_lt = p[..., lt * V7X_LANES:(lt + 1) * V7X_LANES]
        lane = lax.broadcasted_iota(jnp.int32, p_lt.shape, p_lt.ndim - 1)
        wins = POOL_WINDOWS[lt * V7X_LANES // group_lanes:(lt + 1) * V7X_LANES // group_lanes]
        acc = p_lt
        sums = {1: p_lt}
        for k in range(1, max(wins)):
            acc = acc + load_shifted(k, lt)
            sums[k + 1] = acc
        win_sum = sums[wins[-1]]
        win = jnp.full(p_lt.shape, wins[-1], jnp.int32)
        for gi in range(len(wins) - 2, -1, -1):
            in_group = lane < (gi + 1) * group_lanes
            win_sum = jnp.where(in_group, sums[wins[gi]], win_sum)
            win = jnp.where(in_group, wins[gi], win)
        cnt = jnp.minimum(pos + 1, win).astype(F32)
        outs.append(win_sum / cnt - p_lt)
    return jnp.concatenate(outs, axis=-1)


def _pool_project(pooled, wpool_ref, pool_scale, w_out_b_ref):
    mixed = _dot(pooled.astype(BF16), wpool_ref[...]) * pool_scale
    return _dot(mixed.astype(BF16), w_out_b_ref[...])


def _merge_ffn_final(x, a, b, c, gate_logits, w_o_ref, g_ffn, w_up_ref, w_down_ref, g_final):
    d = x.shape[-1]
    gates = jax.nn.sigmoid(gate_logits)
    merged = gates[:, 0:d] * a + gates[:, d:2 * d] * b + gates[:, 2 * d:3 * d] * c
    x1 = x + _dot(merged.astype(BF16), w_o_ref[...])
    h2 = _rmsnorm(x1, g_ffn).astype(BF16)
    slab = w_up_ref.shape[1] // FFN_SPLIT
    x2 = x1
    for s in range(FFN_SPLIT):
        up = _dot(h2, w_up_ref[:, s * slab:(s + 1) * slab])
        act = jnp.square(jnp.maximum(up, 0.0)).astype(BF16)
        x2 = x2 + _dot(act, w_down_ref[s * slab:(s + 1) * slab, :])
    return _rmsnorm(x2, g_final)


def _mem_kv_kernel(mem_ref, g_mem_ref, w_kv_ref, kt_ref, vt_ref, ktb_ref, vb_ref):
    xw = kt_ref.shape[1]
    hn = _rmsnorm(mem_ref[0], g_mem_ref[...]).astype(BF16)
    kv = _dot(hn, w_kv_ref[...])
    kt = kv[:, 0:xw].T
    v = kv[:, xw:2 * xw]
    kt_ref[0] = kt
    vt_ref[0] = v.T
    ktb_ref[0] = kt.astype(BF16)
    vb_ref[0] = v.astype(BF16)


def _prompt_kernel(x_ref, kt_ref, vb_ref,
                   g_mix_ref, w_in_ref, g_v_ref, b_v_ref, wmix_ref, bias_ref, wpool_ref,
                   pool_scale_ref, w_out_a_ref, w_out_b_ref, w_out_c_ref, w_o_ref,
                   g_ffn_ref, w_up_ref, w_down_ref, g_final_ref,
                   y_ref, cv_ref, ptail_ref, ext_ref):
    j = pl.program_id(1)
    t = x_ref.shape[1]
    sgu_w = g_v_ref.shape[-1]
    pool_w = pool_scale_ref.shape[-1]
    x_w = kt_ref.shape[1]
    x = x_ref[0]
    h = _rmsnorm(x, g_mix_ref[...]).astype(BF16)

    a, vhat = _spatial_gate(h, w_in_ref, g_v_ref[...], b_v_ref[...], wmix_ref, bias_ref,
                            w_out_a_ref, CHUNK)

    @pl.when(j == pl.num_programs(1) - 1)
    def _():
        cv_ref[0] = vhat[t - CHUNK:, :]

    c0 = 2 * sgu_w
    p = _dot(h, w_in_ref[:, c0:c0 + pool_w])

    @pl.when(j == 0)
    def _():
        ext_ref[0:POOL_PAD, :] = jnp.zeros((POOL_PAD, pool_w), F32)

    ext_ref[POOL_PAD:POOL_PAD + t, :] = p

    def load_shifted(k, lt):
        return ext_ref[POOL_PAD - k:POOL_PAD - k + t, lt * V7X_LANES:(lt + 1) * V7X_LANES]

    pos = j * t + lax.broadcasted_iota(jnp.int32, (t, V7X_LANES), 0)
    pooled = _window_means_minus_self(load_shifted, p, pos)
    tail = ext_ref[t:t + POOL_PAD, :]
    ext_ref[0:POOL_PAD, :] = tail

    @pl.when(j == pl.num_programs(1) - 1)
    def _():
        ptail_ref[0] = tail

    b = _pool_project(pooled, wpool_ref, pool_scale_ref[...], w_out_b_ref)

    c1 = c0 + pool_w
    head_dim = x_w // X_HEADS
    q = _dot(h, w_in_ref[:, c1:c1 + x_w]) * (head_dim ** -0.5)
    head = lax.broadcasted_iota(jnp.int32, (t, x_w), 1) >> _log2(head_dim)
    kt = kt_ref[0]
    vb = vb_ref[0]
    o = jnp.zeros((t, x_w), F32)
    for hd in range(X_HEADS):
        in_head = head == hd
        s = _dot(jnp.where(in_head, q, 0.0).astype(BF16), kt)
        e = jnp.exp(s - jnp.max(s, axis=-1, keepdims=True))
        prob = e * (1.0 / jnp.sum(e, axis=-1, keepdims=True))
        o = jnp.where(in_head, _dot(prob.astype(BF16), vb), o)
    c = _dot(o.astype(BF16), w_out_c_ref[...])

    c2 = c1 + x_w
    gate_logits = _dot(h, w_in_ref[:, c2:])
    y_ref[0] = _merge_ffn_final(x, a, b, c, gate_logits, w_o_ref, g_ffn_ref[...],
                                w_up_ref, w_down_ref, g_final_ref[...])


def _sample_attn_kernel(x_ref, kt_ref, vt_ref, g_mix_ref, w_in_ref, o_ref, q_ref):
    t = x_ref.shape[0]
    n_seq, x_w, _ = kt_ref.shape
    rows = t // n_seq
    head_dim = x_w // X_HEADS
    q_col = w_in_ref.shape[1] - 3 * x_ref.shape[1] - x_w
    h = _rmsnorm(x_ref[...], g_mix_ref[...]).astype(BF16)
    q_ref[...] = _dot(h, w_in_ref[:, q_col:q_col + x_w]) * (head_dim ** -0.5)
    stack = X_HEADS * rows
    lane_head = lax.broadcasted_iota(jnp.int32, (stack, x_w), 1) >> _log2(head_dim)
    row_head = lax.broadcasted_iota(jnp.int32, (stack, x_w), 0) >> _log2(rows)
    own_head = lane_head == row_head
    for i in range(n_seq):
        qs = jnp.concatenate([q_ref[i * rows:(i + 1) * rows, :]] * X_HEADS, axis=0)
        qm = jnp.where(own_head, qs, 0.0).astype(BF16)
        s = _dot(qm, kt_ref[i].astype(BF16))
        e = jnp.exp(s - jnp.max(s, axis=-1, keepdims=True))
        prob = e * (1.0 / jnp.sum(e, axis=-1, keepdims=True))
        of = lax.dot_general(prob.astype(BF16), vt_ref[i].astype(BF16),
                             (((1,), (1,)), ((), ())), preferred_element_type=F32)
        of = jnp.where(own_head, of, 0.0)
        out = of[0:rows]
        for hd in range(1, X_HEADS):
            out = out + of[hd * rows:(hd + 1) * rows]
        o_ref[i * rows:(i + 1) * rows, :] = out


def _sample_kernel(start_pos, x_ref, o_ref, st_ref,
                   g_mix_ref, w_in_ref, g_v_ref, b_v_ref, wmix_ref, bias_ref, wpool_ref,
                   pool_scale_ref, w_out_a_ref, w_out_b_ref, w_out_c_ref, w_o_ref,
                   g_ffn_ref, w_up_ref, w_down_ref, g_final_ref,
                   y_ref, cv_ref, ptail_ref, ext_ref):
    t = x_ref.shape[0]
    n_seq = st_ref.shape[0]
    rows = t // n_seq
    sgu_w = g_v_ref.shape[-1]
    pool_w = pool_scale_ref.shape[-1]
    x_w = o_ref.shape[-1]
    x = x_ref[...]
    h = _rmsnorm(x, g_mix_ref[...]).astype(BF16)

    a, vhat = _spatial_gate(h, w_in_ref, g_v_ref[...], b_v_ref[...], wmix_ref, bias_ref,
                            w_out_a_ref, rows)
    cv_ref[...] = vhat

    c0 = 2 * sgu_w
    p = _dot(h, w_in_ref[:, c0:c0 + pool_w]).reshape(n_seq, rows, pool_w)
    ext_ref[:, 0:POOL_PAD, :] = st_ref[...]
    ext_ref[:, POOL_PAD:POOL_PAD + rows, :] = p

    def load_shifted(k, lt):
        return ext_ref[:, POOL_PAD - k:POOL_PAD - k + rows, lt * V7X_LANES:(lt + 1) * V7X_LANES]

    pos = start_pos + lax.broadcasted_iota(jnp.int32, (n_seq, rows, V7X_LANES), 1)
    pooled = _window_means_minus_self(load_shifted, p, pos).reshape(t, pool_w)
    ptail_ref[...] = ext_ref[:, rows:rows + POOL_PAD, :]
    b = _pool_project(pooled, wpool_ref, pool_scale_ref[...], w_out_b_ref)

    c = _dot(o_ref[...].astype(BF16), w_out_c_ref[...])

    c2 = c0 + pool_w + x_w
    gate_logits = _dot(h, w_in_ref[:, c2:])
    y_ref[...] = _merge_ffn_final(x, a, b, c, gate_logits, w_o_ref, g_ffn_ref[...],
                                  w_up_ref, w_down_ref, g_final_ref[...])


def _layer_weight_operands(lw, tile_rows, chunk_rows):
    (g_mix, w_in, g_v, b_v, w_s, b_s, w_pool, pool_scale,
     w_out_a, w_out_b, w_out_c, w_o, g_ffn, w_up, w_down, g_final) = lw
    reps = tile_rows // chunk_rows
    row2 = lambda vec: vec.reshape(1, -1).astype(F32)
    wmix = jnp.tile(w_s[:, :chunk_rows, :chunk_rows], (1, reps, reps)).astype(BF16)
    group_dim = g_v.shape[-1] // w_s.shape[0]
    bias = jnp.tile(jnp.repeat(b_s[:, :chunk_rows].T, group_dim, axis=1), (reps, 1)).astype(F32)
    wpool = jax.scipy.linalg.block_diag(*[w_pool[i] for i in range(w_pool.shape[0])])
    return [row2(g_mix), w_in, row2(g_v), row2(b_v), wmix, bias,
            wpool.astype(BF16), row2(pool_scale), w_out_a, w_out_b,
            w_out_c, w_o, row2(g_ffn), w_up, w_down, row2(g_final)]


def _mem_kv(mem, g_mem, w_kv):
    bsz, n_mem, d = mem.shape
    xw = w_kv.shape[-1] // 2
    return pl.pallas_call(
        _mem_kv_kernel,
        grid=(bsz,),
        in_specs=[pl.BlockSpec((1, n_mem, d), lambda i: (i, 0, 0)),
                  _const_spec((1, d)), _const_spec((d, 2 * xw))],
        out_specs=[pl.BlockSpec((1, xw, n_mem), lambda i: (i, 0, 0)),
                   pl.BlockSpec((1, xw, n_mem), lambda i: (i, 0, 0)),
                   pl.BlockSpec((1, xw, n_mem), lambda i: (i, 0, 0)),
                   pl.BlockSpec((1, n_mem, xw), lambda i: (i, 0, 0))],
        out_shape=[jax.ShapeDtypeStruct((bsz, xw, n_mem), F32),
                   jax.ShapeDtypeStruct((bsz, xw, n_mem), F32),
                   jax.ShapeDtypeStruct((bsz, xw, n_mem), BF16),
                   jax.ShapeDtypeStruct((bsz, n_mem, xw), BF16)],
        compiler_params=pltpu.CompilerParams(dimension_semantics=("arbitrary",)),
        name="mem_kv",
    )(mem, g_mem.reshape(1, d).astype(F32), w_kv.astype(BF16))


def _prompt_layer(x, ktb, vb, lw):
    bsz, seq, d = x.shape
    t = PROMPT_TILE
    ops = _layer_weight_operands(lw, t, CHUNK)
    sgu_w = lw[2].shape[-1]
    pool_w = lw[7].shape[-1]
    n_mem, xw = vb.shape[1:]
    return pl.pallas_call(
        _prompt_kernel,
        grid=(bsz, seq // t),
        in_specs=[pl.BlockSpec((1, t, d), lambda i, j: (i, j, 0)),
                  pl.BlockSpec((1, xw, n_mem), lambda i, j: (i, 0, 0)),
                  pl.BlockSpec((1, n_mem, xw), lambda i, j: (i, 0, 0))]
                 + [_const_spec(o.shape) for o in ops],
        out_specs=[pl.BlockSpec((1, t, d), lambda i, j: (i, j, 0)),
                   pl.BlockSpec((1, CHUNK, sgu_w), lambda i, j: (i, 0, 0)),
                   pl.BlockSpec((1, POOL_PAD, pool_w), lambda i, j: (i, 0, 0))],
        out_shape=[jax.ShapeDtypeStruct((bsz, seq, d), F32),
                   jax.ShapeDtypeStruct((bsz, CHUNK, sgu_w), F32),
                   jax.ShapeDtypeStruct((bsz, POOL_PAD, pool_w), F32)],
        scratch_shapes=[pltpu.VMEM((POOL_PAD + t, pool_w), F32)],
        compiler_params=pltpu.CompilerParams(
            dimension_semantics=("arbitrary", "arbitrary"),
            vmem_limit_bytes=VMEM_LIMIT_BYTES),
        name="prompt_layer",
    )(x, ktb, vb, *ops)


def _sample_attn(x2d, kt, vt, g_mix, w_in_b):
    n_seq, xw, n_mem = kt.shape
    d = x2d.shape[1]
    ts = ATTN_TILE_SEQS
    t = ts * (x2d.shape[0] // n_seq)
    return pl.pallas_call(
        _sample_attn_kernel,
        grid=(n_seq // ts,),
        in_specs=[pl.BlockSpec((t, d), lambda i: (i, 0)),
                  pl.BlockSpec((ts, xw, n_mem), lambda i: (i, 0, 0)),
                  pl.BlockSpec((ts, xw, n_mem), lambda i: (i, 0, 0)),
                  _const_spec((1, d)), _const_spec(w_in_b.shape)],
        out_specs=pl.BlockSpec((t, xw), lambda i: (i, 0)),
        out_shape=jax.ShapeDtypeStruct((x2d.shape[0], xw), F32),
        scratch_shapes=[pltpu.VMEM((t, xw), F32)],
        compiler_params=pltpu.CompilerParams(
            dimension_semantics=("arbitrary",),
            vmem_limit_bytes=VMEM_LIMIT_BYTES),
        name="sample_attn",
    )(x2d, kt, vt, g_mix.reshape(1, d).astype(F32), w_in_b)


def _sample_layer(x2d, attn_out, state, start_pos, lw):
    n_seq = state.shape[0]
    d = x2d.shape[1]
    rows = x2d.shape[0] // n_seq
    ts = SAMPLE_TILE_SEQS
    t = ts * rows
    ops = _layer_weight_operands(lw, t, rows)
    sgu_w = lw[2].shape[-1]
    pool_w = lw[7].shape[-1]
    xw = attn_out.shape[1]
    state_pad = jnp.pad(state, ((0, 0), (POOL_PAD - state.shape[1], 0), (0, 0)))
    y, cv, ptail = pl.pallas_call(
        functools.partial(_sample_kernel, start_pos),
        grid=(n_seq // ts,),
        in_specs=[pl.BlockSpec((t, d), lambda i: (i, 0)),
                  pl.BlockSpec((t, xw), lambda i: (i, 0)),
                  pl.BlockSpec((ts, POOL_PAD, pool_w), lambda i: (i, 0, 0))]
                 + [_const_spec(o.shape) for o in ops],
        out_specs=[pl.BlockSpec((t, d), lambda i: (i, 0)),
                   pl.BlockSpec((t, sgu_w), lambda i: (i, 0)),
                   pl.BlockSpec((ts, POOL_PAD, pool_w), lambda i: (i, 0, 0))],
        out_shape=[jax.ShapeDtypeStruct((n_seq * rows, d), F32),
                   jax.ShapeDtypeStruct((n_seq * rows, sgu_w), F32),
                   jax.ShapeDtypeStruct((n_seq, POOL_PAD, pool_w), F32)],
        scratch_shapes=[pltpu.VMEM((ts, POOL_PAD + rows, pool_w), F32)],
        compiler_params=pltpu.CompilerParams(
            dimension_semantics=("arbitrary",),
            vmem_limit_bytes=VMEM_LIMIT_BYTES),
        name="sample_layer",
    )(x2d, attn_out, state_pad, *ops)
    return (y.reshape(n_seq, rows, d), cv.reshape(n_seq, rows, sgu_w),
            ptail[:, POOL_PAD - POOL_STATE:, :])


def _feature_major(kv):
    bsz, n_mem, heads, head_dim = kv.shape
    return jnp.transpose(kv, (0, 2, 3, 1)).reshape(bsz, heads * head_dim, n_mem)


def _token_major(kvt, heads):
    bsz, xw, n_mem = kvt.shape
    return jnp.transpose(kvt.reshape(bsz, heads, xw // heads, n_mem), (0, 3, 1, 2))


def kernel(x_prompt, x_sample, mem_prompt, cache_mem_k, cache_mem_v, state_pool, g_mix, w_in, g_v, b_v, w_s, b_s, w_pool, pool_scale, g_mem, w_kv, w_out_a, w_out_b, w_out_c, w_o, g_ffn, w_up, w_down, g_final):
    depth = w_in.shape[0]
    assert depth == 1, "the final rmsnorm is fused into the (single) layer kernel"
    heads = cache_mem_k.shape[-2]
    n_seq, rows, d = x_sample.shape
    bf = lambda w: w.astype(BF16)
    lw = (g_mix[0], bf(w_in[0]), g_v[0], b_v[0], w_s[0], b_s[0], w_pool[0], pool_scale[0],
          bf(w_out_a[0]), bf(w_out_b[0]), bf(w_out_c[0]), bf(w_o[0]), g_ffn[0], bf(w_up[0]),
          bf(w_down[0]), g_final)
    mem_kt, mem_vt, ktb, vb = _mem_kv(mem_prompt, g_mem[0], w_kv[0])
    y_prompt, cv_p, ptail_p = _prompt_layer(x_prompt, ktb, vb, lw)
    x2d = x_sample.reshape(n_seq * rows, d)
    attn = _sample_attn(x2d, _feature_major(cache_mem_k[0]), _feature_major(cache_mem_v[0]),
                        g_mix[0], lw[1])
    y_sample, cv_s, ptail_s = _sample_layer(x2d, attn, state_pool[0], PAST_LEN, lw)
    return (y_prompt, y_sample, _token_major(mem_kt, heads)[None], _token_major(mem_vt, heads)[None],
            ptail_p[None, :, POOL_PAD - POOL_STATE:, :], ptail_s[None], cv_p[None], cv_s[None])
```

```python
import functools

import jax
import jax.numpy as jnp
from jax import lax
from jax.experimental import pallas as pl
from jax.experimental.pallas import tpu as pltpu

PAST_LEN = 16384
CHUNK = 128
POOL_WINDOWS = (2, 4, 8, 16)
POOL_STATE = max(POOL_WINDOWS) - 1
X_HEADS = 4
EPS = 1e-6

V7X_VMEM_BYTES = 64 * 1024 * 1024
V7X_LANES = 128
VMEM_LIMIT_BYTES = V7X_VMEM_BYTES - 6 * 1024 * 1024

PROMPT_TILE = 256
SAMPLE_TILE_SEQS = 32
ATTN_TILE_SEQS = 16
FFN_SPLIT = 4
POOL_PAD = POOL_STATE + 1

BF16 = jnp.bfloat16
F32 = jnp.float32


def _dot(a, b):
    return jnp.dot(a, b, preferred_element_type=F32)


def _log2(n):
    assert n > 0 and n & (n - 1) == 0, n
    return n.bit_length() - 1


def _rmsnorm(x, g):
    return x * lax.rsqrt(jnp.mean(x * x, axis=-1, keepdims=True) + EPS) * g


def _layernorm(x, g, b):
    xc = x - jnp.mean(x, axis=-1, keepdims=True)
    return xc * lax.rsqrt(jnp.mean(xc * xc, axis=-1, keepdims=True) + EPS) * g + b


def _const_spec(shape):
    zeros = (0,) * len(shape)
    return pl.BlockSpec(shape, lambda *_: zeros, pipeline_mode=pl.Buffered(1))


def _spatial_gate(h, w_in_ref, g_v, b_v, wmix_ref, bias_ref, w_out_a_ref, chunk_rows):
    t = h.shape[0]
    sgu_w = g_v.shape[-1]
    group_dim = sgu_w // wmix_ref.shape[0]
    u = jax.nn.gelu(_dot(h, w_in_ref[:, 0:sgu_w]))
    v = jax.nn.gelu(_dot(h, w_in_ref[:, sgu_w:2 * sgu_w]))
    vhat = _layernorm(v, g_v, b_v)
    vb = vhat.astype(BF16)
    shift = _log2(chunk_rows)
    reps = t // chunk_rows
    row = lax.broadcasted_iota(jnp.int32, (t, t), 0)
    col = lax.broadcasted_iota(jnp.int32, (t, t), 1)
    same_chunk = (row >> shift) == (col >> shift)
    causal = col <= row
    parts = []
    for g in range(wmix_ref.shape[0]):
        w = jnp.concatenate([wmix_ref[g]] * reps, axis=0)
        w = jnp.where(same_chunk, jnp.where(causal, w, 0.0), 0.0).astype(BF16)
        parts.append(_dot(w, vb[:, g * group_dim:(g + 1) * group_dim]))
    mixed = jnp.concatenate(parts, axis=1) + jnp.concatenate([bias_ref[...]] * reps, axis=0)
    a = _dot((u * mixed).astype(BF16), w_out_a_ref[...])
    return a, vhat


def _window_means_minus_self(load_shifted, p, pos):
    group_lanes = p.shape[-1] // len(POOL_WINDOWS)
    outs = []
    for lt in range(p.shape[-1] // V7X_LANES):
        p_lt = p[..., lt * V7X_LANES:(lt + 1) * V7X_LANES]
        lane = lax.broadcasted_iota(jnp.int32, p_lt.shape, p_lt.ndim - 1)
        wins = POOL_WINDOWS[lt * V7X_LANES // group_lanes:(lt + 1) * V7X_LANES // group_lanes]
        acc = p_lt
        sums = {1: p_lt}
        for k in range(1, max(wins)):
            acc = acc + load_shifted(k, lt)
            sums[k + 1] = acc
        win_sum = sums[wins[-1]]
        win = jnp.full(p_lt.shape, wins[-1], jnp.int32)
        for gi in range(len(wins) - 2, -1, -1):
            in_group = lane < (gi + 1) * group_lanes
            win_sum = jnp.where(in_group, sums[wins[gi]], win_sum)
            win = jnp.where(in_group, wins[gi], win)
        cnt = jnp.minimum(pos + 1, win).astype(F32)
        outs.append(win_sum / cnt - p_lt)
    return jnp.concatenate(outs, axis=-1)


def _pool_project(pooled, wpool_ref, pool_scale, w_out_b_ref):
    mixed = _dot(pooled.astype(BF16), wpool_ref[...]) * pool_scale
    return _dot(mixed.astype(BF16), w_out_b_ref[...])


def _merge_ffn_final(x, a, b, c, gate_logits, w_o_ref, g_ffn, w_up_ref, w_down_ref, g_final):
    d = x.shape[-1]
    gates = jax.nn.sigmoid(gate_logits)
    merged = gates[:, 0:d] * a + gates[:, d:2 * d] * b + gates[:, 2 * d:3 * d] * c
    x1 = x + _dot(merged.astype(BF16), w_o_ref[...])
    h2 = _rmsnorm(x1, g_ffn).astype(BF16)
    slab = w_up_ref.shape[1] // FFN_SPLIT
    x2 = x1
    for s in range(FFN_SPLIT):
        up = _dot(h2, w_up_ref[:, s * slab:(s + 1) * slab])
        act = jnp.square(jnp.maximum(up, 0.0)).astype(BF16)
        x2 = x2 + _dot(act, w_down_ref[s * slab:(s + 1) * slab, :])
    return _rmsnorm(x2, g_final)


def _mem_kv_kernel(mem_ref, g_mem_ref, w_kv_ref, kt_ref, vt_ref, ktb_ref, vb_ref):
    xw = kt_ref.shape[1]
    hn = _rmsnorm(mem_ref[0], g_mem_ref[...]).astype(BF16)
    kv = _dot(hn, w_kv_ref[...])
    kt = kv[:, 0:xw].T
    v = kv[:, xw:2 * xw]
    kt_ref[0] = kt
    vt_ref[0] = v.T
    ktb_ref[0] = kt.astype(BF16)
    vb_ref[0] = v.astype(BF16)


def _prompt_kernel(x_ref, kt_ref, vb_ref,
                   g_mix_ref, w_in_ref, g_v_ref, b_v_ref, wmix_ref, bias_ref, wpool_ref,
                   pool_scale_ref, w_out_a_ref, w_out_b_ref, w_out_c_ref, w_o_ref,
                   g_ffn_ref, w_up_ref, w_down_ref, g_final_ref,
                   y_ref, cv_ref, ptail_ref, ext_ref):
    j = pl.program_id(1)
    t = x_ref.shape[1]
    sgu_w = g_v_ref.shape[-1]
    pool_w = pool_scale_ref.shape[-1]
    x_w = kt_ref.shape[1]

    @pl.when(j == 0)
    def _():
        ext_ref[0:POOL_PAD, :] = jnp.zeros((POOL_PAD, pool_w), F32)

    x = x_ref[0]
    h = _rmsnorm(x, g_mix_ref[...]).astype(BF16)

    a, vhat = _spatial_gate(h, w_in_ref, g_v_ref[...], b_v_ref[...], wmix_ref, bias_ref,
                            w_out_a_ref, CHUNK)
    cv_ref[0] = vhat[t - CHUNK:, :]

    c0 = 2 * sgu_w
    p = _dot(h, w_in_ref[:, c0:c0 + pool_w])
    ext_ref[POOL_PAD:POOL_PAD + t, :] = p

    def load_shifted(k, lt):
        return ext_ref[POOL_PAD - k:POOL_PAD - k + t, lt * V7X_LANES:(lt + 1) * V7X_LANES]

    pos = j * t + lax.broadcasted_iota(jnp.int32, (t, V7X_LANES), 0)
    pooled = _window_means_minus_self(load_shifted, p, pos)
    tail = ext_ref[t:t + POOL_PAD, :]
    ext_ref[0:POOL_PAD, :] = tail
    ptail_ref[0] = tail

    b = _pool_project(pooled, wpool_ref, pool_scale_ref[...], w_out_b_ref)

    c1 = c0 + pool_w
    head_dim = x_w // X_HEADS
    q = _dot(h, w_in_ref[:, c1:c1 + x_w]) * (head_dim ** -0.5)
    head = lax.broadcasted_iota(jnp.int32, (t, x_w), 1) >> _log2(head_dim)
    kt = kt_ref[0]
    vb = vb_ref[0]
    o = jnp.zeros((t, x_w), F32)
    for hd in range(X_HEADS):
        in_head = head == hd
        s = _dot(jnp.where(in_head, q, 0.0).astype(BF16), kt)
        e = jnp.exp(s - jnp.max(s, axis=-1, keepdims=True))
        prob = e * (1.0 / jnp.sum(e, axis=-1, keepdims=True))
        o = jnp.where(in_head, _dot(prob.astype(BF16), vb), o)
    c = _dot(o.astype(BF16), w_out_c_ref[...])

    c2 = c1 + x_w
    gate_logits = _dot(h, w_in_ref[:, c2:])
    y_ref[0] = _merge_ffn_final(x, a, b, c, gate_logits, w_o_ref, g_ffn_ref[...],
                                w_up_ref, w_down_ref, g_final_ref[...])


def _sample_attn_kernel(x_ref, kt_ref, vt_ref, g_mix_ref, w_in_ref, o_ref, q_ref):
    t = x_ref.shape[0]
    n_seq, x_w, _ = kt_ref.shape
    rows = t // n_seq
    head_dim = x_w // X_HEADS
    q_col = w_in_ref.shape[1] - 3 * x_ref.shape[1] - x_w
    h = _rmsnorm(x_ref[...], g_mix_ref[...]).astype(BF16)
    q_ref[...] = _dot(h, w_in_ref[:, q_col:q_col + x_w]) * (head_dim ** -0.5)
    stack = X_HEADS * rows
    lane_head = lax.broadcasted_iota(jnp.int32, (stack, x_w), 1) >> _log2(head_dim)
    row_head = lax.broadcasted_iota(jnp.int32, (stack, x_w), 0) >> _log2(rows)
    own_head = lane_head == row_head
    for i in range(n_seq):
        qs = jnp.concatenate([q_ref[i * rows:(i + 1) * rows, :]] * X_HEADS, axis=0)
        qm = jnp.where(own_head, qs, 0.0).astype(BF16)
        s = _dot(qm, kt_ref[i].astype(BF16))
        e = jnp.exp(s - jnp.max(s, axis=-1, keepdims=True))
        prob = e * (1.0 / jnp.sum(e, axis=-1, keepdims=True))
        of = lax.dot_general(prob.astype(BF16), vt_ref[i].astype(BF16),
                             (((1,), (1,)), ((), ())), preferred_element_type=F32)
        of = jnp.where(own_head, of, 0.0)
        out = of[0:rows]
        for hd in range(1, X_HEADS):
            out = out + of[hd * rows:(hd + 1) * rows]
        o_ref[i * rows:(i + 1) * rows, :] = out


def _sample_kernel(start_pos, x_ref, o_ref, st_ref,
                   g_mix_ref, w_in_ref, g_v_ref, b_v_ref, wmix_ref, bias_ref, wpool_ref,
                   pool_scale_ref, w_out_a_ref, w_out_b_ref, w_out_c_ref, w_o_ref,
                   g_ffn_ref, w_up_ref, w_down_ref, g_final_ref,
                   y_ref, cv_ref, ptail_ref, ext_ref):
    t = x_ref.shape[0]
    n_seq = st_ref.shape[0]
    rows = t // n_seq
    sgu_w = g_v_ref.shape[-1]
    pool_w = pool_scale_ref.shape[-1]
    x_w = o_ref.shape[-1]
    x = x_ref[...]
    h = _rmsnorm(x, g_mix_ref[...]).astype(BF16)

    a, vhat = _spatial_gate(h, w_in_ref, g_v_ref[...], b_v_ref[...], wmix_ref, bias_ref,
                            w_out_a_ref, rows)
    cv_ref[...] = vhat

    c0 = 2 * sgu_w
    p = _dot(h, w_in_ref[:, c0:c0 + pool_w]).reshape(n_seq, rows, pool_w)
    ext_ref[:, 0:POOL_PAD, :] = st_ref[...]
    ext_ref[:, POOL_PAD:POOL_PAD + rows, :] = p

    def load_shifted(k, lt):
        return ext_ref[:, POOL_PAD - k:POOL_PAD - k + rows, lt * V7X_LANES:(lt + 1) * V7X_LANES]

    pos = start_pos + lax.broadcasted_iota(jnp.int32, (n_seq, rows, V7X_LANES), 1)
    pooled = _window_means_minus_self(load_shifted, p, pos).reshape(t, pool_w)
    ptail_ref[...] = ext_ref[:, rows:rows + POOL_PAD, :]
    b = _pool_project(pooled, wpool_ref, pool_scale_ref[...], w_out_b_ref)

    c = _dot(o_ref[...].astype(BF16), w_out_c_ref[...])

    c2 = c0 + pool_w + x_w
    gate_logits = _dot(h, w_in_ref[:, c2:])
    y_ref[...] = _merge_ffn_final(x, a, b, c, gate_logits, w_o_ref, g_ffn_ref[...],
                                  w_up_ref, w_down_ref, g_final_ref[...])


def _layer_weight_operands(lw, tile_rows, chunk_rows):
    (g_mix, w_in, g_v, b_v, w_s, b_s, w_pool, pool_scale,
     w_out_a, w_out_b, w_out_c, w_o, g_ffn, w_up, w_down, g_final) = lw
    reps = tile_rows // chunk_rows
    row2 = lambda vec: vec.reshape(1, -1).astype(F32)
    wmix = jnp.tile(w_s[:, :chunk_rows, :chunk_rows], (1, 1, reps)).astype(F32)
    group_dim = g_v.shape[-1] // w_s.shape[0]
    bias = jnp.repeat(b_s[:, :chunk_rows].T, group_dim, axis=1).astype(F32)
    wpool = jax.scipy.linalg.block_diag(*[w_pool[i] for i in range(w_pool.shape[0])])
    return [row2(g_mix), w_in, row2(g_v), row2(b_v), wmix, bias,
            wpool.astype(BF16), row2(pool_scale), w_out_a, w_out_b,
            w_out_c, w_o, row2(g_ffn), w_up, w_down, row2(g_final)]


def _mem_kv(mem, g_mem, w_kv):
    bsz, n_mem, d = mem.shape
    xw = w_kv.shape[-1] // 2
    return pl.pallas_call(
        _mem_kv_kernel,
        grid=(bsz,),
        in_specs=[pl.BlockSpec((1, n_mem, d), lambda i: (i, 0, 0)),
                  _const_spec((1, d)), _const_spec((d, 2 * xw))],
        out_specs=[pl.BlockSpec((1, xw, n_mem), lambda i: (i, 0, 0)),
                   pl.BlockSpec((1, xw, n_mem), lambda i: (i, 0, 0)),
                   pl.BlockSpec((1, xw, n_mem), lambda i: (i, 0, 0)),
                   pl.BlockSpec((1, n_mem, xw), lambda i: (i, 0, 0))],
        out_shape=[jax.ShapeDtypeStruct((bsz, xw, n_mem), F32),
                   jax.ShapeDtypeStruct((bsz, xw, n_mem), F32),
                   jax.ShapeDtypeStruct((bsz, xw, n_mem), BF16),
                   jax.ShapeDtypeStruct((bsz, n_mem, xw), BF16)],
        compiler_params=pltpu.CompilerParams(dimension_semantics=("arbitrary",)),
        name="mem_kv",
    )(mem, g_mem.reshape(1, d).astype(F32), w_kv.astype(BF16))


def _prompt_layer(x, ktb, vb, lw):
    bsz, seq, d = x.shape
    t = PROMPT_TILE
    ops = _layer_weight_operands(lw, t, CHUNK)
    sgu_w = lw[2].shape[-1]
    pool_w = lw[7].shape[-1]
    n_mem, xw = vb.shape[1:]
    return pl.pallas_call(
        _prompt_kernel,
        grid=(bsz, seq // t),
        in_specs=[pl.BlockSpec((1, t, d), lambda i, j: (i, j, 0)),
                  pl.BlockSpec((1, xw, n_mem), lambda i, j: (i, 0, 0)),
                  pl.BlockSpec((1, n_mem, xw), lambda i, j: (i, 0, 0))]
                 + [_const_spec(o.shape) for o in ops],
        out_specs=[pl.BlockSpec((1, t, d), lambda i, j: (i, j, 0)),
                   pl.BlockSpec((1, CHUNK, sgu_w), lambda i, j: (i, 0, 0)),
                   pl.BlockSpec((1, POOL_PAD, pool_w), lambda i, j: (i, 0, 0))],
        out_shape=[jax.ShapeDtypeStruct((bsz, seq, d), F32),
                   jax.ShapeDtypeStruct((bsz, CHUNK, sgu_w), F32),
                   jax.ShapeDtypeStruct((bsz, POOL_PAD, pool_w), F32)],
        scratch_shapes=[pltpu.VMEM((POOL_PAD + t, pool_w), F32)],
        compiler_params=pltpu.CompilerParams(
            dimension_semantics=("arbitrary", "arbitrary"),
            vmem_limit_bytes=VMEM_LIMIT_BYTES),
        name="prompt_layer",
    )(x, ktb, vb, *ops)


def _sample_attn(x2d, kt, vt, g_mix, w_in_b):
    n_seq, xw, n_mem = kt.shape
    d = x2d.shape[1]
    ts = ATTN_TILE_SEQS
    t = ts * (x2d.shape[0] // n_seq)
    return pl.pallas_call(
        _sample_attn_kernel,
        grid=(n_seq // ts,),
        in_specs=[pl.BlockSpec((t, d), lambda i: (i, 0)),
                  pl.BlockSpec((ts, xw, n_mem), lambda i: (i, 0, 0)),
                  pl.BlockSpec((ts, xw, n_mem), lambda i: (i, 0, 0)),
                  _const_spec((1, d)), _const_spec(w_in_b.shape)],
        out_specs=pl.BlockSpec((t, xw), lambda i: (i, 0)),
        out_shape=jax.ShapeDtypeStruct((x2d.shape[0], xw), F32),
        scratch_shapes=[pltpu.VMEM((t, xw), F32)],
        compiler_params=pltpu.CompilerParams(
            dimension_semantics=("arbitrary",),
            vmem_limit_bytes=VMEM_LIMIT_BYTES),
        name="sample_attn",
    )(x2d, kt, vt, g_mix.reshape(1, d).astype(F32), w_in_b)


def _sample_layer(x2d, attn_out, state, start_pos, lw):
    n_seq = state.shape[0]
    d = x2d.shape[1]
    rows = x2d.shape[0] // n_seq
    ts = SAMPLE_TILE_SEQS
    t = ts * rows
    ops = _layer_weight_operands(lw, t, rows)
    sgu_w = lw[2].shape[-1]
    pool_w = lw[7].shape[-1]
    xw = attn_out.shape[1]
    state_pad = jnp.pad(state, ((0, 0), (POOL_PAD - state.shape[1], 0), (0, 0)))
    y, cv, ptail = pl.pallas_call(
        functools.partial(_sample_kernel, start_pos),
        grid=(n_seq // ts,),
        in_specs=[pl.BlockSpec((t, d), lambda i: (i, 0)),
                  pl.BlockSpec((t, xw), lambda i: (i, 0)),
                  pl.BlockSpec((ts, POOL_PAD, pool_w), lambda i: (i, 0, 0))]
                 + [_const_spec(o.shape) for o in ops],
        out_specs=[pl.BlockSpec((t, d), lambda i: (i, 0)),
                   pl.BlockSpec((t, sgu_w), lambda i: (i, 0)),
                   pl.BlockSpec((ts, POOL_PAD, pool_w), lambda i: (i, 0, 0))],
        out_shape=[jax.ShapeDtypeStruct((n_seq * rows, d), F32),
                   jax.ShapeDtypeStruct((n_seq * rows, sgu_w), F32),
                   jax.ShapeDtypeStruct((n_seq, POOL_PAD, pool_w), F32)],
        scratch_shapes=[pltpu.VMEM((ts, POOL_PAD + rows, pool_w), F32)],
        compiler_params=pltpu.CompilerParams(
            dimension_semantics=("arbitrary",),
            vmem_limit_bytes=VMEM_LIMIT_BYTES),
        name="sample_layer",
    )(x2d, attn_out, state_pad, *ops)
    return (y.reshape(n_seq, rows, d), cv.reshape(n_seq, rows, sgu_w),
            ptail[:, POOL_PAD - POOL_STATE:, :])


def _feature_major(kv):
    bsz, n_mem, heads, head_dim = kv.shape
    return jnp.transpose(kv, (0, 2, 3, 1)).reshape(bsz, heads * head_dim, n_mem)


def _token_major(kvt, heads):
    bsz, xw, n_mem = kvt.shape
    return jnp.transpose(kvt.reshape(bsz, heads, xw // heads, n_mem), (0, 3, 1, 2))


def kernel(x_prompt, x_sample, mem_prompt, cache_mem_k, cache_mem_v, state_pool, g_mix, w_in, g_v, b_v, w_s, b_s, w_pool, pool_scale, g_mem, w_kv, w_out_a, w_out_b, w_out_c, w_o, g_ffn, w_up, w_down, g_final):
    depth = w_in.shape[0]
    assert depth == 1, "the final rmsnorm is fused into the (single) layer kernel"
    heads = cache_mem_k.shape[-2]
    n_seq, rows, d = x_sample.shape
    bf = lambda w: w.astype(BF16)
    lw = (g_mix[0], bf(w_in[0]), g_v[0], b_v[0], w_s[0], b_s[0], w_pool[0], pool_scale[0],
          bf(w_out_a[0]), bf(w_out_b[0]), bf(w_out_c[0]), bf(w_o[0]), g_ffn[0], bf(w_up[0]),
          bf(w_down[0]), g_final)
    mem_kt, mem_vt, ktb, vb = _mem_kv(mem_prompt, g_mem[0], w_kv[0])
    y_prompt, cv_p, ptail_p = _prompt_layer(x_prompt, ktb, vb, lw)
    x2d = x_sample.reshape(n_seq * rows, d)
    attn = _sample_attn(x2d, _feature_major(cache_mem_k[0]), _feature_major(cache_mem_v[0]),
                        g_mix[0], lw[1])
    y_sample, cv_s, ptail_s = _sample_layer(x2d, attn, state_pool[0], PAST_LEN, lw)
    return (y_prompt, y_sample, _token_major(mem_kt, heads)[None], _token_major(mem_vt, heads)[None],
            ptail_p[None, :, POOL_PAD - POOL_STATE:, :], ptail_s[None], cv_p[None], cv_s[None])
```

```python
import functools

import jax
import jax.numpy as jnp
from jax import lax
from jax.experimental import pallas as pl
from jax.experimental.pallas import tpu as pltpu

PAST_LEN = 16384
CHUNK = 128
POOL_WINDOWS = (2, 4, 8, 16)
POOL_STATE = max(POOL_WINDOWS) - 1
X_HEADS = 4
EPS = 1e-6

V7X_VMEM_BYTES = 64 * 1024 * 1024
V7X_LANES = 128
VMEM_LIMIT_BYTES = V7X_VMEM_BYTES - 6 * 1024 * 1024

PROMPT_TILE = 256
SAMPLE_TILE_SEQS = 32
ATTN_TILE_SEQS = 16
FFN_SPLIT = 4
POOL_PAD = POOL_STATE + 1

BF16 = jnp.bfloat16
F32 = jnp.float32


def _dot(a, b):
    return jnp.dot(a, b, preferred_element_type=F32)


def _log2(n):
    assert n > 0 and n & (n - 1) == 0, n
    return n.bit_length() - 1


def _rmsnorm(x, g):
    return x * lax.rsqrt(jnp.mean(x * x, axis=-1, keepdims=True) + EPS) * g


def _layernorm(x, g, b):
    xc = x - jnp.mean(x, axis=-1, keepdims=True)
    return xc * lax.rsqrt(jnp.mean(xc * xc, axis=-1, keepdims=True) + EPS) * g + b


def _const_spec(shape):
    zeros = (0,) * len(shape)
    return pl.BlockSpec(shape, lambda *_: zeros, pipeline_mode=pl.Buffered(1))


def _spatial_gate(h, w_in_ref, g_v, b_v, wmix_ref, bias_ref, w_out_a_ref, chunk_rows):
    t = h.shape[0]
    sgu_w = g_v.shape[-1]
    group_dim = sgu_w // wmix_ref.shape[0]
    u = jax.nn.gelu(_dot(h, w_in_ref[:, 0:sgu_w]))
    v = jax.nn.gelu(_dot(h, w_in_ref[:, sgu_w:2 * sgu_w]))
    vhat = _layernorm(v, g_v, b_v)
    vb = vhat.astype(BF16)
    shift = _log2(chunk_rows)
    reps = t // chunk_rows
    row = lax.broadcasted_iota(jnp.int32, (t, t), 0)
    col = lax.broadcasted_iota(jnp.int32, (t, t), 1)
    same_chunk = (row >> shift) == (col >> shift)
    causal = col <= row
    parts = []
    for g in range(wmix_ref.shape[0]):
        w = jnp.concatenate([wmix_ref[g]] * reps, axis=0)
        w = jnp.where(same_chunk, jnp.where(causal, w, 0.0), 0.0).astype(BF16)
        parts.append(_dot(w, vb[:, g * group_dim:(g + 1) * group_dim]))
    mixed = jnp.concatenate(parts, axis=1) + jnp.concatenate([bias_ref[...]] * reps, axis=0)
    a = _dot((u * mixed).astype(BF16), w_out_a_ref[...])
    return a, vhat


def _window_means_minus_self(load_shifted, p, pos):
    group_lanes = p.shape[-1] // len(POOL_WINDOWS)
    outs = []
    for lt in range(p.shape[-1] // V7X_LANES):
        p_lt = p[..., lt * V7X_LANES:(lt + 1) * V7X_LANES]
        lane = lax.broadcasted_iota(jnp.int32, p_lt.shape, p_lt.ndim - 1)
        wins = POOL_WINDOWS[lt * V7X_LANES // group_lanes:(lt + 1) * V7X_LANES // group_lanes]
        acc = p_lt
        sums = {1: p_lt}
        for k in range(1, max(wins)):
            acc = acc + load_shifted(k, lt)
            sums[k + 1] = acc
        win_sum = sums[wins[-1]]
        win = jnp.full(p_lt.shape, wins[-1], jnp.int32)
        for gi in range(len(wins) - 2, -1, -1):
            in_group = lane < (gi + 1) * group_lanes
            win_sum = jnp.where(in_group, sums[wins[gi]], win_sum)
            win = jnp.where(in_group, wins[gi], win)
        cnt = jnp.minimum(pos + 1, win).astype(F32)
        outs.append(win_sum / cnt - p_lt)
    return jnp.concatenate(outs, axis=-1)


def _pool_project(pooled, wpool_ref, pool_scale, w_out_b_ref):
    mixed = _dot(pooled.astype(BF16), wpool_ref[...]) * pool_scale
    return _dot(mixed.astype(BF16), w_out_b_ref[...])


def _merge(x, a, b, c, gate_logits, w_o_ref):
    d = x.shape[-1]
    gates = jax.nn.sigmoid(gate_logits)
    merged = gates[:, 0:d] * a + gates[:, d:2 * d] * b + gates[:, 2 * d:3 * d] * c
    return x + _dot(merged.astype(BF16), w_o_ref[...])


def _merge_ffn_final(x, a, b, c, gate_logits, w_o_ref, g_ffn, w_up_ref, w_down_ref, g_final):
    x1 = _merge(x, a, b, c, gate_logits, w_o_ref)
    h2 = _rmsnorm(x1, g_ffn).astype(BF16)
    slab = w_up_ref.shape[1] // FFN_SPLIT
    x2 = x1
    for s in range(FFN_SPLIT):
        up = _dot(h2, w_up_ref[:, s * slab:(s + 1) * slab])
        act = jnp.square(jnp.maximum(up, 0.0)).astype(BF16)
        x2 = x2 + _dot(act, w_down_ref[s * slab:(s + 1) * slab, :])
    return _rmsnorm(x2, g_final)


def _mem_kv_kernel(mem_ref, g_mem_ref, w_kv_ref, kt_ref, vt_ref, ktb_ref, vb_ref):
    xw = kt_ref.shape[1]
    hn = _rmsnorm(mem_ref[0], g_mem_ref[...]).astype(BF16)
    kv = _dot(hn, w_kv_ref[...])
    kt = kv[:, 0:xw].T
    v = kv[:, xw:2 * xw]
    kt_ref[0] = kt
    vt_ref[0] = v.T
    ktb_ref[0] = kt.astype(BF16)
    vb_ref[0] = v.astype(BF16)


def _prompt_kernel(tiles_per_seq, n_tiles, x_ref, kt_ref, vb_ref,
                   g_mix_ref, w_in_ref, g_v_ref, b_v_ref, wmix_ref, bias_ref, wpool_ref,
                   pool_scale_ref, w_out_a_ref, w_out_b_ref, w_out_c_ref, w_o_ref,
                   g_ffn_ref, w_up_ref, w_down_ref, g_final_ref,
                   y_ref, cv_ref, ptail_ref, ext_ref, x1_ref, h2_ref):
    step = pl.program_id(0)
    j = jnp.minimum(step, n_tiles - 1) % tiles_per_seq
    t, d = x_ref.shape[1:]
    sgu_w = g_v_ref.shape[-1]
    pool_w = pool_scale_ref.shape[-1]
    x_w = kt_ref.shape[1]
    n_groups = wmix_ref.shape[0]
    group_dim = sgu_w // n_groups
    head_dim = x_w // X_HEADS
    c_pool = 2 * sgu_w
    c_q = c_pool + pool_w
    c_gate = c_q + x_w
    slab = w_up_ref.shape[1] // FFN_SPLIT

    @pl.when(step == 0)
    def _():
        x1_ref[...] = jnp.zeros(x1_ref.shape, F32)
        h2_ref[...] = jnp.zeros(h2_ref.shape, BF16)

    @pl.when(j == 0)
    def _():
        ext_ref[0:POOL_PAD, :] = jnp.zeros((POOL_PAD, pool_w), F32)

    def ffn_up(s):
        up = _dot(h2_ref[...], w_up_ref[:, s * slab:(s + 1) * slab])
        return jnp.square(jnp.maximum(up, 0.0)).astype(BF16)

    def ffn_down(act, s):
        return _dot(act, w_down_ref[s * slab:(s + 1) * slab, :])

    def gate(h, part):
        lo = c_gate + part * d
        return jax.nn.sigmoid(_dot(h, w_in_ref[:, lo:lo + d]))

    act = ffn_up(0)
    x = x_ref[0]
    h = _rmsnorm(x, g_mix_ref[...]).astype(BF16)
    u = _dot(h, w_in_ref[:, 0:sgu_w])
    v = _dot(h, w_in_ref[:, sgu_w:2 * sgu_w])
    gate_a = gate(h, 0)
    x2 = x1_ref[...] + ffn_down(act, 0)

    u = jax.nn.gelu(u)
    vhat = _layernorm(jax.nn.gelu(v), g_v_ref[...], b_v_ref[...])
    cv_ref[0] = vhat[t - CHUNK:, :]
    vhat_b = vhat.astype(BF16)
    reps = t // CHUNK
    row = lax.broadcasted_iota(jnp.int32, (t, t), 0)
    col = lax.broadcasted_iota(jnp.int32, (t, t), 1)
    same_chunk = (row >> _log2(CHUNK)) == (col >> _log2(CHUNK))
    causal = col <= row
    parts = []
    for g in range(n_groups):
        w = jnp.concatenate([wmix_ref[g]] * reps, axis=0)
        w = jnp.where(same_chunk, jnp.where(causal, w, 0.0), 0.0).astype(BF16)
        parts.append(_dot(w, vhat_b[:, g * group_dim:(g + 1) * group_dim]))
    p = _dot(h, w_in_ref[:, c_pool:c_pool + pool_w])
    q = _dot(h, w_in_ref[:, c_q:c_q + x_w]) * (head_dim ** -0.5)
    act = ffn_up(1)
    mixed = jnp.concatenate(parts, axis=1) + jnp.concatenate([bias_ref[...]] * reps, axis=0)
    a_in = (u * mixed).astype(BF16)

    ext_ref[POOL_PAD:POOL_PAD + t, :] = p

    def load_shifted(k, lt):
        return ext_ref[POOL_PAD - k:POOL_PAD - k + t, lt * V7X_LANES:(lt + 1) * V7X_LANES]

    pos = j * t + lax.broadcasted_iota(jnp.int32, (t, V7X_LANES), 0)
    pooled = _window_means_minus_self(load_shifted, p, pos).astype(BF16)
    tail = ext_ref[t:t + POOL_PAD, :]
    ext_ref[0:POOL_PAD, :] = tail
    ptail_ref[0] = tail

    a = _dot(a_in, w_out_a_ref[...])

    head = lax.broadcasted_iota(jnp.int32, (t, x_w), 1) >> _log2(head_dim)
    kt = kt_ref[0]
    vb = vb_ref[0]
    scores = [_dot(jnp.where(head == hd, q, 0.0).astype(BF16), kt) for hd in range(X_HEADS)]
    pool_mixed = _dot(pooled, wpool_ref[...]) * pool_scale_ref[...]
    x2 = x2 + ffn_down(act, 1)
    probs = []
    for s in scores:
        e = jnp.exp(s - jnp.max(s, axis=-1, keepdims=True))
        probs.append((e * (1.0 / jnp.sum(e, axis=-1, keepdims=True))).astype(BF16))
    o = jnp.zeros((t, x_w), F32)
    for hd in range(X_HEADS):
        o = jnp.where(head == hd, _dot(probs[hd], vb), o)
    b = _dot(pool_mixed.astype(BF16), w_out_b_ref[...])
    gate_b = gate(h, 1)
    c = _dot(o.astype(BF16), w_out_c_ref[...])
    act = ffn_up(2)
    gate_c = gate(h, 2)
    x2 = x2 + ffn_down(act, 2)
    merged = (gate_a * a + gate_b * b + gate_c * c).astype(BF16)
    act = ffn_up(3)
    x1 = x + _dot(merged, w_o_ref[...])
    x2 = x2 + ffn_down(act, 3)
    x1_ref[...] = x1
    h2_ref[...] = _rmsnorm(x1, g_ffn_ref[...]).astype(BF16)
    y_ref[0] = _rmsnorm(x2, g_final_ref[...])


def _sample_attn_kernel(x_ref, kt_ref, vt_ref, g_mix_ref, w_in_ref, o_ref, q_ref):
    t = x_ref.shape[0]
    n_seq, x_w, _ = kt_ref.shape
    rows = t // n_seq
    head_dim = x_w // X_HEADS
    q_col = w_in_ref.shape[1] - 3 * x_ref.shape[1] - x_w
    h = _rmsnorm(x_ref[...], g_mix_ref[...]).astype(BF16)
    q_ref[...] = _dot(h, w_in_ref[:, q_col:q_col + x_w]) * (head_dim ** -0.5)
    stack = X_HEADS * rows
    lane_head = lax.broadcasted_iota(jnp.int32, (stack, x_w), 1) >> _log2(head_dim)
    row_head = lax.broadcasted_iota(jnp.int32, (stack, x_w), 0) >> _log2(rows)
    own_head = lane_head == row_head
    for i in range(n_seq):
        qs = jnp.concatenate([q_ref[i * rows:(i + 1) * rows, :]] * X_HEADS, axis=0)
        qm = jnp.where(own_head, qs, 0.0).astype(BF16)
        s = _dot(qm, kt_ref[i].astype(BF16))
        e = jnp.exp(s - jnp.max(s, axis=-1, keepdims=True))
        prob = e * (1.0 / jnp.sum(e, axis=-1, keepdims=True))
        of = lax.dot_general(prob.astype(BF16), vt_ref[i].astype(BF16),
                             (((1,), (1,)), ((), ())), preferred_element_type=F32)
        of = jnp.where(own_head, of, 0.0)
        out = of[0:rows]
        for hd in range(1, X_HEADS):
            out = out + of[hd * rows:(hd + 1) * rows]
        o_ref[i * rows:(i + 1) * rows, :] = out


def _sample_kernel(start_pos, x_ref, o_ref, st_ref,
                   g_mix_ref, w_in_ref, g_v_ref, b_v_ref, wmix_ref, bias_ref, wpool_ref,
                   pool_scale_ref, w_out_a_ref, w_out_b_ref, w_out_c_ref, w_o_ref,
                   g_ffn_ref, w_up_ref, w_down_ref, g_final_ref,
                   y_ref, cv_ref, ptail_ref, ext_ref):
    t = x_ref.shape[0]
    n_seq = st_ref.shape[0]
    rows = t // n_seq
    sgu_w = g_v_ref.shape[-1]
    pool_w = pool_scale_ref.shape[-1]
    x_w = o_ref.shape[-1]
    x = x_ref[...]
    h = _rmsnorm(x, g_mix_ref[...]).astype(BF16)

    a, vhat = _spatial_gate(h, w_in_ref, g_v_ref[...], b_v_ref[...], wmix_ref, bias_ref,
                            w_out_a_ref, rows)
    cv_ref[...] = vhat

    c0 = 2 * sgu_w
    p = _dot(h, w_in_ref[:, c0:c0 + pool_w]).reshape(n_seq, rows, pool_w)
    ext_ref[:, 0:POOL_PAD, :] = st_ref[...]
    ext_ref[:, POOL_PAD:POOL_PAD + rows, :] = p

    def load_shifted(k, lt):
        return ext_ref[:, POOL_PAD - k:POOL_PAD - k + rows, lt * V7X_LANES:(lt + 1) * V7X_LANES]

    pos = start_pos + lax.broadcasted_iota(jnp.int32, (n_seq, rows, V7X_LANES), 1)
    pooled = _window_means_minus_self(load_shifted, p, pos).reshape(t, pool_w)
    ptail_ref[...] = ext_ref[:, rows:rows + POOL_PAD, :]
    b = _pool_project(pooled, wpool_ref, pool_scale_ref[...], w_out_b_ref)

    c = _dot(o_ref[...].astype(BF16), w_out_c_ref[...])

    c2 = c0 + pool_w + x_w
    gate_logits = _dot(h, w_in_ref[:, c2:])
    y_ref[...] = _merge_ffn_final(x, a, b, c, gate_logits, w_o_ref, g_ffn_ref[...],
                                  w_up_ref, w_down_ref, g_final_ref[...])


def _layer_weight_operands(lw, tile_rows, chunk_rows):
    (g_mix, w_in, g_v, b_v, w_s, b_s, w_pool, pool_scale,
     w_out_a, w_out_b, w_out_c, w_o, g_ffn, w_up, w_down, g_final) = lw
    reps = tile_rows // chunk_rows
    row2 = lambda vec: vec.reshape(1, -1).astype(F32)
    wmix = jnp.tile(w_s[:, :chunk_rows, :chunk_rows], (1, 1, reps)).astype(F32)
    group_dim = g_v.shape[-1] // w_s.shape[0]
    bias = jnp.repeat(b_s[:, :chunk_rows].T, group_dim, axis=1).astype(F32)
    wpool = jax.scipy.linalg.block_diag(*[w_pool[i] for i in range(w_pool.shape[0])])
    return [row2(g_mix), w_in, row2(g_v), row2(b_v), wmix, bias,
            wpool.astype(BF16), row2(pool_scale), w_out_a, w_out_b,
            w_out_c, w_o, row2(g_ffn), w_up, w_down, row2(g_final)]


def _mem_kv(mem, g_mem, w_kv):
    bsz, n_mem, d = mem.shape
    xw = w_kv.shape[-1] // 2
    return pl.pallas_call(
        _mem_kv_kernel,
        grid=(bsz,),
        in_specs=[pl.BlockSpec((1, n_mem, d), lambda i: (i, 0, 0)),
                  _const_spec((1, d)), _const_spec((d, 2 * xw))],
        out_specs=[pl.BlockSpec((1, xw, n_mem), lambda i: (i, 0, 0)),
                   pl.BlockSpec((1, xw, n_mem), lambda i: (i, 0, 0)),
                   pl.BlockSpec((1, xw, n_mem), lambda i: (i, 0, 0)),
                   pl.BlockSpec((1, n_mem, xw), lambda i: (i, 0, 0))],
        out_shape=[jax.ShapeDtypeStruct((bsz, xw, n_mem), F32),
                   jax.ShapeDtypeStruct((bsz, xw, n_mem), F32),
                   jax.ShapeDtypeStruct((bsz, xw, n_mem), BF16),
                   jax.ShapeDtypeStruct((bsz, n_mem, xw), BF16)],
        compiler_params=pltpu.CompilerParams(dimension_semantics=("arbitrary",)),
        name="mem_kv",
    )(mem, g_mem.reshape(1, d).astype(F32), w_kv.astype(BF16))


def _prompt_layer(x, ktb, vb, lw):
    bsz, seq, d = x.shape
    t = PROMPT_TILE
    ops = _layer_weight_operands(lw, t, CHUNK)
    sgu_w = lw[2].shape[-1]
    pool_w = lw[7].shape[-1]
    n_mem, xw = vb.shape[1:]
    tps = seq // t
    n_tiles = bsz * tps

    def mixer_tile(s):
        m = jnp.minimum(s, n_tiles - 1)
        return m // tps, m % tps

    def ffn_tile(s):
        m = jnp.maximum(s - 1, 0)
        return m // tps, m % tps

    return pl.pallas_call(
        functools.partial(_prompt_kernel, tps, n_tiles),
        grid=(n_tiles + 1,),
        in_specs=[pl.BlockSpec((1, t, d), lambda s: (*mixer_tile(s), 0)),
                  pl.BlockSpec((1, xw, n_mem), lambda s: (mixer_tile(s)[0], 0, 0)),
                  pl.BlockSpec((1, n_mem, xw), lambda s: (mixer_tile(s)[0], 0, 0))]
                 + [_const_spec(o.shape) for o in ops],
        out_specs=[pl.BlockSpec((1, t, d), lambda s: (*ffn_tile(s), 0)),
                   pl.BlockSpec((1, CHUNK, sgu_w), lambda s: (mixer_tile(s)[0], 0, 0)),
                   pl.BlockSpec((1, POOL_PAD, pool_w), lambda s: (mixer_tile(s)[0], 0, 0))],
        out_shape=[jax.ShapeDtypeStruct((bsz, seq, d), F32),
                   jax.ShapeDtypeStruct((bsz, CHUNK, sgu_w), F32),
                   jax.ShapeDtypeStruct((bsz, POOL_PAD, pool_w), F32)],
        scratch_shapes=[pltpu.VMEM((POOL_PAD + t, pool_w), F32),
                        pltpu.VMEM((t, d), F32),
                        pltpu.VMEM((t, d), BF16)],
        compiler_params=pltpu.CompilerParams(
            dimension_semantics=("arbitrary",),
            vmem_limit_bytes=VMEM_LIMIT_BYTES),
        name="prompt_layer",
    )(x, ktb, vb, *ops)


def _sample_attn(x2d, kt, vt, g_mix, w_in_b):
    n_seq, xw, n_mem = kt.shape
    d = x2d.shape[1]
    ts = ATTN_TILE_SEQS
    t = ts * (x2d.shape[0] // n_seq)
    return pl.pallas_call(
        _sample_attn_kernel,
        grid=(n_seq // ts,),
        in_specs=[pl.BlockSpec((t, d), lambda i: (i, 0)),
                  pl.BlockSpec((ts, xw, n_mem), lambda i: (i, 0, 0)),
                  pl.BlockSpec((ts, xw, n_mem), lambda i: (i, 0, 0)),
                  _const_spec((1, d)), _const_spec(w_in_b.shape)],
        out_specs=pl.BlockSpec((t, xw), lambda i: (i, 0)),
        out_shape=jax.ShapeDtypeStruct((x2d.shape[0], xw), F32),
        scratch_shapes=[pltpu.VMEM((t, xw), F32)],
        compiler_params=pltpu.CompilerParams(
            dimension_semantics=("arbitrary",),
            vmem_limit_bytes=VMEM_LIMIT_BYTES),
        name="sample_attn",
    )(x2d, kt, vt, g_mix.reshape(1, d).astype(F32), w_in_b)


def _sample_layer(x2d, attn_out, state, start_pos, lw):
    n_seq = state.shape[0]
    d = x2d.shape[1]
    rows = x2d.shape[0] // n_seq
    ts = SAMPLE_TILE_SEQS
    t = ts * rows
    ops = _layer_weight_operands(lw, t, rows)
    sgu_w = lw[2].shape[-1]
    pool_w = lw[7].shape[-1]
    xw = attn_out.shape[1]
    state_pad = jnp.pad(state, ((0, 0), (POOL_PAD - state.shape[1], 0), (0, 0)))
    y, cv, ptail = pl.pallas_call(
        functools.partial(_sample_kernel, start_pos),
        grid=(n_seq // ts,),
        in_specs=[pl.BlockSpec((t, d), lambda i: (i, 0)),
                  pl.BlockSpec((t, xw), lambda i: (i, 0)),
                  pl.BlockSpec((ts, POOL_PAD, pool_w), lambda i: (i, 0, 0))]
                 + [_const_spec(o.shape) for o in ops],
        out_specs=[pl.BlockSpec((t, d), lambda i: (i, 0)),
                   pl.BlockSpec((t, sgu_w), lambda i: (i, 0)),
                   pl.BlockSpec((ts, POOL_PAD, pool_w), lambda i: (i, 0, 0))],
        out_shape=[jax.ShapeDtypeStruct((n_seq * rows, d), F32),
                   jax.ShapeDtypeStruct((n_seq * rows, sgu_w), F32),
                   jax.ShapeDtypeStruct((n_seq, POOL_PAD, pool_w), F32)],
        scratch_shapes=[pltpu.VMEM((ts, POOL_PAD + rows, pool_w), F32)],
        compiler_params=pltpu.CompilerParams(
            dimension_semantics=("arbitrary",),
            vmem_limit_bytes=VMEM_LIMIT_BYTES),
        name="sample_layer",
    )(x2d, attn_out, state_pad, *ops)
    return (y.reshape(n_seq, rows, d), cv.reshape(n_seq, rows, sgu_w),
            ptail[:, POOL_PAD - POOL_STATE:, :])


def _feature_major(kv):
    bsz, n_mem, heads, head_dim = kv.shape
    return jnp.transpose(kv, (0, 2, 3, 1)).reshape(bsz, heads * head_dim, n_mem)


def _token_major(kvt, heads):
    bsz, xw, n_mem = kvt.shape
    return jnp.transpose(kvt.reshape(bsz, heads, xw // heads, n_mem), (0, 3, 1, 2))


def kernel(x_prompt, x_sample, mem_prompt, cache_mem_k, cache_mem_v, state_pool, g_mix, w_in, g_v, b_v, w_s, b_s, w_pool, pool_scale, g_mem, w_kv, w_out_a, w_out_b, w_out_c, w_o, g_ffn, w_up, w_down, g_final):
    depth = w_in.shape[0]
    assert depth == 1, "the final rmsnorm is fused into the (single) layer kernel"
    heads = cache_mem_k.shape[-2]
    n_seq, rows, d = x_sample.shape
    bf = lambda w: w.astype(BF16)
    lw = (g_mix[0], bf(w_in[0]), g_v[0], b_v[0], w_s[0], b_s[0], w_pool[0], pool_scale[0],
          bf(w_out_a[0]), bf(w_out_b[0]), bf(w_out_c[0]), bf(w_o[0]), g_ffn[0], bf(w_up[0]),
          bf(w_down[0]), g_final)
    mem_kt, mem_vt, ktb, vb = _mem_kv(mem_prompt, g_mem[0], w_kv[0])
    y_prompt, cv_p, ptail_p = _prompt_layer(x_prompt, ktb, vb, lw)
    x2d = x_sample.reshape(n_seq * rows, d)
    attn = _sample_attn(x2d, _feature_major(cache_mem_k[0]), _feature_major(cache_mem_v[0]),
                        g_mix[0], lw[1])
    y_sample, cv_s, ptail_s = _sample_layer(x2d, attn, state_pool[0], PAST_LEN, lw)
    return (y_prompt, y_sample, _token_major(mem_kt, heads)[None], _token_major(mem_vt, heads)[None],
            ptail_p[None, :, POOL_PAD - POOL_STATE:, :], ptail_s[None], cv_p[None], cv_s[None])
```

```python
import functools

import jax
import jax.numpy as jnp
from jax import lax
from jax.experimental import pallas as pl
from jax.experimental.pallas import tpu as pltpu

PAST_LEN = 16384
CHUNK = 128
POOL_WINDOWS = (2, 4, 8, 16)
POOL_STATE = max(POOL_WINDOWS) - 1
X_HEADS = 4
EPS = 1e-6

V7X_VMEM_BYTES = 64 * 1024 * 1024
V7X_LANES = 128
VMEM_LIMIT_BYTES = V7X_VMEM_BYTES - 6 * 1024 * 1024

PROMPT_TILE = 256
SAMPLE_TILE_SEQS = 32
ATTN_TILE_SEQS = 16
ATTN_GROUP_SEQS = 8
FFN_SPLIT = 4
POOL_PAD = POOL_STATE + 1

BF16 = jnp.bfloat16
F32 = jnp.float32


def _dot(a, b):
    return jnp.dot(a, b, preferred_element_type=F32)


def _log2(n):
    assert n > 0 and n & (n - 1) == 0, n
    return n.bit_length() - 1


def _rmsnorm(x, g):
    return x * lax.rsqrt(jnp.mean(x * x, axis=-1, keepdims=True) + EPS) * g


def _layernorm(x, g, b):
    xc = x - jnp.mean(x, axis=-1, keepdims=True)
    return xc * lax.rsqrt(jnp.mean(xc * xc, axis=-1, keepdims=True) + EPS) * g + b


def _const_spec(shape):
    zeros = (0,) * len(shape)
    return pl.BlockSpec(shape, lambda *_: zeros, pipeline_mode=pl.Buffered(1))


def _chunk_mix(vhat_b, wmix_ref, chunk_rows):
    t, sgu_w = vhat_b.shape
    n_groups = wmix_ref.shape[0]
    group_dim = sgu_w // n_groups
    shift = _log2(chunk_rows)
    reps = t // chunk_rows
    row = lax.broadcasted_iota(jnp.int32, (t, t), 0)
    col = lax.broadcasted_iota(jnp.int32, (t, t), 1)
    same_chunk = (row >> shift) == (col >> shift)
    causal = col <= row
    parts = []
    for g in range(n_groups):
        w = jnp.concatenate([wmix_ref[g]] * reps, axis=0)
        w = jnp.where(same_chunk, jnp.where(causal, w, 0.0), 0.0).astype(BF16)
        parts.append(_dot(w, vhat_b[:, g * group_dim:(g + 1) * group_dim]))
    return jnp.concatenate(parts, axis=1)


def _chunk_bias(bias_ref, t):
    return jnp.concatenate([bias_ref[...]] * (t // bias_ref.shape[0]), axis=0)


def _window_means_minus_self(load_shifted, p, pos):
    group_lanes = p.shape[-1] // len(POOL_WINDOWS)
    outs = []
    for lt in range(p.shape[-1] // V7X_LANES):
        p_lt = p[..., lt * V7X_LANES:(lt + 1) * V7X_LANES]
        lane = lax.broadcasted_iota(jnp.int32, p_lt.shape, p_lt.ndim - 1)
        wins = POOL_WINDOWS[lt * V7X_LANES // group_lanes:(lt + 1) * V7X_LANES // group_lanes]
        acc = p_lt
        sums = {1: p_lt}
        for k in range(1, max(wins)):
            acc = acc + load_shifted(k, lt)
            sums[k + 1] = acc
        win_sum = sums[wins[-1]]
        win = jnp.full(p_lt.shape, wins[-1], jnp.int32)
        for gi in range(len(wins) - 2, -1, -1):
            in_group = lane < (gi + 1) * group_lanes
            win_sum = jnp.where(in_group, sums[wins[gi]], win_sum)
            win = jnp.where(in_group, wins[gi], win)
        cnt = jnp.minimum(pos + 1, win).astype(F32)
        outs.append(win_sum / cnt - p_lt)
    return jnp.concatenate(outs, axis=-1)


def _pool_project(pooled, wpool_ref, pool_scale, w_out_b_ref):
    mixed = _dot(pooled.astype(BF16), wpool_ref[...]) * pool_scale
    return _dot(mixed.astype(BF16), w_out_b_ref[...])


def _mem_kv_kernel(mem_ref, g_mem_ref, w_kv_ref, kt_ref, vt_ref, ktb_ref, vb_ref):
    xw = kt_ref.shape[1]
    hn = _rmsnorm(mem_ref[0], g_mem_ref[...]).astype(BF16)
    kv = _dot(hn, w_kv_ref[...])
    kt = kv[:, 0:xw].T
    v = kv[:, xw:2 * xw]
    kt_ref[0] = kt
    vt_ref[0] = v.T
    ktb_ref[0] = kt.astype(BF16)
    vb_ref[0] = v.astype(BF16)


def _prompt_kernel(tiles_per_seq, n_tiles, x_ref, kt_ref, vb_ref,
                   g_mix_ref, w_in_ref, g_v_ref, b_v_ref, wmix_ref, bias_ref, wpool_ref,
                   pool_scale_ref, w_out_a_ref, w_out_b_ref, w_out_c_ref, w_o_ref,
                   g_ffn_ref, w_up_ref, w_down_ref, g_final_ref,
                   y_ref, cv_ref, ptail_ref, ext_ref, x1_ref, h2_ref):
    step = pl.program_id(0)
    j = jnp.minimum(step, n_tiles - 1) % tiles_per_seq
    t, d = x_ref.shape[1:]
    sgu_w = g_v_ref.shape[-1]
    pool_w = pool_scale_ref.shape[-1]
    x_w = kt_ref.shape[1]
    head_dim = x_w // X_HEADS
    c_pool = 2 * sgu_w
    c_q = c_pool + pool_w
    c_gate = c_q + x_w
    slab = w_up_ref.shape[1] // FFN_SPLIT

    @pl.when(step == 0)
    def _():
        x1_ref[...] = jnp.zeros(x1_ref.shape, F32)
        h2_ref[...] = jnp.zeros(h2_ref.shape, BF16)

    @pl.when(j == 0)
    def _():
        ext_ref[0:POOL_PAD, :] = jnp.zeros((POOL_PAD, pool_w), F32)

    def ffn_up(s):
        up = _dot(h2_ref[...], w_up_ref[:, s * slab:(s + 1) * slab])
        return jnp.square(jnp.maximum(up, 0.0)).astype(BF16)

    def ffn_down(act, s):
        return _dot(act, w_down_ref[s * slab:(s + 1) * slab, :])

    def gate(h, part):
        lo = c_gate + part * d
        return jax.nn.sigmoid(_dot(h, w_in_ref[:, lo:lo + d]))

    act = ffn_up(0)
    x = x_ref[0]
    h = _rmsnorm(x, g_mix_ref[...]).astype(BF16)
    u = _dot(h, w_in_ref[:, 0:sgu_w])
    v = _dot(h, w_in_ref[:, sgu_w:2 * sgu_w])
    gate_a = gate(h, 0)
    x2 = x1_ref[...] + ffn_down(act, 0)

    u = jax.nn.gelu(u)
    vhat = _layernorm(jax.nn.gelu(v), g_v_ref[...], b_v_ref[...])
    cv_ref[0] = vhat[t - CHUNK:, :]
    mixed = _chunk_mix(vhat.astype(BF16), wmix_ref, CHUNK)
    p = _dot(h, w_in_ref[:, c_pool:c_pool + pool_w])
    q = _dot(h, w_in_ref[:, c_q:c_q + x_w]) * (head_dim ** -0.5)
    act = ffn_up(1)
    a_in = (u * (mixed + _chunk_bias(bias_ref, t))).astype(BF16)

    ext_ref[POOL_PAD:POOL_PAD + t, :] = p

    def load_shifted(k, lt):
        return ext_ref[POOL_PAD - k:POOL_PAD - k + t, lt * V7X_LANES:(lt + 1) * V7X_LANES]

    pos = j * t + lax.broadcasted_iota(jnp.int32, (t, V7X_LANES), 0)
    pooled = _window_means_minus_self(load_shifted, p, pos).astype(BF16)
    tail = ext_ref[t:t + POOL_PAD, :]
    ext_ref[0:POOL_PAD, :] = tail
    ptail_ref[0] = tail

    a = _dot(a_in, w_out_a_ref[...])

    head = lax.broadcasted_iota(jnp.int32, (t, x_w), 1) >> _log2(head_dim)
    kt = kt_ref[0]
    vb = vb_ref[0]
    scores = [_dot(jnp.where(head == hd, q, 0.0).astype(BF16), kt) for hd in range(X_HEADS)]
    pool_mixed = _dot(pooled, wpool_ref[...]) * pool_scale_ref[...]
    x2 = x2 + ffn_down(act, 1)
    probs = []
    for s in scores:
        e = jnp.exp(s - jnp.max(s, axis=-1, keepdims=True))
        probs.append((e * (1.0 / jnp.sum(e, axis=-1, keepdims=True))).astype(BF16))
    o = jnp.zeros((t, x_w), F32)
    for hd in range(X_HEADS):
        o = jnp.where(head == hd, _dot(probs[hd], vb), o)
    b = _dot(pool_mixed.astype(BF16), w_out_b_ref[...])
    gate_b = gate(h, 1)
    c = _dot(o.astype(BF16), w_out_c_ref[...])
    act = ffn_up(2)
    gate_c = gate(h, 2)
    x2 = x2 + ffn_down(act, 2)
    merged = (gate_a * a + gate_b * b + gate_c * c).astype(BF16)
    act = ffn_up(3)
    x1 = x + _dot(merged, w_o_ref[...])
    x2 = x2 + ffn_down(act, 3)
    x1_ref[...] = x1
    h2_ref[...] = _rmsnorm(x1, g_ffn_ref[...]).astype(BF16)
    y_ref[0] = _rmsnorm(x2, g_final_ref[...])


def _sample_attn_kernel(x_ref, kt_ref, vt_ref, g_mix_ref, wq_ref, o_ref, q_ref):
    t = x_ref.shape[0]
    n_seq, x_w, _ = kt_ref.shape
    rows = t // n_seq
    head_dim = x_w // X_HEADS
    h = _rmsnorm(x_ref[...], g_mix_ref[...]).astype(BF16)
    q_ref[...] = _dot(h, wq_ref[...]) * (head_dim ** -0.5)
    stack = X_HEADS * rows
    lane_head = lax.broadcasted_iota(jnp.int32, (stack, x_w), 1) >> _log2(head_dim)
    row_head = lax.broadcasted_iota(jnp.int32, (stack, x_w), 0) >> _log2(rows)
    own_head = lane_head == row_head

    def scores(i):
        qs = jnp.concatenate([q_ref[i * rows:(i + 1) * rows, :]] * X_HEADS, axis=0)
        qm = jnp.where(own_head, qs, 0.0).astype(BF16)
        return _dot(qm, kt_ref[i].astype(BF16))

    def softmax(s):
        e = jnp.exp(s - jnp.max(s, axis=-1, keepdims=True))
        return (e * (1.0 / jnp.sum(e, axis=-1, keepdims=True))).astype(BF16)

    def attend(i, prob):
        of = lax.dot_general(prob, vt_ref[i].astype(BF16),
                             (((1,), (1,)), ((), ())), preferred_element_type=F32)
        of = jnp.where(own_head, of, 0.0)
        out = of[0:rows]
        for hd in range(1, X_HEADS):
            out = out + of[hd * rows:(hd + 1) * rows]
        o_ref[i * rows:(i + 1) * rows, :] = out

    groups = [range(g, g + ATTN_GROUP_SEQS) for g in range(0, n_seq, ATTN_GROUP_SEQS)]
    pending = []
    for grp in groups + [()]:
        cur = [(i, scores(i)) for i in grp]
        probs = [(i, softmax(s)) for i, s in pending]
        for i, prob in probs:
            attend(i, prob)
        pending = cur


def _sample_kernel(start_pos, x_ref, o_ref, st_ref,
                   g_mix_ref, w_in_ref, g_v_ref, b_v_ref, wmix_ref, bias_ref, wpool_ref,
                   pool_scale_ref, w_out_a_ref, w_out_b_ref, w_out_c_ref, w_o_ref,
                   g_ffn_ref, w_up_ref, w_down_ref, g_final_ref,
                   y_ref, cv_ref, ptail_ref, ext_ref):
    t = x_ref.shape[0]
    n_seq = st_ref.shape[0]
    rows = t // n_seq
    sgu_w = g_v_ref.shape[-1]
    pool_w = pool_scale_ref.shape[-1]
    x_w = o_ref.shape[-1]
    d = x_ref.shape[1]
    c_pool = 2 * sgu_w
    c_gate = c_pool + pool_w + x_w
    slab = w_up_ref.shape[1] // FFN_SPLIT

    def gate(part):
        lo = c_gate + part * d
        return jax.nn.sigmoid(_dot(h, w_in_ref[:, lo:lo + d]))

    x = x_ref[...]
    h = _rmsnorm(x, g_mix_ref[...]).astype(BF16)
    u = _dot(h, w_in_ref[:, 0:sgu_w])
    v = _dot(h, w_in_ref[:, sgu_w:2 * sgu_w])
    p = _dot(h, w_in_ref[:, c_pool:c_pool + pool_w]).reshape(n_seq, rows, pool_w)
    gate_a = gate(0)
    c = _dot(o_ref[...].astype(BF16), w_out_c_ref[...])

    u = jax.nn.gelu(u)
    vhat = _layernorm(jax.nn.gelu(v), g_v_ref[...], b_v_ref[...])
    cv_ref[...] = vhat
    gate_b = gate(1)
    mixed = _chunk_mix(vhat.astype(BF16), wmix_ref, rows) + _chunk_bias(bias_ref, t)
    gate_c = gate(2)
    a = _dot((u * mixed).astype(BF16), w_out_a_ref[...])

    ext_ref[:, 0:POOL_PAD, :] = st_ref[...]
    ext_ref[:, POOL_PAD:POOL_PAD + rows, :] = p

    def load_shifted(k, lt):
        return ext_ref[:, POOL_PAD - k:POOL_PAD - k + rows, lt * V7X_LANES:(lt + 1) * V7X_LANES]

    pos = start_pos + lax.broadcasted_iota(jnp.int32, (n_seq, rows, V7X_LANES), 1)
    pooled = _window_means_minus_self(load_shifted, p, pos).reshape(t, pool_w)
    ptail_ref[...] = ext_ref[:, rows:rows + POOL_PAD, :]
    b = _pool_project(pooled, wpool_ref, pool_scale_ref[...], w_out_b_ref)

    merged = (gate_a * a + gate_b * b + gate_c * c).astype(BF16)
    x1 = x + _dot(merged, w_o_ref[...])
    h2 = _rmsnorm(x1, g_ffn_ref[...]).astype(BF16)

    def ffn_up(s):
        up = _dot(h2, w_up_ref[:, s * slab:(s + 1) * slab])
        return jnp.square(jnp.maximum(up, 0.0)).astype(BF16)

    x2 = x1
    act = ffn_up(0)
    for s in range(FFN_SPLIT):
        nxt = ffn_up(s + 1) if s + 1 < FFN_SPLIT else None
        x2 = x2 + _dot(act, w_down_ref[s * slab:(s + 1) * slab, :])
        act = nxt
    y_ref[...] = _rmsnorm(x2, g_final_ref[...])


def _layer_weight_operands(lw, tile_rows, chunk_rows):
    (g_mix, w_in, g_v, b_v, w_s, b_s, w_pool, pool_scale,
     w_out_a, w_out_b, w_out_c, w_o, g_ffn, w_up, w_down, g_final) = lw
    reps = tile_rows // chunk_rows
    row2 = lambda vec: vec.reshape(1, -1).astype(F32)
    wmix = jnp.tile(w_s[:, :chunk_rows, :chunk_rows], (1, 1, reps)).astype(F32)
    group_dim = g_v.shape[-1] // w_s.shape[0]
    bias = jnp.repeat(b_s[:, :chunk_rows].T, group_dim, axis=1).astype(F32)
    wpool = jax.scipy.linalg.block_diag(*[w_pool[i] for i in range(w_pool.shape[0])])
    return [row2(g_mix), w_in, row2(g_v), row2(b_v), wmix, bias,
            wpool.astype(BF16), row2(pool_scale), w_out_a, w_out_b,
            w_out_c, w_o, row2(g_ffn), w_up, w_down, row2(g_final)]


def _mem_kv(mem, g_mem, w_kv):
    bsz, n_mem, d = mem.shape
    xw = w_kv.shape[-1] // 2
    return pl.pallas_call(
        _mem_kv_kernel,
        grid=(bsz,),
        in_specs=[pl.BlockSpec((1, n_mem, d), lambda i: (i, 0, 0)),
                  _const_spec((1, d)), _const_spec((d, 2 * xw))],
        out_specs=[pl.BlockSpec((1, xw, n_mem), lambda i: (i, 0, 0)),
                   pl.BlockSpec((1, xw, n_mem), lambda i: (i, 0, 0)),
                   pl.BlockSpec((1, xw, n_mem), lambda i: (i, 0, 0)),
                   pl.BlockSpec((1, n_mem, xw), lambda i: (i, 0, 0))],
        out_shape=[jax.ShapeDtypeStruct((bsz, xw, n_mem), F32),
                   jax.ShapeDtypeStruct((bsz, xw, n_mem), F32),
                   jax.ShapeDtypeStruct((bsz, xw, n_mem), BF16),
                   jax.ShapeDtypeStruct((bsz, n_mem, xw), BF16)],
        compiler_params=pltpu.CompilerParams(dimension_semantics=("arbitrary",)),
        name="mem_kv",
    )(mem, g_mem.reshape(1, d).astype(F32), w_kv.astype(BF16))


def _prompt_layer(x, ktb, vb, lw):
    bsz, seq, d = x.shape
    t = PROMPT_TILE
    ops = _layer_weight_operands(lw, t, CHUNK)
    sgu_w = lw[2].shape[-1]
    pool_w = lw[7].shape[-1]
    n_mem, xw = vb.shape[1:]
    tps = seq // t
    n_tiles = bsz * tps

    def mixer_tile(s):
        m = jnp.minimum(s, n_tiles - 1)
        return m // tps, m % tps

    def ffn_tile(s):
        m = jnp.maximum(s - 1, 0)
        return m // tps, m % tps

    return pl.pallas_call(
        functools.partial(_prompt_kernel, tps, n_tiles),
        grid=(n_tiles + 1,),
        in_specs=[pl.BlockSpec((1, t, d), lambda s: (*mixer_tile(s), 0)),
                  pl.BlockSpec((1, xw, n_mem), lambda s: (mixer_tile(s)[0], 0, 0)),
                  pl.BlockSpec((1, n_mem, xw), lambda s: (mixer_tile(s)[0], 0, 0))]
                 + [_const_spec(o.shape) for o in ops],
        out_specs=[pl.BlockSpec((1, t, d), lambda s: (*ffn_tile(s), 0)),
                   pl.BlockSpec((1, CHUNK, sgu_w), lambda s: (mixer_tile(s)[0], 0, 0)),
                   pl.BlockSpec((1, POOL_PAD, pool_w), lambda s: (mixer_tile(s)[0], 0, 0))],
        out_shape=[jax.ShapeDtypeStruct((bsz, seq, d), F32),
                   jax.ShapeDtypeStruct((bsz, CHUNK, sgu_w), F32),
                   jax.ShapeDtypeStruct((bsz, POOL_PAD, pool_w), F32)],
        scratch_shapes=[pltpu.VMEM((POOL_PAD + t, pool_w), F32),
                        pltpu.VMEM((t, d), F32),
                        pltpu.VMEM((t, d), BF16)],
        compiler_params=pltpu.CompilerParams(
            dimension_semantics=("arbitrary",),
            vmem_limit_bytes=VMEM_LIMIT_BYTES),
        name="prompt_layer",
    )(x, ktb, vb, *ops)


def _sample_attn(x2d, kt, vt, g_mix, wq):
    n_seq, xw, n_mem = kt.shape
    d = x2d.shape[1]
    ts = ATTN_TILE_SEQS
    t = ts * (x2d.shape[0] // n_seq)
    return pl.pallas_call(
        _sample_attn_kernel,
        grid=(n_seq // ts,),
        in_specs=[pl.BlockSpec((t, d), lambda i: (i, 0)),
                  pl.BlockSpec((ts, xw, n_mem), lambda i: (i, 0, 0)),
                  pl.BlockSpec((ts, xw, n_mem), lambda i: (i, 0, 0)),
                  _const_spec((1, d)), _const_spec(wq.shape)],
        out_specs=pl.BlockSpec((t, xw), lambda i: (i, 0)),
        out_shape=jax.ShapeDtypeStruct((x2d.shape[0], xw), F32),
        scratch_shapes=[pltpu.VMEM((t, xw), F32)],
        compiler_params=pltpu.CompilerParams(
            dimension_semantics=("arbitrary",),
            vmem_limit_bytes=VMEM_LIMIT_BYTES),
        name="sample_attn",
    )(x2d, kt, vt, g_mix.reshape(1, d).astype(F32), wq)


def _sample_layer(x2d, attn_out, state, start_pos, lw):
    n_seq = state.shape[0]
    d = x2d.shape[1]
    rows = x2d.shape[0] // n_seq
    ts = SAMPLE_TILE_SEQS
    t = ts * rows
    ops = _layer_weight_operands(lw, t, rows)
    sgu_w = lw[2].shape[-1]
    pool_w = lw[7].shape[-1]
    xw = attn_out.shape[1]
    state_pad = jnp.pad(state, ((0, 0), (POOL_PAD - state.shape[1], 0), (0, 0)))
    y, cv, ptail = pl.pallas_call(
        functools.partial(_sample_kernel, start_pos),
        grid=(n_seq // ts,),
        in_specs=[pl.BlockSpec((t, d), lambda i: (i, 0)),
                  pl.BlockSpec((t, xw), lambda i: (i, 0)),
                  pl.BlockSpec((ts, POOL_PAD, pool_w), lambda i: (i, 0, 0))]
                 + [_const_spec(o.shape) for o in ops],
        out_specs=[pl.BlockSpec((t, d), lambda i: (i, 0)),
                   pl.BlockSpec((t, sgu_w), lambda i: (i, 0)),
                   pl.BlockSpec((ts, POOL_PAD, pool_w), lambda i: (i, 0, 0))],
        out_shape=[jax.ShapeDtypeStruct((n_seq * rows, d), F32),
                   jax.ShapeDtypeStruct((n_seq * rows, sgu_w), F32),
                   jax.ShapeDtypeStruct((n_seq, POOL_PAD, pool_w), F32)],
        scratch_shapes=[pltpu.VMEM((ts, POOL_PAD + rows, pool_w), F32)],
        compiler_params=pltpu.CompilerParams(
            dimension_semantics=("arbitrary",),
            vmem_limit_bytes=VMEM_LIMIT_BYTES),
        name="sample_layer",
    )(x2d, attn_out, state_pad, *ops)
    return (y.reshape(n_seq, rows, d), cv.reshape(n_seq, rows, sgu_w),
            ptail[:, POOL_PAD - POOL_STATE:, :])


def _feature_major(kv):
    bsz, n_mem, heads, head_dim = kv.shape
    return jnp.transpose(kv, (0, 2, 3, 1)).reshape(bsz, heads * head_dim, n_mem)


def _token_major(kvt, heads):
    bsz, xw, n_mem = kvt.shape
    return jnp.transpose(kvt.reshape(bsz, heads, xw // heads, n_mem), (0, 3, 1, 2))


def kernel(x_prompt, x_sample, mem_prompt, cache_mem_k, cache_mem_v, state_pool, g_mix, w_in, g_v, b_v, w_s, b_s, w_pool, pool_scale, g_mem, w_kv, w_out_a, w_out_b, w_out_c, w_o, g_ffn, w_up, w_down, g_final):
    depth = w_in.shape[0]
    assert depth == 1, "the final rmsnorm is fused into the (single) layer kernel"
    heads = cache_mem_k.shape[-2]
    n_seq, rows, d = x_sample.shape
    bf = lambda w: w.astype(BF16)
    lw = (g_mix[0], bf(w_in[0]), g_v[0], b_v[0], w_s[0], b_s[0], w_pool[0], pool_scale[0],
          bf(w_out_a[0]), bf(w_out_b[0]), bf(w_out_c[0]), bf(w_o[0]), g_ffn[0], bf(w_up[0]),
          bf(w_down[0]), g_final)
    mem_kt, mem_vt, ktb, vb = _mem_kv(mem_prompt, g_mem[0], w_kv[0])
    y_prompt, cv_p, ptail_p = _prompt_layer(x_prompt, ktb, vb, lw)
    x2d = x_sample.reshape(n_seq * rows, d)
    xw = heads * cache_mem_k.shape[-1]
    c_q = w_in.shape[-1] - 3 * d - xw
    attn = _sample_attn(x2d, _feature_major(cache_mem_k[0]), _feature_major(cache_mem_v[0]),
                        g_mix[0], lw[1][:, c_q:c_q + xw])
    y_sample, cv_s, ptail_s = _sample_layer(x2d, attn, state_pool[0], PAST_LEN, lw)
    return (y_prompt, y_sample, _token_major(mem_kt, heads)[None], _token_major(mem_vt, heads)[None],
            ptail_p[None, :, POOL_PAD - POOL_STATE:, :], ptail_s[None], cv_p[None], cv_s[None])
```

```python
import functools
from typing import NamedTuple

import jax
import jax.numpy as jnp
from jax import lax
from jax.experimental import pallas as pl
from jax.experimental.pallas import tpu as pltpu

PAST_LEN = 16384
CHUNK = 128
POOL_WINDOWS = (2, 4, 8, 16)
POOL_STATE = max(POOL_WINDOWS) - 1
X_HEADS = 4
EPS = 1e-6

V7X_VMEM_BYTES = 64 * 1024 * 1024
V7X_LANES = 128
VMEM_LIMIT_BYTES = V7X_VMEM_BYTES - 6 * 1024 * 1024

TILE = 256
ATTN_TILE_SEQS = 16
ATTN_GROUP_SEQS = 8
FFN_SPLIT = 4
POOL_PAD = POOL_STATE + 1

BF16 = jnp.bfloat16
F32 = jnp.float32


def _dot(a, b):
    return jnp.dot(a, b, preferred_element_type=F32)


def _log2(n):
    assert n > 0 and n & (n - 1) == 0, n
    return n.bit_length() - 1


def _rmsnorm(x, g):
    return x * lax.rsqrt(jnp.mean(x * x, axis=-1, keepdims=True) + EPS) * g


def _layernorm(x, g, b):
    xc = x - jnp.mean(x, axis=-1, keepdims=True)
    return xc * lax.rsqrt(jnp.mean(xc * xc, axis=-1, keepdims=True) + EPS) * g + b


def _softmax_bf16(s):
    e = jnp.exp(s - jnp.max(s, axis=-1, keepdims=True))
    return (e * (1.0 / jnp.sum(e, axis=-1, keepdims=True))).astype(BF16)


def _const_spec(shape):
    zeros = (0,) * len(shape)
    return pl.BlockSpec(shape, lambda *_: zeros, pipeline_mode=pl.Buffered(1))


def _chunk_mix(vhat_b, wmix_ref, chunk_rows):
    t, sgu_w = vhat_b.shape
    n_groups = wmix_ref.shape[0]
    group_dim = sgu_w // n_groups
    shift = _log2(chunk_rows)
    reps = t // chunk_rows
    row = lax.broadcasted_iota(jnp.int32, (t, t), 0)
    col = lax.broadcasted_iota(jnp.int32, (t, t), 1)
    same_chunk = (row >> shift) == (col >> shift)
    causal = col <= row
    parts = []
    for g in range(n_groups):
        w = jnp.concatenate([wmix_ref[g]] * reps, axis=0)
        w = jnp.where(same_chunk, jnp.where(causal, w, 0.0), 0.0).astype(BF16)
        parts.append(_dot(w, vhat_b[:, g * group_dim:(g + 1) * group_dim]))
    return jnp.concatenate(parts, axis=1)


def _chunk_bias(bias_ref, t):
    return jnp.concatenate([bias_ref[...]] * (t // bias_ref.shape[0]), axis=0)


def _window_means_minus_self(load_shifted, p, pos):
    group_lanes = p.shape[-1] // len(POOL_WINDOWS)
    outs = []
    for lt in range(p.shape[-1] // V7X_LANES):
        p_lt = p[..., lt * V7X_LANES:(lt + 1) * V7X_LANES]
        lane = lax.broadcasted_iota(jnp.int32, p_lt.shape, p_lt.ndim - 1)
        wins = POOL_WINDOWS[lt * V7X_LANES // group_lanes:(lt + 1) * V7X_LANES // group_lanes]
        acc = p_lt
        sums = {1: p_lt}
        for k in range(1, max(wins)):
            acc = acc + load_shifted(k, lt)
            sums[k + 1] = acc
        win_sum = sums[wins[-1]]
        win = jnp.full(p_lt.shape, wins[-1], jnp.int32)
        for gi in range(len(wins) - 2, -1, -1):
            in_group = lane < (gi + 1) * group_lanes
            win_sum = jnp.where(in_group, sums[wins[gi]], win_sum)
            win = jnp.where(in_group, wins[gi], win)
        cnt = jnp.minimum(pos + 1, win).astype(F32)
        outs.append(win_sum / cnt - p_lt)
    return jnp.concatenate(outs, axis=-1)


def _mem_kv_kernel(mem_ref, g_mem_ref, w_kv_ref, kt_ref, vt_ref, ktb_ref, vb_ref):
    xw = kt_ref.shape[1]
    hn = _rmsnorm(mem_ref[0], g_mem_ref[...]).astype(BF16)
    kv = _dot(hn, w_kv_ref[...])
    kt = kv[:, 0:xw].T
    v = kv[:, xw:2 * xw]
    kt_ref[0] = kt
    vt_ref[0] = v.T
    ktb_ref[0] = kt.astype(BF16)
    vb_ref[0] = v.astype(BF16)


def _sample_attn_kernel(x_ref, kt_ref, vt_ref, g_mix_ref, wq_ref, o_ref, q_ref):
    t = x_ref.shape[0]
    n_seq, x_w, _ = kt_ref.shape
    rows = t // n_seq
    head_dim = x_w // X_HEADS
    h = _rmsnorm(x_ref[...], g_mix_ref[...]).astype(BF16)
    q_ref[...] = _dot(h, wq_ref[...]) * (head_dim ** -0.5)
    stack = X_HEADS * rows
    lane_head = lax.broadcasted_iota(jnp.int32, (stack, x_w), 1) >> _log2(head_dim)
    row_head = lax.broadcasted_iota(jnp.int32, (stack, x_w), 0) >> _log2(rows)
    own_head = lane_head == row_head

    def scores(i):
        qs = jnp.concatenate([q_ref[i * rows:(i + 1) * rows, :]] * X_HEADS, axis=0)
        qm = jnp.where(own_head, qs, 0.0).astype(BF16)
        return _dot(qm, kt_ref[i].astype(BF16))

    def attend(i, prob):
        of = lax.dot_general(prob, vt_ref[i].astype(BF16),
                             (((1,), (1,)), ((), ())), preferred_element_type=F32)
        of = jnp.where(own_head, of, 0.0)
        out = of[0:rows]
        for hd in range(1, X_HEADS):
            out = out + of[hd * rows:(hd + 1) * rows]
        o_ref[i * rows:(i + 1) * rows, :] = out

    groups = [range(g, g + ATTN_GROUP_SEQS) for g in range(0, n_seq, ATTN_GROUP_SEQS)]
    pending = []
    for grp in groups + [()]:
        cur = [(i, scores(i)) for i in grp]
        probs = [(i, _softmax_bf16(s)) for i, s in pending]
        for i, prob in probs:
            attend(i, prob)
        pending = cur


class _LayerRefs(NamedTuple):
    xp: object
    xs: object
    kt: object
    vb: object
    attn: object
    state: object
    g_mix: object
    w_in: object
    g_v: object
    b_v: object
    wmix_p: object
    bias_p: object
    wmix_s: object
    bias_s: object
    wpool: object
    pool_scale: object
    w_out_a: object
    w_out_b: object
    w_out_c: object
    w_o: object
    g_ffn: object
    w_up: object
    w_down: object
    g_final: object
    yp: object
    ys: object
    cv_p: object
    ptail_p: object
    cv_s: object
    ptail_s: object
    ext_p: object
    ext_s: object
    x1: object
    h2: object


def _layer_step(r, decode, y_ref, j):
    t, d = r.x1.shape
    sgu_w = r.g_v.shape[-1]
    pool_w = r.pool_scale.shape[-1]
    x_w = r.attn.shape[-1]
    head_dim = x_w // X_HEADS
    c_pool = 2 * sgu_w
    c_q = c_pool + pool_w
    c_gate = c_q + x_w
    slab = r.w_up.shape[1] // FFN_SPLIT

    def ffn_up(s):
        up = _dot(r.h2[...], r.w_up[:, s * slab:(s + 1) * slab])
        return jnp.square(jnp.maximum(up, 0.0)).astype(BF16)

    def ffn_down(act, s):
        return _dot(act, r.w_down[s * slab:(s + 1) * slab, :])

    def gate(h, part):
        lo = c_gate + part * d
        return jax.nn.sigmoid(_dot(h, r.w_in[:, lo:lo + d]))

    act = ffn_up(0)
    x = r.xs[...] if decode else r.xp[0]
    h = _rmsnorm(x, r.g_mix[...]).astype(BF16)
    u = _dot(h, r.w_in[:, 0:sgu_w])
    v = _dot(h, r.w_in[:, sgu_w:2 * sgu_w])
    gate_a = gate(h, 0)
    x2 = r.x1[...] + ffn_down(act, 0)

    u = jax.nn.gelu(u)
    vhat = _layernorm(jax.nn.gelu(v), r.g_v[...], r.b_v[...])
    if decode:
        rows = r.wmix_s.shape[1]
        r.cv_s[...] = vhat
        mixed = _chunk_mix(vhat.astype(BF16), r.wmix_s, rows)
        bias = _chunk_bias(r.bias_s, t)
    else:
        r.cv_p[0] = vhat[t - CHUNK:, :]
        mixed = _chunk_mix(vhat.astype(BF16), r.wmix_p, CHUNK)
        bias = _chunk_bias(r.bias_p, t)
    p = _dot(h, r.w_in[:, c_pool:c_pool + pool_w])
    if not decode:
        q = _dot(h, r.w_in[:, c_q:c_q + x_w]) * (head_dim ** -0.5)
    act = ffn_up(1)
    a_in = (u * (mixed + bias)).astype(BF16)

    if decode:
        n_seq = t // rows
        p = p.reshape(n_seq, rows, pool_w)
        r.ext_s[:, 0:POOL_PAD, :] = r.state[...]
        r.ext_s[:, POOL_PAD:POOL_PAD + rows, :] = p

        def load_shifted(k, lt):
            return r.ext_s[:, POOL_PAD - k:POOL_PAD - k + rows,
                           lt * V7X_LANES:(lt + 1) * V7X_LANES]

        pos = PAST_LEN + lax.broadcasted_iota(jnp.int32, (n_seq, rows, V7X_LANES), 1)
        pooled = _window_means_minus_self(load_shifted, p, pos).reshape(t, pool_w)
        r.ptail_s[...] = r.ext_s[:, rows:rows + POOL_PAD, :]
    else:
        r.ext_p[POOL_PAD:POOL_PAD + t, :] = p

        def load_shifted(k, lt):
            return r.ext_p[POOL_PAD - k:POOL_PAD - k + t, lt * V7X_LANES:(lt + 1) * V7X_LANES]

        pos = j * t + lax.broadcasted_iota(jnp.int32, (t, V7X_LANES), 0)
        pooled = _window_means_minus_self(load_shifted, p, pos)
        tail = r.ext_p[t:t + POOL_PAD, :]
        r.ext_p[0:POOL_PAD, :] = tail
        r.ptail_p[0] = tail
    pooled = pooled.astype(BF16)

    a = _dot(a_in, r.w_out_a[...])

    if not decode:
        head = lax.broadcasted_iota(jnp.int32, (t, x_w), 1) >> _log2(head_dim)
        kt = r.kt[0]
        scores = [_dot(jnp.where(head == hd, q, 0.0).astype(BF16), kt) for hd in range(X_HEADS)]
    pool_mixed = _dot(pooled, r.wpool[...]) * r.pool_scale[...]
    x2 = x2 + ffn_down(act, 1)
    if decode:
        o = r.attn[...]
    else:
        probs = [_softmax_bf16(s) for s in scores]
        vb = r.vb[0]
        o = jnp.zeros((t, x_w), F32)
        for hd in range(X_HEADS):
            o = jnp.where(head == hd, _dot(probs[hd], vb), o)
    b = _dot(pool_mixed.astype(BF16), r.w_out_b[...])
    gate_b = gate(h, 1)
    c = _dot(o.astype(BF16), r.w_out_c[...])
    act = ffn_up(2)
    gate_c = gate(h, 2)
    x2 = x2 + ffn_down(act, 2)
    merged = (gate_a * a + gate_b * b + gate_c * c).astype(BF16)
    act = ffn_up(3)
    x1 = x + _dot(merged, r.w_o[...])
    x2 = x2 + ffn_down(act, 3)
    r.x1[...] = x1
    r.h2[...] = _rmsnorm(x1, r.g_ffn[...]).astype(BF16)
    y = _rmsnorm(x2, r.g_final[...])
    if y_ref.shape[0] == 1:
        y_ref[0] = y
    else:
        y_ref[...] = y


def _layer_kernel(tiles_per_seq, n_prompt, *refs):
    r = _LayerRefs(*refs)
    step = pl.program_id(0)
    j = jnp.minimum(step, n_prompt - 1) % tiles_per_seq

    @pl.when(step == 0)
    def _():
        r.x1[...] = jnp.zeros(r.x1.shape, F32)
        r.h2[...] = jnp.zeros(r.h2.shape, BF16)

    @pl.when(jnp.logical_and(step < n_prompt, j == 0))
    def _():
        r.ext_p[0:POOL_PAD, :] = jnp.zeros((POOL_PAD, r.ext_p.shape[1]), F32)

    @pl.when(step < n_prompt)
    def _():
        _layer_step(r, False, r.yp, j)

    @pl.when(step == n_prompt)
    def _():
        _layer_step(r, True, r.yp, j)

    @pl.when(step > n_prompt)
    def _():
        _layer_step(r, True, r.ys, j)


def _mem_kv(mem, g_mem, w_kv):
    bsz, n_mem, d = mem.shape
    xw = w_kv.shape[-1] // 2
    return pl.pallas_call(
        _mem_kv_kernel,
        grid=(bsz,),
        in_specs=[pl.BlockSpec((1, n_mem, d), lambda i: (i, 0, 0)),
                  _const_spec((1, d)), _const_spec((d, 2 * xw))],
        out_specs=[pl.BlockSpec((1, xw, n_mem), lambda i: (i, 0, 0)),
                   pl.BlockSpec((1, xw, n_mem), lambda i: (i, 0, 0)),
                   pl.BlockSpec((1, xw, n_mem), lambda i: (i, 0, 0)),
                   pl.BlockSpec((1, n_mem, xw), lambda i: (i, 0, 0))],
        out_shape=[jax.ShapeDtypeStruct((bsz, xw, n_mem), F32),
                   jax.ShapeDtypeStruct((bsz, xw, n_mem), F32),
                   jax.ShapeDtypeStruct((bsz, xw, n_mem), BF16),
                   jax.ShapeDtypeStruct((bsz, n_mem, xw), BF16)],
        compiler_params=pltpu.CompilerParams(dimension_semantics=("arbitrary",)),
        name="mem_kv",
    )(mem, g_mem.reshape(1, d).astype(F32), w_kv.astype(BF16))


def _sample_attn(x2d, kt, vt, g_mix, wq):
    n_seq, xw, n_mem = kt.shape
    d = x2d.shape[1]
    ts = ATTN_TILE_SEQS
    t = ts * (x2d.shape[0] // n_seq)
    return pl.pallas_call(
        _sample_attn_kernel,
        grid=(n_seq // ts,),
        in_specs=[pl.BlockSpec((t, d), lambda i: (i, 0)),
                  pl.BlockSpec((ts, xw, n_mem), lambda i: (i, 0, 0)),
                  pl.BlockSpec((ts, xw, n_mem), lambda i: (i, 0, 0)),
                  _const_spec((1, d)), _const_spec(wq.shape)],
        out_specs=pl.BlockSpec((t, xw), lambda i: (i, 0)),
        out_shape=jax.ShapeDtypeStruct((x2d.shape[0], xw), F32),
        scratch_shapes=[pltpu.VMEM((t, xw), F32)],
        compiler_params=pltpu.CompilerParams(
            dimension_semantics=("arbitrary",),
            vmem_limit_bytes=VMEM_LIMIT_BYTES),
        name="sample_attn",
    )(x2d, kt, vt, g_mix.reshape(1, d).astype(F32), wq)


def _spatial_operands(w_s, b_s, group_dim, chunk_rows, tile_rows):
    wmix = jnp.tile(w_s[:, :chunk_rows, :chunk_rows], (1, 1, tile_rows // chunk_rows))
    bias = jnp.repeat(b_s[:, :chunk_rows].T, group_dim, axis=1)
    return wmix.astype(F32), bias.astype(F32)


def _layer(x_prompt, x2d, ktb, vb, attn, state, lw):
    (g_mix, w_in, g_v, b_v, w_s, b_s, w_pool, pool_scale,
     w_out_a, w_out_b, w_out_c, w_o, g_ffn, w_up, w_down, g_final) = lw
    bsz, seq, d = x_prompt.shape
    n_seq = state.shape[0]
    rows = x2d.shape[0] // n_seq
    t = TILE
    ts = t // rows
    tps = seq // t
    n_prompt = bsz * tps
    n_decode = n_seq // ts
    sgu_w = g_v.shape[-1]
    pool_w = pool_scale.shape[-1]
    n_mem, xw = vb.shape[1:]
    group_dim = sgu_w // w_s.shape[0]
    row2 = lambda vec: vec.reshape(1, -1).astype(F32)
    wmix_p, bias_p = _spatial_operands(w_s, b_s, group_dim, CHUNK, t)
    wmix_s, bias_s = _spatial_operands(w_s, b_s, group_dim, rows, t)
    wpool = jax.scipy.linalg.block_diag(*[w_pool[i] for i in range(w_pool.shape[0])])
    state_pad = jnp.pad(state, ((0, 0), (POOL_PAD - state.shape[1], 0), (0, 0)))
    weights = [row2(g_mix), w_in, row2(g_v), row2(b_v), wmix_p, bias_p, wmix_s, bias_s,
               wpool.astype(BF16), row2(pool_scale), w_out_a, w_out_b, w_out_c, w_o,
               row2(g_ffn), w_up, w_down, row2(g_final)]

    def prompt_tile(m):
        m = jnp.clip(m, 0, n_prompt - 1)
        return m // tps, m % tps

    def decode_tile(m):
        return jnp.clip(m - n_prompt, 0, n_decode - 1)

    yp, ys, cv_p, ptail_p, cv_s, ptail_s = pl.pallas_call(
        functools.partial(_layer_kernel, tps, n_prompt),
        grid=(n_prompt + n_decode + 1,),
        in_specs=[pl.BlockSpec((1, t, d), lambda s: (*prompt_tile(s), 0)),
                  pl.BlockSpec((t, d), lambda s: (decode_tile(s), 0)),
                  pl.BlockSpec((1, xw, n_mem), lambda s: (prompt_tile(s)[0], 0, 0)),
                  pl.BlockSpec((1, n_mem, xw), lambda s: (prompt_tile(s)[0], 0, 0)),
                  pl.BlockSpec((t, xw), lambda s: (decode_tile(s), 0)),
                  pl.BlockSpec((ts, POOL_PAD, pool_w), lambda s: (decode_tile(s), 0, 0))]
                 + [_const_spec(w.shape) for w in weights],
        out_specs=[pl.BlockSpec((1, t, d), lambda s: (*prompt_tile(s - 1), 0)),
                   pl.BlockSpec((t, d), lambda s: (decode_tile(s - 1), 0)),
                   pl.BlockSpec((1, CHUNK, sgu_w), lambda s: (prompt_tile(s)[0], 0, 0)),
                   pl.BlockSpec((1, POOL_PAD, pool_w), lambda s: (prompt_tile(s)[0], 0, 0)),
                   pl.BlockSpec((t, sgu_w), lambda s: (decode_tile(s), 0)),
                   pl.BlockSpec((ts, POOL_PAD, pool_w), lambda s: (decode_tile(s), 0, 0))],
        out_shape=[jax.ShapeDtypeStruct((bsz, seq, d), F32),
                   jax.ShapeDtypeStruct((n_seq * rows, d), F32),
                   jax.ShapeDtypeStruct((bsz, CHUNK, sgu_w), F32),
                   jax.ShapeDtypeStruct((bsz, POOL_PAD, pool_w), F32),
                   jax.ShapeDtypeStruct((n_seq * rows, sgu_w), F32),
                   jax.ShapeDtypeStruct((n_seq, POOL_PAD, pool_w), F32)],
        scratch_shapes=[pltpu.VMEM((POOL_PAD + t, pool_w), F32),
                        pltpu.VMEM((ts, POOL_PAD + rows, pool_w), F32),
                        pltpu.VMEM((t, d), F32),
                        pltpu.VMEM((t, d), BF16)],
        compiler_params=pltpu.CompilerParams(
            dimension_semantics=("arbitrary",),
            vmem_limit_bytes=VMEM_LIMIT_BYTES),
        name="layer",
    )(x_prompt, x2d, ktb, vb, attn, state_pad, *weights)
    return (yp, ys.reshape(n_seq, rows, d), cv_p, ptail_p[:, POOL_PAD - POOL_STATE:, :],
            cv_s.reshape(n_seq, rows, sgu_w), ptail_s[:, POOL_PAD - POOL_STATE:, :])


def _feature_major(kv):
    bsz, n_mem, heads, head_dim = kv.shape
    return jnp.transpose(kv, (0, 2, 3, 1)).reshape(bsz, heads * head_dim, n_mem)


def _token_major(kvt, heads):
    bsz, xw, n_mem = kvt.shape
    return jnp.transpose(kvt.reshape(bsz, heads, xw // heads, n_mem), (0, 3, 1, 2))


def kernel(x_prompt, x_sample, mem_prompt, cache_mem_k, cache_mem_v, state_pool, g_mix, w_in, g_v, b_v, w_s, b_s, w_pool, pool_scale, g_mem, w_kv, w_out_a, w_out_b, w_out_c, w_o, g_ffn, w_up, w_down, g_final):
    depth = w_in.shape[0]
    assert depth == 1, "the final rmsnorm is fused into the (single) layer kernel"
    heads = cache_mem_k.shape[-2]
    n_seq, rows, d = x_sample.shape
    bf = lambda w: w.astype(BF16)
    lw = (g_mix[0], bf(w_in[0]), g_v[0], b_v[0], w_s[0], b_s[0], w_pool[0], pool_scale[0],
          bf(w_out_a[0]), bf(w_out_b[0]), bf(w_out_c[0]), bf(w_o[0]), g_ffn[0], bf(w_up[0]),
          bf(w_down[0]), g_final)
    mem_kt, mem_vt, ktb, vb = _mem_kv(mem_prompt, g_mem[0], w_kv[0])
    x2d = x_sample.reshape(n_seq * rows, d)
    xw = heads * cache_mem_k.shape[-1]
    c_q = w_in.shape[-1] - 3 * d - xw
    attn = _sample_attn(x2d, _feature_major(cache_mem_k[0]), _feature_major(cache_mem_v[0]),
                        g_mix[0], lw[1][:, c_q:c_q + xw])
    y_prompt, y_sample, cv_p, ptail_p, cv_s, ptail_s = _layer(
        x_prompt, x2d, ktb, vb, attn, state_pool[0], lw)
    return (y_prompt, y_sample, _token_major(mem_kt, heads)[None], _token_major(mem_vt, heads)[None],
            ptail_p[None], ptail_s[None], cv_p[None], cv_s[None])
```

```python
import functools
from typing import NamedTuple

import jax
import jax.numpy as jnp
from jax import lax
from jax.experimental import pallas as pl
from jax.experimental.pallas import tpu as pltpu

PAST_LEN = 16384
CHUNK = 128
POOL_WINDOWS = (2, 4, 8, 16)
POOL_STATE = max(POOL_WINDOWS) - 1
X_HEADS = 4
EPS = 1e-6

V7X_VMEM_BYTES = 64 * 1024 * 1024
V7X_LANES = 128
VMEM_LIMIT_BYTES = V7X_VMEM_BYTES - 6 * 1024 * 1024

TILE = 256
ATTN_TILE_SEQS = 16
ATTN_GROUP_SEQS = 8
FFN_SPLIT = 4
WEIGHT_CHUNK_ROWS = 64
POOL_PAD = POOL_STATE + 1

BF16 = jnp.bfloat16
F32 = jnp.float32


def _dot(a, b):
    return jnp.dot(a, b, preferred_element_type=F32)


def _log2(n):
    assert n > 0 and n & (n - 1) == 0, n
    return n.bit_length() - 1


def _rmsnorm(x, g):
    return x * lax.rsqrt(jnp.mean(x * x, axis=-1, keepdims=True) + EPS) * g


def _layernorm(x, g, b):
    xc = x - jnp.mean(x, axis=-1, keepdims=True)
    return xc * lax.rsqrt(jnp.mean(xc * xc, axis=-1, keepdims=True) + EPS) * g + b


def _softmax_bf16(s):
    e = jnp.exp(s - jnp.max(s, axis=-1, keepdims=True))
    return (e * (1.0 / jnp.sum(e, axis=-1, keepdims=True))).astype(BF16)


def _const_spec(shape):
    zeros = (0,) * len(shape)
    return pl.BlockSpec(shape, lambda *_: zeros, pipeline_mode=pl.Buffered(1))


def _chunk_mix(vhat_b, wmix_ref, chunk_rows):
    t, sgu_w = vhat_b.shape
    n_groups = wmix_ref.shape[0]
    group_dim = sgu_w // n_groups
    shift = _log2(chunk_rows)
    reps = t // chunk_rows
    row = lax.broadcasted_iota(jnp.int32, (t, t), 0)
    col = lax.broadcasted_iota(jnp.int32, (t, t), 1)
    same_chunk = (row >> shift) == (col >> shift)
    causal = col <= row
    parts = []
    for g in range(n_groups):
        w = jnp.concatenate([wmix_ref[g]] * reps, axis=0)
        w = jnp.where(same_chunk, jnp.where(causal, w, 0.0), 0.0).astype(BF16)
        parts.append(_dot(w, vhat_b[:, g * group_dim:(g + 1) * group_dim]))
    return jnp.concatenate(parts, axis=1)


def _chunk_bias(bias_ref, t):
    return jnp.concatenate([bias_ref[...]] * (t // bias_ref.shape[0]), axis=0)


def _window_means_minus_self(load_shifted, p, pos):
    group_lanes = p.shape[-1] // len(POOL_WINDOWS)
    outs = []
    for lt in range(p.shape[-1] // V7X_LANES):
        p_lt = p[..., lt * V7X_LANES:(lt + 1) * V7X_LANES]
        lane = lax.broadcasted_iota(jnp.int32, p_lt.shape, p_lt.ndim - 1)
        wins = POOL_WINDOWS[lt * V7X_LANES // group_lanes:(lt + 1) * V7X_LANES // group_lanes]
        acc = p_lt
        sums = {1: p_lt}
        for k in range(1, max(wins)):
            acc = acc + load_shifted(k, lt)
            sums[k + 1] = acc
        win_sum = sums[wins[-1]]
        win = jnp.full(p_lt.shape, wins[-1], jnp.int32)
        for gi in range(len(wins) - 2, -1, -1):
            in_group = lane < (gi + 1) * group_lanes
            win_sum = jnp.where(in_group, sums[wins[gi]], win_sum)
            win = jnp.where(in_group, wins[gi], win)
        cnt = jnp.minimum(pos + 1, win).astype(F32)
        outs.append(win_sum / cnt - p_lt)
    return jnp.concatenate(outs, axis=-1)


def _mem_kv_kernel(mem_ref, g_mem_ref, w_kv_ref, kt_ref, vt_ref, ktb_ref, vb_ref):
    xw = kt_ref.shape[1]
    hn = _rmsnorm(mem_ref[0], g_mem_ref[...]).astype(BF16)
    kv = _dot(hn, w_kv_ref[...])
    kt = kv[:, 0:xw].T
    v = kv[:, xw:2 * xw]
    kt_ref[0] = kt
    vt_ref[0] = v.T
    ktb_ref[0] = kt.astype(BF16)
    vb_ref[0] = v.astype(BF16)


def _sample_attn_kernel(x_ref, kt_ref, vt_ref, g_mix_ref, wq_ref, o_ref, q_ref):
    t = x_ref.shape[0]
    n_seq, x_w, _ = kt_ref.shape
    rows = t // n_seq
    head_dim = x_w // X_HEADS
    h = _rmsnorm(x_ref[...], g_mix_ref[...]).astype(BF16)
    q_ref[...] = _dot(h, wq_ref[...]) * (head_dim ** -0.5)
    stack = X_HEADS * rows
    lane_head = lax.broadcasted_iota(jnp.int32, (stack, x_w), 1) >> _log2(head_dim)
    row_head = lax.broadcasted_iota(jnp.int32, (stack, x_w), 0) >> _log2(rows)
    own_head = lane_head == row_head

    def scores(i):
        qs = jnp.concatenate([q_ref[i * rows:(i + 1) * rows, :]] * X_HEADS, axis=0)
        qm = jnp.where(own_head, qs, 0.0).astype(BF16)
        return _dot(qm, kt_ref[i].astype(BF16))

    def attend(i, prob):
        of = lax.dot_general(prob, vt_ref[i].astype(BF16),
                             (((1,), (1,)), ((), ())), preferred_element_type=F32)
        of = jnp.where(own_head, of, 0.0)
        out = of[0:rows]
        for hd in range(1, X_HEADS):
            out = out + of[hd * rows:(hd + 1) * rows]
        o_ref[i * rows:(i + 1) * rows, :] = out

    groups = [range(g, g + ATTN_GROUP_SEQS) for g in range(0, n_seq, ATTN_GROUP_SEQS)]
    pending = []
    for grp in groups + [()]:
        cur = [(i, scores(i)) for i in grp]
        probs = [(i, _softmax_bf16(s)) for i, s in pending]
        for i, prob in probs:
            attend(i, prob)
        pending = cur


class _LayerRefs(NamedTuple):
    xp: object
    xs: object
    kt: object
    vb: object
    attn: object
    state: object
    g_mix: object
    w_in_f32: object
    g_v: object
    b_v: object
    wmix_p: object
    bias_p: object
    wmix_s: object
    bias_s: object
    wpool: object
    pool_scale: object
    w_out_a_f32: object
    w_out_b_f32: object
    w_out_c_f32: object
    w_o_f32: object
    g_ffn: object
    w_up_f32: object
    w_down_f32: object
    g_final: object
    yp: object
    ys: object
    cv_p: object
    ptail_p: object
    cv_s: object
    ptail_s: object
    ext_p: object
    ext_s: object
    x1: object
    h2: object
    w_in: object
    w_out_a: object
    w_out_b: object
    w_out_c: object
    w_o: object
    w_up: object
    w_down: object
    stage: object
    load_sem: object


def _layer_step(r, decode, y_ref, j):
    t, d = r.x1.shape
    sgu_w = r.g_v.shape[-1]
    pool_w = r.pool_scale.shape[-1]
    x_w = r.attn.shape[-1]
    head_dim = x_w // X_HEADS
    c_pool = 2 * sgu_w
    c_q = c_pool + pool_w
    c_gate = c_q + x_w
    slab = r.w_up.shape[1] // FFN_SPLIT

    def ffn_up(s):
        up = _dot(r.h2[...], r.w_up[:, s * slab:(s + 1) * slab])
        return jnp.square(jnp.maximum(up, 0.0)).astype(BF16)

    def ffn_down(act, s):
        return _dot(act, r.w_down[s * slab:(s + 1) * slab, :])

    def gate(h, part):
        lo = c_gate + part * d
        return jax.nn.sigmoid(_dot(h, r.w_in[:, lo:lo + d]))

    act = ffn_up(0)
    x = r.xs[...] if decode else r.xp[0]
    h = _rmsnorm(x, r.g_mix[...]).astype(BF16)
    u = _dot(h, r.w_in[:, 0:sgu_w])
    v = _dot(h, r.w_in[:, sgu_w:2 * sgu_w])
    gate_a = gate(h, 0)
    x2 = r.x1[...] + ffn_down(act, 0)

    u = jax.nn.gelu(u)
    vhat = _layernorm(jax.nn.gelu(v), r.g_v[...], r.b_v[...])
    if decode:
        rows = r.wmix_s.shape[1]
        r.cv_s[...] = vhat
        mixed = _chunk_mix(vhat.astype(BF16), r.wmix_s, rows)
        bias = _chunk_bias(r.bias_s, t)
    else:
        r.cv_p[0] = vhat[t - CHUNK:, :]
        mixed = _chunk_mix(vhat.astype(BF16), r.wmix_p, CHUNK)
        bias = _chunk_bias(r.bias_p, t)
    p = _dot(h, r.w_in[:, c_pool:c_pool + pool_w])
    if not decode:
        q = _dot(h, r.w_in[:, c_q:c_q + x_w]) * (head_dim ** -0.5)
    act = ffn_up(1)
    a_in = (u * (mixed + bias)).astype(BF16)

    if decode:
        n_seq = t // rows
        p = p.reshape(n_seq, rows, pool_w)
        r.ext_s[:, 0:POOL_PAD, :] = r.state[...]
        r.ext_s[:, POOL_PAD:POOL_PAD + rows, :] = p

        def load_shifted(k, lt):
            return r.ext_s[:, POOL_PAD - k:POOL_PAD - k + rows,
                           lt * V7X_LANES:(lt + 1) * V7X_LANES]

        pos = PAST_LEN + lax.broadcasted_iota(jnp.int32, (n_seq, rows, V7X_LANES), 1)
        pooled = _window_means_minus_self(load_shifted, p, pos).reshape(t, pool_w)
        r.ptail_s[...] = r.ext_s[:, rows:rows + POOL_PAD, :]
    else:
        r.ext_p[POOL_PAD:POOL_PAD + t, :] = p

        def load_shifted(k, lt):
            return r.ext_p[POOL_PAD - k:POOL_PAD - k + t, lt * V7X_LANES:(lt + 1) * V7X_LANES]

        pos = j * t + lax.broadcasted_iota(jnp.int32, (t, V7X_LANES), 0)
        pooled = _window_means_minus_self(load_shifted, p, pos)
        tail = r.ext_p[t:t + POOL_PAD, :]
        r.ext_p[0:POOL_PAD, :] = tail
        r.ptail_p[0] = tail
    pooled = pooled.astype(BF16)

    a = _dot(a_in, r.w_out_a[...])

    if not decode:
        head = lax.broadcasted_iota(jnp.int32, (t, x_w), 1) >> _log2(head_dim)
        kt = r.kt[0]
        scores = [_dot(jnp.where(head == hd, q, 0.0).astype(BF16), kt) for hd in range(X_HEADS)]
    pool_mixed = _dot(pooled, r.wpool[...]) * r.pool_scale[...]
    x2 = x2 + ffn_down(act, 1)
    if decode:
        o = r.attn[...]
    else:
        probs = [_softmax_bf16(s) for s in scores]
        vb = r.vb[0]
        o = jnp.zeros((t, x_w), F32)
        for hd in range(X_HEADS):
            o = jnp.where(head == hd, _dot(probs[hd], vb), o)
    b = _dot(pool_mixed.astype(BF16), r.w_out_b[...])
    gate_b = gate(h, 1)
    c = _dot(o.astype(BF16), r.w_out_c[...])
    act = ffn_up(2)
    gate_c = gate(h, 2)
    x2 = x2 + ffn_down(act, 2)
    merged = (gate_a * a + gate_b * b + gate_c * c).astype(BF16)
    act = ffn_up(3)
    x1 = x + _dot(merged, r.w_o[...])
    x2 = x2 + ffn_down(act, 3)
    r.x1[...] = x1
    r.h2[...] = _rmsnorm(x1, r.g_ffn[...]).astype(BF16)
    y = _rmsnorm(x2, r.g_final[...])
    if y_ref.shape[0] == 1:
        y_ref[0] = y
    else:
        y_ref[...] = y


def _load_weights_bf16(r):
    ch = r.stage.shape[1]
    pairs = [(r.w_in_f32, r.w_in), (r.w_up_f32, r.w_up), (r.w_down_f32, r.w_down),
             (r.w_o_f32, r.w_o), (r.w_out_a_f32, r.w_out_a), (r.w_out_b_f32, r.w_out_b),
             (r.w_out_c_f32, r.w_out_c)]

    def chunk_copy(src, i, slot):
        return pltpu.make_async_copy(src.at[pl.ds(i * ch, ch), :],
                                     r.stage.at[slot, :, 0:src.shape[1]], r.load_sem.at[slot])

    chunk_copy(pairs[0][0], 0, 0).start()
    for w, (src, dst) in enumerate(pairs):
        n_chunks = src.shape[0] // ch
        assert src.shape[0] % ch == 0 and n_chunks % 2 == 0, src.shape
        nxt = pairs[w + 1][0] if w + 1 < len(pairs) else None

        def body(i, carry, src=src, dst=dst, n_chunks=n_chunks, nxt=nxt):
            slot = lax.rem(i, 2)

            @pl.when(i + 1 < n_chunks)
            def _():
                chunk_copy(src, i + 1, 1 - slot).start()

            if nxt is not None:
                @pl.when(i + 1 == n_chunks)
                def _():
                    chunk_copy(nxt, 0, 0).start()

            chunk_copy(src, i, slot).wait()
            row0 = pl.multiple_of(i * ch, ch)
            dst[pl.ds(row0, ch), :] = r.stage[slot, :, 0:src.shape[1]].astype(BF16)
            return carry

        lax.fori_loop(0, n_chunks, body, 0)


def _layer_kernel(tiles_per_seq, n_prompt, *refs):
    r = _LayerRefs(*refs)
    step = pl.program_id(0)
    j = jnp.minimum(step, n_prompt - 1) % tiles_per_seq

    @pl.when(step == 0)
    def _():
        r.x1[...] = jnp.zeros(r.x1.shape, F32)
        r.h2[...] = jnp.zeros(r.h2.shape, BF16)
        _load_weights_bf16(r)

    @pl.when(jnp.logical_and(step < n_prompt, j == 0))
    def _():
        r.ext_p[0:POOL_PAD, :] = jnp.zeros((POOL_PAD, r.ext_p.shape[1]), F32)

    @pl.when(step < n_prompt)
    def _():
        _layer_step(r, False, r.yp, j)

    @pl.when(step == n_prompt)
    def _():
        _layer_step(r, True, r.yp, j)

    @pl.when(step > n_prompt)
    def _():
        _layer_step(r, True, r.ys, j)


def _mem_kv(mem, g_mem, w_kv):
    bsz, n_mem, d = mem.shape
    xw = w_kv.shape[-1] // 2
    return pl.pallas_call(
        _mem_kv_kernel,
        grid=(bsz,),
        in_specs=[pl.BlockSpec((1, n_mem, d), lambda i: (i, 0, 0)),
                  _const_spec((1, d)), _const_spec((d, 2 * xw))],
        out_specs=[pl.BlockSpec((1, xw, n_mem), lambda i: (i, 0, 0)),
                   pl.BlockSpec((1, xw, n_mem), lambda i: (i, 0, 0)),
                   pl.BlockSpec((1, xw, n_mem), lambda i: (i, 0, 0)),
                   pl.BlockSpec((1, n_mem, xw), lambda i: (i, 0, 0))],
        out_shape=[jax.ShapeDtypeStruct((bsz, xw, n_mem), F32),
                   jax.ShapeDtypeStruct((bsz, xw, n_mem), F32),
                   jax.ShapeDtypeStruct((bsz, xw, n_mem), BF16),
                   jax.ShapeDtypeStruct((bsz, n_mem, xw), BF16)],
        compiler_params=pltpu.CompilerParams(dimension_semantics=("arbitrary",)),
        name="mem_kv",
    )(mem, g_mem.reshape(1, d).astype(F32), w_kv.astype(BF16))


def _sample_attn(x2d, kt, vt, g_mix, wq):
    n_seq, xw, n_mem = kt.shape
    d = x2d.shape[1]
    ts = ATTN_TILE_SEQS
    t = ts * (x2d.shape[0] // n_seq)
    return pl.pallas_call(
        _sample_attn_kernel,
        grid=(n_seq // ts,),
        in_specs=[pl.BlockSpec((t, d), lambda i: (i, 0)),
                  pl.BlockSpec((ts, xw, n_mem), lambda i: (i, 0, 0)),
                  pl.BlockSpec((ts, xw, n_mem), lambda i: (i, 0, 0)),
                  _const_spec((1, d)), _const_spec(wq.shape)],
        out_specs=pl.BlockSpec((t, xw), lambda i: (i, 0)),
        out_shape=jax.ShapeDtypeStruct((x2d.shape[0], xw), F32),
        scratch_shapes=[pltpu.VMEM((t, xw), F32)],
        compiler_params=pltpu.CompilerParams(
            dimension_semantics=("arbitrary",),
            vmem_limit_bytes=VMEM_LIMIT_BYTES),
        name="sample_attn",
    )(x2d, kt, vt, g_mix.reshape(1, d).astype(F32), wq)


def _spatial_operands(w_s, b_s, group_dim, chunk_rows, tile_rows):
    wmix = jnp.tile(w_s[:, :chunk_rows, :chunk_rows], (1, 1, tile_rows // chunk_rows))
    bias = jnp.repeat(b_s[:, :chunk_rows].T, group_dim, axis=1)
    return wmix.astype(F32), bias.astype(F32)


def _layer(x_prompt, x2d, ktb, vb, attn, state, lw):
    (g_mix, w_in, g_v, b_v, w_s, b_s, w_pool, pool_scale,
     w_out_a, w_out_b, w_out_c, w_o, g_ffn, w_up, w_down, g_final) = lw
    bsz, seq, d = x_prompt.shape
    n_seq = state.shape[0]
    rows = x2d.shape[0] // n_seq
    t = TILE
    ts = t // rows
    tps = seq // t
    n_prompt = bsz * tps
    n_decode = n_seq // ts
    sgu_w = g_v.shape[-1]
    pool_w = pool_scale.shape[-1]
    n_mem, xw = vb.shape[1:]
    group_dim = sgu_w // w_s.shape[0]
    row2 = lambda vec: vec.reshape(1, -1).astype(F32)
    wmix_p, bias_p = _spatial_operands(w_s, b_s, group_dim, CHUNK, t)
    wmix_s, bias_s = _spatial_operands(w_s, b_s, group_dim, rows, t)
    wpool = jax.scipy.linalg.block_diag(*[w_pool[i] for i in range(w_pool.shape[0])])
    state_pad = jnp.pad(state, ((0, 0), (POOL_PAD - state.shape[1], 0), (0, 0)))
    big = [w_in, w_out_a, w_out_b, w_out_c, w_o, w_up, w_down]
    weights = [row2(g_mix), w_in, row2(g_v), row2(b_v), wmix_p, bias_p, wmix_s, bias_s,
               wpool.astype(BF16), row2(pool_scale), w_out_a, w_out_b, w_out_c, w_o,
               row2(g_ffn), w_up, w_down, row2(g_final)]

    def prompt_tile(m):
        m = jnp.clip(m, 0, n_prompt - 1)
        return m // tps, m % tps

    def decode_tile(m):
        return jnp.clip(m - n_prompt, 0, n_decode - 1)

    yp, ys, cv_p, ptail_p, cv_s, ptail_s = pl.pallas_call(
        functools.partial(_layer_kernel, tps, n_prompt),
        grid=(n_prompt + n_decode + 1,),
        in_specs=[pl.BlockSpec((1, t, d), lambda s: (*prompt_tile(s), 0)),
                  pl.BlockSpec((t, d), lambda s: (decode_tile(s), 0)),
                  pl.BlockSpec((1, xw, n_mem), lambda s: (prompt_tile(s)[0], 0, 0)),
                  pl.BlockSpec((1, n_mem, xw), lambda s: (prompt_tile(s)[0], 0, 0)),
                  pl.BlockSpec((t, xw), lambda s: (decode_tile(s), 0)),
                  pl.BlockSpec((ts, POOL_PAD, pool_w), lambda s: (decode_tile(s), 0, 0))]
                 + [pl.BlockSpec(memory_space=pl.ANY) if any(w is b for b in big)
                    else _const_spec(w.shape) for w in weights],
        out_specs=[pl.BlockSpec((1, t, d), lambda s: (*prompt_tile(s - 1), 0)),
                   pl.BlockSpec((t, d), lambda s: (decode_tile(s - 1), 0)),
                   pl.BlockSpec((1, CHUNK, sgu_w), lambda s: (prompt_tile(s)[0], 0, 0)),
                   pl.BlockSpec((1, POOL_PAD, pool_w), lambda s: (prompt_tile(s)[0], 0, 0)),
                   pl.BlockSpec((t, sgu_w), lambda s: (decode_tile(s), 0)),
                   pl.BlockSpec((ts, POOL_PAD, pool_w), lambda s: (decode_tile(s), 0, 0))],
        out_shape=[jax.ShapeDtypeStruct((bsz, seq, d), F32),
                   jax.ShapeDtypeStruct((n_seq * rows, d), F32),
                   jax.ShapeDtypeStruct((bsz, CHUNK, sgu_w), F32),
                   jax.ShapeDtypeStruct((bsz, POOL_PAD, pool_w), F32),
                   jax.ShapeDtypeStruct((n_seq * rows, sgu_w), F32),
                   jax.ShapeDtypeStruct((n_seq, POOL_PAD, pool_w), F32)],
        scratch_shapes=[pltpu.VMEM((POOL_PAD + t, pool_w), F32),
                        pltpu.VMEM((ts, POOL_PAD + rows, pool_w), F32),
                        pltpu.VMEM((t, d), F32),
                        pltpu.VMEM((t, d), BF16)]
                       + [pltpu.VMEM(w.shape, BF16) for w in big]
                       + [pltpu.VMEM((2, WEIGHT_CHUNK_ROWS, max(w.shape[1] for w in big)), F32),
                          pltpu.SemaphoreType.DMA((2,))],
        compiler_params=pltpu.CompilerParams(
            dimension_semantics=("arbitrary",),
            vmem_limit_bytes=VMEM_LIMIT_BYTES),
        name="layer",
    )(x_prompt, x2d, ktb, vb, attn, state_pad, *weights)
    return (yp, ys.reshape(n_seq, rows, d), cv_p, ptail_p[:, POOL_PAD - POOL_STATE:, :],
            cv_s.reshape(n_seq, rows, sgu_w), ptail_s[:, POOL_PAD - POOL_STATE:, :])


def _feature_major(kv):
    bsz, n_mem, heads, head_dim = kv.shape
    return jnp.transpose(kv, (0, 2, 3, 1)).reshape(bsz, heads * head_dim, n_mem)


def _token_major(kvt, heads):
    bsz, xw, n_mem = kvt.shape
    return jnp.transpose(kvt.reshape(bsz, heads, xw // heads, n_mem), (0, 3, 1, 2))


def kernel(x_prompt, x_sample, mem_prompt, cache_mem_k, cache_mem_v, state_pool, g_mix, w_in, g_v, b_v, w_s, b_s, w_pool, pool_scale, g_mem, w_kv, w_out_a, w_out_b, w_out_c, w_o, g_ffn, w_up, w_down, g_final):
    depth = w_in.shape[0]
    assert depth == 1, "the final rmsnorm is fused into the (single) layer kernel"
    heads = cache_mem_k.shape[-2]
    n_seq, rows, d = x_sample.shape
    lw = (g_mix[0], w_in[0], g_v[0], b_v[0], w_s[0], b_s[0], w_pool[0], pool_scale[0],
          w_out_a[0], w_out_b[0], w_out_c[0], w_o[0], g_ffn[0], w_up[0], w_down[0], g_final)
    mem_kt, mem_vt, ktb, vb = _mem_kv(mem_prompt, g_mem[0], w_kv[0])
    x2d = x_sample.reshape(n_seq * rows, d)
    xw = heads * cache_mem_k.shape[-1]
    c_q = w_in.shape[-1] - 3 * d - xw
    attn = _sample_attn(x2d, _feature_major(cache_mem_k[0]), _feature_major(cache_mem_v[0]),
                        g_mix[0], w_in[0][:, c_q:c_q + xw].astype(BF16))
    y_prompt, y_sample, cv_p, ptail_p, cv_s, ptail_s = _layer(
        x_prompt, x2d, ktb, vb, attn, state_pool[0], lw)
    return (y_prompt, y_sample, _token_major(mem_kt, heads)[None], _token_major(mem_vt, heads)[None],
            ptail_p[None], ptail_s[None], cv_p[None], cv_s[None])
```

```python
import functools
from typing import NamedTuple

import jax
import jax.numpy as jnp
from jax import lax
from jax.experimental import pallas as pl
from jax.experimental.pallas import tpu as pltpu

PAST_LEN = 16384
CHUNK = 128
POOL_WINDOWS = (2, 4, 8, 16)
POOL_STATE = max(POOL_WINDOWS) - 1
X_HEADS = 4
EPS = 1e-6

V7X_VMEM_BYTES = 64 * 1024 * 1024
V7X_LANES = 128
VMEM_LIMIT_BYTES = V7X_VMEM_BYTES - 6 * 1024 * 1024

TILE = 256
ATTN_TILE_SEQS = 16
ATTN_GROUP_SEQS = 8
FFN_SPLIT = 4
LOAD_SLOTS = 8
LOAD_WIDE_ROWS = 32
LOAD_NARROW_ROWS = 128
POOL_PAD = POOL_STATE + 1

BF16 = jnp.bfloat16
F32 = jnp.float32


def _dot(a, b):
    return jnp.dot(a, b, preferred_element_type=F32)


def _log2(n):
    assert n > 0 and n & (n - 1) == 0, n
    return n.bit_length() - 1


def _rmsnorm(x, g):
    return x * lax.rsqrt(jnp.mean(x * x, axis=-1, keepdims=True) + EPS) * g


def _layernorm(x, g, b):
    xc = x - jnp.mean(x, axis=-1, keepdims=True)
    return xc * lax.rsqrt(jnp.mean(xc * xc, axis=-1, keepdims=True) + EPS) * g + b


def _softmax_bf16(s):
    e = jnp.exp(s - jnp.max(s, axis=-1, keepdims=True))
    return (e * (1.0 / jnp.sum(e, axis=-1, keepdims=True))).astype(BF16)


def _const_spec(shape):
    zeros = (0,) * len(shape)
    return pl.BlockSpec(shape, lambda *_: zeros, pipeline_mode=pl.Buffered(1))


def _chunk_mix(vhat_b, wmix_ref, chunk_rows):
    t, sgu_w = vhat_b.shape
    n_groups = wmix_ref.shape[0]
    group_dim = sgu_w // n_groups
    shift = _log2(chunk_rows)
    reps = t // chunk_rows
    row = lax.broadcasted_iota(jnp.int32, (t, t), 0)
    col = lax.broadcasted_iota(jnp.int32, (t, t), 1)
    same_chunk = (row >> shift) == (col >> shift)
    causal = col <= row
    parts = []
    for g in range(n_groups):
        w = jnp.concatenate([wmix_ref[g]] * reps, axis=0)
        w = jnp.where(same_chunk, jnp.where(causal, w, 0.0), 0.0).astype(BF16)
        parts.append(_dot(w, vhat_b[:, g * group_dim:(g + 1) * group_dim]))
    return jnp.concatenate(parts, axis=1)


def _chunk_bias(bias_ref, t):
    return jnp.concatenate([bias_ref[...]] * (t // bias_ref.shape[0]), axis=0)


def _window_means_minus_self(load_shifted, p, pos):
    group_lanes = p.shape[-1] // len(POOL_WINDOWS)
    outs = []
    for lt in range(p.shape[-1] // V7X_LANES):
        p_lt = p[..., lt * V7X_LANES:(lt + 1) * V7X_LANES]
        lane = lax.broadcasted_iota(jnp.int32, p_lt.shape, p_lt.ndim - 1)
        wins = POOL_WINDOWS[lt * V7X_LANES // group_lanes:(lt + 1) * V7X_LANES // group_lanes]
        acc = p_lt
        sums = {1: p_lt}
        for k in range(1, max(wins)):
            acc = acc + load_shifted(k, lt)
            sums[k + 1] = acc
        win_sum = sums[wins[-1]]
        win = jnp.full(p_lt.shape, wins[-1], jnp.int32)
        for gi in range(len(wins) - 2, -1, -1):
            in_group = lane < (gi + 1) * group_lanes
            win_sum = jnp.where(in_group, sums[wins[gi]], win_sum)
            win = jnp.where(in_group, wins[gi], win)
        cnt = jnp.minimum(pos + 1, win).astype(F32)
        outs.append(win_sum / cnt - p_lt)
    return jnp.concatenate(outs, axis=-1)


def _mem_kv_kernel(mem_ref, g_mem_ref, w_kv_ref, kt_ref, vt_ref, ktb_ref, vb_ref):
    xw = kt_ref.shape[1]
    hn = _rmsnorm(mem_ref[0], g_mem_ref[...]).astype(BF16)
    kv = _dot(hn, w_kv_ref[...])
    kt = kv[:, 0:xw].T
    v = kv[:, xw:2 * xw]
    kt_ref[0] = kt
    vt_ref[0] = v.T
    ktb_ref[0] = kt.astype(BF16)
    vb_ref[0] = v.astype(BF16)


def _sample_attn_kernel(x_ref, kt_ref, vt_ref, g_mix_ref, wq_ref, o_ref, q_ref):
    t = x_ref.shape[0]
    n_seq, x_w, _ = kt_ref.shape
    rows = t // n_seq
    head_dim = x_w // X_HEADS
    h = _rmsnorm(x_ref[...], g_mix_ref[...]).astype(BF16)
    q_ref[...] = _dot(h, wq_ref[...]) * (head_dim ** -0.5)
    stack = X_HEADS * rows
    lane_head = lax.broadcasted_iota(jnp.int32, (stack, x_w), 1) >> _log2(head_dim)
    row_head = lax.broadcasted_iota(jnp.int32, (stack, x_w), 0) >> _log2(rows)
    own_head = lane_head == row_head

    def scores(i):
        qs = jnp.concatenate([q_ref[i * rows:(i + 1) * rows, :]] * X_HEADS, axis=0)
        qm = jnp.where(own_head, qs, 0.0).astype(BF16)
        return _dot(qm, kt_ref[i].astype(BF16))

    def attend(i, prob):
        of = lax.dot_general(prob, vt_ref[i].astype(BF16),
                             (((1,), (1,)), ((), ())), preferred_element_type=F32)
        of = jnp.where(own_head, of, 0.0)
        out = of[0:rows]
        for hd in range(1, X_HEADS):
            out = out + of[hd * rows:(hd + 1) * rows]
        o_ref[i * rows:(i + 1) * rows, :] = out

    groups = [range(g, g + ATTN_GROUP_SEQS) for g in range(0, n_seq, ATTN_GROUP_SEQS)]
    pending = []
    for grp in groups + [()]:
        cur = [(i, scores(i)) for i in grp]
        probs = [(i, _softmax_bf16(s)) for i, s in pending]
        for i, prob in probs:
            attend(i, prob)
        pending = cur


class _LayerRefs(NamedTuple):
    xp: object
    xs: object
    kt: object
    vb: object
    attn: object
    state: object
    g_mix: object
    w_in_f32: object
    g_v: object
    b_v: object
    wmix_p: object
    bias_p: object
    wmix_s: object
    bias_s: object
    wpool: object
    pool_scale: object
    w_out_a_f32: object
    w_out_b_f32: object
    w_out_c_f32: object
    w_o_f32: object
    g_ffn: object
    w_up_f32: object
    w_down_f32: object
    g_final: object
    yp: object
    ys: object
    cv_p: object
    ptail_p: object
    cv_s: object
    ptail_s: object
    ext_p: object
    ext_s: object
    x1: object
    h2: object
    w_in: object
    w_out_a: object
    w_out_b: object
    w_out_c: object
    w_o: object
    w_up: object
    w_down: object


def _layer_step(r, decode, y_ref, j):
    t, d = r.x1.shape
    sgu_w = r.g_v.shape[-1]
    pool_w = r.pool_scale.shape[-1]
    x_w = r.attn.shape[-1]
    head_dim = x_w // X_HEADS
    c_pool = 2 * sgu_w
    c_q = c_pool + pool_w
    c_gate = c_q + x_w
    slab = r.w_up.shape[1] // FFN_SPLIT

    def ffn_up(s):
        up = _dot(r.h2[...], r.w_up[:, s * slab:(s + 1) * slab])
        return jnp.square(jnp.maximum(up, 0.0)).astype(BF16)

    def ffn_down(act, s):
        return _dot(act, r.w_down[s * slab:(s + 1) * slab, :])

    def gate(h, part):
        lo = c_gate + part * d
        return jax.nn.sigmoid(_dot(h, r.w_in[:, lo:lo + d]))

    act = ffn_up(0)
    x = r.xs[...] if decode else r.xp[0]
    h = _rmsnorm(x, r.g_mix[...]).astype(BF16)
    u = _dot(h, r.w_in[:, 0:sgu_w])
    v = _dot(h, r.w_in[:, sgu_w:2 * sgu_w])
    gate_a = gate(h, 0)
    x2 = r.x1[...] + ffn_down(act, 0)

    u = jax.nn.gelu(u)
    vhat = _layernorm(jax.nn.gelu(v), r.g_v[...], r.b_v[...])
    if decode:
        rows = r.wmix_s.shape[1]
        r.cv_s[...] = vhat
        mixed = _chunk_mix(vhat.astype(BF16), r.wmix_s, rows)
        bias = _chunk_bias(r.bias_s, t)
    else:
        r.cv_p[0] = vhat[t - CHUNK:, :]
        mixed = _chunk_mix(vhat.astype(BF16), r.wmix_p, CHUNK)
        bias = _chunk_bias(r.bias_p, t)
    p = _dot(h, r.w_in[:, c_pool:c_pool + pool_w])
    if not decode:
        q = _dot(h, r.w_in[:, c_q:c_q + x_w]) * (head_dim ** -0.5)
    act = ffn_up(1)
    a_in = (u * (mixed + bias)).astype(BF16)

    if decode:
        n_seq = t // rows
        p = p.reshape(n_seq, rows, pool_w)
        r.ext_s[:, 0:POOL_PAD, :] = r.state[...]
        r.ext_s[:, POOL_PAD:POOL_PAD + rows, :] = p

        def load_shifted(k, lt):
            return r.ext_s[:, POOL_PAD - k:POOL_PAD - k + rows,
                           lt * V7X_LANES:(lt + 1) * V7X_LANES]

        pos = PAST_LEN + lax.broadcasted_iota(jnp.int32, (n_seq, rows, V7X_LANES), 1)
        pooled = _window_means_minus_self(load_shifted, p, pos).reshape(t, pool_w)
        r.ptail_s[...] = r.ext_s[:, rows:rows + POOL_PAD, :]
    else:
        r.ext_p[POOL_PAD:POOL_PAD + t, :] = p

        def load_shifted(k, lt):
            return r.ext_p[POOL_PAD - k:POOL_PAD - k + t, lt * V7X_LANES:(lt + 1) * V7X_LANES]

        pos = j * t + lax.broadcasted_iota(jnp.int32, (t, V7X_LANES), 0)
        pooled = _window_means_minus_self(load_shifted, p, pos)
        tail = r.ext_p[t:t + POOL_PAD, :]
        r.ext_p[0:POOL_PAD, :] = tail
        r.ptail_p[0] = tail
    pooled = pooled.astype(BF16)

    a = _dot(a_in, r.w_out_a[...])

    if not decode:
        head = lax.broadcasted_iota(jnp.int32, (t, x_w), 1) >> _log2(head_dim)
        kt = r.kt[0]
        scores = [_dot(jnp.where(head == hd, q, 0.0).astype(BF16), kt) for hd in range(X_HEADS)]
    pool_mixed = _dot(pooled, r.wpool[...]) * r.pool_scale[...]
    x2 = x2 + ffn_down(act, 1)
    if decode:
        o = r.attn[...]
    else:
        probs = [_softmax_bf16(s) for s in scores]
        vb = r.vb[0]
        o = jnp.zeros((t, x_w), F32)
        for hd in range(X_HEADS):
            o = jnp.where(head == hd, _dot(probs[hd], vb), o)
    b = _dot(pool_mixed.astype(BF16), r.w_out_b[...])
    gate_b = gate(h, 1)
    c = _dot(o.astype(BF16), r.w_out_c[...])
    act = ffn_up(2)
    gate_c = gate(h, 2)
    x2 = x2 + ffn_down(act, 2)
    merged = (gate_a * a + gate_b * b + gate_c * c).astype(BF16)
    act = ffn_up(3)
    x1 = x + _dot(merged, r.w_o[...])
    x2 = x2 + ffn_down(act, 3)
    r.x1[...] = x1
    r.h2[...] = _rmsnorm(x1, r.g_ffn[...]).astype(BF16)
    y = _rmsnorm(x2, r.g_final[...])
    if y_ref.shape[0] == 1:
        y_ref[0] = y
    else:
        y_ref[...] = y


def _load_weights_bf16(r):
    pairs = [(r.w_in_f32, r.w_in), (r.w_up_f32, r.w_up), (r.w_down_f32, r.w_down),
             (r.w_o_f32, r.w_o), (r.w_out_a_f32, r.w_out_a), (r.w_out_b_f32, r.w_out_b),
             (r.w_out_c_f32, r.w_out_c)]
    wide_cols = max(src.shape[1] for src, _ in pairs)
    narrow_cols = min(src.shape[1] for src, _ in pairs)

    def load(stage_w, stage_n, sem_w, sem_n):
        chunks = []
        used = {True: 0, False: 0}
        for src, dst in pairs:
            narrow = src.shape[1] == narrow_cols
            stage, sem = (stage_n, sem_n) if narrow else (stage_w, sem_w)
            rows = stage.shape[1]
            assert src.shape[0] % rows == 0, (src.shape, rows)
            for row0 in range(0, src.shape[0], rows):
                slot = used[narrow] % LOAD_SLOTS
                used[narrow] += 1
                landing = stage.at[slot, :, 0:src.shape[1]]
                copy = pltpu.make_async_copy(src.at[pl.ds(row0, rows), :], landing, sem.at[slot])
                chunks.append((copy, landing, dst.at[pl.ds(row0, rows), :]))
        started = 0
        for idx, (copy, landing, dst_rows) in enumerate(chunks):
            while started < min(idx + LOAD_SLOTS - 1, len(chunks)):
                chunks[started][0].start()
                started += 1
            copy.wait()
            dst_rows[...] = landing[...].astype(BF16)

    pl.run_scoped(load,
                  pltpu.VMEM((LOAD_SLOTS, LOAD_WIDE_ROWS, wide_cols), F32),
                  pltpu.VMEM((LOAD_SLOTS, LOAD_NARROW_ROWS, narrow_cols), F32),
                  pltpu.SemaphoreType.DMA((LOAD_SLOTS,)),
                  pltpu.SemaphoreType.DMA((LOAD_SLOTS,)))


def _layer_kernel(tiles_per_seq, n_prompt, *refs):
    r = _LayerRefs(*refs)
    step = pl.program_id(0)
    j = jnp.minimum(step, n_prompt - 1) % tiles_per_seq

    @pl.when(step == 0)
    def _():
        r.x1[...] = jnp.zeros(r.x1.shape, F32)
        r.h2[...] = jnp.zeros(r.h2.shape, BF16)
        _load_weights_bf16(r)

    @pl.when(jnp.logical_and(step < n_prompt, j == 0))
    def _():
        r.ext_p[0:POOL_PAD, :] = jnp.zeros((POOL_PAD, r.ext_p.shape[1]), F32)

    @pl.when(step < n_prompt)
    def _():
        _layer_step(r, False, r.yp, j)

    @pl.when(step == n_prompt)
    def _():
        _layer_step(r, True, r.yp, j)

    @pl.when(step > n_prompt)
    def _():
        _layer_step(r, True, r.ys, j)


def _mem_kv(mem, g_mem, w_kv):
    bsz, n_mem, d = mem.shape
    xw = w_kv.shape[-1] // 2
    return pl.pallas_call(
        _mem_kv_kernel,
        grid=(bsz,),
        in_specs=[pl.BlockSpec((1, n_mem, d), lambda i: (i, 0, 0)),
                  _const_spec((1, d)), _const_spec((d, 2 * xw))],
        out_specs=[pl.BlockSpec((1, xw, n_mem), lambda i: (i, 0, 0)),
                   pl.BlockSpec((1, xw, n_mem), lambda i: (i, 0, 0)),
                   pl.BlockSpec((1, xw, n_mem), lambda i: (i, 0, 0)),
                   pl.BlockSpec((1, n_mem, xw), lambda i: (i, 0, 0))],
        out_shape=[jax.ShapeDtypeStruct((bsz, xw, n_mem), F32),
                   jax.ShapeDtypeStruct((bsz, xw, n_mem), F32),
                   jax.ShapeDtypeStruct((bsz, xw, n_mem), BF16),
                   jax.ShapeDtypeStruct((bsz, n_mem, xw), BF16)],
        compiler_params=pltpu.CompilerParams(dimension_semantics=("arbitrary",)),
        name="mem_kv",
    )(mem, g_mem.reshape(1, d).astype(F32), w_kv.astype(BF16))


def _sample_attn(x2d, kt, vt, g_mix, wq):
    n_seq, xw, n_mem = kt.shape
    d = x2d.shape[1]
    ts = ATTN_TILE_SEQS
    t = ts * (x2d.shape[0] // n_seq)
    return pl.pallas_call(
        _sample_attn_kernel,
        grid=(n_seq // ts,),
        in_specs=[pl.BlockSpec((t, d), lambda i: (i, 0)),
                  pl.BlockSpec((ts, xw, n_mem), lambda i: (i, 0, 0)),
                  pl.BlockSpec((ts, xw, n_mem), lambda i: (i, 0, 0)),
                  _const_spec((1, d)), _const_spec(wq.shape)],
        out_specs=pl.BlockSpec((t, xw), lambda i: (i, 0)),
        out_shape=jax.ShapeDtypeStruct((x2d.shape[0], xw), F32),
        scratch_shapes=[pltpu.VMEM((t, xw), F32)],
        compiler_params=pltpu.CompilerParams(
            dimension_semantics=("arbitrary",),
            vmem_limit_bytes=VMEM_LIMIT_BYTES),
        name="sample_attn",
    )(x2d, kt, vt, g_mix.reshape(1, d).astype(F32), wq)


def _spatial_operands(w_s, b_s, group_dim, chunk_rows, tile_rows):
    wmix = jnp.tile(w_s[:, :chunk_rows, :chunk_rows], (1, 1, tile_rows // chunk_rows))
    bias = jnp.repeat(b_s[:, :chunk_rows].T, group_dim, axis=1)
    return wmix.astype(F32), bias.astype(F32)


def _layer(x_prompt, x2d, ktb, vb, attn, state, lw):
    (g_mix, w_in, g_v, b_v, w_s, b_s, w_pool, pool_scale,
     w_out_a, w_out_b, w_out_c, w_o, g_ffn, w_up, w_down, g_final) = lw
    bsz, seq, d = x_prompt.shape
    n_seq = state.shape[0]
    rows = x2d.shape[0] // n_seq
    t = TILE
    ts = t // rows
    tps = seq // t
    n_prompt = bsz * tps
    n_decode = n_seq // ts
    sgu_w = g_v.shape[-1]
    pool_w = pool_scale.shape[-1]
    n_mem, xw = vb.shape[1:]
    group_dim = sgu_w // w_s.shape[0]
    row2 = lambda vec: vec.reshape(1, -1).astype(F32)
    wmix_p, bias_p = _spatial_operands(w_s, b_s, group_dim, CHUNK, t)
    wmix_s, bias_s = _spatial_operands(w_s, b_s, group_dim, rows, t)
    wpool = jax.scipy.linalg.block_diag(*[w_pool[i] for i in range(w_pool.shape[0])])
    state_pad = jnp.pad(state, ((0, 0), (POOL_PAD - state.shape[1], 0), (0, 0)))
    big = [w_in, w_out_a, w_out_b, w_out_c, w_o, w_up, w_down]
    weights = [row2(g_mix), w_in, row2(g_v), row2(b_v), wmix_p, bias_p, wmix_s, bias_s,
               wpool.astype(BF16), row2(pool_scale), w_out_a, w_out_b, w_out_c, w_o,
               row2(g_ffn), w_up, w_down, row2(g_final)]

    def prompt_tile(m):
        m = jnp.clip(m, 0, n_prompt - 1)
        return m // tps, m % tps

    def decode_tile(m):
        return jnp.clip(m - n_prompt, 0, n_decode - 1)

    yp, ys, cv_p, ptail_p, cv_s, ptail_s = pl.pallas_call(
        functools.partial(_layer_kernel, tps, n_prompt),
        grid=(n_prompt + n_decode + 1,),
        in_specs=[pl.BlockSpec((1, t, d), lambda s: (*prompt_tile(s), 0)),
                  pl.BlockSpec((t, d), lambda s: (decode_tile(s), 0)),
                  pl.BlockSpec((1, xw, n_mem), lambda s: (prompt_tile(s)[0], 0, 0)),
                  pl.BlockSpec((1, n_mem, xw), lambda s: (prompt_tile(s)[0], 0, 0)),
                  pl.BlockSpec((t, xw), lambda s: (decode_tile(s), 0)),
                  pl.BlockSpec((ts, POOL_PAD, pool_w), lambda s: (decode_tile(s), 0, 0))]
                 + [pl.BlockSpec(memory_space=pl.ANY) if any(w is b for b in big)
                    else _const_spec(w.shape) for w in weights],
        out_specs=[pl.BlockSpec((1, t, d), lambda s: (*prompt_tile(s - 1), 0)),
                   pl.BlockSpec((t, d), lambda s: (decode_tile(s - 1), 0)),
                   pl.BlockSpec((1, CHUNK, sgu_w), lambda s: (prompt_tile(s)[0], 0, 0)),
                   pl.BlockSpec((1, POOL_PAD, pool_w), lambda s: (prompt_tile(s)[0], 0, 0)),
                   pl.BlockSpec((t, sgu_w), lambda s: (decode_tile(s), 0)),
                   pl.BlockSpec((ts, POOL_PAD, pool_w), lambda s: (decode_tile(s), 0, 0))],
        out_shape=[jax.ShapeDtypeStruct((bsz, seq, d), F32),
                   jax.ShapeDtypeStruct((n_seq * rows, d), F32),
                   jax.ShapeDtypeStruct((bsz, CHUNK, sgu_w), F32),
                   jax.ShapeDtypeStruct((bsz, POOL_PAD, pool_w), F32),
                   jax.ShapeDtypeStruct((n_seq * rows, sgu_w), F32),
                   jax.ShapeDtypeStruct((n_seq, POOL_PAD, pool_w), F32)],
        scratch_shapes=[pltpu.VMEM((POOL_PAD + t, pool_w), F32),
                        pltpu.VMEM((ts, POOL_PAD + rows, pool_w), F32),
                        pltpu.VMEM((t, d), F32),
                        pltpu.VMEM((t, d), BF16)]
                       + [pltpu.VMEM(w.shape, BF16) for w in big],
        compiler_params=pltpu.CompilerParams(
            dimension_semantics=("arbitrary",),
            vmem_limit_bytes=VMEM_LIMIT_BYTES),
        name="layer",
    )(x_prompt, x2d, ktb, vb, attn, state_pad, *weights)
    return (yp, ys.reshape(n_seq, rows, d), cv_p, ptail_p[:, POOL_PAD - POOL_STATE:, :],
            cv_s.reshape(n_seq, rows, sgu_w), ptail_s[:, POOL_PAD - POOL_STATE:, :])


def _feature_major(kv):
    bsz, n_mem, heads, head_dim = kv.shape
    return jnp.transpose(kv, (0, 2, 3, 1)).reshape(bsz, heads * head_dim, n_mem)


def _token_major(kvt, heads):
    bsz, xw, n_mem = kvt.shape
    return jnp.transpose(kvt.reshape(bsz, heads, xw // heads, n_mem), (0, 3, 1, 2))


def kernel(x_prompt, x_sample, mem_prompt, cache_mem_k, cache_mem_v, state_pool, g_mix, w_in, g_v, b_v, w_s, b_s, w_pool, pool_scale, g_mem, w_kv, w_out_a, w_out_b, w_out_c, w_o, g_ffn, w_up, w_down, g_final):
    depth = w_in.shape[0]
    assert depth == 1, "the final rmsnorm is fused into the (single) layer kernel"
    heads = cache_mem_k.shape[-2]
    n_seq, rows, d = x_sample.shape
    lw = (g_mix[0], w_in[0], g_v[0], b_v[0], w_s[0], b_s[0], w_pool[0], pool_scale[0],
          w_out_a[0], w_out_b[0], w_out_c[0], w_o[0], g_ffn[0], w_up[0], w_down[0], g_final)
    mem_kt, mem_vt, ktb, vb = _mem_kv(mem_prompt, g_mem[0], w_kv[0])
    x2d = x_sample.reshape(n_seq * rows, d)
    xw = heads * cache_mem_k.shape[-1]
    c_q = w_in.shape[-1] - 3 * d - xw
    attn = _sample_attn(x2d, _feature_major(cache_mem_k[0]), _feature_major(cache_mem_v[0]),
                        g_mix[0], w_in[0][:, c_q:c_q + xw].astype(BF16))
    y_prompt, y_sample, cv_p, ptail_p, cv_s, ptail_s = _layer(
        x_prompt, x2d, ktb, vb, attn, state_pool[0], lw)
    return (y_prompt, y_sample, _token_major(mem_kt, heads)[None], _token_major(mem_vt, heads)[None],
            ptail_p[None], ptail_s[None], cv_p[None], cv_s[None])
```

```python
import functools
from typing import NamedTuple

import jax
import jax.numpy as jnp
from jax import lax
from jax.experimental import pallas as pl
from jax.experimental.pallas import tpu as pltpu

PAST_LEN = 16384
CHUNK = 128
POOL_WINDOWS = (2, 4, 8, 16)
POOL_STATE = max(POOL_WINDOWS) - 1
X_HEADS = 4
EPS = 1e-6

V7X_VMEM_BYTES = 64 * 1024 * 1024
V7X_LANES = 128
VMEM_LIMIT_BYTES = V7X_VMEM_BYTES - 6 * 1024 * 1024

TILE = 256
FFN_SPLIT = 4
LOAD_SLOTS = 8
LOAD_WIDE_ROWS = 32
LOAD_NARROW_ROWS = 128
POOL_PAD = POOL_STATE + 1

BF16 = jnp.bfloat16
F32 = jnp.float32


def _dot(a, b):
    return jnp.dot(a, b, preferred_element_type=F32)


def _log2(n):
    assert n > 0 and n & (n - 1) == 0, n
    return n.bit_length() - 1


def _rmsnorm(x, g):
    return x * lax.rsqrt(jnp.mean(x * x, axis=-1, keepdims=True) + EPS) * g


def _layernorm(x, g, b):
    xc = x - jnp.mean(x, axis=-1, keepdims=True)
    return xc * lax.rsqrt(jnp.mean(xc * xc, axis=-1, keepdims=True) + EPS) * g + b


def _softmax_bf16(s):
    e = jnp.exp(s - jnp.max(s, axis=-1, keepdims=True))
    return (e * (1.0 / jnp.sum(e, axis=-1, keepdims=True))).astype(BF16)


def _const_spec(shape):
    zeros = (0,) * len(shape)
    return pl.BlockSpec(shape, lambda *_: zeros, pipeline_mode=pl.Buffered(1))


def _chunk_mix(vhat_b, wmix_ref, chunk_rows):
    t, sgu_w = vhat_b.shape
    n_groups = wmix_ref.shape[0]
    group_dim = sgu_w // n_groups
    shift = _log2(chunk_rows)
    reps = t // chunk_rows
    row = lax.broadcasted_iota(jnp.int32, (t, t), 0)
    col = lax.broadcasted_iota(jnp.int32, (t, t), 1)
    same_chunk = (row >> shift) == (col >> shift)
    causal = col <= row
    parts = []
    for g in range(n_groups):
        w = jnp.concatenate([wmix_ref[g]] * reps, axis=0)
        w = jnp.where(same_chunk, jnp.where(causal, w, 0.0), 0.0).astype(BF16)
        parts.append(_dot(w, vhat_b[:, g * group_dim:(g + 1) * group_dim]))
    return jnp.concatenate(parts, axis=1)


def _chunk_bias(bias_ref, t):
    return jnp.concatenate([bias_ref[...]] * (t // bias_ref.shape[0]), axis=0)


def _window_means_minus_self(load_shifted, p, pos):
    group_lanes = p.shape[-1] // len(POOL_WINDOWS)
    outs = []
    for lt in range(p.shape[-1] // V7X_LANES):
        p_lt = p[..., lt * V7X_LANES:(lt + 1) * V7X_LANES]
        lane = lax.broadcasted_iota(jnp.int32, p_lt.shape, p_lt.ndim - 1)
        wins = POOL_WINDOWS[lt * V7X_LANES // group_lanes:(lt + 1) * V7X_LANES // group_lanes]
        acc = p_lt
        sums = {1: p_lt}
        for k in range(1, max(wins)):
            acc = acc + load_shifted(k, lt)
            sums[k + 1] = acc
        win_sum = sums[wins[-1]]
        win = jnp.full(p_lt.shape, wins[-1], jnp.int32)
        for gi in range(len(wins) - 2, -1, -1):
            in_group = lane < (gi + 1) * group_lanes
            win_sum = jnp.where(in_group, sums[wins[gi]], win_sum)
            win = jnp.where(in_group, wins[gi], win)
        cnt = jnp.minimum(pos + 1, win).astype(F32)
        outs.append(win_sum / cnt - p_lt)
    return jnp.concatenate(outs, axis=-1)


def _mem_kv_kernel(mem_ref, g_mem_ref, w_kv_ref, kt_ref, vt_ref, ktb_ref, vb_ref):
    xw = kt_ref.shape[1]
    hn = _rmsnorm(mem_ref[0], g_mem_ref[...]).astype(BF16)
    kv = _dot(hn, w_kv_ref[...])
    kt = kv[:, 0:xw].T
    v = kv[:, xw:2 * xw]
    kt_ref[0] = kt
    vt_ref[0] = v.T
    ktb_ref[0] = kt.astype(BF16)
    vb_ref[0] = v.astype(BF16)


def _decode_queries(r):
    x_w = r.ckt.shape[1]
    c_q = r.w_in.shape[1] - 3 * r.xq.shape[1] - x_w
    h = _rmsnorm(r.xq[...], r.g_mix[...]).astype(BF16)
    return _dot(h, r.w_in[:, c_q:c_q + x_w]) * ((x_w // X_HEADS) ** -0.5)


def _decode_own_head(r):
    n_seq, x_w, _ = r.ckt.shape
    rows = r.xq.shape[0] // n_seq
    stack = X_HEADS * rows
    lane_head = lax.broadcasted_iota(jnp.int32, (stack, x_w), 1) >> _log2(x_w // X_HEADS)
    row_head = lax.broadcasted_iota(jnp.int32, (stack, x_w), 0) >> _log2(rows)
    return lane_head == row_head


def _decode_scores(r, q):
    n_seq = r.ckt.shape[0]
    rows = r.xq.shape[0] // n_seq
    own_head = _decode_own_head(r)
    scores = []
    for i in range(n_seq):
        qs = jnp.concatenate([q[i * rows:(i + 1) * rows, :]] * X_HEADS, axis=0)
        qm = jnp.where(own_head, qs, 0.0).astype(BF16)
        scores.append(_dot(qm, r.ckt[i].astype(BF16)))
    return scores


def _decode_values(r, probs, step):
    n_seq = r.cvt.shape[0]
    rows = r.xq.shape[0] // n_seq
    own_head = _decode_own_head(r)
    for i in range(n_seq):
        of = lax.dot_general(probs[i], r.cvt[i].astype(BF16),
                             (((1,), (1,)), ((), ())), preferred_element_type=F32)
        of = jnp.where(own_head, of, 0.0)
        out = of[0:rows]
        for hd in range(1, X_HEADS):
            out = out + of[hd * rows:(hd + 1) * rows]
        row0 = pl.multiple_of((step * n_seq + i) * rows, rows)
        r.attn[pl.ds(row0, rows), :] = out


class _LayerRefs(NamedTuple):
    xp: object
    xs: object
    kt: object
    vb: object
    xq: object
    ckt: object
    cvt: object
    state: object
    g_mix: object
    w_in_f32: object
    g_v: object
    b_v: object
    wmix_p: object
    bias_p: object
    wmix_s: object
    bias_s: object
    wpool: object
    pool_scale: object
    w_out_a_f32: object
    w_out_b_f32: object
    w_out_c_f32: object
    w_o_f32: object
    g_ffn: object
    w_up_f32: object
    w_down_f32: object
    g_final: object
    yp: object
    ys: object
    cv_p: object
    ptail_p: object
    cv_s: object
    ptail_s: object
    ext_p: object
    ext_s: object
    x1: object
    h2: object
    w_in: object
    w_out_a: object
    w_out_b: object
    w_out_c: object
    w_o: object
    w_up: object
    w_down: object
    attn: object


def _layer_step(r, decode, y_ref, j, step, n_prompt):
    t, d = r.x1.shape
    sgu_w = r.g_v.shape[-1]
    pool_w = r.pool_scale.shape[-1]
    x_w = r.attn.shape[-1]
    n_decode = r.attn.shape[0] // t
    head_dim = x_w // X_HEADS
    c_pool = 2 * sgu_w
    c_q = c_pool + pool_w
    c_gate = c_q + x_w
    slab = r.w_up.shape[1] // FFN_SPLIT

    def ffn_up(s):
        up = _dot(r.h2[...], r.w_up[:, s * slab:(s + 1) * slab])
        return jnp.square(jnp.maximum(up, 0.0)).astype(BF16)

    def ffn_down(act, s):
        return _dot(act, r.w_down[s * slab:(s + 1) * slab, :])

    def gate(h, part):
        lo = c_gate + part * d
        return jax.nn.sigmoid(_dot(h, r.w_in[:, lo:lo + d]))

    act = ffn_up(0)
    if not decode:
        dec_q = _decode_queries(r)
    x = r.xs[...] if decode else r.xp[0]
    h = _rmsnorm(x, r.g_mix[...]).astype(BF16)
    u = _dot(h, r.w_in[:, 0:sgu_w])
    v = _dot(h, r.w_in[:, sgu_w:2 * sgu_w])
    gate_a = gate(h, 0)
    x2 = r.x1[...] + ffn_down(act, 0)

    u = jax.nn.gelu(u)
    vhat = _layernorm(jax.nn.gelu(v), r.g_v[...], r.b_v[...])
    if decode:
        rows = r.wmix_s.shape[1]
        r.cv_s[...] = vhat
        mixed = _chunk_mix(vhat.astype(BF16), r.wmix_s, rows)
        bias = _chunk_bias(r.bias_s, t)
    else:
        r.cv_p[0] = vhat[t - CHUNK:, :]
        mixed = _chunk_mix(vhat.astype(BF16), r.wmix_p, CHUNK)
        bias = _chunk_bias(r.bias_p, t)
    p = _dot(h, r.w_in[:, c_pool:c_pool + pool_w])
    if not decode:
        q = _dot(h, r.w_in[:, c_q:c_q + x_w]) * (head_dim ** -0.5)
        dec_scores = _decode_scores(r, dec_q)
    act = ffn_up(1)
    a_in = (u * (mixed + bias)).astype(BF16)

    if decode:
        n_seq = t // rows
        p = p.reshape(n_seq, rows, pool_w)
        r.ext_s[:, 0:POOL_PAD, :] = r.state[...]
        r.ext_s[:, POOL_PAD:POOL_PAD + rows, :] = p

        def load_shifted(k, lt):
            return r.ext_s[:, POOL_PAD - k:POOL_PAD - k + rows,
                           lt * V7X_LANES:(lt + 1) * V7X_LANES]

        pos = PAST_LEN + lax.broadcasted_iota(jnp.int32, (n_seq, rows, V7X_LANES), 1)
        pooled = _window_means_minus_self(load_shifted, p, pos).reshape(t, pool_w)
        r.ptail_s[...] = r.ext_s[:, rows:rows + POOL_PAD, :]
    else:
        r.ext_p[POOL_PAD:POOL_PAD + t, :] = p

        def load_shifted(k, lt):
            return r.ext_p[POOL_PAD - k:POOL_PAD - k + t, lt * V7X_LANES:(lt + 1) * V7X_LANES]

        pos = j * t + lax.broadcasted_iota(jnp.int32, (t, V7X_LANES), 0)
        pooled = _window_means_minus_self(load_shifted, p, pos)
        tail = r.ext_p[t:t + POOL_PAD, :]
        r.ext_p[0:POOL_PAD, :] = tail
        r.ptail_p[0] = tail
    pooled = pooled.astype(BF16)

    a = _dot(a_in, r.w_out_a[...])

    if not decode:
        head = lax.broadcasted_iota(jnp.int32, (t, x_w), 1) >> _log2(head_dim)
        kt = r.kt[0]
        scores = [_dot(jnp.where(head == hd, q, 0.0).astype(BF16), kt) for hd in range(X_HEADS)]
    pool_mixed = _dot(pooled, r.wpool[...]) * r.pool_scale[...]
    x2 = x2 + ffn_down(act, 1)
    if decode:
        row0 = pl.multiple_of(jnp.clip(step - n_prompt, 0, n_decode - 1) * t, t)
        o = r.attn[pl.ds(row0, t), :]
    else:
        probs = [_softmax_bf16(s) for s in scores]
        dec_probs = [_softmax_bf16(s) for s in dec_scores]
        vb = r.vb[0]
        o = jnp.zeros((t, x_w), F32)
        for hd in range(X_HEADS):
            o = jnp.where(head == hd, _dot(probs[hd], vb), o)
        _decode_values(r, dec_probs, step)
    b = _dot(pool_mixed.astype(BF16), r.w_out_b[...])
    gate_b = gate(h, 1)
    c = _dot(o.astype(BF16), r.w_out_c[...])
    act = ffn_up(2)
    gate_c = gate(h, 2)
    x2 = x2 + ffn_down(act, 2)
    merged = (gate_a * a + gate_b * b + gate_c * c).astype(BF16)
    act = ffn_up(3)
    x1 = x + _dot(merged, r.w_o[...])
    x2 = x2 + ffn_down(act, 3)
    r.x1[...] = x1
    r.h2[...] = _rmsnorm(x1, r.g_ffn[...]).astype(BF16)
    y = _rmsnorm(x2, r.g_final[...])
    if y_ref.shape[0] == 1:
        y_ref[0] = y
    else:
        y_ref[...] = y


def _load_weights_bf16(r):
    pairs = [(r.w_in_f32, r.w_in), (r.w_up_f32, r.w_up), (r.w_down_f32, r.w_down),
             (r.w_o_f32, r.w_o), (r.w_out_a_f32, r.w_out_a), (r.w_out_b_f32, r.w_out_b),
             (r.w_out_c_f32, r.w_out_c)]
    wide_cols = max(src.shape[1] for src, _ in pairs)
    narrow_cols = min(src.shape[1] for src, _ in pairs)

    def load(stage_w, stage_n, sem_w, sem_n):
        chunks = []
        used = {True: 0, False: 0}
        for src, dst in pairs:
            narrow = src.shape[1] == narrow_cols
            stage, sem = (stage_n, sem_n) if narrow else (stage_w, sem_w)
            rows = stage.shape[1]
            assert src.shape[0] % rows == 0, (src.shape, rows)
            for row0 in range(0, src.shape[0], rows):
                slot = used[narrow] % LOAD_SLOTS
                used[narrow] += 1
                landing = stage.at[slot, :, 0:src.shape[1]]
                copy = pltpu.make_async_copy(src.at[pl.ds(row0, rows), :], landing, sem.at[slot])
                chunks.append((copy, landing, dst.at[pl.ds(row0, rows), :]))
        started = 0
        for idx, (copy, landing, dst_rows) in enumerate(chunks):
            while started < min(idx + LOAD_SLOTS - 1, len(chunks)):
                chunks[started][0].start()
                started += 1
            copy.wait()
            dst_rows[...] = landing[...].astype(BF16)

    pl.run_scoped(load,
                  pltpu.VMEM((LOAD_SLOTS, LOAD_WIDE_ROWS, wide_cols), F32),
                  pltpu.VMEM((LOAD_SLOTS, LOAD_NARROW_ROWS, narrow_cols), F32),
                  pltpu.SemaphoreType.DMA((LOAD_SLOTS,)),
                  pltpu.SemaphoreType.DMA((LOAD_SLOTS,)))


def _layer_kernel(tiles_per_seq, n_prompt, *refs):
    r = _LayerRefs(*refs)
    step = pl.program_id(0)
    j = jnp.minimum(step, n_prompt - 1) % tiles_per_seq

    @pl.when(step == 0)
    def _():
        r.x1[...] = jnp.zeros(r.x1.shape, F32)
        r.h2[...] = jnp.zeros(r.h2.shape, BF16)
        _load_weights_bf16(r)

    @pl.when(jnp.logical_and(step < n_prompt, j == 0))
    def _():
        r.ext_p[0:POOL_PAD, :] = jnp.zeros((POOL_PAD, r.ext_p.shape[1]), F32)

    @pl.when(step < n_prompt)
    def _():
        _layer_step(r, False, r.yp, j, step, n_prompt)

    @pl.when(step == n_prompt)
    def _():
        _layer_step(r, True, r.yp, j, step, n_prompt)

    @pl.when(step > n_prompt)
    def _():
        _layer_step(r, True, r.ys, j, step, n_prompt)


def _mem_kv(mem, g_mem, w_kv):
    bsz, n_mem, d = mem.shape
    xw = w_kv.shape[-1] // 2
    return pl.pallas_call(
        _mem_kv_kernel,
        grid=(bsz,),
        in_specs=[pl.BlockSpec((1, n_mem, d), lambda i: (i, 0, 0)),
                  _const_spec((1, d)), _const_spec((d, 2 * xw))],
        out_specs=[pl.BlockSpec((1, xw, n_mem), lambda i: (i, 0, 0)),
                   pl.BlockSpec((1, xw, n_mem), lambda i: (i, 0, 0)),
                   pl.BlockSpec((1, xw, n_mem), lambda i: (i, 0, 0)),
                   pl.BlockSpec((1, n_mem, xw), lambda i: (i, 0, 0))],
        out_shape=[jax.ShapeDtypeStruct((bsz, xw, n_mem), F32),
                   jax.ShapeDtypeStruct((bsz, xw, n_mem), F32),
                   jax.ShapeDtypeStruct((bsz, xw, n_mem), BF16),
                   jax.ShapeDtypeStruct((bsz, n_mem, xw), BF16)],
        compiler_params=pltpu.CompilerParams(dimension_semantics=("arbitrary",)),
        name="mem_kv",
    )(mem, g_mem.reshape(1, d).astype(F32), w_kv.astype(BF16))


def _spatial_operands(w_s, b_s, group_dim, chunk_rows, tile_rows):
    wmix = jnp.tile(w_s[:, :chunk_rows, :chunk_rows], (1, 1, tile_rows // chunk_rows))
    bias = jnp.repeat(b_s[:, :chunk_rows].T, group_dim, axis=1)
    return wmix.astype(F32), bias.astype(F32)


def _layer(x_prompt, x2d, ktb, vb, cache_kt, cache_vt, state, lw):
    (g_mix, w_in, g_v, b_v, w_s, b_s, w_pool, pool_scale,
     w_out_a, w_out_b, w_out_c, w_o, g_ffn, w_up, w_down, g_final) = lw
    bsz, seq, d = x_prompt.shape
    n_seq = state.shape[0]
    rows = x2d.shape[0] // n_seq
    t = TILE
    ts = t // rows
    tps = seq // t
    n_prompt = bsz * tps
    n_decode = n_seq // ts
    assert n_seq % n_prompt == 0, (n_seq, n_prompt)
    sps = n_seq // n_prompt
    sgu_w = g_v.shape[-1]
    pool_w = pool_scale.shape[-1]
    n_mem, xw = vb.shape[1:]
    group_dim = sgu_w // w_s.shape[0]
    row2 = lambda vec: vec.reshape(1, -1).astype(F32)
    wmix_p, bias_p = _spatial_operands(w_s, b_s, group_dim, CHUNK, t)
    wmix_s, bias_s = _spatial_operands(w_s, b_s, group_dim, rows, t)
    wpool = jax.scipy.linalg.block_diag(*[w_pool[i] for i in range(w_pool.shape[0])])
    state_pad = jnp.pad(state, ((0, 0), (POOL_PAD - state.shape[1], 0), (0, 0)))
    big = [w_in, w_out_a, w_out_b, w_out_c, w_o, w_up, w_down]
    weights = [row2(g_mix), w_in, row2(g_v), row2(b_v), wmix_p, bias_p, wmix_s, bias_s,
               wpool.astype(BF16), row2(pool_scale), w_out_a, w_out_b, w_out_c, w_o,
               row2(g_ffn), w_up, w_down, row2(g_final)]

    def prompt_tile(m):
        m = jnp.clip(m, 0, n_prompt - 1)
        return m // tps, m % tps

    def decode_tile(m):
        return jnp.clip(m - n_prompt, 0, n_decode - 1)

    yp, ys, cv_p, ptail_p, cv_s, ptail_s = pl.pallas_call(
        functools.partial(_layer_kernel, tps, n_prompt),
        grid=(n_prompt + n_decode + 1,),
        in_specs=[pl.BlockSpec((1, t, d), lambda s: (*prompt_tile(s), 0)),
                  pl.BlockSpec((t, d), lambda s: (decode_tile(s), 0)),
                  pl.BlockSpec((1, xw, n_mem), lambda s: (prompt_tile(s)[0], 0, 0)),
                  pl.BlockSpec((1, n_mem, xw), lambda s: (prompt_tile(s)[0], 0, 0)),
                  pl.BlockSpec((sps * rows, d), lambda s: (jnp.clip(s, 0, n_prompt - 1), 0)),
                  pl.BlockSpec((sps, xw, n_mem), lambda s: (jnp.clip(s, 0, n_prompt - 1), 0, 0)),
                  pl.BlockSpec((sps, xw, n_mem), lambda s: (jnp.clip(s, 0, n_prompt - 1), 0, 0)),
                  pl.BlockSpec((ts, POOL_PAD, pool_w), lambda s: (decode_tile(s), 0, 0))]
                 + [pl.BlockSpec(memory_space=pl.ANY) if any(w is b for b in big)
                    else _const_spec(w.shape) for w in weights],
        out_specs=[pl.BlockSpec((1, t, d), lambda s: (*prompt_tile(s - 1), 0)),
                   pl.BlockSpec((t, d), lambda s: (decode_tile(s - 1), 0)),
                   pl.BlockSpec((1, CHUNK, sgu_w), lambda s: (prompt_tile(s)[0], 0, 0)),
                   pl.BlockSpec((1, POOL_PAD, pool_w), lambda s: (prompt_tile(s)[0], 0, 0)),
                   pl.BlockSpec((t, sgu_w), lambda s: (decode_tile(s), 0)),
                   pl.BlockSpec((ts, POOL_PAD, pool_w), lambda s: (decode_tile(s), 0, 0))],
        out_shape=[jax.ShapeDtypeStruct((bsz, seq, d), F32),
                   jax.ShapeDtypeStruct((n_seq * rows, d), F32),
                   jax.ShapeDtypeStruct((bsz, CHUNK, sgu_w), F32),
                   jax.ShapeDtypeStruct((bsz, POOL_PAD, pool_w), F32),
                   jax.ShapeDtypeStruct((n_seq * rows, sgu_w), F32),
                   jax.ShapeDtypeStruct((n_seq, POOL_PAD, pool_w), F32)],
        scratch_shapes=[pltpu.VMEM((POOL_PAD + t, pool_w), F32),
                        pltpu.VMEM((ts, POOL_PAD + rows, pool_w), F32),
                        pltpu.VMEM((t, d), F32),
                        pltpu.VMEM((t, d), BF16)]
                       + [pltpu.VMEM(w.shape, BF16) for w in big]
                       + [pltpu.VMEM((n_seq * rows, xw), F32)],
        compiler_params=pltpu.CompilerParams(
            dimension_semantics=("arbitrary",),
            vmem_limit_bytes=VMEM_LIMIT_BYTES),
        name="layer",
    )(x_prompt, x2d, ktb, vb, x2d, cache_kt, cache_vt, state_pad, *weights)
    return (yp, ys.reshape(n_seq, rows, d), cv_p, ptail_p[:, POOL_PAD - POOL_STATE:, :],
            cv_s.reshape(n_seq, rows, sgu_w), ptail_s[:, POOL_PAD - POOL_STATE:, :])


def _feature_major(kv):
    bsz, n_mem, heads, head_dim = kv.shape
    return jnp.transpose(kv, (0, 2, 3, 1)).reshape(bsz, heads * head_dim, n_mem)


def _token_major(kvt, heads):
    bsz, xw, n_mem = kvt.shape
    return jnp.transpose(kvt.reshape(bsz, heads, xw // heads, n_mem), (0, 3, 1, 2))


def kernel(x_prompt, x_sample, mem_prompt, cache_mem_k, cache_mem_v, state_pool, g_mix, w_in, g_v, b_v, w_s, b_s, w_pool, pool_scale, g_mem, w_kv, w_out_a, w_out_b, w_out_c, w_o, g_ffn, w_up, w_down, g_final):
    depth = w_in.shape[0]
    assert depth == 1, "the final rmsnorm is fused into the (single) layer kernel"
    heads = cache_mem_k.shape[-2]
    n_seq, rows, d = x_sample.shape
    lw = (g_mix[0], w_in[0], g_v[0], b_v[0], w_s[0], b_s[0], w_pool[0], pool_scale[0],
          w_out_a[0], w_out_b[0], w_out_c[0], w_o[0], g_ffn[0], w_up[0], w_down[0], g_final)
    mem_kt, mem_vt, ktb, vb = _mem_kv(mem_prompt, g_mem[0], w_kv[0])
    x2d = x_sample.reshape(n_seq * rows, d)
    y_prompt, y_sample, cv_p, ptail_p, cv_s, ptail_s = _layer(
        x_prompt, x2d, ktb, vb, _feature_major(cache_mem_k[0]), _feature_major(cache_mem_v[0]),
        state_pool[0], lw)
    return (y_prompt, y_sample, _token_major(mem_kt, heads)[None], _token_major(mem_vt, heads)[None],
            ptail_p[None], ptail_s[None], cv_p[None], cv_s[None])
```

```python
import functools
from typing import NamedTuple

import jax
import jax.numpy as jnp
from jax import lax
from jax.experimental import pallas as pl
from jax.experimental.pallas import tpu as pltpu

PAST_LEN = 16384
CHUNK = 128
POOL_WINDOWS = (2, 4, 8, 16)
POOL_STATE = max(POOL_WINDOWS) - 1
X_HEADS = 4
EPS = 1e-6

V7X_VMEM_BYTES = 64 * 1024 * 1024
V7X_LANES = 128
VMEM_LIMIT_BYTES = V7X_VMEM_BYTES - 6 * 1024 * 1024

TILE = 256
FFN_SPLIT = 4
LOAD_SLOTS = 8
LOAD_WIDE_ROWS = 32
LOAD_NARROW_ROWS = 128
POOL_PAD = POOL_STATE + 1

BF16 = jnp.bfloat16
F32 = jnp.float32


def _dot(a, b):
    return jnp.dot(a, b, preferred_element_type=F32)


def _log2(n):
    assert n > 0 and n & (n - 1) == 0, n
    return n.bit_length() - 1


def _rmsnorm(x, g):
    return x * lax.rsqrt(jnp.mean(x * x, axis=-1, keepdims=True) + EPS) * g


def _layernorm(x, g, b):
    xc = x - jnp.mean(x, axis=-1, keepdims=True)
    return xc * lax.rsqrt(jnp.mean(xc * xc, axis=-1, keepdims=True) + EPS) * g + b


def _softmax_bf16(s):
    e = jnp.exp(s - jnp.max(s, axis=-1, keepdims=True))
    return (e * (1.0 / jnp.sum(e, axis=-1, keepdims=True))).astype(BF16)


def _const_spec(shape):
    zeros = (0,) * len(shape)
    return pl.BlockSpec(shape, lambda *_: zeros, pipeline_mode=pl.Buffered(1))


def _chunk_mix(vhat_b, wmix_ref, chunk_rows):
    t, sgu_w = vhat_b.shape
    n_groups = wmix_ref.shape[0]
    group_dim = sgu_w // n_groups
    shift = _log2(chunk_rows)
    reps = t // chunk_rows
    row = lax.broadcasted_iota(jnp.int32, (t, t), 0)
    col = lax.broadcasted_iota(jnp.int32, (t, t), 1)
    same_chunk = (row >> shift) == (col >> shift)
    causal = col <= row
    parts = []
    for g in range(n_groups):
        w = jnp.concatenate([wmix_ref[g]] * reps, axis=0)
        w = jnp.where(same_chunk, jnp.where(causal, w, 0.0), 0.0).astype(BF16)
        parts.append(_dot(w, vhat_b[:, g * group_dim:(g + 1) * group_dim]))
    return jnp.concatenate(parts, axis=1)


def _chunk_bias(bias_ref, t):
    return jnp.concatenate([bias_ref[...]] * (t // bias_ref.shape[0]), axis=0)


def _window_means_minus_self(load_shifted, p, pos):
    group_lanes = p.shape[-1] // len(POOL_WINDOWS)
    outs = []
    for lt in range(p.shape[-1] // V7X_LANES):
        p_lt = p[..., lt * V7X_LANES:(lt + 1) * V7X_LANES]
        lane = lax.broadcasted_iota(jnp.int32, p_lt.shape, p_lt.ndim - 1)
        wins = POOL_WINDOWS[lt * V7X_LANES // group_lanes:(lt + 1) * V7X_LANES // group_lanes]
        acc = p_lt
        sums = {1: p_lt}
        for k in range(1, max(wins)):
            acc = acc + load_shifted(k, lt)
            sums[k + 1] = acc
        win_sum = sums[wins[-1]]
        win = jnp.full(p_lt.shape, wins[-1], jnp.int32)
        for gi in range(len(wins) - 2, -1, -1):
            in_group = lane < (gi + 1) * group_lanes
            win_sum = jnp.where(in_group, sums[wins[gi]], win_sum)
            win = jnp.where(in_group, wins[gi], win)
        cnt = jnp.minimum(pos + 1, win).astype(F32)
        outs.append(win_sum / cnt - p_lt)
    return jnp.concatenate(outs, axis=-1)


def _mem_kv_kernel(mem_ref, g_mem_ref, w_kv_ref, kt_ref, vt_ref, ktb_ref, vb_ref):
    xw = kt_ref.shape[1]
    hn = _rmsnorm(mem_ref[0], g_mem_ref[...]).astype(BF16)
    kv = _dot(hn, w_kv_ref[...])
    kt = kv[:, 0:xw].T
    v = kv[:, xw:2 * xw]
    kt_ref[0] = kt
    vt_ref[0] = v.T
    ktb_ref[0] = kt.astype(BF16)
    vb_ref[0] = v.astype(BF16)


def _decode_queries(r):
    x_w = r.ckt.shape[1]
    c_q = r.w_in.shape[1] - 3 * r.xq.shape[1] - x_w
    h = _rmsnorm(r.xq[...], r.g_mix[...]).astype(BF16)
    return _dot(h, r.w_in[:, c_q:c_q + x_w]) * ((x_w // X_HEADS) ** -0.5)


def _decode_own_head(r):
    n_seq, x_w, _ = r.ckt.shape
    rows = r.xq.shape[0] // n_seq
    stack = X_HEADS * rows
    lane_head = lax.broadcasted_iota(jnp.int32, (stack, x_w), 1) >> _log2(x_w // X_HEADS)
    row_head = lax.broadcasted_iota(jnp.int32, (stack, x_w), 0) >> _log2(rows)
    return lane_head == row_head


def _decode_scores(r, q):
    n_seq = r.ckt.shape[0]
    rows = r.xq.shape[0] // n_seq
    own_head = _decode_own_head(r)
    scores = []
    for i in range(n_seq):
        qs = jnp.concatenate([q[i * rows:(i + 1) * rows, :]] * X_HEADS, axis=0)
        qm = jnp.where(own_head, qs, 0.0).astype(BF16)
        scores.append(_dot(qm, r.ckt[i].astype(BF16)))
    return scores


def _decode_values(r, probs, step):
    n_seq = r.cvt.shape[0]
    rows = r.xq.shape[0] // n_seq
    own_head = _decode_own_head(r)
    for i in range(n_seq):
        of = lax.dot_general(probs[i], r.cvt[i].astype(BF16),
                             (((1,), (1,)), ((), ())), preferred_element_type=F32)
        of = jnp.where(own_head, of, 0.0)
        out = of[0:rows]
        for hd in range(1, X_HEADS):
            out = out + of[hd * rows:(hd + 1) * rows]
        row0 = pl.multiple_of((step * n_seq + i) * rows, rows)
        r.attn[pl.ds(row0, rows), :] = out


class _LayerRefs(NamedTuple):
    xp: object
    xs: object
    kt: object
    vb: object
    xq: object
    ckt: object
    cvt: object
    state: object
    g_mix: object
    w_in_f32: object
    g_v: object
    b_v: object
    wmix_p: object
    bias_p: object
    wmix_s: object
    bias_s: object
    wpool: object
    pool_scale: object
    w_out_a_f32: object
    w_out_b_f32: object
    w_out_c_f32: object
    w_o_f32: object
    g_ffn: object
    w_up_f32: object
    w_down_f32: object
    g_final: object
    yp: object
    ys: object
    cv_p: object
    ptail_p: object
    cv_s: object
    ptail_s: object
    ext_p: object
    pool_in: object
    pool_out: object
    x1: object
    h2: object
    w_in: object
    w_out_a: object
    w_out_b: object
    w_out_c: object
    w_o: object
    w_up: object
    w_down: object
    attn: object


def _layer_step(r, decode, y_ref, j, step, n_prompt):
    t, d = r.x1.shape
    sgu_w = r.g_v.shape[-1]
    pool_w = r.pool_scale.shape[-1]
    x_w = r.attn.shape[-1]
    n_decode = r.attn.shape[0] // t
    head_dim = x_w // X_HEADS
    c_pool = 2 * sgu_w
    c_q = c_pool + pool_w
    c_gate = c_q + x_w
    slab = r.w_up.shape[1] // FFN_SPLIT

    def ffn_up(s):
        up = _dot(r.h2[...], r.w_up[:, s * slab:(s + 1) * slab])
        return jnp.square(jnp.maximum(up, 0.0)).astype(BF16)

    def ffn_down(act, s):
        return _dot(act, r.w_down[s * slab:(s + 1) * slab, :])

    def gate(h, part):
        lo = c_gate + part * d
        return jax.nn.sigmoid(_dot(h, r.w_in[:, lo:lo + d]))

    act = ffn_up(0)
    if not decode:
        dec_q = _decode_queries(r)
    x = r.xs[...] if decode else r.xp[0]
    h = _rmsnorm(x, r.g_mix[...]).astype(BF16)
    u = _dot(h, r.w_in[:, 0:sgu_w])
    v = _dot(h, r.w_in[:, sgu_w:2 * sgu_w])
    gate_a = gate(h, 0)
    x2 = r.x1[...] + ffn_down(act, 0)

    u = jax.nn.gelu(u)
    vhat = _layernorm(jax.nn.gelu(v), r.g_v[...], r.b_v[...])
    if decode:
        rows = r.wmix_s.shape[1]
        r.cv_s[...] = vhat
        mixed = _chunk_mix(vhat.astype(BF16), r.wmix_s, rows)
        bias = _chunk_bias(r.bias_s, t)
    else:
        r.cv_p[0] = vhat[t - CHUNK:, :]
        mixed = _chunk_mix(vhat.astype(BF16), r.wmix_p, CHUNK)
        bias = _chunk_bias(r.bias_p, t)
    p = _dot(h, r.w_in[:, c_pool:c_pool + pool_w])
    if not decode:
        q = _dot(h, r.w_in[:, c_q:c_q + x_w]) * (head_dim ** -0.5)
        dec_scores = _decode_scores(r, dec_q)
    act = ffn_up(1)
    a_in = (u * (mixed + bias)).astype(BF16)

    if decode:
        n_seq = t // rows
        lane_tiles = range(pool_w // V7X_LANES)
        for lt in lane_tiles:
            r.pool_in[lt] = p[:, lt * V7X_LANES:(lt + 1) * V7X_LANES]
        ext = [r.state[k] for k in range(POOL_STATE)]
        ext += [jnp.concatenate([r.pool_in[lt, pl.ds(i, n_seq, stride=rows), :]
                                 for lt in lane_tiles], axis=-1) for i in range(rows)]
        for i in range(rows):
            cur = POOL_STATE + i

            def load_shifted(k, lt, cur=cur):
                return ext[cur - k][:, lt * V7X_LANES:(lt + 1) * V7X_LANES]

            res = _window_means_minus_self(load_shifted, ext[cur], PAST_LEN + i)
            for lt in lane_tiles:
                r.pool_out[lt, pl.ds(i, n_seq, stride=rows), :] = (
                    res[:, lt * V7X_LANES:(lt + 1) * V7X_LANES])
        for k in range(POOL_STATE):
            r.ptail_s[k] = ext[rows + k]
        pooled = jnp.concatenate([r.pool_out[lt] for lt in lane_tiles], axis=-1)
    else:
        r.ext_p[POOL_PAD:POOL_PAD + t, :] = p

        def load_shifted(k, lt):
            return r.ext_p[POOL_PAD - k:POOL_PAD - k + t, lt * V7X_LANES:(lt + 1) * V7X_LANES]

        pos = j * t + lax.broadcasted_iota(jnp.int32, (t, V7X_LANES), 0)
        pooled = _window_means_minus_self(load_shifted, p, pos)
        tail = r.ext_p[t:t + POOL_PAD, :]
        r.ext_p[0:POOL_PAD, :] = tail
        r.ptail_p[0] = tail
    pooled = pooled.astype(BF16)

    a = _dot(a_in, r.w_out_a[...])

    if not decode:
        head = lax.broadcasted_iota(jnp.int32, (t, x_w), 1) >> _log2(head_dim)
        kt = r.kt[0]
        scores = [_dot(jnp.where(head == hd, q, 0.0).astype(BF16), kt) for hd in range(X_HEADS)]
    pool_mixed = _dot(pooled, r.wpool[...]) * r.pool_scale[...]
    x2 = x2 + ffn_down(act, 1)
    if decode:
        row0 = pl.multiple_of(jnp.clip(step - n_prompt, 0, n_decode - 1) * t, t)
        o = r.attn[pl.ds(row0, t), :]
    else:
        probs = [_softmax_bf16(s) for s in scores]
        dec_probs = [_softmax_bf16(s) for s in dec_scores]
        vb = r.vb[0]
        o = jnp.zeros((t, x_w), F32)
        for hd in range(X_HEADS):
            o = jnp.where(head == hd, _dot(probs[hd], vb), o)
        _decode_values(r, dec_probs, step)
    b = _dot(pool_mixed.astype(BF16), r.w_out_b[...])
    gate_b = gate(h, 1)
    c = _dot(o.astype(BF16), r.w_out_c[...])
    act = ffn_up(2)
    gate_c = gate(h, 2)
    x2 = x2 + ffn_down(act, 2)
    merged = (gate_a * a + gate_b * b + gate_c * c).astype(BF16)
    act = ffn_up(3)
    x1 = x + _dot(merged, r.w_o[...])
    x2 = x2 + ffn_down(act, 3)
    r.x1[...] = x1
    r.h2[...] = _rmsnorm(x1, r.g_ffn[...]).astype(BF16)
    y = _rmsnorm(x2, r.g_final[...])
    if y_ref.shape[0] == 1:
        y_ref[0] = y
    else:
        y_ref[...] = y


def _load_weights_bf16(r):
    pairs = [(r.w_in_f32, r.w_in), (r.w_up_f32, r.w_up), (r.w_down_f32, r.w_down),
             (r.w_o_f32, r.w_o), (r.w_out_a_f32, r.w_out_a), (r.w_out_b_f32, r.w_out_b),
             (r.w_out_c_f32, r.w_out_c)]
    wide_cols = max(src.shape[1] for src, _ in pairs)
    narrow_cols = min(src.shape[1] for src, _ in pairs)

    def load(stage_w, stage_n, sem_w, sem_n):
        chunks = []
        used = {True: 0, False: 0}
        for src, dst in pairs:
            narrow = src.shape[1] == narrow_cols
            stage, sem = (stage_n, sem_n) if narrow else (stage_w, sem_w)
            rows = stage.shape[1]
            assert src.shape[0] % rows == 0, (src.shape, rows)
            for row0 in range(0, src.shape[0], rows):
                slot = used[narrow] % LOAD_SLOTS
                used[narrow] += 1
                landing = stage.at[slot, :, 0:src.shape[1]]
                copy = pltpu.make_async_copy(src.at[pl.ds(row0, rows), :], landing, sem.at[slot])
                chunks.append((copy, landing, dst.at[pl.ds(row0, rows), :]))
        started = 0
        for idx, (copy, landing, dst_rows) in enumerate(chunks):
            while started < min(idx + LOAD_SLOTS - 1, len(chunks)):
                chunks[started][0].start()
                started += 1
            copy.wait()
            dst_rows[...] = landing[...].astype(BF16)

    pl.run_scoped(load,
                  pltpu.VMEM((LOAD_SLOTS, LOAD_WIDE_ROWS, wide_cols), F32),
                  pltpu.VMEM((LOAD_SLOTS, LOAD_NARROW_ROWS, narrow_cols), F32),
                  pltpu.SemaphoreType.DMA((LOAD_SLOTS,)),
                  pltpu.SemaphoreType.DMA((LOAD_SLOTS,)))


def _layer_kernel(tiles_per_seq, n_prompt, *refs):
    r = _LayerRefs(*refs)
    step = pl.program_id(0)
    j = jnp.minimum(step, n_prompt - 1) & (tiles_per_seq - 1)

    @pl.when(step == 0)
    def _():
        r.x1[...] = jnp.zeros(r.x1.shape, F32)
        r.h2[...] = jnp.zeros(r.h2.shape, BF16)
        _load_weights_bf16(r)

    @pl.when(jnp.logical_and(step < n_prompt, j == 0))
    def _():
        r.ext_p[0:POOL_PAD, :] = jnp.zeros((POOL_PAD, r.ext_p.shape[1]), F32)

    @pl.when(step < n_prompt)
    def _():
        _layer_step(r, False, r.yp, j, step, n_prompt)

    @pl.when(step == n_prompt)
    def _():
        _layer_step(r, True, r.yp, j, step, n_prompt)

    @pl.when(step > n_prompt)
    def _():
        _layer_step(r, True, r.ys, j, step, n_prompt)


def _mem_kv(mem, g_mem, w_kv):
    bsz, n_mem, d = mem.shape
    xw = w_kv.shape[-1] // 2
    return pl.pallas_call(
        _mem_kv_kernel,
        grid=(bsz,),
        in_specs=[pl.BlockSpec((1, n_mem, d), lambda i: (i, 0, 0)),
                  _const_spec((1, d)), _const_spec((d, 2 * xw))],
        out_specs=[pl.BlockSpec((1, xw, n_mem), lambda i: (i, 0, 0)),
                   pl.BlockSpec((1, xw, n_mem), lambda i: (i, 0, 0)),
                   pl.BlockSpec((1, xw, n_mem), lambda i: (i, 0, 0)),
                   pl.BlockSpec((1, n_mem, xw), lambda i: (i, 0, 0))],
        out_shape=[jax.ShapeDtypeStruct((bsz, xw, n_mem), F32),
                   jax.ShapeDtypeStruct((bsz, xw, n_mem), F32),
                   jax.ShapeDtypeStruct((bsz, xw, n_mem), BF16),
                   jax.ShapeDtypeStruct((bsz, n_mem, xw), BF16)],
        compiler_params=pltpu.CompilerParams(dimension_semantics=("arbitrary",)),
        name="mem_kv",
    )(mem, g_mem.reshape(1, d).astype(F32), w_kv.astype(BF16))


def _spatial_operands(w_s, b_s, group_dim, chunk_rows, tile_rows):
    wmix = jnp.tile(w_s[:, :chunk_rows, :chunk_rows], (1, 1, tile_rows // chunk_rows))
    bias = jnp.repeat(b_s[:, :chunk_rows].T, group_dim, axis=1)
    return wmix.astype(F32), bias.astype(F32)


def _layer(x_prompt, x2d, ktb, vb, cache_kt, cache_vt, state, lw):
    (g_mix, w_in, g_v, b_v, w_s, b_s, w_pool, pool_scale,
     w_out_a, w_out_b, w_out_c, w_o, g_ffn, w_up, w_down, g_final) = lw
    bsz, seq, d = x_prompt.shape
    n_seq = state.shape[0]
    rows = x2d.shape[0] // n_seq
    t = TILE
    ts = t // rows
    tps = seq // t
    n_prompt = bsz * tps
    n_decode = n_seq // ts
    assert n_seq % n_prompt == 0, (n_seq, n_prompt)
    sps = n_seq // n_prompt
    sgu_w = g_v.shape[-1]
    pool_w = pool_scale.shape[-1]
    n_mem, xw = vb.shape[1:]
    group_dim = sgu_w // w_s.shape[0]
    row2 = lambda vec: vec.reshape(1, -1).astype(F32)
    wmix_p, bias_p = _spatial_operands(w_s, b_s, group_dim, CHUNK, t)
    wmix_s, bias_s = _spatial_operands(w_s, b_s, group_dim, rows, t)
    wpool = jax.scipy.linalg.block_diag(*[w_pool[i] for i in range(w_pool.shape[0])])
    state_rows = jnp.transpose(state, (1, 0, 2))
    big = [w_in, w_out_a, w_out_b, w_out_c, w_o, w_up, w_down]
    weights = [row2(g_mix), w_in, row2(g_v), row2(b_v), wmix_p, bias_p, wmix_s, bias_s,
               wpool.astype(BF16), row2(pool_scale), w_out_a, w_out_b, w_out_c, w_o,
               row2(g_ffn), w_up, w_down, row2(g_final)]

    def prompt_tile(m):
        return jnp.clip(m, 0, n_prompt - 1)

    def prompt_seq(m):
        return lax.shift_right_logical(prompt_tile(m), _log2(tps))

    def decode_tile(m):
        return jnp.clip(m - n_prompt, 0, n_decode - 1)

    yp, ys, cv_p, ptail_p, cv_s, ptail_s = pl.pallas_call(
        functools.partial(_layer_kernel, tps, n_prompt),
        grid=(n_prompt + n_decode + 1,),
        in_specs=[pl.BlockSpec((1, t, d), lambda s: (prompt_tile(s), 0, 0)),
                  pl.BlockSpec((t, d), lambda s: (decode_tile(s), 0)),
                  pl.BlockSpec((1, xw, n_mem), lambda s: (prompt_seq(s), 0, 0)),
                  pl.BlockSpec((1, n_mem, xw), lambda s: (prompt_seq(s), 0, 0)),
                  pl.BlockSpec((sps * rows, d), lambda s: (prompt_tile(s), 0)),
                  pl.BlockSpec((sps, xw, n_mem), lambda s: (prompt_tile(s), 0, 0)),
                  pl.BlockSpec((sps, xw, n_mem), lambda s: (prompt_tile(s), 0, 0)),
                  pl.BlockSpec((POOL_STATE, ts, pool_w), lambda s: (0, decode_tile(s), 0))]
                 + [pl.BlockSpec(memory_space=pl.ANY) if any(w is b for b in big)
                    else _const_spec(w.shape) for w in weights],
        out_specs=[pl.BlockSpec((1, t, d), lambda s: (prompt_tile(s - 1), 0, 0)),
                   pl.BlockSpec((t, d), lambda s: (decode_tile(s - 1), 0)),
                   pl.BlockSpec((1, CHUNK, sgu_w), lambda s: (prompt_seq(s), 0, 0)),
                   pl.BlockSpec((1, POOL_PAD, pool_w), lambda s: (prompt_seq(s), 0, 0)),
                   pl.BlockSpec((t, sgu_w), lambda s: (decode_tile(s), 0)),
                   pl.BlockSpec((POOL_STATE, ts, pool_w), lambda s: (0, decode_tile(s), 0))],
        out_shape=[jax.ShapeDtypeStruct((n_prompt, t, d), F32),
                   jax.ShapeDtypeStruct((n_seq * rows, d), F32),
                   jax.ShapeDtypeStruct((bsz, CHUNK, sgu_w), F32),
                   jax.ShapeDtypeStruct((bsz, POOL_PAD, pool_w), F32),
                   jax.ShapeDtypeStruct((n_seq * rows, sgu_w), F32),
                   jax.ShapeDtypeStruct((POOL_STATE, n_seq, pool_w), F32)],
        scratch_shapes=[pltpu.VMEM((POOL_PAD + t, pool_w), F32),
                        pltpu.VMEM((pool_w // V7X_LANES, t, V7X_LANES), F32),
                        pltpu.VMEM((pool_w // V7X_LANES, t, V7X_LANES), F32),
                        pltpu.VMEM((t, d), F32),
                        pltpu.VMEM((t, d), BF16)]
                       + [pltpu.VMEM(w.shape, BF16) for w in big]
                       + [pltpu.VMEM((n_seq * rows, xw), F32)],
        compiler_params=pltpu.CompilerParams(
            dimension_semantics=("arbitrary",),
            vmem_limit_bytes=VMEM_LIMIT_BYTES),
        name="layer",
    )(x_prompt.reshape(n_prompt, t, d), x2d, ktb, vb, x2d, cache_kt, cache_vt, state_rows, *weights)
    return (yp.reshape(bsz, seq, d), ys.reshape(n_seq, rows, d), cv_p, ptail_p[:, POOL_PAD - POOL_STATE:, :],
            cv_s.reshape(n_seq, rows, sgu_w), jnp.transpose(ptail_s, (1, 0, 2)))


def _feature_major(kv):
    bsz, n_mem, heads, head_dim = kv.shape
    return jnp.transpose(kv, (0, 2, 3, 1)).reshape(bsz, heads * head_dim, n_mem)


def _token_major(kvt, heads):
    bsz, xw, n_mem = kvt.shape
    return jnp.transpose(kvt.reshape(bsz, heads, xw // heads, n_mem), (0, 3, 1, 2))


def kernel(x_prompt, x_sample, mem_prompt, cache_mem_k, cache_mem_v, state_pool, g_mix, w_in, g_v, b_v, w_s, b_s, w_pool, pool_scale, g_mem, w_kv, w_out_a, w_out_b, w_out_c, w_o, g_ffn, w_up, w_down, g_final):
    depth = w_in.shape[0]
    assert depth == 1, "the final rmsnorm is fused into the (single) layer kernel"
    heads = cache_mem_k.shape[-2]
    n_seq, rows, d = x_sample.shape
    lw = (g_mix[0], w_in[0], g_v[0], b_v[0], w_s[0], b_s[0], w_pool[0], pool_scale[0],
          w_out_a[0], w_out_b[0], w_out_c[0], w_o[0], g_ffn[0], w_up[0], w_down[0], g_final)
    mem_kt, mem_vt, ktb, vb = _mem_kv(mem_prompt, g_mem[0], w_kv[0])
    x2d = x_sample.reshape(n_seq * rows, d)
    y_prompt, y_sample, cv_p, ptail_p, cv_s, ptail_s = _layer(
        x_prompt, x2d, ktb, vb, _feature_major(cache_mem_k[0]), _feature_major(cache_mem_v[0]),
        state_pool[0], lw)
    return (y_prompt, y_sample, _token_major(mem_kt, heads)[None], _token_major(mem_vt, heads)[None],
            ptail_p[None], ptail_s[None], cv_p[None], cv_s[None])
```

```python
import functools
from typing import NamedTuple

import jax
import jax.numpy as jnp
from jax import lax
from jax.experimental import pallas as pl
from jax.experimental.pallas import tpu as pltpu

PAST_LEN = 16384
CHUNK = 128
POOL_WINDOWS = (2, 4, 8, 16)
POOL_STATE = max(POOL_WINDOWS) - 1
X_HEADS = 4
EPS = 1e-6

V7X_VMEM_BYTES = 64 * 1024 * 1024
V7X_LANES = 128
VMEM_LIMIT_BYTES = V7X_VMEM_BYTES - 6 * 1024 * 1024

TILE = 256
FFN_SPLIT = 4
LOAD_SLOTS = 8
LOAD_WIDE_ROWS = 32
LOAD_NARROW_ROWS = 128
POOL_PAD = POOL_STATE + 1

BF16 = jnp.bfloat16
F32 = jnp.float32


def _dot(a, b):
    return jnp.dot(a, b, preferred_element_type=F32)


def _log2(n):
    assert n > 0 and n & (n - 1) == 0, n
    return n.bit_length() - 1


def _rmsnorm(x, g):
    return x * lax.rsqrt(jnp.mean(x * x, axis=-1, keepdims=True) + EPS) * g


def _layernorm(x, g, b):
    xc = x - jnp.mean(x, axis=-1, keepdims=True)
    return xc * lax.rsqrt(jnp.mean(xc * xc, axis=-1, keepdims=True) + EPS) * g + b


def _softmax_bf16(s):
    e = jnp.exp(s - jnp.max(s, axis=-1, keepdims=True))
    return (e * (1.0 / jnp.sum(e, axis=-1, keepdims=True))).astype(BF16)


def _const_spec(shape):
    zeros = (0,) * len(shape)
    return pl.BlockSpec(shape, lambda *_: zeros, pipeline_mode=pl.Buffered(1))


def _chunk_mix(vhat_b, wmix_ref, chunk_rows):
    t, sgu_w = vhat_b.shape
    n_groups = wmix_ref.shape[0]
    group_dim = sgu_w // n_groups
    shift = _log2(chunk_rows)
    reps = t // chunk_rows
    row = lax.broadcasted_iota(jnp.int32, (t, t), 0)
    col = lax.broadcasted_iota(jnp.int32, (t, t), 1)
    same_chunk = (row >> shift) == (col >> shift)
    causal = col <= row
    parts = []
    for g in range(n_groups):
        w = jnp.concatenate([wmix_ref[g]] * reps, axis=0)
        w = jnp.where(same_chunk, jnp.where(causal, w, 0.0), 0.0).astype(BF16)
        parts.append(_dot(w, vhat_b[:, g * group_dim:(g + 1) * group_dim]))
    return jnp.concatenate(parts, axis=1)


def _chunk_bias(bias_ref, t):
    return jnp.concatenate([bias_ref[...]] * (t // bias_ref.shape[0]), axis=0)


def _window_means_minus_self(load_shifted, p, pos):
    group_lanes = p.shape[-1] // len(POOL_WINDOWS)
    outs = []
    for lt in range(p.shape[-1] // V7X_LANES):
        p_lt = p[..., lt * V7X_LANES:(lt + 1) * V7X_LANES]
        lane = lax.broadcasted_iota(jnp.int32, p_lt.shape, p_lt.ndim - 1)
        wins = POOL_WINDOWS[lt * V7X_LANES // group_lanes:(lt + 1) * V7X_LANES // group_lanes]
        acc = p_lt
        sums = {1: p_lt}
        for k in range(1, max(wins)):
            acc = acc + load_shifted(k, lt)
            sums[k + 1] = acc
        win_sum = sums[wins[-1]]
        win = jnp.full(p_lt.shape, wins[-1], jnp.int32)
        for gi in range(len(wins) - 2, -1, -1):
            in_group = lane < (gi + 1) * group_lanes
            win_sum = jnp.where(in_group, sums[wins[gi]], win_sum)
            win = jnp.where(in_group, wins[gi], win)
        cnt = jnp.minimum(pos + 1, win).astype(F32)
        outs.append(win_sum / cnt - p_lt)
    return jnp.concatenate(outs, axis=-1)


def _mem_kv_kernel(mem_ref, g_mem_ref, w_kv_ref, kt_ref, vt_ref, ktb_ref, vb_ref):
    xw = kt_ref.shape[1]
    hn = _rmsnorm(mem_ref[0], g_mem_ref[...]).astype(BF16)
    kv = _dot(hn, w_kv_ref[...])
    kt = kv[:, 0:xw].T
    v = kv[:, xw:2 * xw]
    kt_ref[0] = kt
    vt_ref[0] = v.T
    ktb_ref[0] = kt.astype(BF16)
    vb_ref[0] = v.astype(BF16)


def _decode_queries(r):
    x_w = r.ckt.shape[1]
    c_q = r.w_in.shape[1] - 3 * r.xq.shape[1] - x_w
    h = _rmsnorm(r.xq[...], r.g_mix[...]).astype(BF16)
    return _dot(h, r.w_in[:, c_q:c_q + x_w]) * ((x_w // X_HEADS) ** -0.5)


def _decode_own_head(r):
    n_seq, x_w, _ = r.ckt.shape
    rows = r.xq.shape[0] // n_seq
    stack = X_HEADS * rows
    lane_head = lax.broadcasted_iota(jnp.int32, (stack, x_w), 1) >> _log2(x_w // X_HEADS)
    row_head = lax.broadcasted_iota(jnp.int32, (stack, x_w), 0) >> _log2(rows)
    return lane_head == row_head


def _decode_scores(r, q):
    n_seq = r.ckt.shape[0]
    rows = r.xq.shape[0] // n_seq
    own_head = _decode_own_head(r)
    scores = []
    for i in range(n_seq):
        qs = jnp.concatenate([q[i * rows:(i + 1) * rows, :]] * X_HEADS, axis=0)
        qm = jnp.where(own_head, qs, 0.0).astype(BF16)
        scores.append(_dot(qm, r.ckt[i].astype(BF16)))
    return scores


def _decode_values(r, probs, step):
    n_seq = r.cvt.shape[0]
    rows = r.xq.shape[0] // n_seq
    own_head = _decode_own_head(r)
    for i in range(n_seq):
        of = lax.dot_general(probs[i], r.cvt[i].astype(BF16),
                             (((1,), (1,)), ((), ())), preferred_element_type=F32)
        of = jnp.where(own_head, of, 0.0)
        out = of[0:rows]
        for hd in range(1, X_HEADS):
            out = out + of[hd * rows:(hd + 1) * rows]
        row0 = pl.multiple_of((step * n_seq + i) * rows, rows)
        r.attn[pl.ds(row0, rows), :] = out


class _LayerRefs(NamedTuple):
    xp: object
    xs: object
    kt: object
    vb: object
    xq: object
    ckt: object
    cvt: object
    state: object
    g_mix: object
    w_in_f32: object
    g_v: object
    b_v: object
    wmix_p: object
    bias_p: object
    wmix_s: object
    bias_s: object
    wpool: object
    pool_scale: object
    w_out_a_f32: object
    w_out_b_f32: object
    w_out_c_f32: object
    w_o_f32: object
    g_ffn: object
    w_up_f32: object
    w_down_f32: object
    g_final: object
    yp: object
    ys: object
    cv_p: object
    ptail_p: object
    cv_s: object
    ptail_s: object
    ext_p: object
    pool_in: object
    pool_out: object
    x1: object
    h2: object
    w_in: object
    w_out_a: object
    w_out_b: object
    w_out_c: object
    w_o: object
    w_up: object
    w_down: object
    attn: object


def _layer_step(r, decode, y_ref, j, step, n_prompt):
    t, d = r.x1.shape
    sgu_w = r.g_v.shape[-1]
    pool_w = r.pool_scale.shape[-1]
    x_w = r.attn.shape[-1]
    n_decode = r.attn.shape[0] // t
    head_dim = x_w // X_HEADS
    c_pool = 2 * sgu_w
    c_q = c_pool + pool_w
    c_gate = c_q + x_w

    def ffn_up(s):
        return None if y_ref is None else _ffn_up(r, s)

    def ffn_down(act, s):
        return 0.0 if y_ref is None else _ffn_down(r, act, s)

    def gate(h, part):
        lo = c_gate + part * d
        return jax.nn.sigmoid(_dot(h, r.w_in[:, lo:lo + d]))

    act = ffn_up(0)
    if not decode:
        dec_q = _decode_queries(r)
    x = r.xs[...] if decode else r.xp[0]
    h = _rmsnorm(x, r.g_mix[...]).astype(BF16)
    u = _dot(h, r.w_in[:, 0:sgu_w])
    v = _dot(h, r.w_in[:, sgu_w:2 * sgu_w])
    gate_a = gate(h, 0)
    x2 = ffn_down(act, 0) if y_ref is None else r.x1[...] + ffn_down(act, 0)

    u = jax.nn.gelu(u)
    vhat = _layernorm(jax.nn.gelu(v), r.g_v[...], r.b_v[...])
    if decode:
        rows = r.wmix_s.shape[1]
        r.cv_s[...] = vhat
        mixed = _chunk_mix(vhat.astype(BF16), r.wmix_s, rows)
        bias = _chunk_bias(r.bias_s, t)
    else:
        r.cv_p[0] = vhat[t - CHUNK:, :]
        mixed = _chunk_mix(vhat.astype(BF16), r.wmix_p, CHUNK)
        bias = _chunk_bias(r.bias_p, t)
    p = _dot(h, r.w_in[:, c_pool:c_pool + pool_w])
    if not decode:
        q = _dot(h, r.w_in[:, c_q:c_q + x_w]) * (head_dim ** -0.5)
        dec_scores = _decode_scores(r, dec_q)
    act = ffn_up(1)
    a_in = (u * (mixed + bias)).astype(BF16)

    if decode:
        n_seq = t // rows
        lane_tiles = range(pool_w // V7X_LANES)
        for lt in lane_tiles:
            r.pool_in[lt] = p[:, lt * V7X_LANES:(lt + 1) * V7X_LANES]
        ext = [r.state[k] for k in range(POOL_STATE)]
        ext += [jnp.concatenate([r.pool_in[lt, pl.ds(i, n_seq, stride=rows), :]
                                 for lt in lane_tiles], axis=-1) for i in range(rows)]
        for i in range(rows):
            cur = POOL_STATE + i

            def load_shifted(k, lt, cur=cur):
                return ext[cur - k][:, lt * V7X_LANES:(lt + 1) * V7X_LANES]

            res = _window_means_minus_self(load_shifted, ext[cur], PAST_LEN + i)
            for lt in lane_tiles:
                r.pool_out[lt, pl.ds(i, n_seq, stride=rows), :] = (
                    res[:, lt * V7X_LANES:(lt + 1) * V7X_LANES])
        for k in range(POOL_STATE):
            r.ptail_s[k] = ext[rows + k]
        pooled = jnp.concatenate([r.pool_out[lt] for lt in lane_tiles], axis=-1)
    else:
        r.ext_p[POOL_PAD:POOL_PAD + t, :] = p

        def load_shifted(k, lt):
            return r.ext_p[POOL_PAD - k:POOL_PAD - k + t, lt * V7X_LANES:(lt + 1) * V7X_LANES]

        pos = j * t + lax.broadcasted_iota(jnp.int32, (t, V7X_LANES), 0)
        pooled = _window_means_minus_self(load_shifted, p, pos)
        tail = r.ext_p[t:t + POOL_PAD, :]
        r.ext_p[0:POOL_PAD, :] = tail
        r.ptail_p[0] = tail
    pooled = pooled.astype(BF16)

    a = _dot(a_in, r.w_out_a[...])

    if not decode:
        head = lax.broadcasted_iota(jnp.int32, (t, x_w), 1) >> _log2(head_dim)
        kt = r.kt[0]
        scores = [_dot(jnp.where(head == hd, q, 0.0).astype(BF16), kt) for hd in range(X_HEADS)]
    pool_mixed = _dot(pooled, r.wpool[...]) * r.pool_scale[...]
    x2 = x2 + ffn_down(act, 1)
    if decode:
        row0 = pl.multiple_of(jnp.clip(step - n_prompt, 0, n_decode - 1) * t, t)
        o = r.attn[pl.ds(row0, t), :]
    else:
        probs = [_softmax_bf16(s) for s in scores]
        dec_probs = [_softmax_bf16(s) for s in dec_scores]
        vb = r.vb[0]
        o = jnp.zeros((t, x_w), F32)
        for hd in range(X_HEADS):
            o = jnp.where(head == hd, _dot(probs[hd], vb), o)
        _decode_values(r, dec_probs, step)
    b = _dot(pool_mixed.astype(BF16), r.w_out_b[...])
    gate_b = gate(h, 1)
    c = _dot(o.astype(BF16), r.w_out_c[...])
    act = ffn_up(2)
    gate_c = gate(h, 2)
    x2 = x2 + ffn_down(act, 2)
    merged = (gate_a * a + gate_b * b + gate_c * c).astype(BF16)
    act = ffn_up(3)
    x1 = x + _dot(merged, r.w_o[...])
    x2 = x2 + ffn_down(act, 3)
    r.x1[...] = x1
    r.h2[...] = _rmsnorm(x1, r.g_ffn[...]).astype(BF16)
    if y_ref is not None:
        _store_final(r, x2, y_ref)


def _ffn_up(r, s):
    slab = r.w_up.shape[1] // FFN_SPLIT
    up = _dot(r.h2[...], r.w_up[:, s * slab:(s + 1) * slab])
    return jnp.square(jnp.maximum(up, 0.0)).astype(BF16)


def _ffn_down(r, act, s):
    slab = r.w_up.shape[1] // FFN_SPLIT
    return _dot(act, r.w_down[s * slab:(s + 1) * slab, :])


def _store_final(r, x2, y_ref):
    y = _rmsnorm(x2, r.g_final[...])
    if y_ref.shape[0] == 1:
        y_ref[0] = y
    else:
        y_ref[...] = y


def _drain_step(r, y_ref):
    x2 = r.x1[...]
    act = _ffn_up(r, 0)
    for s in range(FFN_SPLIT):
        nxt = _ffn_up(r, s + 1) if s + 1 < FFN_SPLIT else None
        x2 = x2 + _ffn_down(r, act, s)
        act = nxt
    _store_final(r, x2, y_ref)


def _load_weights_bf16(r):
    pairs = [(r.w_in_f32, r.w_in), (r.w_up_f32, r.w_up), (r.w_down_f32, r.w_down),
             (r.w_o_f32, r.w_o), (r.w_out_a_f32, r.w_out_a), (r.w_out_b_f32, r.w_out_b),
             (r.w_out_c_f32, r.w_out_c)]
    wide_cols = max(src.shape[1] for src, _ in pairs)
    narrow_cols = min(src.shape[1] for src, _ in pairs)

    def load(stage_w, stage_n, sem_w, sem_n):
        chunks = []
        used = {True: 0, False: 0}
        for src, dst in pairs:
            narrow = src.shape[1] == narrow_cols
            stage, sem = (stage_n, sem_n) if narrow else (stage_w, sem_w)
            rows = stage.shape[1]
            assert src.shape[0] % rows == 0, (src.shape, rows)
            for row0 in range(0, src.shape[0], rows):
                slot = used[narrow] % LOAD_SLOTS
                used[narrow] += 1
                landing = stage.at[slot, :, 0:src.shape[1]]
                copy = pltpu.make_async_copy(src.at[pl.ds(row0, rows), :], landing, sem.at[slot])
                chunks.append((copy, landing, dst.at[pl.ds(row0, rows), :]))
        started = 0
        for idx, (copy, landing, dst_rows) in enumerate(chunks):
            while started < min(idx + LOAD_SLOTS - 1, len(chunks)):
                chunks[started][0].start()
                started += 1
            copy.wait()
            dst_rows[...] = landing[...].astype(BF16)

    pl.run_scoped(load,
                  pltpu.VMEM((LOAD_SLOTS, LOAD_WIDE_ROWS, wide_cols), F32),
                  pltpu.VMEM((LOAD_SLOTS, LOAD_NARROW_ROWS, narrow_cols), F32),
                  pltpu.SemaphoreType.DMA((LOAD_SLOTS,)),
                  pltpu.SemaphoreType.DMA((LOAD_SLOTS,)))


def _layer_kernel(tiles_per_seq, n_prompt, *refs):
    r = _LayerRefs(*refs)
    step = pl.program_id(0)
    last = pl.num_programs(0) - 1
    j = jnp.minimum(step, n_prompt - 1) & (tiles_per_seq - 1)

    @pl.when(step == 0)
    def _():
        _load_weights_bf16(r)

    @pl.when(jnp.logical_and(step < n_prompt, j == 0))
    def _():
        r.ext_p[0:POOL_PAD, :] = jnp.zeros((POOL_PAD, r.ext_p.shape[1]), F32)

    @pl.when(step == 0)
    def _():
        _layer_step(r, False, None, j, step, n_prompt)

    @pl.when(jnp.logical_and(step > 0, step < n_prompt))
    def _():
        _layer_step(r, False, r.yp, j, step, n_prompt)

    @pl.when(step == n_prompt)
    def _():
        _layer_step(r, True, r.yp, j, step, n_prompt)

    @pl.when(jnp.logical_and(step > n_prompt, step < last))
    def _():
        _layer_step(r, True, r.ys, j, step, n_prompt)

    @pl.when(step == last)
    def _():
        _drain_step(r, r.ys)


def _mem_kv(mem, g_mem, w_kv):
    bsz, n_mem, d = mem.shape
    xw = w_kv.shape[-1] // 2
    return pl.pallas_call(
        _mem_kv_kernel,
        grid=(bsz,),
        in_specs=[pl.BlockSpec((1, n_mem, d), lambda i: (i, 0, 0)),
                  _const_spec((1, d)), _const_spec((d, 2 * xw))],
        out_specs=[pl.BlockSpec((1, xw, n_mem), lambda i: (i, 0, 0)),
                   pl.BlockSpec((1, xw, n_mem), lambda i: (i, 0, 0)),
                   pl.BlockSpec((1, xw, n_mem), lambda i: (i, 0, 0)),
                   pl.BlockSpec((1, n_mem, xw), lambda i: (i, 0, 0))],
        out_shape=[jax.ShapeDtypeStruct((bsz, xw, n_mem), F32),
                   jax.ShapeDtypeStruct((bsz, xw, n_mem), F32),
                   jax.ShapeDtypeStruct((bsz, xw, n_mem), BF16),
                   jax.ShapeDtypeStruct((bsz, n_mem, xw), BF16)],
        compiler_params=pltpu.CompilerParams(dimension_semantics=("arbitrary",)),
        name="mem_kv",
    )(mem, g_mem.reshape(1, d).astype(F32), w_kv.astype(BF16))


def _spatial_operands(w_s, b_s, group_dim, chunk_rows, tile_rows):
    wmix = jnp.tile(w_s[:, :chunk_rows, :chunk_rows], (1, 1, tile_rows // chunk_rows))
    bias = jnp.repeat(b_s[:, :chunk_rows].T, group_dim, axis=1)
    return wmix.astype(F32), bias.astype(F32)


def _layer(x_prompt, x2d, ktb, vb, cache_kt, cache_vt, state, lw):
    (g_mix, w_in, g_v, b_v, w_s, b_s, w_pool, pool_scale,
     w_out_a, w_out_b, w_out_c, w_o, g_ffn, w_up, w_down, g_final) = lw
    bsz, seq, d = x_prompt.shape
    n_seq = state.shape[0]
    rows = x2d.shape[0] // n_seq
    t = TILE
    ts = t // rows
    tps = seq // t
    n_prompt = bsz * tps
    n_decode = n_seq // ts
    assert n_seq % n_prompt == 0, (n_seq, n_prompt)
    sps = n_seq // n_prompt
    sgu_w = g_v.shape[-1]
    pool_w = pool_scale.shape[-1]
    n_mem, xw = vb.shape[1:]
    group_dim = sgu_w // w_s.shape[0]
    row2 = lambda vec: vec.reshape(1, -1).astype(F32)
    wmix_p, bias_p = _spatial_operands(w_s, b_s, group_dim, CHUNK, t)
    wmix_s, bias_s = _spatial_operands(w_s, b_s, group_dim, rows, t)
    wpool = jax.scipy.linalg.block_diag(*[w_pool[i] for i in range(w_pool.shape[0])])
    state_rows = jnp.transpose(state, (1, 0, 2))
    big = [w_in, w_out_a, w_out_b, w_out_c, w_o, w_up, w_down]
    weights = [row2(g_mix), w_in, row2(g_v), row2(b_v), wmix_p, bias_p, wmix_s, bias_s,
               wpool.astype(BF16), row2(pool_scale), w_out_a, w_out_b, w_out_c, w_o,
               row2(g_ffn), w_up, w_down, row2(g_final)]

    def prompt_tile(m):
        return jnp.clip(m, 0, n_prompt - 1)

    def prompt_seq(m):
        return lax.shift_right_logical(prompt_tile(m), _log2(tps))

    def decode_tile(m):
        return jnp.clip(m - n_prompt, 0, n_decode - 1)

    yp, ys, cv_p, ptail_p, cv_s, ptail_s = pl.pallas_call(
        functools.partial(_layer_kernel, tps, n_prompt),
        grid=(n_prompt + n_decode + 1,),
        in_specs=[pl.BlockSpec((1, t, d), lambda s: (prompt_tile(s), 0, 0)),
                  pl.BlockSpec((t, d), lambda s: (decode_tile(s), 0)),
                  pl.BlockSpec((1, xw, n_mem), lambda s: (prompt_seq(s), 0, 0)),
                  pl.BlockSpec((1, n_mem, xw), lambda s: (prompt_seq(s), 0, 0)),
                  pl.BlockSpec((sps * rows, d), lambda s: (prompt_tile(s), 0)),
                  pl.BlockSpec((sps, xw, n_mem), lambda s: (prompt_tile(s), 0, 0)),
                  pl.BlockSpec((sps, xw, n_mem), lambda s: (prompt_tile(s), 0, 0)),
                  pl.BlockSpec((POOL_STATE, ts, pool_w), lambda s: (0, decode_tile(s), 0))]
                 + [pl.BlockSpec(memory_space=pl.ANY) if any(w is b for b in big)
                    else _const_spec(w.shape) for w in weights],
        out_specs=[pl.BlockSpec((1, t, d), lambda s: (prompt_tile(s - 1), 0, 0)),
                   pl.BlockSpec((t, d), lambda s: (decode_tile(s - 1), 0)),
                   pl.BlockSpec((1, CHUNK, sgu_w), lambda s: (prompt_seq(s), 0, 0)),
                   pl.BlockSpec((1, POOL_PAD, pool_w), lambda s: (prompt_seq(s), 0, 0)),
                   pl.BlockSpec((t, sgu_w), lambda s: (decode_tile(s), 0)),
                   pl.BlockSpec((POOL_STATE, ts, pool_w), lambda s: (0, decode_tile(s), 0))],
        out_shape=[jax.ShapeDtypeStruct((n_prompt, t, d), F32),
                   jax.ShapeDtypeStruct((n_seq * rows, d), F32),
                   jax.ShapeDtypeStruct((bsz, CHUNK, sgu_w), F32),
                   jax.ShapeDtypeStruct((bsz, POOL_PAD, pool_w), F32),
                   jax.ShapeDtypeStruct((n_seq * rows, sgu_w), F32),
                   jax.ShapeDtypeStruct((POOL_STATE, n_seq, pool_w), F32)],
        scratch_shapes=[pltpu.VMEM((POOL_PAD + t, pool_w), F32),
                        pltpu.VMEM((pool_w // V7X_LANES, t, V7X_LANES), F32),
                        pltpu.VMEM((pool_w // V7X_LANES, t, V7X_LANES), F32),
                        pltpu.VMEM((t, d), F32),
                        pltpu.VMEM((t, d), BF16)]
                       + [pltpu.VMEM(w.shape, BF16) for w in big]
                       + [pltpu.VMEM((n_seq * rows, xw), F32)],
        compiler_params=pltpu.CompilerParams(
            dimension_semantics=("arbitrary",),
            vmem_limit_bytes=VMEM_LIMIT_BYTES),
        name="layer",
    )(x_prompt.reshape(n_prompt, t, d), x2d, ktb, vb, x2d, cache_kt, cache_vt, state_rows, *weights)
    return (yp.reshape(bsz, seq, d), ys.reshape(n_seq, rows, d), cv_p, ptail_p[:, POOL_PAD - POOL_STATE:, :],
            cv_s.reshape(n_seq, rows, sgu_w), jnp.transpose(ptail_s, (1, 0, 2)))


def _feature_major(kv):
    bsz, n_mem, heads, head_dim = kv.shape
    return jnp.transpose(kv, (0, 2, 3, 1)).reshape(bsz, heads * head_dim, n_mem)


def _token_major(kvt, heads):
    bsz, xw, n_mem = kvt.shape
    return jnp.transpose(kvt.reshape(bsz, heads, xw // heads, n_mem), (0, 3, 1, 2))


def kernel(x_prompt, x_sample, mem_prompt, cache_mem_k, cache_mem_v, state_pool, g_mix, w_in, g_v, b_v, w_s, b_s, w_pool, pool_scale, g_mem, w_kv, w_out_a, w_out_b, w_out_c, w_o, g_ffn, w_up, w_down, g_final):
    depth = w_in.shape[0]
    assert depth == 1, "the final rmsnorm is fused into the (single) layer kernel"
    heads = cache_mem_k.shape[-2]
    n_seq, rows, d = x_sample.shape
    lw = (g_mix[0], w_in[0], g_v[0], b_v[0], w_s[0], b_s[0], w_pool[0], pool_scale[0],
          w_out_a[0], w_out_b[0], w_out_c[0], w_o[0], g_ffn[0], w_up[0], w_down[0], g_final)
    mem_kt, mem_vt, ktb, vb = _mem_kv(mem_prompt, g_mem[0], w_kv[0])
    x2d = x_sample.reshape(n_seq * rows, d)
    y_prompt, y_sample, cv_p, ptail_p, cv_s, ptail_s = _layer(
        x_prompt, x2d, ktb, vb, _feature_major(cache_mem_k[0]), _feature_major(cache_mem_v[0]),
        state_pool[0], lw)
    return (y_prompt, y_sample, _token_major(mem_kt, heads)[None], _token_major(mem_vt, heads)[None],
            ptail_p[None], ptail_s[None], cv_p[None], cv_s[None])
```

```python
import functools
from typing import NamedTuple

import jax
import jax.numpy as jnp
from jax import lax
from jax.experimental import pallas as pl
from jax.experimental.pallas import tpu as pltpu

PAST_LEN = 16384
CHUNK = 128
POOL_WINDOWS = (2, 4, 8, 16)
POOL_STATE = max(POOL_WINDOWS) - 1
X_HEADS = 4
EPS = 1e-6

V7X_VMEM_BYTES = 64 * 1024 * 1024
V7X_LANES = 128
VMEM_LIMIT_BYTES = V7X_VMEM_BYTES - 6 * 1024 * 1024

TILE = 256
FFN_SPLIT = 4
LOAD_SLOTS = 8
LOAD_WIDE_ROWS = 32
LOAD_NARROW_ROWS = 128
POOL_PAD = POOL_STATE + 1

BF16 = jnp.bfloat16
F32 = jnp.float32


def _dot(a, b):
    return jnp.dot(a, b, preferred_element_type=F32)


def _log2(n):
    assert n > 0 and n & (n - 1) == 0, n
    return n.bit_length() - 1


def _rmsnorm(x, g):
    return x * lax.rsqrt(jnp.mean(x * x, axis=-1, keepdims=True) + EPS) * g


def _layernorm(x, g, b):
    xc = x - jnp.mean(x, axis=-1, keepdims=True)
    return xc * lax.rsqrt(jnp.mean(xc * xc, axis=-1, keepdims=True) + EPS) * g + b


def _softmax_bf16(s):
    e = jnp.exp(s - jnp.max(s, axis=-1, keepdims=True))
    return (e * (1.0 / jnp.sum(e, axis=-1, keepdims=True))).astype(BF16)


def _const_spec(shape):
    zeros = (0,) * len(shape)
    return pl.BlockSpec(shape, lambda *_: zeros, pipeline_mode=pl.Buffered(1))


def _chunk_mix(vhat_b, wmix_ref, chunk_rows):
    t, sgu_w = vhat_b.shape
    n_groups = wmix_ref.shape[0]
    group_dim = sgu_w // n_groups
    shift = _log2(chunk_rows)
    reps = t // chunk_rows
    row = lax.broadcasted_iota(jnp.int32, (t, t), 0)
    col = lax.broadcasted_iota(jnp.int32, (t, t), 1)
    same_chunk = (row >> shift) == (col >> shift)
    causal = col <= row
    parts = []
    for g in range(n_groups):
        w = jnp.concatenate([wmix_ref[g]] * reps, axis=0)
        w = jnp.where(same_chunk, jnp.where(causal, w, 0.0), 0.0).astype(BF16)
        parts.append(_dot(w, vhat_b[:, g * group_dim:(g + 1) * group_dim]))
    return jnp.concatenate(parts, axis=1)


def _chunk_bias(bias_ref, t):
    return jnp.concatenate([bias_ref[...]] * (t // bias_ref.shape[0]), axis=0)


def _window_means_minus_self(load_shifted, p, pos):
    group_lanes = p.shape[-1] // len(POOL_WINDOWS)
    outs = []
    for lt in range(p.shape[-1] // V7X_LANES):
        p_lt = p[..., lt * V7X_LANES:(lt + 1) * V7X_LANES]
        lane = lax.broadcasted_iota(jnp.int32, p_lt.shape, p_lt.ndim - 1)
        wins = POOL_WINDOWS[lt * V7X_LANES // group_lanes:(lt + 1) * V7X_LANES // group_lanes]
        acc = p_lt
        sums = {1: p_lt}
        for k in range(1, max(wins)):
            acc = acc + load_shifted(k, lt)
            sums[k + 1] = acc
        win_sum = sums[wins[-1]]
        win = jnp.full(p_lt.shape, wins[-1], jnp.int32)
        for gi in range(len(wins) - 2, -1, -1):
            in_group = lane < (gi + 1) * group_lanes
            win_sum = jnp.where(in_group, sums[wins[gi]], win_sum)
            win = jnp.where(in_group, wins[gi], win)
        cnt = jnp.minimum(pos + 1, win).astype(F32)
        outs.append(win_sum / cnt - p_lt)
    return jnp.concatenate(outs, axis=-1)


def _mem_kv_kernel(mem_ref, g_mem_ref, w_kv_ref, kt_ref, vt_ref, ktb_ref, vb_ref):
    xw = kt_ref.shape[1]
    hn = _rmsnorm(mem_ref[0], g_mem_ref[...]).astype(BF16)
    kv = _dot(hn, w_kv_ref[...])
    kt = kv[:, 0:xw].T
    v = kv[:, xw:2 * xw]
    kt_ref[0] = kt
    vt_ref[0] = v.T
    ktb_ref[0] = kt.astype(BF16)
    vb_ref[0] = v.astype(BF16)


def _decode_queries(r):
    x_w = r.ckt.shape[1]
    c_q = r.w_in.shape[1] - 3 * r.xq.shape[1] - x_w
    h = _rmsnorm(r.xq[...], r.g_mix[...]).astype(BF16)
    return _dot(h, r.w_in[:, c_q:c_q + x_w]) * ((x_w // X_HEADS) ** -0.5)


def _decode_own_head(r):
    n_seq, x_w, _ = r.ckt.shape
    rows = r.xq.shape[0] // n_seq
    stack = X_HEADS * rows
    lane_head = lax.broadcasted_iota(jnp.int32, (stack, x_w), 1) >> _log2(x_w // X_HEADS)
    row_head = lax.broadcasted_iota(jnp.int32, (stack, x_w), 0) >> _log2(rows)
    return lane_head == row_head


def _decode_scores(r, q):
    n_seq = r.ckt.shape[0]
    rows = r.xq.shape[0] // n_seq
    own_head = _decode_own_head(r)
    scores = []
    for i in range(n_seq):
        qs = jnp.concatenate([q[i * rows:(i + 1) * rows, :]] * X_HEADS, axis=0)
        qm = jnp.where(own_head, qs, 0.0).astype(BF16)
        scores.append(_dot(qm, r.ckt[i].astype(BF16)))
    return scores


def _decode_values(r, probs, step):
    n_seq = r.cvt.shape[0]
    rows = r.xq.shape[0] // n_seq
    own_head = _decode_own_head(r)
    for i in range(n_seq):
        of = lax.dot_general(probs[i], r.cvt[i].astype(BF16),
                             (((1,), (1,)), ((), ())), preferred_element_type=F32)
        of = jnp.where(own_head, of, 0.0)
        out = of[0:rows]
        for hd in range(1, X_HEADS):
            out = out + of[hd * rows:(hd + 1) * rows]
        row0 = pl.multiple_of((step * n_seq + i) * rows, rows)
        r.attn[pl.ds(row0, rows), :] = out


class _LayerRefs(NamedTuple):
    xp: object
    xs: object
    kt: object
    vb: object
    xq: object
    ckt: object
    cvt: object
    state: object
    g_mix: object
    w_in_f32: object
    g_v: object
    b_v: object
    wmix_p: object
    bias_p: object
    wmix_s: object
    bias_s: object
    wpool: object
    pool_scale: object
    w_out_a_f32: object
    w_out_b_f32: object
    w_out_c_f32: object
    w_o_f32: object
    g_ffn: object
    w_up_f32: object
    w_down_f32: object
    g_final: object
    yp: object
    ys: object
    cv_p: object
    ptail_p: object
    cv_s: object
    ptail_s: object
    ext_p: object
    pool_in: object
    pool_out: object
    x1: object
    h2: object
    w_in: object
    w_out_a: object
    w_out_b: object
    w_out_c: object
    w_o: object
    w_up: object
    w_down: object
    attn: object


def _layer_step(r, decode, j, step, n_prompt):
    t, d = r.x1.shape
    sgu_w = r.g_v.shape[-1]
    pool_w = r.pool_scale.shape[-1]
    x_w = r.attn.shape[-1]
    n_decode = r.attn.shape[0] // t
    head_dim = x_w // X_HEADS
    c_pool = 2 * sgu_w
    c_q = c_pool + pool_w
    c_gate = c_q + x_w

    ffn_up = functools.partial(_ffn_up, r)
    ffn_down = functools.partial(_ffn_down, r)

    def gate(h, part):
        lo = c_gate + part * d
        return jax.nn.sigmoid(_dot(h, r.w_in[:, lo:lo + d]))

    act = ffn_up(0)
    if not decode:
        dec_q = _decode_queries(r)
    x = r.xs[...] if decode else r.xp[0]
    h = _rmsnorm(x, r.g_mix[...]).astype(BF16)
    u = _dot(h, r.w_in[:, 0:sgu_w])
    v = _dot(h, r.w_in[:, sgu_w:2 * sgu_w])
    gate_a = gate(h, 0)
    x2 = r.x1[...] + ffn_down(act, 0)

    u = jax.nn.gelu(u)
    vhat = _layernorm(jax.nn.gelu(v), r.g_v[...], r.b_v[...])
    if decode:
        rows = r.wmix_s.shape[1]
        r.cv_s[...] = vhat
        mixed = _chunk_mix(vhat.astype(BF16), r.wmix_s, rows)
        bias = _chunk_bias(r.bias_s, t)
    else:
        r.cv_p[0] = vhat[t - CHUNK:, :]
        mixed = _chunk_mix(vhat.astype(BF16), r.wmix_p, CHUNK)
        bias = _chunk_bias(r.bias_p, t)
    p = _dot(h, r.w_in[:, c_pool:c_pool + pool_w])
    if not decode:
        q = _dot(h, r.w_in[:, c_q:c_q + x_w]) * (head_dim ** -0.5)
        dec_scores = _decode_scores(r, dec_q)
    act = ffn_up(1)
    a_in = (u * (mixed + bias)).astype(BF16)

    if decode:
        n_seq = t // rows
        lane_tiles = range(pool_w // V7X_LANES)
        for lt in lane_tiles:
            r.pool_in[lt] = p[:, lt * V7X_LANES:(lt + 1) * V7X_LANES]
        ext = [r.state[k] for k in range(POOL_STATE)]
        ext += [jnp.concatenate([r.pool_in[lt, pl.ds(i, n_seq, stride=rows), :]
                                 for lt in lane_tiles], axis=-1) for i in range(rows)]
        for i in range(rows):
            cur = POOL_STATE + i

            def load_shifted(k, lt, cur=cur):
                return ext[cur - k][:, lt * V7X_LANES:(lt + 1) * V7X_LANES]

            res = _window_means_minus_self(load_shifted, ext[cur], PAST_LEN + i)
            for lt in lane_tiles:
                r.pool_out[lt, pl.ds(i, n_seq, stride=rows), :] = (
                    res[:, lt * V7X_LANES:(lt + 1) * V7X_LANES])
        for k in range(POOL_STATE):
            r.ptail_s[k] = ext[rows + k]
        pooled = jnp.concatenate([r.pool_out[lt] for lt in lane_tiles], axis=-1)
    else:
        r.ext_p[POOL_PAD:POOL_PAD + t, :] = p

        def load_shifted(k, lt):
            return r.ext_p[POOL_PAD - k:POOL_PAD - k + t, lt * V7X_LANES:(lt + 1) * V7X_LANES]

        pos = j * t + lax.broadcasted_iota(jnp.int32, (t, V7X_LANES), 0)
        pooled = _window_means_minus_self(load_shifted, p, pos)
        tail = r.ext_p[t:t + POOL_PAD, :]
        r.ext_p[0:POOL_PAD, :] = tail
        r.ptail_p[0] = tail
    pooled = pooled.astype(BF16)

    a = _dot(a_in, r.w_out_a[...])

    if not decode:
        head = lax.broadcasted_iota(jnp.int32, (t, x_w), 1) >> _log2(head_dim)
        kt = r.kt[0]
        scores = [_dot(jnp.where(head == hd, q, 0.0).astype(BF16), kt) for hd in range(X_HEADS)]
    pool_mixed = _dot(pooled, r.wpool[...]) * r.pool_scale[...]
    x2 = x2 + ffn_down(act, 1)
    if decode:
        row0 = pl.multiple_of(jnp.clip(step - n_prompt, 0, n_decode - 1) * t, t)
        o = r.attn[pl.ds(row0, t), :]
    else:
        probs = [_softmax_bf16(s) for s in scores]
        dec_probs = [_softmax_bf16(s) for s in dec_scores]
        vb = r.vb[0]
        o = jnp.zeros((t, x_w), F32)
        for hd in range(X_HEADS):
            o = jnp.where(head == hd, _dot(probs[hd], vb), o)
        _decode_values(r, dec_probs, step)
    b = _dot(pool_mixed.astype(BF16), r.w_out_b[...])
    gate_b = gate(h, 1)
    c = _dot(o.astype(BF16), r.w_out_c[...])
    act = ffn_up(2)
    gate_c = gate(h, 2)
    x2 = x2 + ffn_down(act, 2)
    merged = (gate_a * a + gate_b * b + gate_c * c).astype(BF16)
    act = ffn_up(3)
    x1 = x + _dot(merged, r.w_o[...])
    x2 = x2 + ffn_down(act, 3)
    r.x1[...] = x1
    r.h2[...] = _rmsnorm(x1, r.g_ffn[...]).astype(BF16)
    y = _rmsnorm(x2, r.g_final[...])
    if not decode:
        r.yp[0] = y
    else:
        @pl.when(step == n_prompt)
        def _():
            r.yp[0] = y

        @pl.when(step != n_prompt)
        def _():
            r.ys[...] = y


def _ffn_up(r, s):
    slab = r.w_up.shape[1] // FFN_SPLIT
    up = _dot(r.h2[...], r.w_up[:, s * slab:(s + 1) * slab])
    return jnp.square(jnp.maximum(up, 0.0)).astype(BF16)


def _ffn_down(r, act, s):
    slab = r.w_up.shape[1] // FFN_SPLIT
    return _dot(act, r.w_down[s * slab:(s + 1) * slab, :])


def _drain_step(r):
    x2 = r.x1[...]
    act = _ffn_up(r, 0)
    for s in range(FFN_SPLIT):
        nxt = _ffn_up(r, s + 1) if s + 1 < FFN_SPLIT else None
        x2 = x2 + _ffn_down(r, act, s)
        act = nxt
    r.ys[...] = _rmsnorm(x2, r.g_final[...])


def _load_weights_bf16(r):
    ahead = LOAD_SLOTS - 1
    narrow_cols = r.w_o_f32.shape[1]
    wide_cols = max(r.w_in_f32.shape[1], r.w_up_f32.shape[1])

    def load(stage_w, stage_n, sem_w, sem_n):
        def stream(src, dst):
            narrow = src.shape[1] == narrow_cols
            stage, sem = (stage_n, sem_n) if narrow else (stage_w, sem_w)
            rows = stage.shape[1]
            n_chunks = src.shape[0] // rows
            assert src.shape[0] % rows == 0, (src.shape, rows)

            def copy(i, slot):
                return pltpu.make_async_copy(src.at[pl.ds(i * rows, rows), :],
                                             stage.at[slot, :, 0:src.shape[1]], sem.at[slot])

            def prime():
                for i in range(min(ahead, n_chunks)):
                    copy(i, i % LOAD_SLOTS).start()

            def drain():
                def body(i, carry):
                    @pl.when(i + ahead < n_chunks)
                    def _():
                        copy(i + ahead, lax.rem(i + ahead, LOAD_SLOTS)).start()

                    slot = lax.rem(i, LOAD_SLOTS)
                    copy(i, slot).wait()
                    row0 = pl.multiple_of(i * rows, rows)
                    dst[pl.ds(row0, rows), :] = stage[slot, :, 0:src.shape[1]].astype(BF16)
                    return carry

                lax.fori_loop(0, n_chunks, body, 0)

            return prime, drain

        order = [(r.w_in_f32, r.w_in), (r.w_down_f32, r.w_down), (r.w_up_f32, r.w_up),
                 (r.w_o_f32, r.w_o), (r.w_out_a_f32, r.w_out_a), (r.w_out_b_f32, r.w_out_b),
                 (r.w_out_c_f32, r.w_out_c)]
        streams = [stream(src, dst) for src, dst in order]
        streams[0][0]()
        for k, (_, drain) in enumerate(streams):
            nxt_uses_other_ring = (k + 1 < len(order)
                                   and (order[k + 1][0].shape[1] == narrow_cols)
                                   != (order[k][0].shape[1] == narrow_cols))
            if nxt_uses_other_ring:
                streams[k + 1][0]()
            drain()
            if k + 1 < len(order) and not nxt_uses_other_ring:
                streams[k + 1][0]()

    pl.run_scoped(load,
                  pltpu.VMEM((LOAD_SLOTS, LOAD_WIDE_ROWS, wide_cols), F32),
                  pltpu.VMEM((LOAD_SLOTS, LOAD_NARROW_ROWS, narrow_cols), F32),
                  pltpu.SemaphoreType.DMA((LOAD_SLOTS,)),
                  pltpu.SemaphoreType.DMA((LOAD_SLOTS,)))


def _layer_kernel(tiles_per_seq, n_prompt, *refs):
    r = _LayerRefs(*refs)
    step = pl.program_id(0)
    last = pl.num_programs(0) - 1
    j = jnp.minimum(step, n_prompt - 1) & (tiles_per_seq - 1)

    @pl.when(step == 0)
    def _():
        r.x1[...] = jnp.zeros(r.x1.shape, F32)
        r.h2[...] = jnp.zeros(r.h2.shape, BF16)
        _load_weights_bf16(r)

    @pl.when(jnp.logical_and(step < n_prompt, j == 0))
    def _():
        r.ext_p[0:POOL_PAD, :] = jnp.zeros((POOL_PAD, r.ext_p.shape[1]), F32)

    @pl.when(step < n_prompt)
    def _():
        _layer_step(r, False, j, step, n_prompt)

    @pl.when(jnp.logical_and(step >= n_prompt, step < last))
    def _():
        _layer_step(r, True, j, step, n_prompt)

    @pl.when(step == last)
    def _():
        _drain_step(r)


def _mem_kv(mem, g_mem, w_kv):
    bsz, n_mem, d = mem.shape
    xw = w_kv.shape[-1] // 2
    return pl.pallas_call(
        _mem_kv_kernel,
        grid=(bsz,),
        in_specs=[pl.BlockSpec((1, n_mem, d), lambda i: (i, 0, 0)),
                  _const_spec((1, d)), _const_spec((d, 2 * xw))],
        out_specs=[pl.BlockSpec((1, xw, n_mem), lambda i: (i, 0, 0)),
                   pl.BlockSpec((1, xw, n_mem), lambda i: (i, 0, 0)),
                   pl.BlockSpec((1, xw, n_mem), lambda i: (i, 0, 0)),
                   pl.BlockSpec((1, n_mem, xw), lambda i: (i, 0, 0))],
        out_shape=[jax.ShapeDtypeStruct((bsz, xw, n_mem), F32),
                   jax.ShapeDtypeStruct((bsz, xw, n_mem), F32),
                   jax.ShapeDtypeStruct((bsz, xw, n_mem), BF16),
                   jax.ShapeDtypeStruct((bsz, n_mem, xw), BF16)],
        compiler_params=pltpu.CompilerParams(dimension_semantics=("arbitrary",)),
        name="mem_kv",
    )(mem, g_mem.reshape(1, d).astype(F32), w_kv.astype(BF16))


def _spatial_operands(w_s, b_s, group_dim, chunk_rows, tile_rows):
    wmix = jnp.tile(w_s[:, :chunk_rows, :chunk_rows], (1, 1, tile_rows // chunk_rows))
    bias = jnp.repeat(b_s[:, :chunk_rows].T, group_dim, axis=1)
    return wmix.astype(F32), bias.astype(F32)


def _layer(x_prompt, x2d, ktb, vb, cache_kt, cache_vt, state, lw):
    (g_mix, w_in, g_v, b_v, w_s, b_s, w_pool, pool_scale,
     w_out_a, w_out_b, w_out_c, w_o, g_ffn, w_up, w_down, g_final) = lw
    bsz, seq, d = x_prompt.shape
    n_seq = state.shape[0]
    rows = x2d.shape[0] // n_seq
    t = TILE
    ts = t // rows
    tps = seq // t
    n_prompt = bsz * tps
    n_decode = n_seq // ts
    assert n_seq % n_prompt == 0, (n_seq, n_prompt)
    sps = n_seq // n_prompt
    sgu_w = g_v.shape[-1]
    pool_w = pool_scale.shape[-1]
    n_mem, xw = vb.shape[1:]
    group_dim = sgu_w // w_s.shape[0]
    row2 = lambda vec: vec.reshape(1, -1).astype(F32)
    wmix_p, bias_p = _spatial_operands(w_s, b_s, group_dim, CHUNK, t)
    wmix_s, bias_s = _spatial_operands(w_s, b_s, group_dim, rows, t)
    wpool = jax.scipy.linalg.block_diag(*[w_pool[i] for i in range(w_pool.shape[0])])
    state_rows = jnp.transpose(state, (1, 0, 2))
    big = [w_in, w_out_a, w_out_b, w_out_c, w_o, w_up, w_down]
    weights = [row2(g_mix), w_in, row2(g_v), row2(b_v), wmix_p, bias_p, wmix_s, bias_s,
               wpool.astype(BF16), row2(pool_scale), w_out_a, w_out_b, w_out_c, w_o,
               row2(g_ffn), w_up, w_down, row2(g_final)]

    def prompt_tile(m):
        return jnp.clip(m, 0, n_prompt - 1)

    def prompt_seq(m):
        return lax.shift_right_logical(prompt_tile(m), _log2(tps))

    def decode_tile(m):
        return jnp.clip(m - n_prompt, 0, n_decode - 1)

    yp, ys, cv_p, ptail_p, cv_s, ptail_s = pl.pallas_call(
        functools.partial(_layer_kernel, tps, n_prompt),
        grid=(n_prompt + n_decode + 1,),
        in_specs=[pl.BlockSpec((1, t, d), lambda s: (prompt_tile(s), 0, 0)),
                  pl.BlockSpec((t, d), lambda s: (decode_tile(s), 0)),
                  pl.BlockSpec((1, xw, n_mem), lambda s: (prompt_seq(s), 0, 0)),
                  pl.BlockSpec((1, n_mem, xw), lambda s: (prompt_seq(s), 0, 0)),
                  pl.BlockSpec((sps * rows, d), lambda s: (prompt_tile(s), 0)),
                  pl.BlockSpec((sps, xw, n_mem), lambda s: (prompt_tile(s), 0, 0)),
                  pl.BlockSpec((sps, xw, n_mem), lambda s: (prompt_tile(s), 0, 0)),
                  pl.BlockSpec((POOL_STATE, ts, pool_w), lambda s: (0, decode_tile(s), 0))]
                 + [pl.BlockSpec(memory_space=pl.ANY) if any(w is b for b in big)
                    else _const_spec(w.shape) for w in weights],
        out_specs=[pl.BlockSpec((1, t, d), lambda s: (prompt_tile(s - 1), 0, 0)),
                   pl.BlockSpec((t, d), lambda s: (decode_tile(s - 1), 0)),
                   pl.BlockSpec((1, CHUNK, sgu_w), lambda s: (prompt_seq(s), 0, 0)),
                   pl.BlockSpec((1, POOL_PAD, pool_w), lambda s: (prompt_seq(s), 0, 0)),
                   pl.BlockSpec((t, sgu_w), lambda s: (decode_tile(s), 0)),
                   pl.BlockSpec((POOL_STATE, ts, pool_w), lambda s: (0, decode_tile(s), 0))],
        out_shape=[jax.ShapeDtypeStruct((n_prompt, t, d), F32),
                   jax.ShapeDtypeStruct((n_seq * rows, d), F32),
                   jax.ShapeDtypeStruct((bsz, CHUNK, sgu_w), F32),
                   jax.ShapeDtypeStruct((bsz, POOL_PAD, pool_w), F32),
                   jax.ShapeDtypeStruct((n_seq * rows, sgu_w), F32),
                   jax.ShapeDtypeStruct((POOL_STATE, n_seq, pool_w), F32)],
        scratch_shapes=[pltpu.VMEM((POOL_PAD + t, pool_w), F32),
                        pltpu.VMEM((pool_w // V7X_LANES, t, V7X_LANES), F32),
                        pltpu.VMEM((pool_w // V7X_LANES, t, V7X_LANES), F32),
                        pltpu.VMEM((t, d), F32),
                        pltpu.VMEM((t, d), BF16)]
                       + [pltpu.VMEM(w.shape, BF16) for w in big]
                       + [pltpu.VMEM((n_seq * rows, xw), F32)],
        compiler_params=pltpu.CompilerParams(
            dimension_semantics=("arbitrary",),
            vmem_limit_bytes=VMEM_LIMIT_BYTES),
        name="layer",
    )(x_prompt.reshape(n_prompt, t, d), x2d, ktb, vb, x2d, cache_kt, cache_vt, state_rows, *weights)
    return (yp.reshape(bsz, seq, d), ys.reshape(n_seq, rows, d), cv_p, ptail_p[:, POOL_PAD - POOL_STATE:, :],
            cv_s.reshape(n_seq, rows, sgu_w), jnp.transpose(ptail_s, (1, 0, 2)))


def _feature_major(kv):
    bsz, n_mem, heads, head_dim = kv.shape
    return jnp.transpose(kv, (0, 2, 3, 1)).reshape(bsz, heads * head_dim, n_mem)


def _token_major(kvt, heads):
    bsz, xw, n_mem = kvt.shape
    return jnp.transpose(kvt.reshape(bsz, heads, xw // heads, n_mem), (0, 3, 1, 2))


def kernel(x_prompt, x_sample, mem_prompt, cache_mem_k, cache_mem_v, state_pool, g_mix, w_in, g_v, b_v, w_s, b_s, w_pool, pool_scale, g_mem, w_kv, w_out_a, w_out_b, w_out_c, w_o, g_ffn, w_up, w_down, g_final):
    depth = w_in.shape[0]
    assert depth == 1, "the final rmsnorm is fused into the (single) layer kernel"
    heads = cache_mem_k.shape[-2]
    n_seq, rows, d = x_sample.shape
    lw = (g_mix[0], w_in[0], g_v[0], b_v[0], w_s[0], b_s[0], w_pool[0], pool_scale[0],
          w_out_a[0], w_out_b[0], w_out_c[0], w_o[0], g_ffn[0], w_up[0], w_down[0], g_final)
    mem_kt, mem_vt, ktb, vb = _mem_kv(mem_prompt, g_mem[0], w_kv[0])
    x2d = x_sample.reshape(n_seq * rows, d)
    y_prompt, y_sample, cv_p, ptail_p, cv_s, ptail_s = _layer(
        x_prompt, x2d, ktb, vb, _feature_major(cache_mem_k[0]), _feature_major(cache_mem_v[0]),
        state_pool[0], lw)
    return (y_prompt, y_sample, _token_major(mem_kt, heads)[None], _token_major(mem_vt, heads)[None],
            ptail_p[None], ptail_s[None], cv_p[None], cv_s[None])
```

```python
import functools
from typing import NamedTuple

import jax
import jax.numpy as jnp
from jax import lax
from jax.experimental import pallas as pl
from jax.experimental.pallas import tpu as pltpu

PAST_LEN = 16384
CHUNK = 128
POOL_WINDOWS = (2, 4, 8, 16)
POOL_STATE = max(POOL_WINDOWS) - 1
X_HEADS = 4
EPS = 1e-6

V7X_VMEM_BYTES = 64 * 1024 * 1024
V7X_LANES = 128
VMEM_LIMIT_BYTES = V7X_VMEM_BYTES - 6 * 1024 * 1024

TILE = 256
FFN_SPLIT = 4
LOAD_SLOTS = 8
LOAD_WIDE_ROWS = 32
LOAD_NARROW_ROWS = 128
POOL_PAD = POOL_STATE + 1

BF16 = jnp.bfloat16
F32 = jnp.float32


def _dot(a, b):
    return jnp.dot(a, b, preferred_element_type=F32)


def _log2(n):
    assert n > 0 and n & (n - 1) == 0, n
    return n.bit_length() - 1


def _rmsnorm(x, g):
    return x * lax.rsqrt(jnp.mean(x * x, axis=-1, keepdims=True) + EPS) * g


def _layernorm(x, g, b):
    xc = x - jnp.mean(x, axis=-1, keepdims=True)
    return xc * lax.rsqrt(jnp.mean(xc * xc, axis=-1, keepdims=True) + EPS) * g + b


def _softmax_bf16(s):
    e = jnp.exp(s - jnp.max(s, axis=-1, keepdims=True))
    return (e * (1.0 / jnp.sum(e, axis=-1, keepdims=True))).astype(BF16)


def _const_spec(shape):
    zeros = (0,) * len(shape)
    return pl.BlockSpec(shape, lambda *_: zeros, pipeline_mode=pl.Buffered(1))


def _chunk_mix(vhat_b, wmix_ref, chunk_rows):
    t, sgu_w = vhat_b.shape
    n_groups = wmix_ref.shape[0]
    group_dim = sgu_w // n_groups
    shift = _log2(chunk_rows)
    reps = t // chunk_rows
    row = lax.broadcasted_iota(jnp.int32, (t, t), 0)
    col = lax.broadcasted_iota(jnp.int32, (t, t), 1)
    same_chunk = (row >> shift) == (col >> shift)
    causal = col <= row
    parts = []
    for g in range(n_groups):
        w = jnp.concatenate([wmix_ref[g]] * reps, axis=0)
        w = jnp.where(same_chunk, jnp.where(causal, w, 0.0), 0.0).astype(BF16)
        parts.append(_dot(w, vhat_b[:, g * group_dim:(g + 1) * group_dim]))
    return jnp.concatenate(parts, axis=1)


def _chunk_bias(bias_ref, t):
    return jnp.concatenate([bias_ref[...]] * (t // bias_ref.shape[0]), axis=0)


def _window_means_minus_self(load_shifted, p, pos):
    group_lanes = p.shape[-1] // len(POOL_WINDOWS)
    outs = []
    for lt in range(p.shape[-1] // V7X_LANES):
        p_lt = p[..., lt * V7X_LANES:(lt + 1) * V7X_LANES]
        lane = lax.broadcasted_iota(jnp.int32, p_lt.shape, p_lt.ndim - 1)
        wins = POOL_WINDOWS[lt * V7X_LANES // group_lanes:(lt + 1) * V7X_LANES // group_lanes]
        acc = p_lt
        sums = {1: p_lt}
        for k in range(1, max(wins)):
            acc = acc + load_shifted(k, lt)
            sums[k + 1] = acc
        win_sum = sums[wins[-1]]
        win = jnp.full(p_lt.shape, wins[-1], jnp.int32)
        for gi in range(len(wins) - 2, -1, -1):
            in_group = lane < (gi + 1) * group_lanes
            win_sum = jnp.where(in_group, sums[wins[gi]], win_sum)
            win = jnp.where(in_group, wins[gi], win)
        cnt = jnp.minimum(pos + 1, win).astype(F32)
        outs.append(win_sum / cnt - p_lt)
    return jnp.concatenate(outs, axis=-1)


def _mem_kv_kernel(mem_ref, g_mem_ref, w_kv_ref, kt_ref, vt_ref, ktb_ref, vb_ref):
    xw = kt_ref.shape[1]
    hn = _rmsnorm(mem_ref[0], g_mem_ref[...]).astype(BF16)
    kv = _dot(hn, w_kv_ref[...])
    kt = kv[:, 0:xw].T
    v = kv[:, xw:2 * xw]
    kt_ref[0] = kt
    vt_ref[0] = v.T
    ktb_ref[0] = kt.astype(BF16)
    vb_ref[0] = v.astype(BF16)


def _decode_queries(r):
    x_w = r.ckt.shape[1]
    c_q = r.w_in.shape[1] - 3 * r.xq.shape[1] - x_w
    h = _rmsnorm(r.xq[...], r.g_mix[...]).astype(BF16)
    return _dot(h, r.w_in[:, c_q:c_q + x_w]) * ((x_w // X_HEADS) ** -0.5)


def _decode_own_head(r):
    n_seq, x_w, _ = r.ckt.shape
    rows = r.xq.shape[0] // n_seq
    stack = X_HEADS * rows
    lane_head = lax.broadcasted_iota(jnp.int32, (stack, x_w), 1) >> _log2(x_w // X_HEADS)
    row_head = lax.broadcasted_iota(jnp.int32, (stack, x_w), 0) >> _log2(rows)
    return lane_head == row_head


def _decode_scores(r, q):
    n_seq = r.ckt.shape[0]
    rows = r.xq.shape[0] // n_seq
    own_head = _decode_own_head(r)
    scores = []
    for i in range(n_seq):
        qs = jnp.concatenate([q[i * rows:(i + 1) * rows, :]] * X_HEADS, axis=0)
        qm = jnp.where(own_head, qs, 0.0).astype(BF16)
        scores.append(_dot(qm, r.ckt[i].astype(BF16)))
    return scores


def _decode_values(r, probs, step):
    n_seq = r.cvt.shape[0]
    rows = r.xq.shape[0] // n_seq
    own_head = _decode_own_head(r)
    for i in range(n_seq):
        of = lax.dot_general(probs[i], r.cvt[i].astype(BF16),
                             (((1,), (1,)), ((), ())), preferred_element_type=F32)
        of = jnp.where(own_head, of, 0.0)
        out = of[0:rows]
        for hd in range(1, X_HEADS):
            out = out + of[hd * rows:(hd + 1) * rows]
        row0 = pl.multiple_of((step * n_seq + i) * rows, rows)
        r.attn[pl.ds(row0, rows), :] = out


class _LayerRefs(NamedTuple):
    xp: object
    xs: object
    kt: object
    vb: object
    xq: object
    ckt: object
    cvt: object
    state: object
    g_mix: object
    w_in_f32: object
    g_v: object
    b_v: object
    wmix_p: object
    bias_p: object
    wmix_s: object
    bias_s: object
    wpool: object
    pool_scale: object
    w_out_a_f32: object
    w_out_b_f32: object
    w_out_c_f32: object
    w_o_f32: object
    g_ffn: object
    w_up_f32: object
    w_down_f32: object
    g_final: object
    yp: object
    ys: object
    cv_p: object
    ptail_p: object
    cv_s: object
    ptail_s: object
    ext_p: object
    pool_in: object
    pool_out: object
    x1: object
    h2: object
    x2: object
    w_in: object
    w_out_a: object
    w_out_b: object
    w_out_c: object
    w_o: object
    w_up: object
    w_down: object
    attn: object


def _layer_step(r, decode, j, step, n_prompt):
    t, d = r.x1.shape
    sgu_w = r.g_v.shape[-1]
    pool_w = r.pool_scale.shape[-1]
    x_w = r.attn.shape[-1]
    n_decode = r.attn.shape[0] // t
    head_dim = x_w // X_HEADS
    c_pool = 2 * sgu_w
    c_q = c_pool + pool_w
    c_gate = c_q + x_w

    ffn_up = functools.partial(_ffn_up, r)
    ffn_down = functools.partial(_ffn_down, r)

    def gate(h, part):
        lo = c_gate + part * d
        return jax.nn.sigmoid(_dot(h, r.w_in[:, lo:lo + d]))

    _store_final(r, decode, step, n_prompt)
    act = ffn_up(0)
    if not decode:
        dec_q = _decode_queries(r)
    x = r.xs[...] if decode else r.xp[0]
    h = _rmsnorm(x, r.g_mix[...]).astype(BF16)
    u = _dot(h, r.w_in[:, 0:sgu_w])
    v = _dot(h, r.w_in[:, sgu_w:2 * sgu_w])
    gate_a = gate(h, 0)
    x2 = r.x1[...] + ffn_down(act, 0)

    u = jax.nn.gelu(u)
    vhat = _layernorm(jax.nn.gelu(v), r.g_v[...], r.b_v[...])
    if decode:
        rows = r.wmix_s.shape[1]
        r.cv_s[...] = vhat
        mixed = _chunk_mix(vhat.astype(BF16), r.wmix_s, rows)
        bias = _chunk_bias(r.bias_s, t)
    else:
        r.cv_p[0] = vhat[t - CHUNK:, :]
        mixed = _chunk_mix(vhat.astype(BF16), r.wmix_p, CHUNK)
        bias = _chunk_bias(r.bias_p, t)
    p = _dot(h, r.w_in[:, c_pool:c_pool + pool_w])
    if not decode:
        q = _dot(h, r.w_in[:, c_q:c_q + x_w]) * (head_dim ** -0.5)
        dec_scores = _decode_scores(r, dec_q)
    act = ffn_up(1)
    a_in = (u * (mixed + bias)).astype(BF16)

    if decode:
        n_seq = t // rows
        lane_tiles = range(pool_w // V7X_LANES)
        for lt in lane_tiles:
            r.pool_in[lt] = p[:, lt * V7X_LANES:(lt + 1) * V7X_LANES]
        ext = [r.state[k] for k in range(POOL_STATE)]
        ext += [jnp.concatenate([r.pool_in[lt, pl.ds(i, n_seq, stride=rows), :]
                                 for lt in lane_tiles], axis=-1) for i in range(rows)]
        for i in range(rows):
            cur = POOL_STATE + i

            def load_shifted(k, lt, cur=cur):
                return ext[cur - k][:, lt * V7X_LANES:(lt + 1) * V7X_LANES]

            res = _window_means_minus_self(load_shifted, ext[cur], PAST_LEN + i)
            for lt in lane_tiles:
                r.pool_out[lt, pl.ds(i, n_seq, stride=rows), :] = (
                    res[:, lt * V7X_LANES:(lt + 1) * V7X_LANES])
        for k in range(POOL_STATE):
            r.ptail_s[k] = ext[rows + k]
        pooled = jnp.concatenate([r.pool_out[lt] for lt in lane_tiles], axis=-1)
    else:
        r.ext_p[POOL_PAD:POOL_PAD + t, :] = p

        def load_shifted(k, lt):
            return r.ext_p[POOL_PAD - k:POOL_PAD - k + t, lt * V7X_LANES:(lt + 1) * V7X_LANES]

        pos = j * t + lax.broadcasted_iota(jnp.int32, (t, V7X_LANES), 0)
        pooled = _window_means_minus_self(load_shifted, p, pos)
        tail = r.ext_p[t:t + POOL_PAD, :]
        r.ext_p[0:POOL_PAD, :] = tail
        r.ptail_p[0] = tail
    pooled = pooled.astype(BF16)

    a = _dot(a_in, r.w_out_a[...])

    if not decode:
        head = lax.broadcasted_iota(jnp.int32, (t, x_w), 1) >> _log2(head_dim)
        kt = r.kt[0]
        scores = [_dot(jnp.where(head == hd, q, 0.0).astype(BF16), kt) for hd in range(X_HEADS)]
    pool_mixed = _dot(pooled, r.wpool[...]) * r.pool_scale[...]
    x2 = x2 + ffn_down(act, 1)
    if decode:
        row0 = pl.multiple_of(jnp.clip(step - n_prompt, 0, n_decode - 1) * t, t)
        o = r.attn[pl.ds(row0, t), :]
    else:
        probs = [_softmax_bf16(s) for s in scores]
        dec_probs = [_softmax_bf16(s) for s in dec_scores]
        vb = r.vb[0]
        o = jnp.zeros((t, x_w), F32)
        for hd in range(X_HEADS):
            o = jnp.where(head == hd, _dot(probs[hd], vb), o)
        _decode_values(r, dec_probs, step)
    b = _dot(pool_mixed.astype(BF16), r.w_out_b[...])
    gate_b = gate(h, 1)
    c = _dot(o.astype(BF16), r.w_out_c[...])
    act = ffn_up(2)
    gate_c = gate(h, 2)
    x2 = x2 + ffn_down(act, 2)
    merged = (gate_a * a + gate_b * b + gate_c * c).astype(BF16)
    act = ffn_up(3)
    x1 = x + _dot(merged, r.w_o[...])
    x2 = x2 + ffn_down(act, 3)
    r.x1[...] = x1
    r.h2[...] = _rmsnorm(x1, r.g_ffn[...]).astype(BF16)
    r.x2[...] = x2


def _store_final(r, decode, step, n_prompt):
    y = _rmsnorm(r.x2[...], r.g_final[...])
    if not decode:
        r.yp[0] = y
    else:
        @pl.when(step < n_prompt + 2)
        def _():
            r.yp[0] = y

        @pl.when(step >= n_prompt + 2)
        def _():
            r.ys[...] = y


def _ffn_up(r, s):
    slab = r.w_up.shape[1] // FFN_SPLIT
    up = _dot(r.h2[...], r.w_up[:, s * slab:(s + 1) * slab])
    return jnp.square(jnp.maximum(up, 0.0)).astype(BF16)


def _ffn_down(r, act, s):
    slab = r.w_up.shape[1] // FFN_SPLIT
    return _dot(act, r.w_down[s * slab:(s + 1) * slab, :])


def _drain_step(r):
    r.ys[...] = _rmsnorm(r.x2[...], r.g_final[...])
    x2 = r.x1[...]
    act = _ffn_up(r, 0)
    for s in range(FFN_SPLIT):
        nxt = _ffn_up(r, s + 1) if s + 1 < FFN_SPLIT else None
        x2 = x2 + _ffn_down(r, act, s)
        act = nxt
    r.x2[...] = x2


def _load_weights_bf16(r):
    ahead = LOAD_SLOTS - 1
    narrow_cols = r.w_o_f32.shape[1]
    wide_cols = max(r.w_in_f32.shape[1], r.w_up_f32.shape[1])

    def load(stage_w, stage_n, sem_w, sem_n):
        def stream(src, dst):
            narrow = src.shape[1] == narrow_cols
            stage, sem = (stage_n, sem_n) if narrow else (stage_w, sem_w)
            rows = stage.shape[1]
            n_chunks = src.shape[0] // rows
            assert src.shape[0] % rows == 0, (src.shape, rows)

            def copy(i, slot):
                return pltpu.make_async_copy(src.at[pl.ds(i * rows, rows), :],
                                             stage.at[slot, :, 0:src.shape[1]], sem.at[slot])

            def prime():
                for i in range(min(ahead, n_chunks)):
                    copy(i, i % LOAD_SLOTS).start()

            def drain():
                def body(i, carry):
                    @pl.when(i + ahead < n_chunks)
                    def _():
                        copy(i + ahead, lax.rem(i + ahead, LOAD_SLOTS)).start()

                    slot = lax.rem(i, LOAD_SLOTS)
                    copy(i, slot).wait()
                    row0 = pl.multiple_of(i * rows, rows)
                    dst[pl.ds(row0, rows), :] = stage[slot, :, 0:src.shape[1]].astype(BF16)
                    return carry

                lax.fori_loop(0, n_chunks, body, 0)

            return prime, drain

        order = [(r.w_in_f32, r.w_in), (r.w_down_f32, r.w_down), (r.w_up_f32, r.w_up),
                 (r.w_o_f32, r.w_o), (r.w_out_a_f32, r.w_out_a), (r.w_out_b_f32, r.w_out_b),
                 (r.w_out_c_f32, r.w_out_c)]
        streams = [stream(src, dst) for src, dst in order]
        streams[0][0]()
        for k, (_, drain) in enumerate(streams):
            nxt_uses_other_ring = (k + 1 < len(order)
                                   and (order[k + 1][0].shape[1] == narrow_cols)
                                   != (order[k][0].shape[1] == narrow_cols))
            if nxt_uses_other_ring:
                streams[k + 1][0]()
            drain()
            if k + 1 < len(order) and not nxt_uses_other_ring:
                streams[k + 1][0]()

    pl.run_scoped(load,
                  pltpu.VMEM((LOAD_SLOTS, LOAD_WIDE_ROWS, wide_cols), F32),
                  pltpu.VMEM((LOAD_SLOTS, LOAD_NARROW_ROWS, narrow_cols), F32),
                  pltpu.SemaphoreType.DMA((LOAD_SLOTS,)),
                  pltpu.SemaphoreType.DMA((LOAD_SLOTS,)))


def _layer_kernel(tiles_per_seq, n_prompt, *refs):
    r = _LayerRefs(*refs)
    step = pl.program_id(0)
    last = pl.num_programs(0) - 1
    j = jnp.minimum(step, n_prompt - 1) & (tiles_per_seq - 1)

    @pl.when(step == 0)
    def _():
        r.x1[...] = jnp.zeros(r.x1.shape, F32)
        r.h2[...] = jnp.zeros(r.h2.shape, BF16)
        r.x2[...] = jnp.zeros(r.x2.shape, F32)
        _load_weights_bf16(r)

    @pl.when(jnp.logical_and(step < n_prompt, j == 0))
    def _():
        r.ext_p[0:POOL_PAD, :] = jnp.zeros((POOL_PAD, r.ext_p.shape[1]), F32)

    @pl.when(step < n_prompt)
    def _():
        _layer_step(r, False, j, step, n_prompt)

    @pl.when(jnp.logical_and(step >= n_prompt, step < last - 1))
    def _():
        _layer_step(r, True, j, step, n_prompt)

    @pl.when(step == last - 1)
    def _():
        _drain_step(r)

    @pl.when(step == last)
    def _():
        r.ys[...] = _rmsnorm(r.x2[...], r.g_final[...])


def _mem_kv(mem, g_mem, w_kv):
    bsz, n_mem, d = mem.shape
    xw = w_kv.shape[-1] // 2
    return pl.pallas_call(
        _mem_kv_kernel,
        grid=(bsz,),
        in_specs=[pl.BlockSpec((1, n_mem, d), lambda i: (i, 0, 0)),
                  _const_spec((1, d)), _const_spec((d, 2 * xw))],
        out_specs=[pl.BlockSpec((1, xw, n_mem), lambda i: (i, 0, 0)),
                   pl.BlockSpec((1, xw, n_mem), lambda i: (i, 0, 0)),
                   pl.BlockSpec((1, xw, n_mem), lambda i: (i, 0, 0)),
                   pl.BlockSpec((1, n_mem, xw), lambda i: (i, 0, 0))],
        out_shape=[jax.ShapeDtypeStruct((bsz, xw, n_mem), F32),
                   jax.ShapeDtypeStruct((bsz, xw, n_mem), F32),
                   jax.ShapeDtypeStruct((bsz, xw, n_mem), BF16),
                   jax.ShapeDtypeStruct((bsz, n_mem, xw), BF16)],
        compiler_params=pltpu.CompilerParams(dimension_semantics=("arbitrary",)),
        name="mem_kv",
    )(mem, g_mem.reshape(1, d).astype(F32), w_kv.astype(BF16))


def _spatial_operands(w_s, b_s, group_dim, chunk_rows, tile_rows):
    wmix = jnp.tile(w_s[:, :chunk_rows, :chunk_rows], (1, 1, tile_rows // chunk_rows))
    bias = jnp.repeat(b_s[:, :chunk_rows].T, group_dim, axis=1)
    return wmix.astype(F32), bias.astype(F32)


def _layer(x_prompt, x2d, ktb, vb, cache_kt, cache_vt, state, lw):
    (g_mix, w_in, g_v, b_v, w_s, b_s, w_pool, pool_scale,
     w_out_a, w_out_b, w_out_c, w_o, g_ffn, w_up, w_down, g_final) = lw
    bsz, seq, d = x_prompt.shape
    n_seq = state.shape[0]
    rows = x2d.shape[0] // n_seq
    t = TILE
    ts = t // rows
    tps = seq // t
    n_prompt = bsz * tps
    n_decode = n_seq // ts
    assert n_seq % n_prompt == 0, (n_seq, n_prompt)
    sps = n_seq // n_prompt
    sgu_w = g_v.shape[-1]
    pool_w = pool_scale.shape[-1]
    n_mem, xw = vb.shape[1:]
    group_dim = sgu_w // w_s.shape[0]
    row2 = lambda vec: vec.reshape(1, -1).astype(F32)
    wmix_p, bias_p = _spatial_operands(w_s, b_s, group_dim, CHUNK, t)
    wmix_s, bias_s = _spatial_operands(w_s, b_s, group_dim, rows, t)
    wpool = jax.scipy.linalg.block_diag(*[w_pool[i] for i in range(w_pool.shape[0])])
    state_rows = jnp.transpose(state, (1, 0, 2))
    big = [w_in, w_out_a, w_out_b, w_out_c, w_o, w_up, w_down]
    weights = [row2(g_mix), w_in, row2(g_v), row2(b_v), wmix_p, bias_p, wmix_s, bias_s,
               wpool.astype(BF16), row2(pool_scale), w_out_a, w_out_b, w_out_c, w_o,
               row2(g_ffn), w_up, w_down, row2(g_final)]

    def prompt_tile(m):
        return jnp.clip(m, 0, n_prompt - 1)

    def prompt_seq(m):
        return lax.shift_right_logical(prompt_tile(m), _log2(tps))

    def decode_tile(m):
        return jnp.clip(m - n_prompt, 0, n_decode - 1)

    yp, ys, cv_p, ptail_p, cv_s, ptail_s = pl.pallas_call(
        functools.partial(_layer_kernel, tps, n_prompt),
        grid=(n_prompt + n_decode + 2,),
        in_specs=[pl.BlockSpec((1, t, d), lambda s: (prompt_tile(s), 0, 0)),
                  pl.BlockSpec((t, d), lambda s: (decode_tile(s), 0)),
                  pl.BlockSpec((1, xw, n_mem), lambda s: (prompt_seq(s), 0, 0)),
                  pl.BlockSpec((1, n_mem, xw), lambda s: (prompt_seq(s), 0, 0)),
                  pl.BlockSpec((sps * rows, d), lambda s: (prompt_tile(s), 0)),
                  pl.BlockSpec((sps, xw, n_mem), lambda s: (prompt_tile(s), 0, 0)),
                  pl.BlockSpec((sps, xw, n_mem), lambda s: (prompt_tile(s), 0, 0)),
                  pl.BlockSpec((POOL_STATE, ts, pool_w), lambda s: (0, decode_tile(s), 0))]
                 + [pl.BlockSpec(memory_space=pl.ANY) if any(w is b for b in big)
                    else _const_spec(w.shape) for w in weights],
        out_specs=[pl.BlockSpec((1, t, d), lambda s: (prompt_tile(s - 2), 0, 0)),
                   pl.BlockSpec((t, d), lambda s: (decode_tile(s - 2), 0)),
                   pl.BlockSpec((1, CHUNK, sgu_w), lambda s: (prompt_seq(s), 0, 0)),
                   pl.BlockSpec((1, POOL_PAD, pool_w), lambda s: (prompt_seq(s), 0, 0)),
                   pl.BlockSpec((t, sgu_w), lambda s: (decode_tile(s), 0)),
                   pl.BlockSpec((POOL_STATE, ts, pool_w), lambda s: (0, decode_tile(s), 0))],
        out_shape=[jax.ShapeDtypeStruct((n_prompt, t, d), F32),
                   jax.ShapeDtypeStruct((n_seq * rows, d), F32),
                   jax.ShapeDtypeStruct((bsz, CHUNK, sgu_w), F32),
                   jax.ShapeDtypeStruct((bsz, POOL_PAD, pool_w), F32),
                   jax.ShapeDtypeStruct((n_seq * rows, sgu_w), F32),
                   jax.ShapeDtypeStruct((POOL_STATE, n_seq, pool_w), F32)],
        scratch_shapes=[pltpu.VMEM((POOL_PAD + t, pool_w), F32),
                        pltpu.VMEM((pool_w // V7X_LANES, t, V7X_LANES), F32),
                        pltpu.VMEM((pool_w // V7X_LANES, t, V7X_LANES), F32),
                        pltpu.VMEM((t, d), F32),
                        pltpu.VMEM((t, d), BF16),
                        pltpu.VMEM((t, d), F32)]
                       + [pltpu.VMEM(w.shape, BF16) for w in big]
                       + [pltpu.VMEM((n_seq * rows, xw), F32)],
        compiler_params=pltpu.CompilerParams(
            dimension_semantics=("arbitrary",),
            vmem_limit_bytes=VMEM_LIMIT_BYTES),
        name="layer",
    )(x_prompt.reshape(n_prompt, t, d), x2d, ktb, vb, x2d, cache_kt, cache_vt, state_rows, *weights)
    return (yp.reshape(bsz, seq, d), ys.reshape(n_seq, rows, d), cv_p, ptail_p[:, POOL_PAD - POOL_STATE:, :],
            cv_s.reshape(n_seq, rows, sgu_w), jnp.transpose(ptail_s, (1, 0, 2)))


def _feature_major(kv):
    bsz, n_mem, heads, head_dim = kv.shape
    return jnp.transpose(kv, (0, 2, 3, 1)).reshape(bsz, heads * head_dim, n_mem)


def _token_major(kvt, heads):
    bsz, xw, n_mem = kvt.shape
    return jnp.transpose(kvt.reshape(bsz, heads, xw // heads, n_mem), (0, 3, 1, 2))


def kernel(x_prompt, x_sample, mem_prompt, cache_mem_k, cache_mem_v, state_pool, g_mix, w_in, g_v, b_v, w_s, b_s, w_pool, pool_scale, g_mem, w_kv, w_out_a, w_out_b, w_out_c, w_o, g_ffn, w_up, w_down, g_final):
    depth = w_in.shape[0]
    assert depth == 1, "the final rmsnorm is fused into the (single) layer kernel"
    heads = cache_mem_k.shape[-2]
    n_seq, rows, d = x_sample.shape
    lw = (g_mix[0], w_in[0], g_v[0], b_v[0], w_s[0], b_s[0], w_pool[0], pool_scale[0],
          w_out_a[0], w_out_b[0], w_out_c[0], w_o[0], g_ffn[0], w_up[0], w_down[0], g_final)
    mem_kt, mem_vt, ktb, vb = _mem_kv(mem_prompt, g_mem[0], w_kv[0])
    x2d = x_sample.reshape(n_seq * rows, d)
    y_prompt, y_sample, cv_p, ptail_p, cv_s, ptail_s = _layer(
        x_prompt, x2d, ktb, vb, _feature_major(cache_mem_k[0]), _feature_major(cache_mem_v[0]),
        state_pool[0], lw)
    return (y_prompt, y_sample, _token_major(mem_kt, heads)[None], _token_major(mem_vt, heads)[None],
            ptail_p[None], ptail_s[None], cv_p[None], cv_s[None])
```

```python
import functools
from typing import NamedTuple

import jax
import jax.numpy as jnp
from jax import lax
from jax.experimental import pallas as pl
from jax.experimental.pallas import tpu as pltpu

PAST_LEN = 16384
CHUNK = 128
POOL_WINDOWS = (2, 4, 8, 16)
POOL_STATE = max(POOL_WINDOWS) - 1
X_HEADS = 4
EPS = 1e-6

V7X_VMEM_BYTES = 64 * 1024 * 1024
V7X_LANES = 128
VMEM_LIMIT_BYTES = V7X_VMEM_BYTES - 6 * 1024 * 1024

TILE = 256
MEM_KV_SEQS = 4
FFN_SPLIT = 4
LOAD_SLOTS = 8
LOAD_WIDE_ROWS = 32
LOAD_NARROW_ROWS = 128
POOL_PAD = POOL_STATE + 1

BF16 = jnp.bfloat16
F32 = jnp.float32


def _dot(a, b):
    return jnp.dot(a, b, preferred_element_type=F32)


def _log2(n):
    assert n > 0 and n & (n - 1) == 0, n
    return n.bit_length() - 1


def _rmsnorm(x, g):
    return x * lax.rsqrt(jnp.mean(x * x, axis=-1, keepdims=True) + EPS) * g


def _layernorm(x, g, b):
    xc = x - jnp.mean(x, axis=-1, keepdims=True)
    return xc * lax.rsqrt(jnp.mean(xc * xc, axis=-1, keepdims=True) + EPS) * g + b


def _softmax_bf16(s):
    e = jnp.exp(s - jnp.max(s, axis=-1, keepdims=True))
    return (e * (1.0 / jnp.sum(e, axis=-1, keepdims=True))).astype(BF16)


def _const_spec(shape):
    zeros = (0,) * len(shape)
    return pl.BlockSpec(shape, lambda *_: zeros, pipeline_mode=pl.Buffered(1))


def _chunk_mix(vhat_b, wmix_ref, chunk_rows):
    t, sgu_w = vhat_b.shape
    n_groups = wmix_ref.shape[0]
    group_dim = sgu_w // n_groups
    shift = _log2(chunk_rows)
    reps = t // chunk_rows
    row = lax.broadcasted_iota(jnp.int32, (t, t), 0)
    col = lax.broadcasted_iota(jnp.int32, (t, t), 1)
    same_chunk = (row >> shift) == (col >> shift)
    causal = col <= row
    parts = []
    for g in range(n_groups):
        w = jnp.concatenate([wmix_ref[g]] * (t // wmix_ref.shape[2]), axis=1)
        w = jnp.concatenate([w] * reps, axis=0)
        w = jnp.where(same_chunk, jnp.where(causal, w, 0.0), 0.0).astype(BF16)
        parts.append(_dot(w, vhat_b[:, g * group_dim:(g + 1) * group_dim]))
    return jnp.concatenate(parts, axis=1)


def _chunk_bias(bias_ref, t):
    return jnp.concatenate([bias_ref[...]] * (t // bias_ref.shape[0]), axis=0)


def _window_means_minus_self(load_shifted, p, pos):
    group_lanes = p.shape[-1] // len(POOL_WINDOWS)
    outs = []
    for lt in range(p.shape[-1] // V7X_LANES):
        p_lt = p[..., lt * V7X_LANES:(lt + 1) * V7X_LANES]
        lane = lax.broadcasted_iota(jnp.int32, p_lt.shape, p_lt.ndim - 1)
        wins = POOL_WINDOWS[lt * V7X_LANES // group_lanes:(lt + 1) * V7X_LANES // group_lanes]
        acc = p_lt
        sums = {1: p_lt}
        for k in range(1, max(wins)):
            acc = acc + load_shifted(k, lt)
            sums[k + 1] = acc
        win_sum = sums[wins[-1]]
        win = jnp.full(p_lt.shape, wins[-1], jnp.int32)
        for gi in range(len(wins) - 2, -1, -1):
            in_group = lane < (gi + 1) * group_lanes
            win_sum = jnp.where(in_group, sums[wins[gi]], win_sum)
            win = jnp.where(in_group, wins[gi], win)
        cnt = jnp.minimum(pos + 1, win).astype(F32)
        outs.append(win_sum / cnt - p_lt)
    return jnp.concatenate(outs, axis=-1)


def _mem_kv_kernel(mem_ref, g_mem_ref, w_kv_ref, kt_ref, vt_ref, ktb_ref, vb_ref):
    xw = kt_ref.shape[1]
    w_kv = w_kv_ref[...].astype(BF16)
    for i in range(mem_ref.shape[0]):
        hn = _rmsnorm(mem_ref[i], g_mem_ref[...]).astype(BF16)
        kv = _dot(hn, w_kv)
        kt = kv[:, 0:xw].T
        v = kv[:, xw:2 * xw]
        kt_ref[i] = kt
        vt_ref[i] = v.T
        ktb_ref[i] = kt.astype(BF16)
        vb_ref[i] = v.astype(BF16)


def _decode_queries(r):
    x_w = r.ckt.shape[1]
    c_q = r.w_in.shape[1] - 3 * r.xq.shape[1] - x_w
    h = _rmsnorm(r.xq[...], r.g_mix[...]).astype(BF16)
    return _dot(h, r.w_in[:, c_q:c_q + x_w]) * ((x_w // X_HEADS) ** -0.5)


def _decode_own_head(r):
    n_seq, x_w, _ = r.ckt.shape
    rows = r.xq.shape[0] // n_seq
    stack = X_HEADS * rows
    lane_head = lax.broadcasted_iota(jnp.int32, (stack, x_w), 1) >> _log2(x_w // X_HEADS)
    row_head = lax.broadcasted_iota(jnp.int32, (stack, x_w), 0) >> _log2(rows)
    return lane_head == row_head


def _decode_scores(r, q):
    n_seq = r.ckt.shape[0]
    rows = r.xq.shape[0] // n_seq
    own_head = _decode_own_head(r)
    scores = []
    for i in range(n_seq):
        qs = jnp.concatenate([q[i * rows:(i + 1) * rows, :]] * X_HEADS, axis=0)
        qm = jnp.where(own_head, qs, 0.0).astype(BF16)
        scores.append(_dot(qm, r.ckt[i].astype(BF16)))
    return scores


def _decode_values(r, probs, step):
    n_seq = r.cvt.shape[0]
    rows = r.xq.shape[0] // n_seq
    own_head = _decode_own_head(r)
    for i in range(n_seq):
        of = lax.dot_general(probs[i], r.cvt[i].astype(BF16),
                             (((1,), (1,)), ((), ())), preferred_element_type=F32)
        of = jnp.where(own_head, of, 0.0)
        out = of[0:rows]
        for hd in range(1, X_HEADS):
            out = out + of[hd * rows:(hd + 1) * rows]
        row0 = pl.multiple_of((step * n_seq + i) * rows, rows)
        r.attn[pl.ds(row0, rows), :] = out


class _LayerRefs(NamedTuple):
    xp: object
    xs: object
    kt: object
    vb: object
    xq: object
    ckt: object
    cvt: object
    state: object
    g_mix: object
    w_in_f32: object
    g_v: object
    b_v: object
    wmix_p: object
    bias_p: object
    wmix_s: object
    bias_s: object
    wpool: object
    pool_scale: object
    w_out_a_f32: object
    w_out_b_f32: object
    w_out_c_f32: object
    w_o_f32: object
    g_ffn: object
    w_up_f32: object
    w_down_f32: object
    g_final: object
    yp: object
    ys: object
    cv_p: object
    ptail_p: object
    cv_s: object
    ptail_s: object
    ext_p: object
    pool_in: object
    pool_out: object
    x1: object
    h2: object
    w_in: object
    w_out_a: object
    w_out_b: object
    w_out_c: object
    w_o: object
    w_up: object
    w_down: object
    attn: object


def _layer_step(r, decode, j, step, n_prompt):
    t, d = r.x1.shape
    sgu_w = r.g_v.shape[-1]
    pool_w = r.pool_scale.shape[-1]
    x_w = r.attn.shape[-1]
    n_decode = r.attn.shape[0] // t
    head_dim = x_w // X_HEADS
    c_pool = 2 * sgu_w
    c_q = c_pool + pool_w
    c_gate = c_q + x_w

    assert FFN_SPLIT == 4, "the schedule below places four channel-mixer slabs by hand"
    ffn_up = functools.partial(_ffn_up, r)
    ffn_down = functools.partial(_ffn_down, r)

    def gate(h, part):
        lo = c_gate + part * d
        return jax.nn.sigmoid(_dot(h, r.w_in[:, lo:lo + d]))

    act = ffn_up(0)
    if not decode:
        dec_q = _decode_queries(r)
    x = r.xs[...] if decode else r.xp[0]
    h = _rmsnorm(x, r.g_mix[...]).astype(BF16)
    u = _dot(h, r.w_in[:, 0:sgu_w])
    v = _dot(h, r.w_in[:, sgu_w:2 * sgu_w])
    gate_a = gate(h, 0)
    x2 = r.x1[...] + ffn_down(act, 0)

    u = jax.nn.gelu(u)
    vhat = _layernorm(jax.nn.gelu(v), r.g_v[...], r.b_v[...])
    if decode:
        rows = r.wmix_s.shape[1]
        r.cv_s[...] = vhat
        mixed = _chunk_mix(vhat.astype(BF16), r.wmix_s, rows)
        bias = _chunk_bias(r.bias_s, t)
    else:
        r.cv_p[0] = vhat[t - CHUNK:, :]
        mixed = _chunk_mix(vhat.astype(BF16), r.wmix_p, CHUNK)
        bias = _chunk_bias(r.bias_p, t)
    p = _dot(h, r.w_in[:, c_pool:c_pool + pool_w])
    if not decode:
        q = _dot(h, r.w_in[:, c_q:c_q + x_w]) * (head_dim ** -0.5)
        dec_scores = _decode_scores(r, dec_q)
    act = ffn_up(1)
    a_in = (u * (mixed + bias)).astype(BF16)

    if decode:
        n_seq = t // rows
        lane_tiles = range(pool_w // V7X_LANES)
        for lt in lane_tiles:
            r.pool_in[lt] = p[:, lt * V7X_LANES:(lt + 1) * V7X_LANES]
        ext = [r.state[k] for k in range(POOL_STATE)]
        ext += [jnp.concatenate([r.pool_in[lt, pl.ds(i, n_seq, stride=rows), :]
                                 for lt in lane_tiles], axis=-1) for i in range(rows)]
        for i in range(rows):
            cur = POOL_STATE + i

            def load_shifted(k, lt, cur=cur):
                return ext[cur - k][:, lt * V7X_LANES:(lt + 1) * V7X_LANES]

            res = _window_means_minus_self(load_shifted, ext[cur], PAST_LEN + i)
            for lt in lane_tiles:
                r.pool_out[lt, pl.ds(i, n_seq, stride=rows), :] = (
                    res[:, lt * V7X_LANES:(lt + 1) * V7X_LANES])
        for k in range(POOL_STATE):
            r.ptail_s[k] = ext[rows + k]
        pooled = jnp.concatenate([r.pool_out[lt] for lt in lane_tiles], axis=-1)
    else:
        r.ext_p[POOL_PAD:POOL_PAD + t, :] = p

        def load_shifted(k, lt):
            return r.ext_p[POOL_PAD - k:POOL_PAD - k + t, lt * V7X_LANES:(lt + 1) * V7X_LANES]

        pos = j * t + lax.broadcasted_iota(jnp.int32, (t, V7X_LANES), 0)
        pooled = _window_means_minus_self(load_shifted, p, pos)
        tail = r.ext_p[t:t + POOL_PAD, :]
        r.ext_p[0:POOL_PAD, :] = tail
        r.ptail_p[0] = tail
    pooled = pooled.astype(BF16)

    a = _dot(a_in, r.w_out_a[...])

    if not decode:
        head = lax.broadcasted_iota(jnp.int32, (t, x_w), 1) >> _log2(head_dim)
        kt = r.kt[0]
        scores = [_dot(jnp.where(head == hd, q, 0.0).astype(BF16), kt) for hd in range(X_HEADS)]
    pool_mixed = _dot(pooled, r.wpool[...]) * r.pool_scale[...]
    x2 = x2 + ffn_down(act, 1)
    if decode:
        row0 = pl.multiple_of(jnp.clip(step - n_prompt, 0, n_decode - 1) * t, t)
        o = r.attn[pl.ds(row0, t), :]
    else:
        probs = [_softmax_bf16(s) for s in scores]
        dec_probs = [_softmax_bf16(s) for s in dec_scores]
        vb = r.vb[0]
        o = jnp.zeros((t, x_w), F32)
        for hd in range(X_HEADS):
            o = jnp.where(head == hd, _dot(probs[hd], vb), o)
        _decode_values(r, dec_probs, step)
    b = _dot(pool_mixed.astype(BF16), r.w_out_b[...])
    gate_b = gate(h, 1)
    c = _dot(o.astype(BF16), r.w_out_c[...])
    act = ffn_up(2)
    gate_c = gate(h, 2)
    x2 = x2 + ffn_down(act, 2)
    merged = (gate_a * a + gate_b * b + gate_c * c).astype(BF16)
    act = ffn_up(3)
    x1 = x + _dot(merged, r.w_o[...])
    x2 = x2 + ffn_down(act, 3)
    r.x1[...] = x1
    r.h2[...] = _rmsnorm(x1, r.g_ffn[...]).astype(BF16)
    y = _rmsnorm(x2, r.g_final[...])
    if not decode:
        r.yp[0] = y
    else:
        @pl.when(step == n_prompt)
        def _():
            r.yp[0] = y

        @pl.when(step != n_prompt)
        def _():
            r.ys[...] = y


def _ffn_up(r, s):
    slab = r.w_up.shape[1] // FFN_SPLIT
    up = _dot(r.h2[...], r.w_up[:, s * slab:(s + 1) * slab])
    return jnp.square(jnp.maximum(up, 0.0)).astype(BF16)


def _ffn_down(r, act, s):
    slab = r.w_up.shape[1] // FFN_SPLIT
    return _dot(act, r.w_down[s * slab:(s + 1) * slab, :])


def _drain_step(r):
    x2 = r.x1[...]
    act = _ffn_up(r, 0)
    for s in range(FFN_SPLIT):
        nxt = _ffn_up(r, s + 1) if s + 1 < FFN_SPLIT else None
        x2 = x2 + _ffn_down(r, act, s)
        act = nxt
    r.ys[...] = _rmsnorm(x2, r.g_final[...])


def _load_weights_bf16(r):
    ahead = LOAD_SLOTS - 1
    narrow_cols = r.w_o_f32.shape[1]
    wide_cols = max(r.w_in_f32.shape[1], r.w_up_f32.shape[1])

    def load(stage_w, stage_n, sem_w, sem_n):
        def stream(src, dst):
            narrow = src.shape[1] == narrow_cols
            stage, sem = (stage_n, sem_n) if narrow else (stage_w, sem_w)
            rows = stage.shape[1]
            n_chunks = src.shape[0] // rows
            assert src.shape[0] % rows == 0, (src.shape, rows)

            def copy(i, slot):
                return pltpu.make_async_copy(src.at[pl.ds(i * rows, rows), :],
                                             stage.at[slot, :, 0:src.shape[1]], sem.at[slot])

            def prime():
                for i in range(min(ahead, n_chunks)):
                    copy(i, i % LOAD_SLOTS).start()

            def drain():
                def body(i, carry):
                    @pl.when(i + ahead < n_chunks)
                    def _():
                        copy(i + ahead, lax.rem(i + ahead, LOAD_SLOTS)).start()

                    slot = lax.rem(i, LOAD_SLOTS)
                    copy(i, slot).wait()
                    row0 = pl.multiple_of(i * rows, rows)
                    dst[pl.ds(row0, rows), :] = stage[slot, :, 0:src.shape[1]].astype(BF16)
                    return carry

                lax.fori_loop(0, n_chunks, body, 0)

            return prime, drain

        order = [(r.w_in_f32, r.w_in), (r.w_down_f32, r.w_down), (r.w_up_f32, r.w_up),
                 (r.w_o_f32, r.w_o), (r.w_out_a_f32, r.w_out_a), (r.w_out_b_f32, r.w_out_b),
                 (r.w_out_c_f32, r.w_out_c)]
        streams = [stream(src, dst) for src, dst in order]
        streams[0][0]()
        for k, (_, drain) in enumerate(streams):
            nxt_uses_other_ring = (k + 1 < len(order)
                                   and (order[k + 1][0].shape[1] == narrow_cols)
                                   != (order[k][0].shape[1] == narrow_cols))
            if nxt_uses_other_ring:
                streams[k + 1][0]()
            drain()
            if k + 1 < len(order) and not nxt_uses_other_ring:
                streams[k + 1][0]()

    pl.run_scoped(load,
                  pltpu.VMEM((LOAD_SLOTS, LOAD_WIDE_ROWS, wide_cols), F32),
                  pltpu.VMEM((LOAD_SLOTS, LOAD_NARROW_ROWS, narrow_cols), F32),
                  pltpu.SemaphoreType.DMA((LOAD_SLOTS,)),
                  pltpu.SemaphoreType.DMA((LOAD_SLOTS,)))


def _layer_kernel(tiles_per_seq, n_prompt, *refs):
    r = _LayerRefs(*refs)
    step = pl.program_id(0)
    last = pl.num_programs(0) - 1
    j = jnp.minimum(step, n_prompt - 1) & (tiles_per_seq - 1)

    @pl.when(step == 0)
    def _():
        r.x1[...] = jnp.zeros(r.x1.shape, F32)
        r.h2[...] = jnp.zeros(r.h2.shape, BF16)
        _load_weights_bf16(r)

    @pl.when(jnp.logical_and(step < n_prompt, j == 0))
    def _():
        r.ext_p[0:POOL_PAD, :] = jnp.zeros((POOL_PAD, r.ext_p.shape[1]), F32)

    @pl.when(step < n_prompt)
    def _():
        _layer_step(r, False, j, step, n_prompt)

    @pl.when(jnp.logical_and(step >= n_prompt, step < last))
    def _():
        _layer_step(r, True, j, step, n_prompt)

    @pl.when(step == last)
    def _():
        _drain_step(r)


def _mem_kv(mem, g_mem, w_kv):
    bsz, n_mem, d = mem.shape
    xw = w_kv.shape[-1] // 2
    sb = MEM_KV_SEQS
    assert bsz % sb == 0, (bsz, sb)
    return pl.pallas_call(
        _mem_kv_kernel,
        grid=(bsz // sb,),
        in_specs=[pl.BlockSpec((sb, n_mem, d), lambda i: (i, 0, 0)),
                  _const_spec((1, d)), _const_spec((d, 2 * xw))],
        out_specs=[pl.BlockSpec((sb, xw, n_mem), lambda i: (i, 0, 0)),
                   pl.BlockSpec((sb, xw, n_mem), lambda i: (i, 0, 0)),
                   pl.BlockSpec((sb, xw, n_mem), lambda i: (i, 0, 0)),
                   pl.BlockSpec((sb, n_mem, xw), lambda i: (i, 0, 0))],
        out_shape=[jax.ShapeDtypeStruct((bsz, xw, n_mem), F32),
                   jax.ShapeDtypeStruct((bsz, xw, n_mem), F32),
                   jax.ShapeDtypeStruct((bsz, xw, n_mem), BF16),
                   jax.ShapeDtypeStruct((bsz, n_mem, xw), BF16)],
        compiler_params=pltpu.CompilerParams(
            dimension_semantics=("arbitrary",),
            vmem_limit_bytes=VMEM_LIMIT_BYTES),
        name="mem_kv",
    )(mem, g_mem.reshape(1, d).astype(F32), w_kv)


def _spatial_operands(w_s, b_s, group_dim, chunk_rows, tile_rows):
    wmix = jnp.tile(w_s[:, :chunk_rows, :chunk_rows], (1, 1, tile_rows // chunk_rows))
    bias = jnp.repeat(b_s[:, :chunk_rows].T, group_dim, axis=1)
    return wmix.astype(F32), bias.astype(F32)


def _layer(x_prompt, x2d, ktb, vb, cache_kt, cache_vt, state, lw):
    (g_mix, w_in, g_v, b_v, w_s, b_s, w_pool, pool_scale,
     w_out_a, w_out_b, w_out_c, w_o, g_ffn, w_up, w_down, g_final) = lw
    bsz, seq, d = x_prompt.shape
    n_seq = state.shape[0]
    rows = x2d.shape[0] // n_seq
    t = TILE
    ts = t // rows
    tps = seq // t
    n_prompt = bsz * tps
    n_decode = n_seq // ts
    assert n_seq % n_prompt == 0, (n_seq, n_prompt)
    sps = n_seq // n_prompt
    sgu_w = g_v.shape[-1]
    pool_w = pool_scale.shape[-1]
    n_mem, xw = vb.shape[1:]
    group_dim = sgu_w // w_s.shape[0]
    row2 = lambda vec: vec.reshape(1, -1).astype(F32)
    wmix_p, bias_p = _spatial_operands(w_s, b_s, group_dim, CHUNK, CHUNK)
    wmix_s, bias_s = _spatial_operands(w_s, b_s, group_dim, rows, t)
    wpool = jax.scipy.linalg.block_diag(*[w_pool[i] for i in range(w_pool.shape[0])])
    state_rows = jnp.transpose(state, (1, 0, 2))
    big = [w_in, w_out_a, w_out_b, w_out_c, w_o, w_up, w_down]
    weights = [row2(g_mix), w_in, row2(g_v), row2(b_v), wmix_p, bias_p, wmix_s, bias_s,
               wpool.astype(BF16), row2(pool_scale), w_out_a, w_out_b, w_out_c, w_o,
               row2(g_ffn), w_up, w_down, row2(g_final)]

    def prompt_tile(m):
        return jnp.clip(m, 0, n_prompt - 1)

    def prompt_seq(m):
        return lax.shift_right_logical(prompt_tile(m), _log2(tps))

    def decode_tile(m):
        return jnp.clip(m - n_prompt, 0, n_decode - 1)

    yp, ys, cv_p, ptail_p, cv_s, ptail_s = pl.pallas_call(
        functools.partial(_layer_kernel, tps, n_prompt),
        grid=(n_prompt + n_decode + 1,),
        in_specs=[pl.BlockSpec((1, t, d), lambda s: (prompt_tile(s), 0, 0)),
                  pl.BlockSpec((t, d), lambda s: (decode_tile(s), 0)),
                  pl.BlockSpec((1, xw, n_mem), lambda s: (prompt_seq(s), 0, 0)),
                  pl.BlockSpec((1, n_mem, xw), lambda s: (prompt_seq(s), 0, 0)),
                  pl.BlockSpec((sps * rows, d), lambda s: (prompt_tile(s), 0)),
                  pl.BlockSpec((sps, xw, n_mem), lambda s: (prompt_tile(s), 0, 0)),
                  pl.BlockSpec((sps, xw, n_mem), lambda s: (prompt_tile(s), 0, 0)),
                  pl.BlockSpec((POOL_STATE, ts, pool_w), lambda s: (0, decode_tile(s), 0))]
                 + [pl.BlockSpec(memory_space=pl.ANY) if any(w is b for b in big)
                    else _const_spec(w.shape) for w in weights],
        out_specs=[pl.BlockSpec((1, t, d), lambda s: (prompt_tile(s - 1), 0, 0)),
                   pl.BlockSpec((t, d), lambda s: (decode_tile(s - 1), 0)),
                   pl.BlockSpec((1, CHUNK, sgu_w), lambda s: (prompt_seq(s), 0, 0)),
                   pl.BlockSpec((1, POOL_PAD, pool_w), lambda s: (prompt_seq(s), 0, 0)),
                   pl.BlockSpec((t, sgu_w), lambda s: (decode_tile(s), 0)),
                   pl.BlockSpec((POOL_STATE, ts, pool_w), lambda s: (0, decode_tile(s), 0))],
        out_shape=[jax.ShapeDtypeStruct((n_prompt, t, d), F32),
                   jax.ShapeDtypeStruct((n_seq * rows, d), F32),
                   jax.ShapeDtypeStruct((bsz, CHUNK, sgu_w), F32),
                   jax.ShapeDtypeStruct((bsz, POOL_PAD, pool_w), F32),
                   jax.ShapeDtypeStruct((n_seq * rows, sgu_w), F32),
                   jax.ShapeDtypeStruct((POOL_STATE, n_seq, pool_w), F32)],
        scratch_shapes=[pltpu.VMEM((POOL_PAD + t, pool_w), F32),
                        pltpu.VMEM((pool_w // V7X_LANES, t, V7X_LANES), F32),
                        pltpu.VMEM((pool_w // V7X_LANES, t, V7X_LANES), F32),
                        pltpu.VMEM((t, d), F32),
                        pltpu.VMEM((t, d), BF16)]
                       + [pltpu.VMEM(w.shape, BF16) for w in big]
                       + [pltpu.VMEM((n_seq * rows, xw), F32)],
        compiler_params=pltpu.CompilerParams(
            dimension_semantics=("arbitrary",),
            vmem_limit_bytes=VMEM_LIMIT_BYTES),
        name="layer",
    )(x_prompt.reshape(n_prompt, t, d), x2d, ktb, vb, x2d, cache_kt, cache_vt, state_rows, *weights)
    return (yp.reshape(bsz, seq, d), ys.reshape(n_seq, rows, d), cv_p, ptail_p[:, POOL_PAD - POOL_STATE:, :],
            cv_s.reshape(n_seq, rows, sgu_w), jnp.transpose(ptail_s, (1, 0, 2)))


def _feature_major(kv):
    bsz, n_mem, heads, head_dim = kv.shape
    return jnp.transpose(kv, (0, 2, 3, 1)).reshape(bsz, heads * head_dim, n_mem)


def _token_major(kvt, heads):
    bsz, xw, n_mem = kvt.shape
    return jnp.transpose(kvt.reshape(bsz, heads, xw // heads, n_mem), (0, 3, 1, 2))


def kernel(x_prompt, x_sample, mem_prompt, cache_mem_k, cache_mem_v, state_pool, g_mix, w_in, g_v, b_v, w_s, b_s, w_pool, pool_scale, g_mem, w_kv, w_out_a, w_out_b, w_out_c, w_o, g_ffn, w_up, w_down, g_final):
    depth = w_in.shape[0]
    assert depth == 1, "the final rmsnorm is fused into the (single) layer kernel"
    heads = cache_mem_k.shape[-2]
    n_seq, rows, d = x_sample.shape
    lw = (g_mix[0], w_in[0], g_v[0], b_v[0], w_s[0], b_s[0], w_pool[0], pool_scale[0],
          w_out_a[0], w_out_b[0], w_out_c[0], w_o[0], g_ffn[0], w_up[0], w_down[0], g_final)
    mem_kt, mem_vt, ktb, vb = _mem_kv(mem_prompt, g_mem[0], w_kv[0])
    x2d = x_sample.reshape(n_seq * rows, d)
    y_prompt, y_sample, cv_p, ptail_p, cv_s, ptail_s = _layer(
        x_prompt, x2d, ktb, vb, _feature_major(cache_mem_k[0]), _feature_major(cache_mem_v[0]),
        state_pool[0], lw)
    return (y_prompt, y_sample, _token_major(mem_kt, heads)[None], _token_major(mem_vt, heads)[None],
            ptail_p[None], ptail_s[None], cv_p[None], cv_s[None])
```

```python
import functools
from typing import NamedTuple

import jax
import jax.numpy as jnp
from jax import lax
from jax.experimental import pallas as pl
from jax.experimental.pallas import tpu as pltpu

PAST_LEN = 16384
CHUNK = 128
POOL_WINDOWS = (2, 4, 8, 16)
POOL_STATE = max(POOL_WINDOWS) - 1
X_HEADS = 4
EPS = 1e-6

V7X_VMEM_BYTES = 64 * 1024 * 1024
V7X_LANES = 128
VMEM_RESERVE_BYTES = 6 * 1024 * 1024
VMEM_LIMIT_BYTES = V7X_VMEM_BYTES - VMEM_RESERVE_BYTES

TILE = 256
MEM_KV_SEQS = 4
FFN_SPLIT = 4
LOAD_SLOTS = 8
LOAD_WIDE_ROWS = 32
LOAD_NARROW_ROWS = 128
POOL_PAD = POOL_STATE + 1

BF16 = jnp.bfloat16
F32 = jnp.float32


def _dot(a, b):
    return jnp.dot(a, b, preferred_element_type=F32)


def _log2(n):
    assert n > 0 and n & (n - 1) == 0, n
    return n.bit_length() - 1


def _rmsnorm(x, g):
    return x * lax.rsqrt(jnp.mean(x * x, axis=-1, keepdims=True) + EPS) * g


def _layernorm(x, g, b):
    xc = x - jnp.mean(x, axis=-1, keepdims=True)
    return xc * lax.rsqrt(jnp.mean(xc * xc, axis=-1, keepdims=True) + EPS) * g + b


def _softmax_bf16(s):
    e = jnp.exp(s - jnp.max(s, axis=-1, keepdims=True))
    return (e * (1.0 / jnp.sum(e, axis=-1, keepdims=True))).astype(BF16)


def _const_spec(shape):
    zeros = (0,) * len(shape)
    return pl.BlockSpec(shape, lambda *_: zeros, pipeline_mode=pl.Buffered(1))


def _chunk_mix(vhat_b, wmix_ref, chunk_rows):
    t, sgu_w = vhat_b.shape
    n_groups = wmix_ref.shape[0]
    group_dim = sgu_w // n_groups
    shift = _log2(chunk_rows)
    reps = t // chunk_rows
    row = lax.broadcasted_iota(jnp.int32, (t, t), 0)
    col = lax.broadcasted_iota(jnp.int32, (t, t), 1)
    same_chunk = (row >> shift) == (col >> shift)
    causal = col <= row
    parts = []
    for g in range(n_groups):
        w = jnp.concatenate([wmix_ref[g]] * (t // wmix_ref.shape[2]), axis=1)
        w = jnp.concatenate([w] * reps, axis=0)
        w = jnp.where(same_chunk, jnp.where(causal, w, 0.0), 0.0).astype(BF16)
        parts.append(_dot(w, vhat_b[:, g * group_dim:(g + 1) * group_dim]))
    return jnp.concatenate(parts, axis=1)


def _chunk_bias(bias_ref, t):
    return jnp.concatenate([bias_ref[...]] * (t // bias_ref.shape[0]), axis=0)


def _window_means_minus_self(load_shifted, p, pos):
    group_lanes = p.shape[-1] // len(POOL_WINDOWS)
    outs = []
    for lt in range(p.shape[-1] // V7X_LANES):
        p_lt = p[..., lt * V7X_LANES:(lt + 1) * V7X_LANES]
        lane = lax.broadcasted_iota(jnp.int32, p_lt.shape, p_lt.ndim - 1)
        wins = POOL_WINDOWS[lt * V7X_LANES // group_lanes:(lt + 1) * V7X_LANES // group_lanes]
        acc = p_lt
        sums = {1: p_lt}
        for k in range(1, max(wins)):
            acc = acc + load_shifted(k, lt)
            sums[k + 1] = acc
        win_sum = sums[wins[-1]]
        win = jnp.full(p_lt.shape, wins[-1], jnp.int32)
        for gi in range(len(wins) - 2, -1, -1):
            in_group = lane < (gi + 1) * group_lanes
            win_sum = jnp.where(in_group, sums[wins[gi]], win_sum)
            win = jnp.where(in_group, wins[gi], win)
        cnt = jnp.minimum(pos + 1, win).astype(F32)
        outs.append(win_sum / cnt - p_lt)
    return jnp.concatenate(outs, axis=-1)


def _mem_kv_kernel(mem_ref, g_mem_ref, w_kv_ref, kt_ref, vt_ref, ktb_ref, vb_ref):
    xw = kt_ref.shape[1]
    w_kv = w_kv_ref[...].astype(BF16)
    for i in range(mem_ref.shape[0]):
        hn = _rmsnorm(mem_ref[i], g_mem_ref[...]).astype(BF16)
        kv = _dot(hn, w_kv)
        kt = kv[:, 0:xw].T
        v = kv[:, xw:2 * xw]
        kt_ref[i] = kt
        vt_ref[i] = v.T
        ktb_ref[i] = kt.astype(BF16)
        vb_ref[i] = v.astype(BF16)


def _decode_queries(r):
    x_w = r.ckt.shape[1]
    c_q = r.w_in.shape[1] - 3 * r.xq.shape[1] - x_w
    h = _rmsnorm(r.xq[...], r.g_mix[...]).astype(BF16)
    return _dot(h, r.w_in[:, c_q:c_q + x_w]) * ((x_w // X_HEADS) ** -0.5)


def _decode_own_head(r):
    n_seq, x_w, _ = r.ckt.shape
    rows = r.xq.shape[0] // n_seq
    stack = X_HEADS * rows
    lane_head = lax.broadcasted_iota(jnp.int32, (stack, x_w), 1) >> _log2(x_w // X_HEADS)
    row_head = lax.broadcasted_iota(jnp.int32, (stack, x_w), 0) >> _log2(rows)
    return lane_head == row_head


def _decode_scores(r, q):
    n_seq = r.ckt.shape[0]
    rows = r.xq.shape[0] // n_seq
    own_head = _decode_own_head(r)
    scores = []
    for i in range(n_seq):
        qs = jnp.concatenate([q[i * rows:(i + 1) * rows, :]] * X_HEADS, axis=0)
        qm = jnp.where(own_head, qs, 0.0).astype(BF16)
        scores.append(_dot(qm, r.ckt[i].astype(BF16)))
    return scores


def _decode_values(r, probs, step):
    n_seq = r.cvt.shape[0]
    rows = r.xq.shape[0] // n_seq
    own_head = _decode_own_head(r)
    for i in range(n_seq):
        of = lax.dot_general(probs[i], r.cvt[i].astype(BF16),
                             (((1,), (1,)), ((), ())), preferred_element_type=F32)
        of = jnp.where(own_head, of, 0.0)
        out = of[0:rows]
        for hd in range(1, X_HEADS):
            out = out + of[hd * rows:(hd + 1) * rows]
        row0 = pl.multiple_of((step * n_seq + i) * rows, rows)
        r.attn[pl.ds(row0, rows), :] = out


class _LayerRefs(NamedTuple):
    xp: object
    xs: object
    kt: object
    vb: object
    xq: object
    ckt: object
    cvt: object
    state: object
    g_mix: object
    w_in_f32: object
    g_v: object
    b_v: object
    wmix_p: object
    bias_p: object
    wmix_s: object
    bias_s: object
    wpool: object
    pool_scale: object
    w_out_a_f32: object
    w_out_b_f32: object
    w_out_c_f32: object
    w_o_f32: object
    g_ffn: object
    w_up_f32: object
    w_down_f32: object
    g_final: object
    yp: object
    ys: object
    cv_p: object
    ptail_p: object
    cv_s: object
    ptail_s: object
    ext_p: object
    pool_in: object
    pool_out: object
    x1: object
    h2: object
    w_in: object
    w_out_a: object
    w_out_b: object
    w_out_c: object
    w_o: object
    w_up: object
    w_down: object
    attn: object


def _layer_step(r, decode, j, step, n_prompt):
    t, d = r.x1.shape
    sgu_w = r.g_v.shape[-1]
    pool_w = r.pool_scale.shape[-1]
    x_w = r.attn.shape[-1]
    n_decode = r.attn.shape[0] // t
    head_dim = x_w // X_HEADS
    c_pool = 2 * sgu_w
    c_q = c_pool + pool_w
    c_gate = c_q + x_w

    assert FFN_SPLIT == 4, "the schedule below places four channel-mixer slabs by hand"
    ffn_up = functools.partial(_ffn_up, r)
    ffn_down = functools.partial(_ffn_down, r)

    def gate(h, part):
        lo = c_gate + part * d
        return jax.nn.sigmoid(_dot(h, r.w_in[:, lo:lo + d]))

    act = ffn_up(0)
    if not decode:
        dec_q = _decode_queries(r)
    x = r.xs[...] if decode else r.xp[0]
    h = _rmsnorm(x, r.g_mix[...]).astype(BF16)
    u = _dot(h, r.w_in[:, 0:sgu_w])
    v = _dot(h, r.w_in[:, sgu_w:2 * sgu_w])
    gate_a = gate(h, 0)
    x2 = r.x1[...] + ffn_down(act, 0)

    u = jax.nn.gelu(u)
    vhat = _layernorm(jax.nn.gelu(v), r.g_v[...], r.b_v[...])
    if decode:
        rows = r.wmix_s.shape[1]
        r.cv_s[...] = vhat
        mixed = _chunk_mix(vhat.astype(BF16), r.wmix_s, rows)
        bias = _chunk_bias(r.bias_s, t)
    else:
        r.cv_p[0] = vhat[t - CHUNK:, :]
        mixed = _chunk_mix(vhat.astype(BF16), r.wmix_p, CHUNK)
        bias = _chunk_bias(r.bias_p, t)
    p = _dot(h, r.w_in[:, c_pool:c_pool + pool_w])
    if not decode:
        q = _dot(h, r.w_in[:, c_q:c_q + x_w]) * (head_dim ** -0.5)
        dec_scores = _decode_scores(r, dec_q)
    act = ffn_up(1)
    a_in = (u * (mixed + bias)).astype(BF16)

    if decode:
        n_seq = t // rows
        lane_tiles = range(pool_w // V7X_LANES)
        for lt in lane_tiles:
            r.pool_in[lt] = p[:, lt * V7X_LANES:(lt + 1) * V7X_LANES]
        ext = [r.state[k] for k in range(POOL_STATE)]
        ext += [jnp.concatenate([r.pool_in[lt, pl.ds(i, n_seq, stride=rows), :]
                                 for lt in lane_tiles], axis=-1) for i in range(rows)]
        for i in range(rows):
            cur = POOL_STATE + i

            def load_shifted(k, lt, cur=cur):
                return ext[cur - k][:, lt * V7X_LANES:(lt + 1) * V7X_LANES]

            res = _window_means_minus_self(load_shifted, ext[cur], PAST_LEN + i)
            for lt in lane_tiles:
                r.pool_out[lt, pl.ds(i, n_seq, stride=rows), :] = (
                    res[:, lt * V7X_LANES:(lt + 1) * V7X_LANES])
        for k in range(POOL_STATE):
            r.ptail_s[k] = ext[rows + k]
        pooled = jnp.concatenate([r.pool_out[lt] for lt in lane_tiles], axis=-1)
    else:
        r.ext_p[POOL_PAD:POOL_PAD + t, :] = p

        def load_shifted(k, lt):
            return r.ext_p[POOL_PAD - k:POOL_PAD - k + t, lt * V7X_LANES:(lt + 1) * V7X_LANES]

        pos = j * t + lax.broadcasted_iota(jnp.int32, (t, V7X_LANES), 0)
        pooled = _window_means_minus_self(load_shifted, p, pos)
        tail = r.ext_p[t:t + POOL_PAD, :]
        r.ext_p[0:POOL_PAD, :] = tail
        r.ptail_p[0] = tail
    pooled = pooled.astype(BF16)

    a = _dot(a_in, r.w_out_a[...])

    if not decode:
        head = lax.broadcasted_iota(jnp.int32, (t, x_w), 1) >> _log2(head_dim)
        kt = r.kt[0]
        scores = [_dot(jnp.where(head == hd, q, 0.0).astype(BF16), kt) for hd in range(X_HEADS)]
    pool_mixed = _dot(pooled, r.wpool[...]) * r.pool_scale[...]
    x2 = x2 + ffn_down(act, 1)
    if decode:
        row0 = pl.multiple_of(jnp.clip(step - n_prompt, 0, n_decode - 1) * t, t)
        o = r.attn[pl.ds(row0, t), :]
    else:
        probs = [_softmax_bf16(s) for s in scores]
        dec_probs = [_softmax_bf16(s) for s in dec_scores]
        vb = r.vb[0]
        o = jnp.zeros((t, x_w), F32)
        for hd in range(X_HEADS):
            o = jnp.where(head == hd, _dot(probs[hd], vb), o)
        _decode_values(r, dec_probs, step)
    b = _dot(pool_mixed.astype(BF16), r.w_out_b[...])
    gate_b = gate(h, 1)
    c = _dot(o.astype(BF16), r.w_out_c[...])
    act = ffn_up(2)
    gate_c = gate(h, 2)
    x2 = x2 + ffn_down(act, 2)
    merged = (gate_a * a + gate_b * b + gate_c * c).astype(BF16)
    act = ffn_up(3)
    x1 = x + _dot(merged, r.w_o[...])
    x2 = x2 + ffn_down(act, 3)
    r.x1[...] = x1
    r.h2[...] = _rmsnorm(x1, r.g_ffn[...]).astype(BF16)
    y = _rmsnorm(x2, r.g_final[...])
    if not decode:
        r.yp[0] = y
    else:
        @pl.when(step == n_prompt)
        def _():
            r.yp[0] = y

        @pl.when(step != n_prompt)
        def _():
            r.ys[...] = y


def _ffn_up(r, s):
    slab = r.w_up.shape[1] // FFN_SPLIT
    up = _dot(r.h2[...], r.w_up[:, s * slab:(s + 1) * slab])
    return jnp.square(jnp.maximum(up, 0.0)).astype(BF16)


def _ffn_down(r, act, s):
    slab = r.w_up.shape[1] // FFN_SPLIT
    return _dot(act, r.w_down[s * slab:(s + 1) * slab, :])


def _drain_step(r):
    x2 = r.x1[...]
    act = _ffn_up(r, 0)
    for s in range(FFN_SPLIT):
        nxt = _ffn_up(r, s + 1) if s + 1 < FFN_SPLIT else None
        x2 = x2 + _ffn_down(r, act, s)
        act = nxt
    r.ys[...] = _rmsnorm(x2, r.g_final[...])


def _load_weights_bf16(r):
    ahead = LOAD_SLOTS - 1
    narrow_cols = r.w_o_f32.shape[1]
    wide_cols = max(r.w_in_f32.shape[1], r.w_up_f32.shape[1])

    def load(stage_w, stage_n, sem_w, sem_n):
        def stream(src, dst):
            narrow = src.shape[1] == narrow_cols
            stage, sem = (stage_n, sem_n) if narrow else (stage_w, sem_w)
            rows = stage.shape[1]
            n_chunks = src.shape[0] // rows
            assert src.shape[0] % rows == 0, (src.shape, rows)

            def copy(i, slot):
                return pltpu.make_async_copy(src.at[pl.ds(i * rows, rows), :],
                                             stage.at[slot, :, 0:src.shape[1]], sem.at[slot])

            def prime():
                for i in range(min(ahead, n_chunks)):
                    copy(i, i % LOAD_SLOTS).start()

            def drain():
                def body(i, carry):
                    @pl.when(i + ahead < n_chunks)
                    def _():
                        copy(i + ahead, lax.rem(i + ahead, LOAD_SLOTS)).start()

                    slot = lax.rem(i, LOAD_SLOTS)
                    copy(i, slot).wait()
                    row0 = pl.multiple_of(i * rows, rows)
                    dst[pl.ds(row0, rows), :] = stage[slot, :, 0:src.shape[1]].astype(BF16)
                    return carry

                lax.fori_loop(0, n_chunks, body, 0)

            return prime, drain

        order = [(r.w_in_f32, r.w_in), (r.w_down_f32, r.w_down), (r.w_up_f32, r.w_up),
                 (r.w_o_f32, r.w_o), (r.w_out_a_f32, r.w_out_a), (r.w_out_b_f32, r.w_out_b),
                 (r.w_out_c_f32, r.w_out_c)]
        streams = [stream(src, dst) for src, dst in order]
        streams[0][0]()
        for k, (_, drain) in enumerate(streams):
            nxt_uses_other_ring = (k + 1 < len(order)
                                   and (order[k + 1][0].shape[1] == narrow_cols)
                                   != (order[k][0].shape[1] == narrow_cols))
            if nxt_uses_other_ring:
                streams[k + 1][0]()
            drain()
            if k + 1 < len(order) and not nxt_uses_other_ring:
                streams[k + 1][0]()

    pl.run_scoped(load,
                  pltpu.VMEM((LOAD_SLOTS, LOAD_WIDE_ROWS, wide_cols), F32),
                  pltpu.VMEM((LOAD_SLOTS, LOAD_NARROW_ROWS, narrow_cols), F32),
                  pltpu.SemaphoreType.DMA((LOAD_SLOTS,)),
                  pltpu.SemaphoreType.DMA((LOAD_SLOTS,)))


def _layer_kernel(tiles_per_seq, n_prompt, *refs):
    r = _LayerRefs(*refs)
    step = pl.program_id(0)
    last = pl.num_programs(0) - 1
    j = jnp.minimum(step, n_prompt - 1) & (tiles_per_seq - 1)

    @pl.when(step == 0)
    def _():
        r.x1[...] = jnp.zeros(r.x1.shape, F32)
        r.h2[...] = jnp.zeros(r.h2.shape, BF16)
        _load_weights_bf16(r)

    @pl.when(jnp.logical_and(step < n_prompt, j == 0))
    def _():
        r.ext_p[0:POOL_PAD, :] = jnp.zeros((POOL_PAD, r.ext_p.shape[1]), F32)

    @pl.when(step < n_prompt)
    def _():
        _layer_step(r, False, j, step, n_prompt)

    @pl.when(jnp.logical_and(step >= n_prompt, step < last))
    def _():
        _layer_step(r, True, j, step, n_prompt)

    @pl.when(step == last)
    def _():
        _drain_step(r)


def _mem_kv(mem, g_mem, w_kv):
    bsz, n_mem, d = mem.shape
    xw = w_kv.shape[-1] // 2
    sb = MEM_KV_SEQS
    assert bsz % sb == 0, (bsz, sb)
    return pl.pallas_call(
        _mem_kv_kernel,
        grid=(bsz // sb,),
        in_specs=[pl.BlockSpec((sb, n_mem, d), lambda i: (i, 0, 0)),
                  _const_spec((1, d)), _const_spec((d, 2 * xw))],
        out_specs=[pl.BlockSpec((sb, xw, n_mem), lambda i: (i, 0, 0)),
                   pl.BlockSpec((sb, xw, n_mem), lambda i: (i, 0, 0)),
                   pl.BlockSpec((sb, xw, n_mem), lambda i: (i, 0, 0)),
                   pl.BlockSpec((sb, n_mem, xw), lambda i: (i, 0, 0))],
        out_shape=[jax.ShapeDtypeStruct((bsz, xw, n_mem), F32),
                   jax.ShapeDtypeStruct((bsz, xw, n_mem), F32),
                   jax.ShapeDtypeStruct((bsz, xw, n_mem), BF16),
                   jax.ShapeDtypeStruct((bsz, n_mem, xw), BF16)],
        compiler_params=pltpu.CompilerParams(
            dimension_semantics=("arbitrary",),
            vmem_limit_bytes=VMEM_LIMIT_BYTES),
        name="mem_kv",
    )(mem, g_mem.reshape(1, d).astype(F32), w_kv)


def _spatial_operands(w_s, b_s, group_dim, chunk_rows, tile_rows):
    wmix = jnp.tile(w_s[:, :chunk_rows, :chunk_rows], (1, 1, tile_rows // chunk_rows))
    bias = jnp.repeat(b_s[:, :chunk_rows].T, group_dim, axis=1)
    return wmix.astype(F32), bias.astype(F32)


def _layer(x_prompt, x2d, ktb, vb, cache_kt, cache_vt, state, lw):
    (g_mix, w_in, g_v, b_v, w_s, b_s, w_pool, pool_scale,
     w_out_a, w_out_b, w_out_c, w_o, g_ffn, w_up, w_down, g_final) = lw
    bsz, seq, d = x_prompt.shape
    n_seq = state.shape[0]
    rows = x2d.shape[0] // n_seq
    t = TILE
    ts = t // rows
    tps = seq // t
    n_prompt = bsz * tps
    n_decode = n_seq // ts
    assert n_seq % n_prompt == 0, (n_seq, n_prompt)
    sps = n_seq // n_prompt
    sgu_w = g_v.shape[-1]
    pool_w = pool_scale.shape[-1]
    n_mem, xw = vb.shape[1:]
    group_dim = sgu_w // w_s.shape[0]
    row2 = lambda vec: vec.reshape(1, -1).astype(F32)
    wmix_p, bias_p = _spatial_operands(w_s, b_s, group_dim, CHUNK, CHUNK)
    wmix_s, bias_s = _spatial_operands(w_s, b_s, group_dim, rows, t)
    wpool = jax.scipy.linalg.block_diag(*[w_pool[i] for i in range(w_pool.shape[0])])
    state_rows = jnp.transpose(state, (1, 0, 2))
    big = [w_in, w_out_a, w_out_b, w_out_c, w_o, w_up, w_down]
    weights = [row2(g_mix), w_in, row2(g_v), row2(b_v), wmix_p, bias_p, wmix_s, bias_s,
               wpool.astype(BF16), row2(pool_scale), w_out_a, w_out_b, w_out_c, w_o,
               row2(g_ffn), w_up, w_down, row2(g_final)]

    def prompt_tile(m):
        return jnp.clip(m, 0, n_prompt - 1)

    def prompt_seq(m):
        return lax.shift_right_logical(prompt_tile(m), _log2(tps))

    def decode_tile(m):
        return jnp.clip(m - n_prompt, 0, n_decode - 1)

    yp, ys, cv_p, ptail_p, cv_s, ptail_s = pl.pallas_call(
        functools.partial(_layer_kernel, tps, n_prompt),
        grid=(n_prompt + n_decode + 1,),
        in_specs=[pl.BlockSpec((1, t, d), lambda s: (prompt_tile(s), 0, 0)),
                  pl.BlockSpec((t, d), lambda s: (decode_tile(s), 0)),
                  pl.BlockSpec((1, xw, n_mem), lambda s: (prompt_seq(s), 0, 0)),
                  pl.BlockSpec((1, n_mem, xw), lambda s: (prompt_seq(s), 0, 0)),
                  pl.BlockSpec((sps * rows, d), lambda s: (prompt_tile(s), 0)),
                  pl.BlockSpec((sps, xw, n_mem), lambda s: (prompt_tile(s), 0, 0)),
                  pl.BlockSpec((sps, xw, n_mem), lambda s: (prompt_tile(s), 0, 0)),
                  pl.BlockSpec((POOL_STATE, ts, pool_w), lambda s: (0, decode_tile(s), 0))]
                 + [pl.BlockSpec(memory_space=pl.ANY) if any(w is b for b in big)
                    else _const_spec(w.shape) for w in weights],
        out_specs=[pl.BlockSpec((1, t, d), lambda s: (prompt_tile(s - 1), 0, 0)),
                   pl.BlockSpec((t, d), lambda s: (decode_tile(s - 1), 0)),
                   pl.BlockSpec((1, CHUNK, sgu_w), lambda s: (prompt_seq(s), 0, 0)),
                   pl.BlockSpec((1, POOL_PAD, pool_w), lambda s: (prompt_seq(s), 0, 0)),
                   pl.BlockSpec((t, sgu_w), lambda s: (decode_tile(s), 0)),
                   pl.BlockSpec((POOL_STATE, ts, pool_w), lambda s: (0, decode_tile(s), 0))],
        out_shape=[jax.ShapeDtypeStruct((n_prompt, t, d), F32),
                   jax.ShapeDtypeStruct((n_seq * rows, d), F32),
                   jax.ShapeDtypeStruct((bsz, CHUNK, sgu_w), F32),
                   jax.ShapeDtypeStruct((bsz, POOL_PAD, pool_w), F32),
                   jax.ShapeDtypeStruct((n_seq * rows, sgu_w), F32),
                   jax.ShapeDtypeStruct((POOL_STATE, n_seq, pool_w), F32)],
        scratch_shapes=[pltpu.VMEM((POOL_PAD + t, pool_w), F32),
                        pltpu.VMEM((pool_w // V7X_LANES, t, V7X_LANES), F32),
                        pltpu.VMEM((pool_w // V7X_LANES, t, V7X_LANES), F32),
                        pltpu.VMEM((t, d), F32),
                        pltpu.VMEM((t, d), BF16)]
                       + [pltpu.VMEM(w.shape, BF16) for w in big]
                       + [pltpu.VMEM((n_seq * rows, xw), F32)],
        compiler_params=pltpu.CompilerParams(
            dimension_semantics=("arbitrary",),
            vmem_limit_bytes=VMEM_LIMIT_BYTES),
        name="layer",
    )(x_prompt.reshape(n_prompt, t, d), x2d, ktb, vb, x2d, cache_kt, cache_vt, state_rows, *weights)
    return (yp.reshape(bsz, seq, d), ys.reshape(n_seq, rows, d), cv_p, ptail_p[:, POOL_PAD - POOL_STATE:, :],
            cv_s.reshape(n_seq, rows, sgu_w), jnp.transpose(ptail_s, (1, 0, 2)))


def _feature_major(kv):
    bsz, n_mem, heads, head_dim = kv.shape
    return jnp.transpose(kv, (0, 2, 3, 1)).reshape(bsz, heads * head_dim, n_mem)


def _token_major(kvt, heads):
    bsz, xw, n_mem = kvt.shape
    return jnp.transpose(kvt.reshape(bsz, heads, xw // heads, n_mem), (0, 3, 1, 2))


def kernel(x_prompt, x_sample, mem_prompt, cache_mem_k, cache_mem_v, state_pool, g_mix, w_in, g_v, b_v, w_s, b_s, w_pool, pool_scale, g_mem, w_kv, w_out_a, w_out_b, w_out_c, w_o, g_ffn, w_up, w_down, g_final):
    depth = w_in.shape[0]
    assert depth == 1, "the final rmsnorm is fused into the (single) layer kernel"
    heads = cache_mem_k.shape[-2]
    n_seq, rows, d = x_sample.shape
    lw = (g_mix[0], w_in[0], g_v[0], b_v[0], w_s[0], b_s[0], w_pool[0], pool_scale[0],
          w_out_a[0], w_out_b[0], w_out_c[0], w_o[0], g_ffn[0], w_up[0], w_down[0], g_final)
    mem_kt, mem_vt, ktb, vb = _mem_kv(mem_prompt, g_mem[0], w_kv[0])
    x2d = x_sample.reshape(n_seq * rows, d)
    y_prompt, y_sample, cv_p, ptail_p, cv_s, ptail_s = _layer(
        x_prompt, x2d, ktb, vb, _feature_major(cache_mem_k[0]), _feature_major(cache_mem_v[0]),
        state_pool[0], lw)
    return (y_prompt, y_sample, _token_major(mem_kt, heads)[None], _token_major(mem_vt, heads)[None],
            ptail_p[None], ptail_s[None], cv_p[None], cv_s[None])
```

```python
import functools
from typing import NamedTuple

import jax
import jax.numpy as jnp
from jax import lax
from jax.experimental import pallas as pl
from jax.experimental.pallas import tpu as pltpu

PAST_LEN = 16384
CHUNK = 128
POOL_WINDOWS = (2, 4, 8, 16)
POOL_STATE = max(POOL_WINDOWS) - 1
X_HEADS = 4
EPS = 1e-6

V7X_VMEM_BYTES = 64 * 1024 * 1024
V7X_LANES = 128
VMEM_RESERVE_BYTES = 6 * 1024 * 1024
VMEM_LIMIT_BYTES = V7X_VMEM_BYTES - VMEM_RESERVE_BYTES

TILE = 256
MEM_KV_SEQS = 4
FFN_SPLIT = 4
LOAD_SLOTS = 8
LOAD_WIDE_ROWS = 32
LOAD_NARROW_ROWS = 128
POOL_PAD = POOL_STATE + 1

BF16 = jnp.bfloat16
F32 = jnp.float32


def _dot(a, b):
    return jnp.dot(a, b, preferred_element_type=F32)


def _log2(n):
    assert n > 0 and n & (n - 1) == 0, n
    return n.bit_length() - 1


def _rmsnorm(x, g):
    return x * lax.rsqrt(jnp.mean(x * x, axis=-1, keepdims=True) + EPS) * g


def _layernorm(x, g, b):
    xc = x - jnp.mean(x, axis=-1, keepdims=True)
    return xc * lax.rsqrt(jnp.mean(xc * xc, axis=-1, keepdims=True) + EPS) * g + b


def _softmax_bf16(s):
    e = jnp.exp(s - jnp.max(s, axis=-1, keepdims=True))
    return (e * (1.0 / jnp.sum(e, axis=-1, keepdims=True))).astype(BF16)


def _const_spec(shape):
    zeros = (0,) * len(shape)
    return pl.BlockSpec(shape, lambda *_: zeros, pipeline_mode=pl.Buffered(1))


def _chunk_mix(vhat_b, wmix_ref, chunk_rows):
    t, sgu_w = vhat_b.shape
    n_groups = wmix_ref.shape[0]
    group_dim = sgu_w // n_groups
    shift = _log2(chunk_rows)
    reps = t // chunk_rows
    row = lax.broadcasted_iota(jnp.int32, (t, t), 0)
    col = lax.broadcasted_iota(jnp.int32, (t, t), 1)
    same_chunk = (row >> shift) == (col >> shift)
    causal = col <= row
    parts = []
    for g in range(n_groups):
        w = jnp.concatenate([wmix_ref[g]] * (t // wmix_ref.shape[2]), axis=1)
        w = jnp.concatenate([w] * reps, axis=0)
        w = jnp.where(same_chunk, jnp.where(causal, w, 0.0), 0.0).astype(BF16)
        parts.append(_dot(w, vhat_b[:, g * group_dim:(g + 1) * group_dim]))
    return jnp.concatenate(parts, axis=1)


def _chunk_bias(bias_ref, t):
    return jnp.concatenate([bias_ref[...]] * (t // bias_ref.shape[0]), axis=0)


def _window_means_minus_self(load_shifted, p, pos):
    group_lanes = p.shape[-1] // len(POOL_WINDOWS)
    outs = []
    for lt in range(p.shape[-1] // V7X_LANES):
        p_lt = p[..., lt * V7X_LANES:(lt + 1) * V7X_LANES]
        lane = lax.broadcasted_iota(jnp.int32, p_lt.shape, p_lt.ndim - 1)
        wins = POOL_WINDOWS[lt * V7X_LANES // group_lanes:(lt + 1) * V7X_LANES // group_lanes]
        acc = p_lt
        sums = {1: p_lt}
        for k in range(1, max(wins)):
            acc = acc + load_shifted(k, lt)
            sums[k + 1] = acc
        win_sum = sums[wins[-1]]
        win = jnp.full(p_lt.shape, wins[-1], jnp.int32)
        for gi in range(len(wins) - 2, -1, -1):
            in_group = lane < (gi + 1) * group_lanes
            win_sum = jnp.where(in_group, sums[wins[gi]], win_sum)
            win = jnp.where(in_group, wins[gi], win)
        cnt = jnp.minimum(pos + 1, win).astype(F32)
        outs.append(win_sum / cnt - p_lt)
    return jnp.concatenate(outs, axis=-1)


def _mem_kv_kernel(mem_ref, g_mem_ref, w_kv_ref, kt_ref, vt_ref, ktb_ref, vb_ref):
    xw = kt_ref.shape[1]
    w_kv = w_kv_ref[...].astype(BF16)
    for i in range(mem_ref.shape[0]):
        hn = _rmsnorm(mem_ref[i], g_mem_ref[...]).astype(BF16)
        kv = _dot(hn, w_kv)
        kt = kv[:, 0:xw].T
        v = kv[:, xw:2 * xw]
        kt_ref[i] = kt
        vt_ref[i] = v.T
        ktb_ref[i] = kt.astype(BF16)
        vb_ref[i] = v.astype(BF16)


def _decode_queries(r):
    x_w = r.ckt.shape[1]
    c_q = r.w_in.shape[1] - 3 * r.xq.shape[1] - x_w
    h = _rmsnorm(r.xq[...], r.g_mix[...]).astype(BF16)
    return _dot(h, r.w_in[:, c_q:c_q + x_w]) * ((x_w // X_HEADS) ** -0.5)


def _decode_own_head(r):
    n_seq, x_w, _ = r.ckt.shape
    rows = r.xq.shape[0] // n_seq
    stack = X_HEADS * rows
    lane_head = lax.broadcasted_iota(jnp.int32, (stack, x_w), 1) >> _log2(x_w // X_HEADS)
    row_head = lax.broadcasted_iota(jnp.int32, (stack, x_w), 0) >> _log2(rows)
    return lane_head == row_head


def _decode_scores(r, q):
    n_seq = r.ckt.shape[0]
    rows = r.xq.shape[0] // n_seq
    own_head = _decode_own_head(r)
    scores = []
    for i in range(n_seq):
        qs = jnp.concatenate([q[i * rows:(i + 1) * rows, :]] * X_HEADS, axis=0)
        qm = jnp.where(own_head, qs, 0.0).astype(BF16)
        scores.append(_dot(qm, r.ckt[i].astype(BF16)))
    return scores


def _decode_values(r, probs, step):
    n_seq = r.cvt.shape[0]
    rows = r.xq.shape[0] // n_seq
    own_head = _decode_own_head(r)
    for i in range(n_seq):
        of = _dot(probs[i], r.cvt[i].T.astype(BF16))
        of = jnp.where(own_head, of, 0.0)
        out = of[0:rows]
        for hd in range(1, X_HEADS):
            out = out + of[hd * rows:(hd + 1) * rows]
        row0 = pl.multiple_of((step * n_seq + i) * rows, rows)
        r.attn[pl.ds(row0, rows), :] = out


class _LayerRefs(NamedTuple):
    xp: object
    xs: object
    kt: object
    vb: object
    xq: object
    ckt: object
    cvt: object
    state: object
    g_mix: object
    w_in_f32: object
    g_v: object
    b_v: object
    wmix_p: object
    bias_p: object
    wmix_s: object
    bias_s: object
    wpool: object
    pool_scale: object
    w_out_a_f32: object
    w_out_b_f32: object
    w_out_c_f32: object
    w_o_f32: object
    g_ffn: object
    w_up_f32: object
    w_down_f32: object
    g_final: object
    yp: object
    ys: object
    cv_p: object
    ptail_p: object
    cv_s: object
    ptail_s: object
    ext_p: object
    pool_in: object
    pool_out: object
    x1: object
    h2: object
    w_in: object
    w_out_a: object
    w_out_b: object
    w_out_c: object
    w_o: object
    w_up: object
    w_down: object
    attn: object


def _layer_step(r, decode, j, step, n_prompt):
    t, d = r.x1.shape
    sgu_w = r.g_v.shape[-1]
    pool_w = r.pool_scale.shape[-1]
    x_w = r.attn.shape[-1]
    n_decode = r.attn.shape[0] // t
    head_dim = x_w // X_HEADS
    c_pool = 2 * sgu_w
    c_q = c_pool + pool_w
    c_gate = c_q + x_w

    assert FFN_SPLIT == 4, "the schedule below places four channel-mixer slabs by hand"
    ffn_up = functools.partial(_ffn_up, r)
    ffn_down = functools.partial(_ffn_down, r)

    def gate(h, part):
        lo = c_gate + part * d
        return jax.nn.sigmoid(_dot(h, r.w_in[:, lo:lo + d]))

    act = ffn_up(0)
    if not decode:
        dec_q = _decode_queries(r)
    x = r.xs[...] if decode else r.xp[0]
    h = _rmsnorm(x, r.g_mix[...]).astype(BF16)
    u = _dot(h, r.w_in[:, 0:sgu_w])
    v = _dot(h, r.w_in[:, sgu_w:2 * sgu_w])
    gate_a = gate(h, 0)
    x2 = r.x1[...] + ffn_down(act, 0)

    u = jax.nn.gelu(u)
    vhat = _layernorm(jax.nn.gelu(v), r.g_v[...], r.b_v[...])
    if decode:
        rows = r.wmix_s.shape[1]
        r.cv_s[...] = vhat
        mixed = _chunk_mix(vhat.astype(BF16), r.wmix_s, rows)
        bias = _chunk_bias(r.bias_s, t)
    else:
        r.cv_p[0] = vhat[t - CHUNK:, :]
        mixed = _chunk_mix(vhat.astype(BF16), r.wmix_p, CHUNK)
        bias = _chunk_bias(r.bias_p, t)
    p = _dot(h, r.w_in[:, c_pool:c_pool + pool_w])
    if not decode:
        q = _dot(h, r.w_in[:, c_q:c_q + x_w]) * (head_dim ** -0.5)
        dec_scores = _decode_scores(r, dec_q)
    act = ffn_up(1)
    a_in = (u * (mixed + bias)).astype(BF16)

    if decode:
        n_seq = t // rows
        lane_tiles = range(pool_w // V7X_LANES)
        for lt in lane_tiles:
            r.pool_in[lt] = p[:, lt * V7X_LANES:(lt + 1) * V7X_LANES]
        ext = [r.state[k] for k in range(POOL_STATE)]
        ext += [jnp.concatenate([r.pool_in[lt, pl.ds(i, n_seq, stride=rows), :]
                                 for lt in lane_tiles], axis=-1) for i in range(rows)]
        for i in range(rows):
            cur = POOL_STATE + i

            def load_shifted(k, lt, cur=cur):
                return ext[cur - k][:, lt * V7X_LANES:(lt + 1) * V7X_LANES]

            res = _window_means_minus_self(load_shifted, ext[cur], PAST_LEN + i)
            for lt in lane_tiles:
                r.pool_out[lt, pl.ds(i, n_seq, stride=rows), :] = (
                    res[:, lt * V7X_LANES:(lt + 1) * V7X_LANES])
        for k in range(POOL_STATE):
            r.ptail_s[k] = ext[rows + k]
        pooled = jnp.concatenate([r.pool_out[lt] for lt in lane_tiles], axis=-1)
    else:
        r.ext_p[POOL_PAD:POOL_PAD + t, :] = p

        def load_shifted(k, lt):
            return r.ext_p[POOL_PAD - k:POOL_PAD - k + t, lt * V7X_LANES:(lt + 1) * V7X_LANES]

        pos = j * t + lax.broadcasted_iota(jnp.int32, (t, V7X_LANES), 0)
        pooled = _window_means_minus_self(load_shifted, p, pos)
        tail = r.ext_p[t:t + POOL_PAD, :]
        r.ext_p[0:POOL_PAD, :] = tail
        r.ptail_p[0] = tail
    pooled = pooled.astype(BF16)

    a = _dot(a_in, r.w_out_a[...])

    if not decode:
        head = lax.broadcasted_iota(jnp.int32, (t, x_w), 1) >> _log2(head_dim)
        kt = r.kt[0]
        scores = [_dot(jnp.where(head == hd, q, 0.0).astype(BF16), kt) for hd in range(X_HEADS)]
    pool_mixed = _dot(pooled, r.wpool[...]) * r.pool_scale[...]
    x2 = x2 + ffn_down(act, 1)
    if decode:
        row0 = pl.multiple_of(jnp.clip(step - n_prompt, 0, n_decode - 1) * t, t)
        o = r.attn[pl.ds(row0, t), :]
    else:
        probs = [_softmax_bf16(s) for s in scores]
        dec_probs = [_softmax_bf16(s) for s in dec_scores]
        vb = r.vb[0]
        o = jnp.zeros((t, x_w), F32)
        for hd in range(X_HEADS):
            o = jnp.where(head == hd, _dot(probs[hd], vb), o)
        _decode_values(r, dec_probs, step)
    b = _dot(pool_mixed.astype(BF16), r.w_out_b[...])
    gate_b = gate(h, 1)
    c = _dot(o.astype(BF16), r.w_out_c[...])
    act = ffn_up(2)
    gate_c = gate(h, 2)
    x2 = x2 + ffn_down(act, 2)
    merged = (gate_a * a + gate_b * b + gate_c * c).astype(BF16)
    act = ffn_up(3)
    x1 = x + _dot(merged, r.w_o[...])
    x2 = x2 + ffn_down(act, 3)
    r.x1[...] = x1
    r.h2[...] = _rmsnorm(x1, r.g_ffn[...]).astype(BF16)
    y = _rmsnorm(x2, r.g_final[...])
    if not decode:
        r.yp[0] = y
    else:
        @pl.when(step == n_prompt)
        def _():
            r.yp[0] = y

        @pl.when(step != n_prompt)
        def _():
            r.ys[...] = y


def _ffn_up(r, s):
    slab = r.w_up.shape[1] // FFN_SPLIT
    up = _dot(r.h2[...], r.w_up[:, s * slab:(s + 1) * slab])
    return jnp.square(jnp.maximum(up, 0.0)).astype(BF16)


def _ffn_down(r, act, s):
    slab = r.w_up.shape[1] // FFN_SPLIT
    return _dot(act, r.w_down[s * slab:(s + 1) * slab, :])


def _drain_step(r):
    x2 = r.x1[...]
    act = _ffn_up(r, 0)
    for s in range(FFN_SPLIT):
        nxt = _ffn_up(r, s + 1) if s + 1 < FFN_SPLIT else None
        x2 = x2 + _ffn_down(r, act, s)
        act = nxt
    r.ys[...] = _rmsnorm(x2, r.g_final[...])


def _load_weights_bf16(r):
    ahead = LOAD_SLOTS - 1
    narrow_cols = r.w_o_f32.shape[1]
    wide_cols = max(r.w_in_f32.shape[1], r.w_up_f32.shape[1])

    def load(stage_w, stage_n, sem_w, sem_n):
        def stream(src, dst):
            narrow = src.shape[1] == narrow_cols
            stage, sem = (stage_n, sem_n) if narrow else (stage_w, sem_w)
            rows = stage.shape[1]
            n_chunks = src.shape[0] // rows
            assert src.shape[0] % rows == 0, (src.shape, rows)

            def copy(i, slot):
                return pltpu.make_async_copy(src.at[pl.ds(i * rows, rows), :],
                                             stage.at[slot, :, 0:src.shape[1]], sem.at[slot])

            def prime():
                for i in range(min(ahead, n_chunks)):
                    copy(i, i % LOAD_SLOTS).start()

            def drain():
                def body(i, carry):
                    @pl.when(i + ahead < n_chunks)
                    def _():
                        copy(i + ahead, lax.rem(i + ahead, LOAD_SLOTS)).start()

                    slot = lax.rem(i, LOAD_SLOTS)
                    copy(i, slot).wait()
                    row0 = pl.multiple_of(i * rows, rows)
                    dst[pl.ds(row0, rows), :] = stage[slot, :, 0:src.shape[1]].astype(BF16)
                    return carry

                lax.fori_loop(0, n_chunks, body, 0)

            return prime, drain

        order = [(r.w_in_f32, r.w_in), (r.w_down_f32, r.w_down), (r.w_up_f32, r.w_up),
                 (r.w_o_f32, r.w_o), (r.w_out_a_f32, r.w_out_a), (r.w_out_b_f32, r.w_out_b),
                 (r.w_out_c_f32, r.w_out_c)]
        streams = [stream(src, dst) for src, dst in order]
        streams[0][0]()
        for k, (_, drain) in enumerate(streams):
            nxt_uses_other_ring = (k + 1 < len(order)
                                   and (order[k + 1][0].shape[1] == narrow_cols)
                                   != (order[k][0].shape[1] == narrow_cols))
            if nxt_uses_other_ring:
                streams[k + 1][0]()
            drain()
            if k + 1 < len(order) and not nxt_uses_other_ring:
                streams[k + 1][0]()

    pl.run_scoped(load,
                  pltpu.VMEM((LOAD_SLOTS, LOAD_WIDE_ROWS, wide_cols), F32),
                  pltpu.VMEM((LOAD_SLOTS, LOAD_NARROW_ROWS, narrow_cols), F32),
                  pltpu.SemaphoreType.DMA((LOAD_SLOTS,)),
                  pltpu.SemaphoreType.DMA((LOAD_SLOTS,)))


def _layer_kernel(tiles_per_seq, n_prompt, *refs):
    r = _LayerRefs(*refs)
    step = pl.program_id(0)
    last = pl.num_programs(0) - 1
    j = jnp.minimum(step, n_prompt - 1) & (tiles_per_seq - 1)

    @pl.when(step == 0)
    def _():
        r.x1[...] = jnp.zeros(r.x1.shape, F32)
        r.h2[...] = jnp.zeros(r.h2.shape, BF16)
        _load_weights_bf16(r)

    @pl.when(jnp.logical_and(step < n_prompt, j == 0))
    def _():
        r.ext_p[0:POOL_PAD, :] = jnp.zeros((POOL_PAD, r.ext_p.shape[1]), F32)

    @pl.when(step < n_prompt)
    def _():
        _layer_step(r, False, j, step, n_prompt)

    @pl.when(jnp.logical_and(step >= n_prompt, step < last))
    def _():
        _layer_step(r, True, j, step, n_prompt)

    @pl.when(step == last)
    def _():
        _drain_step(r)


def _mem_kv(mem, g_mem, w_kv):
    bsz, n_mem, d = mem.shape
    xw = w_kv.shape[-1] // 2
    sb = MEM_KV_SEQS
    assert bsz % sb == 0, (bsz, sb)
    return pl.pallas_call(
        _mem_kv_kernel,
        grid=(bsz // sb,),
        in_specs=[pl.BlockSpec((sb, n_mem, d), lambda i: (i, 0, 0)),
                  _const_spec((1, d)), _const_spec((d, 2 * xw))],
        out_specs=[pl.BlockSpec((sb, xw, n_mem), lambda i: (i, 0, 0)),
                   pl.BlockSpec((sb, xw, n_mem), lambda i: (i, 0, 0)),
                   pl.BlockSpec((sb, xw, n_mem), lambda i: (i, 0, 0)),
                   pl.BlockSpec((sb, n_mem, xw), lambda i: (i, 0, 0))],
        out_shape=[jax.ShapeDtypeStruct((bsz, xw, n_mem), F32),
                   jax.ShapeDtypeStruct((bsz, xw, n_mem), F32),
                   jax.ShapeDtypeStruct((bsz, xw, n_mem), BF16),
                   jax.ShapeDtypeStruct((bsz, n_mem, xw), BF16)],
        compiler_params=pltpu.CompilerParams(
            dimension_semantics=("arbitrary",),
            vmem_limit_bytes=VMEM_LIMIT_BYTES),
        name="mem_kv",
    )(mem, g_mem.reshape(1, d).astype(F32), w_kv)


def _spatial_operands(w_s, b_s, group_dim, chunk_rows, tile_rows):
    wmix = jnp.tile(w_s[:, :chunk_rows, :chunk_rows], (1, 1, tile_rows // chunk_rows))
    bias = jnp.repeat(b_s[:, :chunk_rows].T, group_dim, axis=1)
    return wmix.astype(F32), bias.astype(F32)


def _layer(x_prompt, x2d, ktb, vb, cache_kt, cache_vt, state, lw):
    (g_mix, w_in, g_v, b_v, w_s, b_s, w_pool, pool_scale,
     w_out_a, w_out_b, w_out_c, w_o, g_ffn, w_up, w_down, g_final) = lw
    bsz, seq, d = x_prompt.shape
    n_seq = state.shape[0]
    rows = x2d.shape[0] // n_seq
    t = TILE
    ts = t // rows
    tps = seq // t
    n_prompt = bsz * tps
    n_decode = n_seq // ts
    assert n_seq % n_prompt == 0, (n_seq, n_prompt)
    sps = n_seq // n_prompt
    sgu_w = g_v.shape[-1]
    pool_w = pool_scale.shape[-1]
    n_mem, xw = vb.shape[1:]
    group_dim = sgu_w // w_s.shape[0]
    row2 = lambda vec: vec.reshape(1, -1).astype(F32)
    wmix_p, bias_p = _spatial_operands(w_s, b_s, group_dim, CHUNK, CHUNK)
    wmix_s, bias_s = _spatial_operands(w_s, b_s, group_dim, rows, t)
    wpool = jax.scipy.linalg.block_diag(*[w_pool[i] for i in range(w_pool.shape[0])])
    state_rows = jnp.transpose(state, (1, 0, 2))
    big = [w_in, w_out_a, w_out_b, w_out_c, w_o, w_up, w_down]
    weights = [row2(g_mix), w_in, row2(g_v), row2(b_v), wmix_p, bias_p, wmix_s, bias_s,
               wpool.astype(BF16), row2(pool_scale), w_out_a, w_out_b, w_out_c, w_o,
               row2(g_ffn), w_up, w_down, row2(g_final)]

    def prompt_tile(m):
        return jnp.clip(m, 0, n_prompt - 1)

    def prompt_seq(m):
        return lax.shift_right_logical(prompt_tile(m), _log2(tps))

    def decode_tile(m):
        return jnp.clip(m - n_prompt, 0, n_decode - 1)

    yp, ys, cv_p, ptail_p, cv_s, ptail_s = pl.pallas_call(
        functools.partial(_layer_kernel, tps, n_prompt),
        grid=(n_prompt + n_decode + 1,),
        in_specs=[pl.BlockSpec((1, t, d), lambda s: (prompt_tile(s), 0, 0)),
                  pl.BlockSpec((t, d), lambda s: (decode_tile(s), 0)),
                  pl.BlockSpec((1, xw, n_mem), lambda s: (prompt_seq(s), 0, 0)),
                  pl.BlockSpec((1, n_mem, xw), lambda s: (prompt_seq(s), 0, 0)),
                  pl.BlockSpec((sps * rows, d), lambda s: (prompt_tile(s), 0)),
                  pl.BlockSpec((sps, xw, n_mem), lambda s: (prompt_tile(s), 0, 0)),
                  pl.BlockSpec((sps, xw, n_mem), lambda s: (prompt_tile(s), 0, 0)),
                  pl.BlockSpec((POOL_STATE, ts, pool_w), lambda s: (0, decode_tile(s), 0))]
                 + [pl.BlockSpec(memory_space=pl.ANY) if any(w is b for b in big)
                    else _const_spec(w.shape) for w in weights],
        out_specs=[pl.BlockSpec((1, t, d), lambda s: (prompt_tile(s - 1), 0, 0)),
                   pl.BlockSpec((t, d), lambda s: (decode_tile(s - 1), 0)),
                   pl.BlockSpec((1, CHUNK, sgu_w), lambda s: (prompt_seq(s), 0, 0)),
                   pl.BlockSpec((1, POOL_PAD, pool_w), lambda s: (prompt_seq(s), 0, 0)),
                   pl.BlockSpec((t, sgu_w), lambda s: (decode_tile(s), 0)),
                   pl.BlockSpec((POOL_STATE, ts, pool_w), lambda s: (0, decode_tile(s), 0))],
        out_shape=[jax.ShapeDtypeStruct((n_prompt, t, d), F32),
                   jax.ShapeDtypeStruct((n_seq * rows, d), F32),
                   jax.ShapeDtypeStruct((bsz, CHUNK, sgu_w), F32),
                   jax.ShapeDtypeStruct((bsz, POOL_PAD, pool_w), F32),
                   jax.ShapeDtypeStruct((n_seq * rows, sgu_w), F32),
                   jax.ShapeDtypeStruct((POOL_STATE, n_seq, pool_w), F32)],
        scratch_shapes=[pltpu.VMEM((POOL_PAD + t, pool_w), F32),
                        pltpu.VMEM((pool_w // V7X_LANES, t, V7X_LANES), F32),
                        pltpu.VMEM((pool_w // V7X_LANES, t, V7X_LANES), F32),
                        pltpu.VMEM((t, d), F32),
                        pltpu.VMEM((t, d), BF16)]
                       + [pltpu.VMEM(w.shape, BF16) for w in big]
                       + [pltpu.VMEM((n_seq * rows, xw), F32)],
        compiler_params=pltpu.CompilerParams(
            dimension_semantics=("arbitrary",),
            vmem_limit_bytes=VMEM_LIMIT_BYTES),
        name="layer",
    )(x_prompt.reshape(n_prompt, t, d), x2d, ktb, vb, x2d, cache_kt, cache_vt, state_rows, *weights)
    return (yp.reshape(bsz, seq, d), ys.reshape(n_seq, rows, d), cv_p, ptail_p[:, POOL_PAD - POOL_STATE:, :],
            cv_s.reshape(n_seq, rows, sgu_w), jnp.transpose(ptail_s, (1, 0, 2)))


def _feature_major(kv):
    bsz, n_mem, heads, head_dim = kv.shape
    return jnp.transpose(kv, (0, 2, 3, 1)).reshape(bsz, heads * head_dim, n_mem)


def _token_major(kvt, heads):
    bsz, xw, n_mem = kvt.shape
    return jnp.transpose(kvt.reshape(bsz, heads, xw // heads, n_mem), (0, 3, 1, 2))


def kernel(x_prompt, x_sample, mem_prompt, cache_mem_k, cache_mem_v, state_pool, g_mix, w_in, g_v, b_v, w_s, b_s, w_pool, pool_scale, g_mem, w_kv, w_out_a, w_out_b, w_out_c, w_o, g_ffn, w_up, w_down, g_final):
    depth = w_in.shape[0]
    assert depth == 1, "the final rmsnorm is fused into the (single) layer kernel"
    heads = cache_mem_k.shape[-2]
    n_seq, rows, d = x_sample.shape
    lw = (g_mix[0], w_in[0], g_v[0], b_v[0], w_s[0], b_s[0], w_pool[0], pool_scale[0],
          w_out_a[0], w_out_b[0], w_out_c[0], w_o[0], g_ffn[0], w_up[0], w_down[0], g_final)
    mem_kt, mem_vt, ktb, vb = _mem_kv(mem_prompt, g_mem[0], w_kv[0])
    x2d = x_sample.reshape(n_seq * rows, d)
    y_prompt, y_sample, cv_p, ptail_p, cv_s, ptail_s = _layer(
        x_prompt, x2d, ktb, vb, _feature_major(cache_mem_k[0]), _feature_major(cache_mem_v[0]),
        state_pool[0], lw)
    return (y_prompt, y_sample, _token_major(mem_kt, heads)[None], _token_major(mem_vt, heads)[None],
            ptail_p[None], ptail_s[None], cv_p[None], cv_s[None])
```

```python
import functools
from typing import NamedTuple

import jax
import jax.numpy as jnp
from jax import lax
from jax.experimental import pallas as pl
from jax.experimental.pallas import tpu as pltpu

PAST_LEN = 16384
CHUNK = 128
POOL_WINDOWS = (2, 4, 8, 16)
POOL_STATE = max(POOL_WINDOWS) - 1
X_HEADS = 4
EPS = 1e-6

V7X_VMEM_BYTES = 64 * 1024 * 1024
V7X_LANES = 128
VMEM_RESERVE_BYTES = 6 * 1024 * 1024
VMEM_LIMIT_BYTES = V7X_VMEM_BYTES - VMEM_RESERVE_BYTES

TILE = 256
MEM_KV_SEQS = 4
FFN_SPLIT = 4
LOAD_SLOTS = 8
LOAD_WIDE_ROWS = 32
LOAD_NARROW_ROWS = 128
POOL_PAD = POOL_STATE + 1

BF16 = jnp.bfloat16
F32 = jnp.float32


def _dot(a, b):
    return jnp.dot(a, b, preferred_element_type=F32)


def _log2(n):
    assert n > 0 and n & (n - 1) == 0, n
    return n.bit_length() - 1


def _rmsnorm(x, g):
    return x * lax.rsqrt(jnp.mean(x * x, axis=-1, keepdims=True) + EPS) * g


def _layernorm(x, g, b):
    xc = x - jnp.mean(x, axis=-1, keepdims=True)
    return xc * lax.rsqrt(jnp.mean(xc * xc, axis=-1, keepdims=True) + EPS) * g + b


def _softmax_bf16(s):
    e = jnp.exp(s - jnp.max(s, axis=-1, keepdims=True))
    return (e * (1.0 / jnp.sum(e, axis=-1, keepdims=True))).astype(BF16)


def _const_spec(shape):
    zeros = (0,) * len(shape)
    return pl.BlockSpec(shape, lambda *_: zeros, pipeline_mode=pl.Buffered(1))


def _chunk_mix(vhat_b, wmix_ref, chunk_rows):
    t, sgu_w = vhat_b.shape
    n_groups = wmix_ref.shape[0]
    group_dim = sgu_w // n_groups
    shift = _log2(chunk_rows)
    reps = t // chunk_rows
    row = lax.broadcasted_iota(jnp.int32, (t, t), 0)
    col = lax.broadcasted_iota(jnp.int32, (t, t), 1)
    same_chunk = (row >> shift) == (col >> shift)
    causal = col <= row
    parts = []
    for g in range(n_groups):
        w = jnp.concatenate([wmix_ref[g]] * (t // wmix_ref.shape[2]), axis=1)
        w = jnp.concatenate([w] * reps, axis=0)
        w = jnp.where(same_chunk, jnp.where(causal, w, 0.0), 0.0).astype(BF16)
        parts.append(_dot(w, vhat_b[:, g * group_dim:(g + 1) * group_dim]))
    return jnp.concatenate(parts, axis=1)


def _chunk_bias(bias_ref, t):
    return jnp.concatenate([bias_ref[...]] * (t // bias_ref.shape[0]), axis=0)


def _window_means_minus_self(load_shifted, p, pos):
    group_lanes = p.shape[-1] // len(POOL_WINDOWS)
    outs = []
    for lt in range(p.shape[-1] // V7X_LANES):
        p_lt = p[..., lt * V7X_LANES:(lt + 1) * V7X_LANES]
        lane = lax.broadcasted_iota(jnp.int32, p_lt.shape, p_lt.ndim - 1)
        wins = POOL_WINDOWS[lt * V7X_LANES // group_lanes:(lt + 1) * V7X_LANES // group_lanes]
        acc = p_lt
        sums = {1: p_lt}
        for k in range(1, max(wins)):
            acc = acc + load_shifted(k, lt)
            sums[k + 1] = acc
        win_sum = sums[wins[-1]]
        win = jnp.full(p_lt.shape, wins[-1], jnp.int32)
        for gi in range(len(wins) - 2, -1, -1):
            in_group = lane < (gi + 1) * group_lanes
            win_sum = jnp.where(in_group, sums[wins[gi]], win_sum)
            win = jnp.where(in_group, wins[gi], win)
        cnt = jnp.minimum(pos + 1, win).astype(F32)
        outs.append(win_sum / cnt - p_lt)
    return jnp.concatenate(outs, axis=-1)


def _mem_kv_kernel(mem_ref, g_mem_ref, w_kv_ref, kt_ref, vt_ref, ktb_ref, vb_ref):
    xw = kt_ref.shape[1]
    w_kv = w_kv_ref[...].astype(BF16)
    for i in range(mem_ref.shape[0]):
        hn = _rmsnorm(mem_ref[i], g_mem_ref[...]).astype(BF16)
        kv = _dot(hn, w_kv)
        kt = kv[:, 0:xw].T
        v = kv[:, xw:2 * xw]
        kt_ref[i] = kt
        vt_ref[i] = v.T
        ktb_ref[i] = kt.astype(BF16)
        vb_ref[i] = v.astype(BF16)


def _decode_queries(r):
    x_w = r.ckt.shape[1]
    c_q = r.w_in.shape[1] - 3 * r.xq.shape[1] - x_w
    h = _rmsnorm(r.xq[...], r.g_mix[...]).astype(BF16)
    return _dot(h, r.w_in[:, c_q:c_q + x_w]) * ((x_w // X_HEADS) ** -0.5)


def _decode_own_head(r):
    n_seq, x_w, _ = r.ckt.shape
    rows = r.xq.shape[0] // n_seq
    stack = X_HEADS * rows
    lane_head = lax.broadcasted_iota(jnp.int32, (stack, x_w), 1) >> _log2(x_w // X_HEADS)
    row_head = lax.broadcasted_iota(jnp.int32, (stack, x_w), 0) >> _log2(rows)
    return lane_head == row_head


def _decode_scores(r, q):
    n_seq = r.ckt.shape[0]
    rows = r.xq.shape[0] // n_seq
    own_head = _decode_own_head(r)
    scores = []
    for i in range(n_seq):
        qs = jnp.concatenate([q[i * rows:(i + 1) * rows, :]] * X_HEADS, axis=0)
        qm = jnp.where(own_head, qs, 0.0).astype(BF16)
        scores.append(_dot(qm, r.ckt[i].astype(BF16)))
    return scores


def _decode_values(r, probs, step):
    n_seq = r.cvt.shape[0]
    rows = r.xq.shape[0] // n_seq
    own_head = _decode_own_head(r)
    for i in range(n_seq):
        of = _dot(probs[i], r.cvt[i].T.astype(BF16))
        of = jnp.where(own_head, of, 0.0)
        out = of[0:rows]
        for hd in range(1, X_HEADS):
            out = out + of[hd * rows:(hd + 1) * rows]
        row0 = pl.multiple_of((step * n_seq + i) * rows, rows)
        r.attn[pl.ds(row0, rows), :] = out


class _LayerRefs(NamedTuple):
    xp: object
    xs: object
    kt: object
    vb: object
    xq: object
    ckt: object
    cvt: object
    state: object
    g_mix: object
    w_in_f32: object
    g_v: object
    b_v: object
    wmix_p: object
    bias_p: object
    wmix_s: object
    bias_s: object
    wpool: object
    pool_scale: object
    w_out_a_f32: object
    w_out_b_f32: object
    w_out_c_f32: object
    w_o_f32: object
    g_ffn: object
    w_up_f32: object
    w_down_f32: object
    g_final: object
    yp: object
    ys: object
    cv_p: object
    ptail_p: object
    cv_s: object
    ptail_s: object
    ext_p: object
    pool_in: object
    pool_out: object
    x1: object
    h2: object
    w_in: object
    w_out_a: object
    w_out_b: object
    w_out_c: object
    w_o: object
    w_up: object
    w_down: object
    attn: object


def _layer_step(r, decode, j, step, n_prompt):
    t, d = r.x1.shape
    sgu_w = r.g_v.shape[-1]
    pool_w = r.pool_scale.shape[-1]
    x_w = r.attn.shape[-1]
    n_decode = r.attn.shape[0] // t
    head_dim = x_w // X_HEADS
    c_pool = 2 * sgu_w
    c_q = c_pool + pool_w
    c_gate = c_q + x_w

    assert FFN_SPLIT == 4, "the schedule below places four channel-mixer slabs by hand"
    ffn_up = functools.partial(_ffn_up, r)
    ffn_down = functools.partial(_ffn_down, r)

    def gate(h, part):
        lo = c_gate + part * d
        return jax.nn.sigmoid(_dot(h, r.w_in[:, lo:lo + d]))

    act = ffn_up(0)
    if not decode:
        dec_q = _decode_queries(r)
    x = r.xs[...] if decode else r.xp[0]
    h = _rmsnorm(x, r.g_mix[...]).astype(BF16)
    uv = _dot(h, r.w_in[:, 0:2 * sgu_w])
    u = uv[:, 0:sgu_w]
    v = uv[:, sgu_w:2 * sgu_w]
    gate_a = gate(h, 0)
    x2 = r.x1[...] + ffn_down(act, 0)

    u = jax.nn.gelu(u)
    vhat = _layernorm(jax.nn.gelu(v), r.g_v[...], r.b_v[...])
    if decode:
        rows = r.wmix_s.shape[1]
        r.cv_s[...] = vhat
        mixed = _chunk_mix(vhat.astype(BF16), r.wmix_s, rows)
        bias = _chunk_bias(r.bias_s, t)
    else:
        r.cv_p[0] = vhat[t - CHUNK:, :]
        mixed = _chunk_mix(vhat.astype(BF16), r.wmix_p, CHUNK)
        bias = _chunk_bias(r.bias_p, t)
    if decode:
        p = _dot(h, r.w_in[:, c_pool:c_pool + pool_w])
    else:
        pq = _dot(h, r.w_in[:, c_pool:c_q + x_w])
        p = pq[:, 0:pool_w]
        q = pq[:, pool_w:pool_w + x_w] * (head_dim ** -0.5)
        dec_scores = _decode_scores(r, dec_q)
    act = ffn_up(1)
    a_in = (u * (mixed + bias)).astype(BF16)

    if decode:
        n_seq = t // rows
        lane_tiles = range(pool_w // V7X_LANES)
        for lt in lane_tiles:
            r.pool_in[lt] = p[:, lt * V7X_LANES:(lt + 1) * V7X_LANES]
        ext = [r.state[k] for k in range(POOL_STATE)]
        ext += [jnp.concatenate([r.pool_in[lt, pl.ds(i, n_seq, stride=rows), :]
                                 for lt in lane_tiles], axis=-1) for i in range(rows)]
        for i in range(rows):
            cur = POOL_STATE + i

            def load_shifted(k, lt, cur=cur):
                return ext[cur - k][:, lt * V7X_LANES:(lt + 1) * V7X_LANES]

            res = _window_means_minus_self(load_shifted, ext[cur], PAST_LEN + i)
            for lt in lane_tiles:
                r.pool_out[lt, pl.ds(i, n_seq, stride=rows), :] = (
                    res[:, lt * V7X_LANES:(lt + 1) * V7X_LANES])
        for k in range(POOL_STATE):
            r.ptail_s[k] = ext[rows + k]
        pooled = jnp.concatenate([r.pool_out[lt] for lt in lane_tiles], axis=-1)
    else:
        r.ext_p[POOL_PAD:POOL_PAD + t, :] = p

        def load_shifted(k, lt):
            return r.ext_p[POOL_PAD - k:POOL_PAD - k + t, lt * V7X_LANES:(lt + 1) * V7X_LANES]

        pos = j * t + lax.broadcasted_iota(jnp.int32, (t, V7X_LANES), 0)
        pooled = _window_means_minus_self(load_shifted, p, pos)
        tail = r.ext_p[t:t + POOL_PAD, :]
        r.ext_p[0:POOL_PAD, :] = tail
        r.ptail_p[0] = tail
    pooled = pooled.astype(BF16)

    a = _dot(a_in, r.w_out_a[...])

    if not decode:
        head = lax.broadcasted_iota(jnp.int32, (t, x_w), 1) >> _log2(head_dim)
        kt = r.kt[0]
        q_heads = jnp.concatenate(
            [jnp.where(head == hd, q, 0.0).astype(BF16) for hd in range(X_HEADS)], axis=0)
        scores = _dot(q_heads, kt)
    pool_mixed = _dot(pooled, r.wpool[...]) * r.pool_scale[...]
    x2 = x2 + ffn_down(act, 1)
    if decode:
        row0 = pl.multiple_of(jnp.clip(step - n_prompt, 0, n_decode - 1) * t, t)
        o = r.attn[pl.ds(row0, t), :]
    else:
        dec_probs = [_softmax_bf16(s) for s in dec_scores]
        o_heads = _dot(_softmax_bf16(scores), r.vb[0])
        o = jnp.zeros((t, x_w), F32)
        for hd in range(X_HEADS):
            o = jnp.where(head == hd, o_heads[hd * t:(hd + 1) * t], o)
        _decode_values(r, dec_probs, step)
    b = _dot(pool_mixed.astype(BF16), r.w_out_b[...])
    gate_b = gate(h, 1)
    c = _dot(o.astype(BF16), r.w_out_c[...])
    act = ffn_up(2)
    gate_c = gate(h, 2)
    x2 = x2 + ffn_down(act, 2)
    merged = (gate_a * a + gate_b * b + gate_c * c).astype(BF16)
    act = ffn_up(3)
    x2 = x2 + ffn_down(act, 3)
    x1 = x + _dot(merged, r.w_o[...])
    r.x1[...] = x1
    r.h2[...] = _rmsnorm(x1, r.g_ffn[...]).astype(BF16)
    y = _rmsnorm(x2, r.g_final[...])
    if not decode:
        r.yp[0] = y
    else:
        @pl.when(step == n_prompt)
        def _():
            r.yp[0] = y

        @pl.when(step != n_prompt)
        def _():
            r.ys[...] = y


def _ffn_up(r, s):
    slab = r.w_up.shape[1] // FFN_SPLIT
    up = _dot(r.h2[...], r.w_up[:, s * slab:(s + 1) * slab])
    return jnp.square(jnp.maximum(up, 0.0)).astype(BF16)


def _ffn_down(r, act, s):
    slab = r.w_up.shape[1] // FFN_SPLIT
    return _dot(act, r.w_down[s * slab:(s + 1) * slab, :])


def _drain_step(r):
    x2 = r.x1[...]
    act = _ffn_up(r, 0)
    for s in range(FFN_SPLIT):
        nxt = _ffn_up(r, s + 1) if s + 1 < FFN_SPLIT else None
        x2 = x2 + _ffn_down(r, act, s)
        act = nxt
    r.ys[...] = _rmsnorm(x2, r.g_final[...])


def _load_weights_bf16(r):
    ahead = LOAD_SLOTS - 1
    narrow_cols = r.w_o_f32.shape[1]
    wide_cols = max(r.w_in_f32.shape[1], r.w_up_f32.shape[1])

    def load(stage_w, stage_n, sem_w, sem_n):
        def stream(src, dst):
            narrow = src.shape[1] == narrow_cols
            stage, sem = (stage_n, sem_n) if narrow else (stage_w, sem_w)
            rows = stage.shape[1]
            n_chunks = src.shape[0] // rows
            assert src.shape[0] % rows == 0, (src.shape, rows)

            def copy(i, slot):
                return pltpu.make_async_copy(src.at[pl.ds(i * rows, rows), :],
                                             stage.at[slot, :, 0:src.shape[1]], sem.at[slot])

            def prime():
                for i in range(min(ahead, n_chunks)):
                    copy(i, i % LOAD_SLOTS).start()

            def drain():
                def body(i, carry):
                    @pl.when(i + ahead < n_chunks)
                    def _():
                        copy(i + ahead, lax.rem(i + ahead, LOAD_SLOTS)).start()

                    slot = lax.rem(i, LOAD_SLOTS)
                    copy(i, slot).wait()
                    row0 = pl.multiple_of(i * rows, rows)
                    dst[pl.ds(row0, rows), :] = stage[slot, :, 0:src.shape[1]].astype(BF16)
                    return carry

                lax.fori_loop(0, n_chunks, body, 0)

            return prime, drain

        order = [(r.w_in_f32, r.w_in), (r.w_down_f32, r.w_down), (r.w_up_f32, r.w_up),
                 (r.w_o_f32, r.w_o), (r.w_out_a_f32, r.w_out_a), (r.w_out_b_f32, r.w_out_b),
                 (r.w_out_c_f32, r.w_out_c)]
        streams = [stream(src, dst) for src, dst in order]
        streams[0][0]()
        for k, (_, drain) in enumerate(streams):
            nxt_uses_other_ring = (k + 1 < len(order)
                                   and (order[k + 1][0].shape[1] == narrow_cols)
                                   != (order[k][0].shape[1] == narrow_cols))
            if nxt_uses_other_ring:
                streams[k + 1][0]()
            drain()
            if k + 1 < len(order) and not nxt_uses_other_ring:
                streams[k + 1][0]()

    pl.run_scoped(load,
                  pltpu.VMEM((LOAD_SLOTS, LOAD_WIDE_ROWS, wide_cols), F32),
                  pltpu.VMEM((LOAD_SLOTS, LOAD_NARROW_ROWS, narrow_cols), F32),
                  pltpu.SemaphoreType.DMA((LOAD_SLOTS,)),
                  pltpu.SemaphoreType.DMA((LOAD_SLOTS,)))


def _layer_kernel(tiles_per_seq, n_prompt, *refs):
    r = _LayerRefs(*refs)
    step = pl.program_id(0)
    last = pl.num_programs(0) - 1
    j = jnp.minimum(step, n_prompt - 1) & (tiles_per_seq - 1)

    @pl.when(step == 0)
    def _():
        r.x1[...] = jnp.zeros(r.x1.shape, F32)
        r.h2[...] = jnp.zeros(r.h2.shape, BF16)
        _load_weights_bf16(r)

    @pl.when(jnp.logical_and(step < n_prompt, j == 0))
    def _():
        r.ext_p[0:POOL_PAD, :] = jnp.zeros((POOL_PAD, r.ext_p.shape[1]), F32)

    @pl.when(step < n_prompt)
    def _():
        _layer_step(r, False, j, step, n_prompt)

    @pl.when(jnp.logical_and(step >= n_prompt, step < last))
    def _():
        _layer_step(r, True, j, step, n_prompt)

    @pl.when(step == last)
    def _():
        _drain_step(r)


def _mem_kv(mem, g_mem, w_kv):
    bsz, n_mem, d = mem.shape
    xw = w_kv.shape[-1] // 2
    sb = MEM_KV_SEQS
    assert bsz % sb == 0, (bsz, sb)
    return pl.pallas_call(
        _mem_kv_kernel,
        grid=(bsz // sb,),
        in_specs=[pl.BlockSpec((sb, n_mem, d), lambda i: (i, 0, 0)),
                  _const_spec((1, d)), _const_spec((d, 2 * xw))],
        out_specs=[pl.BlockSpec((sb, xw, n_mem), lambda i: (i, 0, 0)),
                   pl.BlockSpec((sb, xw, n_mem), lambda i: (i, 0, 0)),
                   pl.BlockSpec((sb, xw, n_mem), lambda i: (i, 0, 0)),
                   pl.BlockSpec((sb, n_mem, xw), lambda i: (i, 0, 0))],
        out_shape=[jax.ShapeDtypeStruct((bsz, xw, n_mem), F32),
                   jax.ShapeDtypeStruct((bsz, xw, n_mem), F32),
                   jax.ShapeDtypeStruct((bsz, xw, n_mem), BF16),
                   jax.ShapeDtypeStruct((bsz, n_mem, xw), BF16)],
        compiler_params=pltpu.CompilerParams(
            dimension_semantics=("arbitrary",),
            vmem_limit_bytes=VMEM_LIMIT_BYTES),
        name="mem_kv",
    )(mem, g_mem.reshape(1, d).astype(F32), w_kv)


def _spatial_operands(w_s, b_s, group_dim, chunk_rows, tile_rows):
    wmix = jnp.tile(w_s[:, :chunk_rows, :chunk_rows], (1, 1, tile_rows // chunk_rows))
    bias = jnp.repeat(b_s[:, :chunk_rows].T, group_dim, axis=1)
    return wmix.astype(F32), bias.astype(F32)


def _layer(x_prompt, x2d, ktb, vb, cache_kt, cache_vt, state, lw):
    (g_mix, w_in, g_v, b_v, w_s, b_s, w_pool, pool_scale,
     w_out_a, w_out_b, w_out_c, w_o, g_ffn, w_up, w_down, g_final) = lw
    bsz, seq, d = x_prompt.shape
    n_seq = state.shape[0]
    rows = x2d.shape[0] // n_seq
    t = TILE
    ts = t // rows
    tps = seq // t
    n_prompt = bsz * tps
    n_decode = n_seq // ts
    assert n_seq % n_prompt == 0, (n_seq, n_prompt)
    sps = n_seq // n_prompt
    sgu_w = g_v.shape[-1]
    pool_w = pool_scale.shape[-1]
    n_mem, xw = vb.shape[1:]
    group_dim = sgu_w // w_s.shape[0]
    row2 = lambda vec: vec.reshape(1, -1).astype(F32)
    wmix_p, bias_p = _spatial_operands(w_s, b_s, group_dim, CHUNK, CHUNK)
    wmix_s, bias_s = _spatial_operands(w_s, b_s, group_dim, rows, t)
    wpool = jax.scipy.linalg.block_diag(*[w_pool[i] for i in range(w_pool.shape[0])])
    state_rows = jnp.transpose(state, (1, 0, 2))
    big = [w_in, w_out_a, w_out_b, w_out_c, w_o, w_up, w_down]
    weights = [row2(g_mix), w_in, row2(g_v), row2(b_v), wmix_p, bias_p, wmix_s, bias_s,
               wpool.astype(BF16), row2(pool_scale), w_out_a, w_out_b, w_out_c, w_o,
               row2(g_ffn), w_up, w_down, row2(g_final)]

    def prompt_tile(m):
        return jnp.clip(m, 0, n_prompt - 1)

    def prompt_seq(m):
        return lax.shift_right_logical(prompt_tile(m), _log2(tps))

    def decode_tile(m):
        return jnp.clip(m - n_prompt, 0, n_decode - 1)

    yp, ys, cv_p, ptail_p, cv_s, ptail_s = pl.pallas_call(
        functools.partial(_layer_kernel, tps, n_prompt),
        grid=(n_prompt + n_decode + 1,),
        in_specs=[pl.BlockSpec((1, t, d), lambda s: (prompt_tile(s), 0, 0)),
                  pl.BlockSpec((t, d), lambda s: (decode_tile(s), 0)),
                  pl.BlockSpec((1, xw, n_mem), lambda s: (prompt_seq(s), 0, 0)),
                  pl.BlockSpec((1, n_mem, xw), lambda s: (prompt_seq(s), 0, 0)),
                  pl.BlockSpec((sps * rows, d), lambda s: (prompt_tile(s), 0)),
                  pl.BlockSpec((sps, xw, n_mem), lambda s: (prompt_tile(s), 0, 0)),
                  pl.BlockSpec((sps, xw, n_mem), lambda s: (prompt_tile(s), 0, 0)),
                  pl.BlockSpec((POOL_STATE, ts, pool_w), lambda s: (0, decode_tile(s), 0))]
                 + [pl.BlockSpec(memory_space=pl.ANY) if any(w is b for b in big)
                    else _const_spec(w.shape) for w in weights],
        out_specs=[pl.BlockSpec((1, t, d), lambda s: (prompt_tile(s - 1), 0, 0)),
                   pl.BlockSpec((t, d), lambda s: (decode_tile(s - 1), 0)),
                   pl.BlockSpec((1, CHUNK, sgu_w), lambda s: (prompt_seq(s), 0, 0)),
                   pl.BlockSpec((1, POOL_PAD, pool_w), lambda s: (prompt_seq(s), 0, 0)),
                   pl.BlockSpec((t, sgu_w), lambda s: (decode_tile(s), 0)),
                   pl.BlockSpec((POOL_STATE, ts, pool_w), lambda s: (0, decode_tile(s), 0))],
        out_shape=[jax.ShapeDtypeStruct((n_prompt, t, d), F32),
                   jax.ShapeDtypeStruct((n_seq * rows, d), F32),
                   jax.ShapeDtypeStruct((bsz, CHUNK, sgu_w), F32),
                   jax.ShapeDtypeStruct((bsz, POOL_PAD, pool_w), F32),
                   jax.ShapeDtypeStruct((n_seq * rows, sgu_w), F32),
                   jax.ShapeDtypeStruct((POOL_STATE, n_seq, pool_w), F32)],
        scratch_shapes=[pltpu.VMEM((POOL_PAD + t, pool_w), F32),
                        pltpu.VMEM((pool_w // V7X_LANES, t, V7X_LANES), F32),
                        pltpu.VMEM((pool_w // V7X_LANES, t, V7X_LANES), F32),
                        pltpu.VMEM((t, d), F32),
                        pltpu.VMEM((t, d), BF16)]
                       + [pltpu.VMEM(w.shape, BF16) for w in big]
                       + [pltpu.VMEM((n_seq * rows, xw), F32)],
        compiler_params=pltpu.CompilerParams(
            dimension_semantics=("arbitrary",),
            vmem_limit_bytes=VMEM_LIMIT_BYTES),
        name="layer",
    )(x_prompt.reshape(n_prompt, t, d), x2d, ktb, vb, x2d, cache_kt, cache_vt, state_rows, *weights)
    return (yp.reshape(bsz, seq, d), ys.reshape(n_seq, rows, d), cv_p, ptail_p[:, POOL_PAD - POOL_STATE:, :],
            cv_s.reshape(n_seq, rows, sgu_w), jnp.transpose(ptail_s, (1, 0, 2)))


def _feature_major(kv):
    bsz, n_mem, heads, head_dim = kv.shape
    return jnp.transpose(kv, (0, 2, 3, 1)).reshape(bsz, heads * head_dim, n_mem)


def _token_major(kvt, heads):
    bsz, xw, n_mem = kvt.shape
    return jnp.transpose(kvt.reshape(bsz, heads, xw // heads, n_mem), (0, 3, 1, 2))


def kernel(x_prompt, x_sample, mem_prompt, cache_mem_k, cache_mem_v, state_pool, g_mix, w_in, g_v, b_v, w_s, b_s, w_pool, pool_scale, g_mem, w_kv, w_out_a, w_out_b, w_out_c, w_o, g_ffn, w_up, w_down, g_final):
    depth = w_in.shape[0]
    assert depth == 1, "the final rmsnorm is fused into the (single) layer kernel"
    heads = cache_mem_k.shape[-2]
    n_seq, rows, d = x_sample.shape
    lw = (g_mix[0], w_in[0], g_v[0], b_v[0], w_s[0], b_s[0], w_pool[0], pool_scale[0],
          w_out_a[0], w_out_b[0], w_out_c[0], w_o[0], g_ffn[0], w_up[0], w_down[0], g_final)
    mem_kt, mem_vt, ktb, vb = _mem_kv(mem_prompt, g_mem[0], w_kv[0])
    x2d = x_sample.reshape(n_seq * rows, d)
    y_prompt, y_sample, cv_p, ptail_p, cv_s, ptail_s = _layer(
        x_prompt, x2d, ktb, vb, _feature_major(cache_mem_k[0]), _feature_major(cache_mem_v[0]),
        state_pool[0], lw)
    return (y_prompt, y_sample, _token_major(mem_kt, heads)[None], _token_major(mem_vt, heads)[None],
            ptail_p[None], ptail_s[None], cv_p[None], cv_s[None])
```

```python
import functools
from typing import NamedTuple

import jax
import jax.numpy as jnp
from jax import lax
from jax.experimental import pallas as pl
from jax.experimental.pallas import tpu as pltpu

PAST_LEN = 16384
CHUNK = 128
POOL_WINDOWS = (2, 4, 8, 16)
POOL_STATE = max(POOL_WINDOWS) - 1
X_HEADS = 4
EPS = 1e-6

V7X_VMEM_BYTES = 64 * 1024 * 1024
V7X_LANES = 128
VMEM_RESERVE_BYTES = 6 * 1024 * 1024
VMEM_LIMIT_BYTES = V7X_VMEM_BYTES - VMEM_RESERVE_BYTES

TILE = 256
MEM_KV_SEQS = 4
FFN_SPLIT = 2
FFN_POINTS = ((("up", 0),), (("down", 0),), (), (("up", 1),), (), (), (), (("down", 1),))
LOAD_SLOTS = 8
LOAD_WIDE_ROWS = 32
LOAD_NARROW_ROWS = 128
POOL_PAD = POOL_STATE + 1

BF16 = jnp.bfloat16
F32 = jnp.float32


def _dot(a, b):
    return jnp.dot(a, b, preferred_element_type=F32)


def _log2(n):
    assert n > 0 and n & (n - 1) == 0, n
    return n.bit_length() - 1


def _rmsnorm(x, g):
    return x * lax.rsqrt(jnp.mean(x * x, axis=-1, keepdims=True) + EPS) * g


def _layernorm(x, g, b):
    xc = x - jnp.mean(x, axis=-1, keepdims=True)
    return xc * lax.rsqrt(jnp.mean(xc * xc, axis=-1, keepdims=True) + EPS) * g + b


def _softmax_bf16(s):
    e = jnp.exp(s - jnp.max(s, axis=-1, keepdims=True))
    return (e * (1.0 / jnp.sum(e, axis=-1, keepdims=True))).astype(BF16)


def _const_spec(shape):
    zeros = (0,) * len(shape)
    return pl.BlockSpec(shape, lambda *_: zeros, pipeline_mode=pl.Buffered(1))


def _chunk_mix(vhat_b, wmix_ref, chunk_rows):
    t, sgu_w = vhat_b.shape
    n_groups = wmix_ref.shape[0]
    group_dim = sgu_w // n_groups
    shift = _log2(chunk_rows)
    reps = t // chunk_rows
    row = lax.broadcasted_iota(jnp.int32, (t, t), 0)
    col = lax.broadcasted_iota(jnp.int32, (t, t), 1)
    same_chunk = (row >> shift) == (col >> shift)
    causal = col <= row
    parts = []
    for g in range(n_groups):
        w = jnp.concatenate([wmix_ref[g]] * (t // wmix_ref.shape[2]), axis=1)
        w = jnp.concatenate([w] * reps, axis=0)
        w = jnp.where(same_chunk, jnp.where(causal, w, 0.0), 0.0).astype(BF16)
        parts.append(_dot(w, vhat_b[:, g * group_dim:(g + 1) * group_dim]))
    return jnp.concatenate(parts, axis=1)


def _chunk_bias(bias_ref, t):
    return jnp.concatenate([bias_ref[...]] * (t // bias_ref.shape[0]), axis=0)


def _window_means_minus_self(load_shifted, p, pos):
    group_lanes = p.shape[-1] // len(POOL_WINDOWS)
    outs = []
    for lt in range(p.shape[-1] // V7X_LANES):
        p_lt = p[..., lt * V7X_LANES:(lt + 1) * V7X_LANES]
        lane = lax.broadcasted_iota(jnp.int32, p_lt.shape, p_lt.ndim - 1)
        wins = POOL_WINDOWS[lt * V7X_LANES // group_lanes:(lt + 1) * V7X_LANES // group_lanes]
        acc = p_lt
        sums = {1: p_lt}
        for k in range(1, max(wins)):
            acc = acc + load_shifted(k, lt)
            sums[k + 1] = acc
        win_sum = sums[wins[-1]]
        win = jnp.full(p_lt.shape, wins[-1], jnp.int32)
        for gi in range(len(wins) - 2, -1, -1):
            in_group = lane < (gi + 1) * group_lanes
            win_sum = jnp.where(in_group, sums[wins[gi]], win_sum)
            win = jnp.where(in_group, wins[gi], win)
        cnt = jnp.minimum(pos + 1, win).astype(F32)
        outs.append(win_sum / cnt - p_lt)
    return jnp.concatenate(outs, axis=-1)


def _mem_kv_kernel(mem_ref, g_mem_ref, w_kv_ref, kt_ref, vt_ref, ktb_ref, vb_ref):
    xw = kt_ref.shape[1]
    w_kv = w_kv_ref[...].astype(BF16)
    for i in range(mem_ref.shape[0]):
        hn = _rmsnorm(mem_ref[i], g_mem_ref[...]).astype(BF16)
        kv = _dot(hn, w_kv)
        kt = kv[:, 0:xw].T
        v = kv[:, xw:2 * xw]
        kt_ref[i] = kt
        vt_ref[i] = v.T
        ktb_ref[i] = kt.astype(BF16)
        vb_ref[i] = v.astype(BF16)


def _decode_queries(r):
    x_w = r.ckt.shape[1]
    c_q = r.w_in.shape[1] - 3 * r.xq.shape[1] - x_w
    h = _rmsnorm(r.xq[...], r.g_mix[...]).astype(BF16)
    return _dot(h, r.w_in[:, c_q:c_q + x_w]) * ((x_w // X_HEADS) ** -0.5)


def _decode_own_head(r):
    n_seq, x_w, _ = r.ckt.shape
    rows = r.xq.shape[0] // n_seq
    stack = X_HEADS * rows
    lane_head = lax.broadcasted_iota(jnp.int32, (stack, x_w), 1) >> _log2(x_w // X_HEADS)
    row_head = lax.broadcasted_iota(jnp.int32, (stack, x_w), 0) >> _log2(rows)
    return lane_head == row_head


def _decode_scores(r, q):
    n_seq = r.ckt.shape[0]
    rows = r.xq.shape[0] // n_seq
    own_head = _decode_own_head(r)
    scores = []
    for i in range(n_seq):
        qs = jnp.concatenate([q[i * rows:(i + 1) * rows, :]] * X_HEADS, axis=0)
        qm = jnp.where(own_head, qs, 0.0).astype(BF16)
        scores.append(_dot(qm, r.ckt[i].astype(BF16)))
    return scores


def _decode_values(r, probs, step):
    n_seq = r.cvt.shape[0]
    rows = r.xq.shape[0] // n_seq
    own_head = _decode_own_head(r)
    for i in range(n_seq):
        of = _dot(probs[i], r.cvt[i].T.astype(BF16))
        of = jnp.where(own_head, of, 0.0)
        out = of[0:rows]
        for hd in range(1, X_HEADS):
            out = out + of[hd * rows:(hd + 1) * rows]
        row0 = pl.multiple_of((step * n_seq + i) * rows, rows)
        r.attn[pl.ds(row0, rows), :] = out


class _LayerRefs(NamedTuple):
    xp: object
    xs: object
    kt: object
    vb: object
    xq: object
    ckt: object
    cvt: object
    state: object
    g_mix: object
    w_in_f32: object
    g_v: object
    b_v: object
    wmix_p: object
    bias_p: object
    wmix_s: object
    bias_s: object
    wpool: object
    pool_scale: object
    w_out_a_f32: object
    w_out_b_f32: object
    w_out_c_f32: object
    w_o_f32: object
    g_ffn: object
    w_up_f32: object
    w_down_f32: object
    g_final: object
    yp: object
    ys: object
    cv_p: object
    ptail_p: object
    cv_s: object
    ptail_s: object
    ext_p: object
    pool_in: object
    pool_out: object
    x1: object
    h2: object
    w_in: object
    w_out_a: object
    w_out_b: object
    w_out_c: object
    w_o: object
    w_up: object
    w_down: object
    attn: object


def _layer_step(r, decode, j, step, n_prompt):
    t, d = r.x1.shape
    sgu_w = r.g_v.shape[-1]
    pool_w = r.pool_scale.shape[-1]
    x_w = r.attn.shape[-1]
    n_decode = r.attn.shape[0] // t
    head_dim = x_w // X_HEADS
    c_pool = 2 * sgu_w
    c_q = c_pool + pool_w
    c_gate = c_q + x_w

    ffn = _FfnSchedule(r)

    def gate(h, part):
        lo = c_gate + part * d
        return jax.nn.sigmoid(_dot(h, r.w_in[:, lo:lo + d]))

    ffn.at(0)
    if not decode:
        dec_q = _decode_queries(r)
    x = r.xs[...] if decode else r.xp[0]
    h = _rmsnorm(x, r.g_mix[...]).astype(BF16)
    uv = _dot(h, r.w_in[:, 0:2 * sgu_w])
    u = uv[:, 0:sgu_w]
    v = uv[:, sgu_w:2 * sgu_w]
    gate_a = gate(h, 0)
    ffn.at(1)

    u = jax.nn.gelu(u)
    vhat = _layernorm(jax.nn.gelu(v), r.g_v[...], r.b_v[...])
    if decode:
        rows = r.wmix_s.shape[1]
        r.cv_s[...] = vhat
        mixed = _chunk_mix(vhat.astype(BF16), r.wmix_s, rows)
        bias = _chunk_bias(r.bias_s, t)
    else:
        r.cv_p[0] = vhat[t - CHUNK:, :]
        mixed = _chunk_mix(vhat.astype(BF16), r.wmix_p, CHUNK)
        bias = _chunk_bias(r.bias_p, t)
    if decode:
        p = _dot(h, r.w_in[:, c_pool:c_pool + pool_w])
    else:
        pq = _dot(h, r.w_in[:, c_pool:c_q + x_w])
        p = pq[:, 0:pool_w]
        q = pq[:, pool_w:pool_w + x_w] * (head_dim ** -0.5)
        dec_scores = _decode_scores(r, dec_q)
    ffn.at(2)
    a_in = (u * (mixed + bias)).astype(BF16)

    if decode:
        n_seq = t // rows
        lane_tiles = range(pool_w // V7X_LANES)
        for lt in lane_tiles:
            r.pool_in[lt] = p[:, lt * V7X_LANES:(lt + 1) * V7X_LANES]
        ext = [r.state[k] for k in range(POOL_STATE)]
        ext += [jnp.concatenate([r.pool_in[lt, pl.ds(i, n_seq, stride=rows), :]
                                 for lt in lane_tiles], axis=-1) for i in range(rows)]
        for i in range(rows):
            cur = POOL_STATE + i

            def load_shifted(k, lt, cur=cur):
                return ext[cur - k][:, lt * V7X_LANES:(lt + 1) * V7X_LANES]

            res = _window_means_minus_self(load_shifted, ext[cur], PAST_LEN + i)
            for lt in lane_tiles:
                r.pool_out[lt, pl.ds(i, n_seq, stride=rows), :] = (
                    res[:, lt * V7X_LANES:(lt + 1) * V7X_LANES])
        for k in range(POOL_STATE):
            r.ptail_s[k] = ext[rows + k]
        pooled = jnp.concatenate([r.pool_out[lt] for lt in lane_tiles], axis=-1)
    else:
        r.ext_p[POOL_PAD:POOL_PAD + t, :] = p

        def load_shifted(k, lt):
            return r.ext_p[POOL_PAD - k:POOL_PAD - k + t, lt * V7X_LANES:(lt + 1) * V7X_LANES]

        pos = j * t + lax.broadcasted_iota(jnp.int32, (t, V7X_LANES), 0)
        pooled = _window_means_minus_self(load_shifted, p, pos)
        tail = r.ext_p[t:t + POOL_PAD, :]
        r.ext_p[0:POOL_PAD, :] = tail
        r.ptail_p[0] = tail
    pooled = pooled.astype(BF16)

    a = _dot(a_in, r.w_out_a[...])

    if not decode:
        head = lax.broadcasted_iota(jnp.int32, (t, x_w), 1) >> _log2(head_dim)
        kt = r.kt[0]
        q_heads = jnp.concatenate(
            [jnp.where(head == hd, q, 0.0).astype(BF16) for hd in range(X_HEADS)], axis=0)
        scores = _dot(q_heads, kt)
    pool_mixed = _dot(pooled, r.wpool[...]) * r.pool_scale[...]
    ffn.at(3)
    if decode:
        row0 = pl.multiple_of(jnp.clip(step - n_prompt, 0, n_decode - 1) * t, t)
        o = r.attn[pl.ds(row0, t), :]
    else:
        dec_probs = [_softmax_bf16(s) for s in dec_scores]
        o_heads = _dot(_softmax_bf16(scores), r.vb[0])
        o = jnp.zeros((t, x_w), F32)
        for hd in range(X_HEADS):
            o = jnp.where(head == hd, o_heads[hd * t:(hd + 1) * t], o)
        _decode_values(r, dec_probs, step)
    b = _dot(pool_mixed.astype(BF16), r.w_out_b[...])
    gate_b = gate(h, 1)
    c = _dot(o.astype(BF16), r.w_out_c[...])
    ffn.at(4)
    gate_c = gate(h, 2)
    ffn.at(5)
    merged = (gate_a * a + gate_b * b + gate_c * c).astype(BF16)
    ffn.at(6)
    ffn.at(7)
    x2 = ffn.x2
    x1 = x + _dot(merged, r.w_o[...])
    r.x1[...] = x1
    r.h2[...] = _rmsnorm(x1, r.g_ffn[...]).astype(BF16)
    y = _rmsnorm(x2, r.g_final[...])
    if not decode:
        r.yp[0] = y
    else:
        @pl.when(step == n_prompt)
        def _():
            r.yp[0] = y

        @pl.when(step != n_prompt)
        def _():
            r.ys[...] = y


def _ffn_up(r, s):
    slab = r.w_up.shape[1] // FFN_SPLIT
    up = _dot(r.h2[...], r.w_up[:, s * slab:(s + 1) * slab])
    return jnp.square(jnp.maximum(up, 0.0)).astype(BF16)


def _ffn_down(r, act, s):
    slab = r.w_up.shape[1] // FFN_SPLIT
    return _dot(act, r.w_down[s * slab:(s + 1) * slab, :])


class _FfnSchedule:
    def __init__(self, r):
        self.r = r
        self.acts = {}
        self.x2 = None

    def at(self, point):
        for kind, s in FFN_POINTS[point]:
            if kind == "up":
                self.acts[s] = _ffn_up(self.r, s)
            else:
                base = self.r.x1[...] if self.x2 is None else self.x2
                self.x2 = base + _ffn_down(self.r, self.acts.pop(s), s)


def _drain_step(r):
    x2 = r.x1[...]
    act = _ffn_up(r, 0)
    for s in range(FFN_SPLIT):
        nxt = _ffn_up(r, s + 1) if s + 1 < FFN_SPLIT else None
        x2 = x2 + _ffn_down(r, act, s)
        act = nxt
    r.ys[...] = _rmsnorm(x2, r.g_final[...])


def _load_weights_bf16(r):
    ahead = LOAD_SLOTS - 1
    narrow_cols = r.w_o_f32.shape[1]
    wide_cols = max(r.w_in_f32.shape[1], r.w_up_f32.shape[1])

    def load(stage_w, stage_n, sem_w, sem_n):
        def stream(src, dst):
            narrow = src.shape[1] == narrow_cols
            stage, sem = (stage_n, sem_n) if narrow else (stage_w, sem_w)
            rows = stage.shape[1]
            n_chunks = src.shape[0] // rows
            assert src.shape[0] % rows == 0, (src.shape, rows)

            def copy(i, slot):
                return pltpu.make_async_copy(src.at[pl.ds(i * rows, rows), :],
                                             stage.at[slot, :, 0:src.shape[1]], sem.at[slot])

            def prime():
                for i in range(min(ahead, n_chunks)):
                    copy(i, i % LOAD_SLOTS).start()

            def drain():
                def body(i, carry):
                    @pl.when(i + ahead < n_chunks)
                    def _():
                        copy(i + ahead, lax.rem(i + ahead, LOAD_SLOTS)).start()

                    slot = lax.rem(i, LOAD_SLOTS)
                    copy(i, slot).wait()
                    row0 = pl.multiple_of(i * rows, rows)
                    dst[pl.ds(row0, rows), :] = stage[slot, :, 0:src.shape[1]].astype(BF16)
                    return carry

                lax.fori_loop(0, n_chunks, body, 0)

            return prime, drain

        order = [(r.w_in_f32, r.w_in), (r.w_down_f32, r.w_down), (r.w_up_f32, r.w_up),
                 (r.w_o_f32, r.w_o), (r.w_out_a_f32, r.w_out_a), (r.w_out_b_f32, r.w_out_b),
                 (r.w_out_c_f32, r.w_out_c)]
        streams = [stream(src, dst) for src, dst in order]
        streams[0][0]()
        for k, (_, drain) in enumerate(streams):
            nxt_uses_other_ring = (k + 1 < len(order)
                                   and (order[k + 1][0].shape[1] == narrow_cols)
                                   != (order[k][0].shape[1] == narrow_cols))
            if nxt_uses_other_ring:
                streams[k + 1][0]()
            drain()
            if k + 1 < len(order) and not nxt_uses_other_ring:
                streams[k + 1][0]()

    pl.run_scoped(load,
                  pltpu.VMEM((LOAD_SLOTS, LOAD_WIDE_ROWS, wide_cols), F32),
                  pltpu.VMEM((LOAD_SLOTS, LOAD_NARROW_ROWS, narrow_cols), F32),
                  pltpu.SemaphoreType.DMA((LOAD_SLOTS,)),
                  pltpu.SemaphoreType.DMA((LOAD_SLOTS,)))


def _layer_kernel(tiles_per_seq, n_prompt, *refs):
    r = _LayerRefs(*refs)
    step = pl.program_id(0)
    last = pl.num_programs(0) - 1
    j = jnp.minimum(step, n_prompt - 1) & (tiles_per_seq - 1)

    @pl.when(step == 0)
    def _():
        r.x1[...] = jnp.zeros(r.x1.shape, F32)
        r.h2[...] = jnp.zeros(r.h2.shape, BF16)
        _load_weights_bf16(r)

    @pl.when(jnp.logical_and(step < n_prompt, j == 0))
    def _():
        r.ext_p[0:POOL_PAD, :] = jnp.zeros((POOL_PAD, r.ext_p.shape[1]), F32)

    @pl.when(step < n_prompt)
    def _():
        _layer_step(r, False, j, step, n_prompt)

    @pl.when(jnp.logical_and(step >= n_prompt, step < last))
    def _():
        _layer_step(r, True, j, step, n_prompt)

    @pl.when(step == last)
    def _():
        _drain_step(r)


def _mem_kv(mem, g_mem, w_kv):
    bsz, n_mem, d = mem.shape
    xw = w_kv.shape[-1] // 2
    sb = MEM_KV_SEQS
    assert bsz % sb == 0, (bsz, sb)
    return pl.pallas_call(
        _mem_kv_kernel,
        grid=(bsz // sb,),
        in_specs=[pl.BlockSpec((sb, n_mem, d), lambda i: (i, 0, 0)),
                  _const_spec((1, d)), _const_spec((d, 2 * xw))],
        out_specs=[pl.BlockSpec((sb, xw, n_mem), lambda i: (i, 0, 0)),
                   pl.BlockSpec((sb, xw, n_mem), lambda i: (i, 0, 0)),
                   pl.BlockSpec((sb, xw, n_mem), lambda i: (i, 0, 0)),
                   pl.BlockSpec((sb, n_mem, xw), lambda i: (i, 0, 0))],
        out_shape=[jax.ShapeDtypeStruct((bsz, xw, n_mem), F32),
                   jax.ShapeDtypeStruct((bsz, xw, n_mem), F32),
                   jax.ShapeDtypeStruct((bsz, xw, n_mem), BF16),
                   jax.ShapeDtypeStruct((bsz, n_mem, xw), BF16)],
        compiler_params=pltpu.CompilerParams(
            dimension_semantics=("arbitrary",),
            vmem_limit_bytes=VMEM_LIMIT_BYTES),
        name="mem_kv",
    )(mem, g_mem.reshape(1, d).astype(F32), w_kv)


def _spatial_operands(w_s, b_s, group_dim, chunk_rows, tile_rows):
    wmix = jnp.tile(w_s[:, :chunk_rows, :chunk_rows], (1, 1, tile_rows // chunk_rows))
    bias = jnp.repeat(b_s[:, :chunk_rows].T, group_dim, axis=1)
    return wmix.astype(F32), bias.astype(F32)


def _layer(x_prompt, x2d, ktb, vb, cache_kt, cache_vt, state, lw):
    (g_mix, w_in, g_v, b_v, w_s, b_s, w_pool, pool_scale,
     w_out_a, w_out_b, w_out_c, w_o, g_ffn, w_up, w_down, g_final) = lw
    bsz, seq, d = x_prompt.shape
    n_seq = state.shape[0]
    rows = x2d.shape[0] // n_seq
    t = TILE
    ts = t // rows
    tps = seq // t
    n_prompt = bsz * tps
    n_decode = n_seq // ts
    assert n_seq % n_prompt == 0, (n_seq, n_prompt)
    sps = n_seq // n_prompt
    sgu_w = g_v.shape[-1]
    pool_w = pool_scale.shape[-1]
    n_mem, xw = vb.shape[1:]
    group_dim = sgu_w // w_s.shape[0]
    row2 = lambda vec: vec.reshape(1, -1).astype(F32)
    wmix_p, bias_p = _spatial_operands(w_s, b_s, group_dim, CHUNK, CHUNK)
    wmix_s, bias_s = _spatial_operands(w_s, b_s, group_dim, rows, t)
    wpool = jax.scipy.linalg.block_diag(*[w_pool[i] for i in range(w_pool.shape[0])])
    state_rows = jnp.transpose(state, (1, 0, 2))
    big = [w_in, w_out_a, w_out_b, w_out_c, w_o, w_up, w_down]
    weights = [row2(g_mix), w_in, row2(g_v), row2(b_v), wmix_p, bias_p, wmix_s, bias_s,
               wpool.astype(BF16), row2(pool_scale), w_out_a, w_out_b, w_out_c, w_o,
               row2(g_ffn), w_up, w_down, row2(g_final)]

    def prompt_tile(m):
        return jnp.clip(m, 0, n_prompt - 1)

    def prompt_seq(m):
        return lax.shift_right_logical(prompt_tile(m), _log2(tps))

    def decode_tile(m):
        return jnp.clip(m - n_prompt, 0, n_decode - 1)

    yp, ys, cv_p, ptail_p, cv_s, ptail_s = pl.pallas_call(
        functools.partial(_layer_kernel, tps, n_prompt),
        grid=(n_prompt + n_decode + 1,),
        in_specs=[pl.BlockSpec((1, t, d), lambda s: (prompt_tile(s), 0, 0)),
                  pl.BlockSpec((t, d), lambda s: (decode_tile(s), 0)),
                  pl.BlockSpec((1, xw, n_mem), lambda s: (prompt_seq(s), 0, 0)),
                  pl.BlockSpec((1, n_mem, xw), lambda s: (prompt_seq(s), 0, 0)),
                  pl.BlockSpec((sps * rows, d), lambda s: (prompt_tile(s), 0)),
                  pl.BlockSpec((sps, xw, n_mem), lambda s: (prompt_tile(s), 0, 0)),
                  pl.BlockSpec((sps, xw, n_mem), lambda s: (prompt_tile(s), 0, 0)),
                  pl.BlockSpec((POOL_STATE, ts, pool_w), lambda s: (0, decode_tile(s), 0))]
                 + [pl.BlockSpec(memory_space=pl.ANY) if any(w is b for b in big)
                    else _const_spec(w.shape) for w in weights],
        out_specs=[pl.BlockSpec((1, t, d), lambda s: (prompt_tile(s - 1), 0, 0)),
                   pl.BlockSpec((t, d), lambda s: (decode_tile(s - 1), 0)),
                   pl.BlockSpec((1, CHUNK, sgu_w), lambda s: (prompt_seq(s), 0, 0)),
                   pl.BlockSpec((1, POOL_PAD, pool_w), lambda s: (prompt_seq(s), 0, 0)),
                   pl.BlockSpec((t, sgu_w), lambda s: (decode_tile(s), 0)),
                   pl.BlockSpec((POOL_STATE, ts, pool_w), lambda s: (0, decode_tile(s), 0))],
        out_shape=[jax.ShapeDtypeStruct((n_prompt, t, d), F32),
                   jax.ShapeDtypeStruct((n_seq * rows, d), F32),
                   jax.ShapeDtypeStruct((bsz, CHUNK, sgu_w), F32),
                   jax.ShapeDtypeStruct((bsz, POOL_PAD, pool_w), F32),
                   jax.ShapeDtypeStruct((n_seq * rows, sgu_w), F32),
                   jax.ShapeDtypeStruct((POOL_STATE, n_seq, pool_w), F32)],
        scratch_shapes=[pltpu.VMEM((POOL_PAD + t, pool_w), F32),
                        pltpu.VMEM((pool_w // V7X_LANES, t, V7X_LANES), F32),
                        pltpu.VMEM((pool_w // V7X_LANES, t, V7X_LANES), F32),
                        pltpu.VMEM((t, d), F32),
                        pltpu.VMEM((t, d), BF16)]
                       + [pltpu.VMEM(w.shape, BF16) for w in big]
                       + [pltpu.VMEM((n_seq * rows, xw), F32)],
        compiler_params=pltpu.CompilerParams(
            dimension_semantics=("arbitrary",),
            vmem_limit_bytes=VMEM_LIMIT_BYTES),
        name="layer",
    )(x_prompt.reshape(n_prompt, t, d), x2d, ktb, vb, x2d, cache_kt, cache_vt, state_rows, *weights)
    return (yp.reshape(bsz, seq, d), ys.reshape(n_seq, rows, d), cv_p, ptail_p[:, POOL_PAD - POOL_STATE:, :],
            cv_s.reshape(n_seq, rows, sgu_w), jnp.transpose(ptail_s, (1, 0, 2)))


def _feature_major(kv):
    bsz, n_mem, heads, head_dim = kv.shape
    return jnp.transpose(kv, (0, 2, 3, 1)).reshape(bsz, heads * head_dim, n_mem)


def _token_major(kvt, heads):
    bsz, xw, n_mem = kvt.shape
    return jnp.transpose(kvt.reshape(bsz, heads, xw // heads, n_mem), (0, 3, 1, 2))


def kernel(x_prompt, x_sample, mem_prompt, cache_mem_k, cache_mem_v, state_pool, g_mix, w_in, g_v, b_v, w_s, b_s, w_pool, pool_scale, g_mem, w_kv, w_out_a, w_out_b, w_out_c, w_o, g_ffn, w_up, w_down, g_final):
    depth = w_in.shape[0]
    assert depth == 1, "the final rmsnorm is fused into the (single) layer kernel"
    heads = cache_mem_k.shape[-2]
    n_seq, rows, d = x_sample.shape
    lw = (g_mix[0], w_in[0], g_v[0], b_v[0], w_s[0], b_s[0], w_pool[0], pool_scale[0],
          w_out_a[0], w_out_b[0], w_out_c[0], w_o[0], g_ffn[0], w_up[0], w_down[0], g_final)
    mem_kt, mem_vt, ktb, vb = _mem_kv(mem_prompt, g_mem[0], w_kv[0])
    x2d = x_sample.reshape(n_seq * rows, d)
    y_prompt, y_sample, cv_p, ptail_p, cv_s, ptail_s = _layer(
        x_prompt, x2d, ktb, vb, _feature_major(cache_mem_k[0]), _feature_major(cache_mem_v[0]),
        state_pool[0], lw)
    return (y_prompt, y_sample, _token_major(mem_kt, heads)[None], _token_major(mem_vt, heads)[None],
            ptail_p[None], ptail_s[None], cv_p[None], cv_s[None])
```

```python
import functools
from typing import NamedTuple

import jax
import jax.numpy as jnp
from jax import lax
from jax.experimental import pallas as pl
from jax.experimental.pallas import tpu as pltpu

PAST_LEN = 16384
CHUNK = 128
POOL_WINDOWS = (2, 4, 8, 16)
POOL_STATE = max(POOL_WINDOWS) - 1
X_HEADS = 4
EPS = 1e-6

V7X_VMEM_BYTES = 64 * 1024 * 1024
V7X_LANES = 128
VMEM_RESERVE_BYTES = 6 * 1024 * 1024
VMEM_LIMIT_BYTES = V7X_VMEM_BYTES - VMEM_RESERVE_BYTES

TILE = 256
MEM_KV_SEQS = 4
FFN_SPLIT = 4
FFN_POINTS = ((("up", 0),), (("down", 0),), (("up", 1),), (("down", 1),),
              (("up", 2),), (("down", 2),), (("up", 3),), (("down", 3),))
LOAD_SLOTS = 8
LOAD_WIDE_ROWS = 32
LOAD_NARROW_ROWS = 128
POOL_PAD = POOL_STATE + 1

BF16 = jnp.bfloat16
F32 = jnp.float32


def _dot(a, b):
    return jnp.dot(a, b, preferred_element_type=F32)


def _log2(n):
    assert n > 0 and n & (n - 1) == 0, n
    return n.bit_length() - 1


def _rmsnorm(x, g):
    return x * lax.rsqrt(jnp.mean(x * x, axis=-1, keepdims=True) + EPS) * g


def _layernorm(x, g, b):
    xc = x - jnp.mean(x, axis=-1, keepdims=True)
    return xc * lax.rsqrt(jnp.mean(xc * xc, axis=-1, keepdims=True) + EPS) * g + b


def _softmax_bf16(s):
    e = jnp.exp(s - jnp.max(s, axis=-1, keepdims=True))
    return (e * (1.0 / jnp.sum(e, axis=-1, keepdims=True))).astype(BF16)


def _const_spec(shape):
    zeros = (0,) * len(shape)
    return pl.BlockSpec(shape, lambda *_: zeros, pipeline_mode=pl.Buffered(1))


def _chunk_mix(vhat_b, wmix_ref, chunk_rows):
    t, sgu_w = vhat_b.shape
    n_groups = wmix_ref.shape[0]
    group_dim = sgu_w // n_groups
    shift = _log2(chunk_rows)
    reps = t // chunk_rows
    row = lax.broadcasted_iota(jnp.int32, (t, t), 0)
    col = lax.broadcasted_iota(jnp.int32, (t, t), 1)
    same_chunk = (row >> shift) == (col >> shift)
    causal = col <= row
    parts = []
    for g in range(n_groups):
        w = jnp.concatenate([wmix_ref[g]] * (t // wmix_ref.shape[2]), axis=1)
        w = jnp.concatenate([w] * reps, axis=0)
        w = jnp.where(same_chunk, jnp.where(causal, w, 0.0), 0.0).astype(BF16)
        parts.append(_dot(w, vhat_b[:, g * group_dim:(g + 1) * group_dim]))
    return jnp.concatenate(parts, axis=1)


def _chunk_bias(bias_ref, t):
    return jnp.concatenate([bias_ref[...]] * (t // bias_ref.shape[0]), axis=0)


def _window_means_minus_self(load_shifted, p, pos):
    group_lanes = p.shape[-1] // len(POOL_WINDOWS)
    outs = []
    for lt in range(p.shape[-1] // V7X_LANES):
        p_lt = p[..., lt * V7X_LANES:(lt + 1) * V7X_LANES]
        lane = lax.broadcasted_iota(jnp.int32, p_lt.shape, p_lt.ndim - 1)
        wins = POOL_WINDOWS[lt * V7X_LANES // group_lanes:(lt + 1) * V7X_LANES // group_lanes]
        acc = p_lt
        sums = {1: p_lt}
        for k in range(1, max(wins)):
            acc = acc + load_shifted(k, lt)
            sums[k + 1] = acc
        win_sum = sums[wins[-1]]
        win = jnp.full(p_lt.shape, wins[-1], jnp.int32)
        for gi in range(len(wins) - 2, -1, -1):
            in_group = lane < (gi + 1) * group_lanes
            win_sum = jnp.where(in_group, sums[wins[gi]], win_sum)
            win = jnp.where(in_group, wins[gi], win)
        cnt = jnp.minimum(pos + 1, win).astype(F32)
        outs.append(win_sum / cnt - p_lt)
    return jnp.concatenate(outs, axis=-1)


def _mem_kv_kernel(mem_ref, g_mem_ref, w_kv_ref, kt_ref, vt_ref, ktb_ref, vb_ref):
    xw = kt_ref.shape[1]
    w_kv = w_kv_ref[...].astype(BF16)
    for i in range(mem_ref.shape[0]):
        hn = _rmsnorm(mem_ref[i], g_mem_ref[...]).astype(BF16)
        kv = _dot(hn, w_kv)
        kt = kv[:, 0:xw].T
        v = kv[:, xw:2 * xw]
        kt_ref[i] = kt
        vt_ref[i] = v.T
        ktb_ref[i] = kt.astype(BF16)
        vb_ref[i] = v.astype(BF16)


def _decode_queries(r):
    x_w = r.ckt.shape[1]
    c_q = r.w_in.shape[1] - 3 * r.xq.shape[1] - x_w
    h = _rmsnorm(r.xq[...], r.g_mix[...]).astype(BF16)
    return _dot(h, r.w_in[:, c_q:c_q + x_w]) * ((x_w // X_HEADS) ** -0.5)


def _decode_own_head(r):
    n_seq, x_w, _ = r.ckt.shape
    rows = r.xq.shape[0] // n_seq
    stack = X_HEADS * rows
    lane_head = lax.broadcasted_iota(jnp.int32, (stack, x_w), 1) >> _log2(x_w // X_HEADS)
    row_head = lax.broadcasted_iota(jnp.int32, (stack, x_w), 0) >> _log2(rows)
    return lane_head == row_head


def _decode_scores(r, q):
    n_seq = r.ckt.shape[0]
    rows = r.xq.shape[0] // n_seq
    own_head = _decode_own_head(r)
    scores = []
    for i in range(n_seq):
        qs = jnp.concatenate([q[i * rows:(i + 1) * rows, :]] * X_HEADS, axis=0)
        qm = jnp.where(own_head, qs, 0.0).astype(BF16)
        scores.append(_dot(qm, r.ckt[i].astype(BF16)))
    return scores


def _decode_values(r, probs, step):
    n_seq = r.cvt.shape[0]
    rows = r.xq.shape[0] // n_seq
    own_head = _decode_own_head(r)
    for i in range(n_seq):
        of = _dot(probs[i], r.cvt[i].T.astype(BF16))
        of = jnp.where(own_head, of, 0.0)
        out = of[0:rows]
        for hd in range(1, X_HEADS):
            out = out + of[hd * rows:(hd + 1) * rows]
        row0 = pl.multiple_of((step * n_seq + i) * rows, rows)
        r.attn[pl.ds(row0, rows), :] = out


class _LayerRefs(NamedTuple):
    xp: object
    xs: object
    kt: object
    vb: object
    xq: object
    ckt: object
    cvt: object
    state: object
    g_mix: object
    w_in_f32: object
    g_v: object
    b_v: object
    wmix_p: object
    bias_p: object
    wmix_s: object
    bias_s: object
    wpool: object
    pool_scale: object
    w_out_a_f32: object
    w_out_b_f32: object
    w_out_c_f32: object
    w_o_f32: object
    g_ffn: object
    w_up_f32: object
    w_down_f32: object
    g_final: object
    yp: object
    ys: object
    cv_p: object
    ptail_p: object
    cv_s: object
    ptail_s: object
    ext_p: object
    pool_in: object
    pool_out: object
    x1: object
    h2: object
    w_in: object
    w_out_a: object
    w_out_b: object
    w_out_c: object
    w_o: object
    w_up: object
    w_down: object
    attn: object


def _layer_step(r, decode, j, step, n_prompt):
    t, d = r.x1.shape
    sgu_w = r.g_v.shape[-1]
    pool_w = r.pool_scale.shape[-1]
    x_w = r.attn.shape[-1]
    n_decode = r.attn.shape[0] // t
    head_dim = x_w // X_HEADS
    c_pool = 2 * sgu_w
    c_q = c_pool + pool_w
    c_gate = c_q + x_w

    ffn = _FfnSchedule(r)

    def gate(h, part):
        lo = c_gate + part * d
        return jax.nn.sigmoid(_dot(h, r.w_in[:, lo:lo + d]))

    ffn.at(0)
    if not decode:
        dec_q = _decode_queries(r)
    x = r.xs[...] if decode else r.xp[0]
    h = _rmsnorm(x, r.g_mix[...]).astype(BF16)
    z = _dot(h, r.w_in[:, 0:c_q if decode else c_gate])
    u = z[:, 0:sgu_w]
    v = z[:, sgu_w:c_pool]
    p = z[:, c_pool:c_q]
    gate_a = gate(h, 0)
    ffn.at(1)

    u = jax.nn.gelu(u)
    vhat = _layernorm(jax.nn.gelu(v), r.g_v[...], r.b_v[...])
    if decode:
        rows = r.wmix_s.shape[1]
        r.cv_s[...] = vhat
        mixed = _chunk_mix(vhat.astype(BF16), r.wmix_s, rows)
        bias = _chunk_bias(r.bias_s, t)
    else:
        r.cv_p[0] = vhat[t - CHUNK:, :]
        mixed = _chunk_mix(vhat.astype(BF16), r.wmix_p, CHUNK)
        bias = _chunk_bias(r.bias_p, t)
    if not decode:
        q = z[:, c_q:c_gate] * (head_dim ** -0.5)
        dec_scores = _decode_scores(r, dec_q)
    ffn.at(2)
    a_in = (u * (mixed + bias)).astype(BF16)

    if decode:
        n_seq = t // rows
        lane_tiles = range(pool_w // V7X_LANES)
        for lt in lane_tiles:
            r.pool_in[lt] = p[:, lt * V7X_LANES:(lt + 1) * V7X_LANES]
        ext = [r.state[k] for k in range(POOL_STATE)]
        ext += [jnp.concatenate([r.pool_in[lt, pl.ds(i, n_seq, stride=rows), :]
                                 for lt in lane_tiles], axis=-1) for i in range(rows)]
        for i in range(rows):
            cur = POOL_STATE + i

            def load_shifted(k, lt, cur=cur):
                return ext[cur - k][:, lt * V7X_LANES:(lt + 1) * V7X_LANES]

            res = _window_means_minus_self(load_shifted, ext[cur], PAST_LEN + i)
            for lt in lane_tiles:
                r.pool_out[lt, pl.ds(i, n_seq, stride=rows), :] = (
                    res[:, lt * V7X_LANES:(lt + 1) * V7X_LANES])
        for k in range(POOL_STATE):
            r.ptail_s[k] = ext[rows + k]
        pooled = jnp.concatenate([r.pool_out[lt] for lt in lane_tiles], axis=-1)
    else:
        r.ext_p[POOL_PAD:POOL_PAD + t, :] = p

        def load_shifted(k, lt):
            return r.ext_p[POOL_PAD - k:POOL_PAD - k + t, lt * V7X_LANES:(lt + 1) * V7X_LANES]

        pos = j * t + lax.broadcasted_iota(jnp.int32, (t, V7X_LANES), 0)
        pooled = _window_means_minus_self(load_shifted, p, pos)
        tail = r.ext_p[t:t + POOL_PAD, :]
        r.ext_p[0:POOL_PAD, :] = tail
        r.ptail_p[0] = tail
    pooled = pooled.astype(BF16)

    a = _dot(a_in, r.w_out_a[...])

    if not decode:
        head = lax.broadcasted_iota(jnp.int32, (t, x_w), 1) >> _log2(head_dim)
        kt = r.kt[0]
        q_heads = jnp.concatenate(
            [jnp.where(head == hd, q, 0.0).astype(BF16) for hd in range(X_HEADS)], axis=0)
        scores = _dot(q_heads, kt)
    pool_mixed = _dot(pooled, r.wpool[...]) * r.pool_scale[...]
    ffn.at(3)
    if decode:
        row0 = pl.multiple_of(jnp.clip(step - n_prompt, 0, n_decode - 1) * t, t)
        o = r.attn[pl.ds(row0, t), :]
    else:
        dec_probs = [_softmax_bf16(s) for s in dec_scores]
        o_heads = _dot(_softmax_bf16(scores), r.vb[0])
        o = jnp.zeros((t, x_w), F32)
        for hd in range(X_HEADS):
            o = jnp.where(head == hd, o_heads[hd * t:(hd + 1) * t], o)
        _decode_values(r, dec_probs, step)
    b = _dot(pool_mixed.astype(BF16), r.w_out_b[...])
    gate_b = gate(h, 1)
    c = _dot(o.astype(BF16), r.w_out_c[...])
    ffn.at(4)
    gate_c = gate(h, 2)
    ffn.at(5)
    merged = (gate_a * a + gate_b * b + gate_c * c).astype(BF16)
    ffn.at(6)
    ffn.at(7)
    x2 = ffn.x2
    x1 = x + _dot(merged, r.w_o[...])
    r.x1[...] = x1
    r.h2[...] = _rmsnorm(x1, r.g_ffn[...]).astype(BF16)
    y = _rmsnorm(x2, r.g_final[...])
    if not decode:
        r.yp[0] = y
    else:
        @pl.when(step == n_prompt)
        def _():
            r.yp[0] = y

        @pl.when(step != n_prompt)
        def _():
            r.ys[...] = y


def _ffn_up(r, s):
    slab = r.w_up.shape[1] // FFN_SPLIT
    up = _dot(r.h2[...], r.w_up[:, s * slab:(s + 1) * slab])
    return jnp.square(jnp.maximum(up, 0.0)).astype(BF16)


def _ffn_down(r, act, s):
    slab = r.w_up.shape[1] // FFN_SPLIT
    return _dot(act, r.w_down[s * slab:(s + 1) * slab, :])


class _FfnSchedule:
    def __init__(self, r):
        self.r = r
        self.acts = {}
        self.x2 = None

    def at(self, point):
        for kind, s in FFN_POINTS[point]:
            if kind == "up":
                self.acts[s] = _ffn_up(self.r, s)
            else:
                base = self.r.x1[...] if self.x2 is None else self.x2
                self.x2 = base + _ffn_down(self.r, self.acts.pop(s), s)


def _drain_step(r):
    x2 = r.x1[...]
    act = _ffn_up(r, 0)
    for s in range(FFN_SPLIT):
        nxt = _ffn_up(r, s + 1) if s + 1 < FFN_SPLIT else None
        x2 = x2 + _ffn_down(r, act, s)
        act = nxt
    r.ys[...] = _rmsnorm(x2, r.g_final[...])


def _load_weights_bf16(r):
    ahead = LOAD_SLOTS - 1
    narrow_cols = r.w_o_f32.shape[1]
    wide_cols = max(r.w_in_f32.shape[1], r.w_up_f32.shape[1])

    def load(stage_w, stage_n, sem_w, sem_n):
        def stream(src, dst):
            narrow = src.shape[1] == narrow_cols
            stage, sem = (stage_n, sem_n) if narrow else (stage_w, sem_w)
            rows = stage.shape[1]
            n_chunks = src.shape[0] // rows
            assert src.shape[0] % rows == 0, (src.shape, rows)

            def copy(i, slot):
                return pltpu.make_async_copy(src.at[pl.ds(i * rows, rows), :],
                                             stage.at[slot, :, 0:src.shape[1]], sem.at[slot])

            def prime():
                for i in range(min(ahead, n_chunks)):
                    copy(i, i % LOAD_SLOTS).start()

            def drain():
                def body(i, carry):
                    @pl.when(i + ahead < n_chunks)
                    def _():
                        copy(i + ahead, lax.rem(i + ahead, LOAD_SLOTS)).start()

                    slot = lax.rem(i, LOAD_SLOTS)
                    copy(i, slot).wait()
                    row0 = pl.multiple_of(i * rows, rows)
                    dst[pl.ds(row0, rows), :] = stage[slot, :, 0:src.shape[1]].astype(BF16)
                    return carry

                lax.fori_loop(0, n_chunks, body, 0)

            return prime, drain

        order = [(r.w_in_f32, r.w_in), (r.w_down_f32, r.w_down), (r.w_up_f32, r.w_up),
                 (r.w_o_f32, r.w_o), (r.w_out_a_f32, r.w_out_a), (r.w_out_b_f32, r.w_out_b),
                 (r.w_out_c_f32, r.w_out_c)]
        streams = [stream(src, dst) for src, dst in order]
        streams[0][0]()
        for k, (_, drain) in enumerate(streams):
            nxt_uses_other_ring = (k + 1 < len(order)
                                   and (order[k + 1][0].shape[1] == narrow_cols)
                                   != (order[k][0].shape[1] == narrow_cols))
            if nxt_uses_other_ring:
                streams[k + 1][0]()
            drain()
            if k + 1 < len(order) and not nxt_uses_other_ring:
                streams[k + 1][0]()

    pl.run_scoped(load,
                  pltpu.VMEM((LOAD_SLOTS, LOAD_WIDE_ROWS, wide_cols), F32),
                  pltpu.VMEM((LOAD_SLOTS, LOAD_NARROW_ROWS, narrow_cols), F32),
                  pltpu.SemaphoreType.DMA((LOAD_SLOTS,)),
                  pltpu.SemaphoreType.DMA((LOAD_SLOTS,)))


def _layer_kernel(tiles_per_seq, n_prompt, *refs):
    r = _LayerRefs(*refs)
    step = pl.program_id(0)
    last = pl.num_programs(0) - 1
    j = jnp.minimum(step, n_prompt - 1) & (tiles_per_seq - 1)

    @pl.when(step == 0)
    def _():
        r.x1[...] = jnp.zeros(r.x1.shape, F32)
        r.h2[...] = jnp.zeros(r.h2.shape, BF16)
        _load_weights_bf16(r)

    @pl.when(jnp.logical_and(step < n_prompt, j == 0))
    def _():
        r.ext_p[0:POOL_PAD, :] = jnp.zeros((POOL_PAD, r.ext_p.shape[1]), F32)

    @pl.when(step < n_prompt)
    def _():
        _layer_step(r, False, j, step, n_prompt)

    @pl.when(jnp.logical_and(step >= n_prompt, step < last))
    def _():
        _layer_step(r, True, j, step, n_prompt)

    @pl.when(step == last)
    def _():
        _drain_step(r)


def _mem_kv(mem, g_mem, w_kv):
    bsz, n_mem, d = mem.shape
    xw = w_kv.shape[-1] // 2
    sb = MEM_KV_SEQS
    assert bsz % sb == 0, (bsz, sb)
    return pl.pallas_call(
        _mem_kv_kernel,
        grid=(bsz // sb,),
        in_specs=[pl.BlockSpec((sb, n_mem, d), lambda i: (i, 0, 0)),
                  _const_spec((1, d)), _const_spec((d, 2 * xw))],
        out_specs=[pl.BlockSpec((sb, xw, n_mem), lambda i: (i, 0, 0)),
                   pl.BlockSpec((sb, xw, n_mem), lambda i: (i, 0, 0)),
                   pl.BlockSpec((sb, xw, n_mem), lambda i: (i, 0, 0)),
                   pl.BlockSpec((sb, n_mem, xw), lambda i: (i, 0, 0))],
        out_shape=[jax.ShapeDtypeStruct((bsz, xw, n_mem), F32),
                   jax.ShapeDtypeStruct((bsz, xw, n_mem), F32),
                   jax.ShapeDtypeStruct((bsz, xw, n_mem), BF16),
                   jax.ShapeDtypeStruct((bsz, n_mem, xw), BF16)],
        compiler_params=pltpu.CompilerParams(
            dimension_semantics=("arbitrary",),
            vmem_limit_bytes=VMEM_LIMIT_BYTES),
        name="mem_kv",
    )(mem, g_mem.reshape(1, d).astype(F32), w_kv)


def _spatial_operands(w_s, b_s, group_dim, chunk_rows, tile_rows):
    wmix = jnp.tile(w_s[:, :chunk_rows, :chunk_rows], (1, 1, tile_rows // chunk_rows))
    bias = jnp.repeat(b_s[:, :chunk_rows].T, group_dim, axis=1)
    return wmix.astype(F32), bias.astype(F32)


def _layer(x_prompt, x2d, ktb, vb, cache_kt, cache_vt, state, lw):
    (g_mix, w_in, g_v, b_v, w_s, b_s, w_pool, pool_scale,
     w_out_a, w_out_b, w_out_c, w_o, g_ffn, w_up, w_down, g_final) = lw
    bsz, seq, d = x_prompt.shape
    n_seq = state.shape[0]
    rows = x2d.shape[0] // n_seq
    t = TILE
    ts = t // rows
    tps = seq // t
    n_prompt = bsz * tps
    n_decode = n_seq // ts
    assert n_seq % n_prompt == 0, (n_seq, n_prompt)
    sps = n_seq // n_prompt
    sgu_w = g_v.shape[-1]
    pool_w = pool_scale.shape[-1]
    n_mem, xw = vb.shape[1:]
    group_dim = sgu_w // w_s.shape[0]
    row2 = lambda vec: vec.reshape(1, -1).astype(F32)
    wmix_p, bias_p = _spatial_operands(w_s, b_s, group_dim, CHUNK, CHUNK)
    wmix_s, bias_s = _spatial_operands(w_s, b_s, group_dim, rows, t)
    wpool = jax.scipy.linalg.block_diag(*[w_pool[i] for i in range(w_pool.shape[0])])
    state_rows = jnp.transpose(state, (1, 0, 2))
    big = [w_in, w_out_a, w_out_b, w_out_c, w_o, w_up, w_down]
    weights = [row2(g_mix), w_in, row2(g_v), row2(b_v), wmix_p, bias_p, wmix_s, bias_s,
               wpool.astype(BF16), row2(pool_scale), w_out_a, w_out_b, w_out_c, w_o,
               row2(g_ffn), w_up, w_down, row2(g_final)]

    def prompt_tile(m):
        return jnp.clip(m, 0, n_prompt - 1)

    def prompt_seq(m):
        return lax.shift_right_logical(prompt_tile(m), _log2(tps))

    def decode_tile(m):
        return jnp.clip(m - n_prompt, 0, n_decode - 1)

    yp, ys, cv_p, ptail_p, cv_s, ptail_s = pl.pallas_call(
        functools.partial(_layer_kernel, tps, n_prompt),
        grid=(n_prompt + n_decode + 1,),
        in_specs=[pl.BlockSpec((1, t, d), lambda s: (prompt_tile(s), 0, 0)),
                  pl.BlockSpec((t, d), lambda s: (decode_tile(s), 0)),
                  pl.BlockSpec((1, xw, n_mem), lambda s: (prompt_seq(s), 0, 0)),
                  pl.BlockSpec((1, n_mem, xw), lambda s: (prompt_seq(s), 0, 0)),
                  pl.BlockSpec((sps * rows, d), lambda s: (prompt_tile(s), 0)),
                  pl.BlockSpec((sps, xw, n_mem), lambda s: (prompt_tile(s), 0, 0)),
                  pl.BlockSpec((sps, xw, n_mem), lambda s: (prompt_tile(s), 0, 0)),
                  pl.BlockSpec((POOL_STATE, ts, pool_w), lambda s: (0, decode_tile(s), 0))]
                 + [pl.BlockSpec(memory_space=pl.ANY) if any(w is b for b in big)
                    else _const_spec(w.shape) for w in weights],
        out_specs=[pl.BlockSpec((1, t, d), lambda s: (prompt_tile(s - 1), 0, 0)),
                   pl.BlockSpec((t, d), lambda s: (decode_tile(s - 1), 0)),
                   pl.BlockSpec((1, CHUNK, sgu_w), lambda s: (prompt_seq(s), 0, 0)),
                   pl.BlockSpec((1, POOL_PAD, pool_w), lambda s: (prompt_seq(s), 0, 0)),
                   pl.BlockSpec((t, sgu_w), lambda s: (decode_tile(s), 0)),
                   pl.BlockSpec((POOL_STATE, ts, pool_w), lambda s: (0, decode_tile(s), 0))],
        out_shape=[jax.ShapeDtypeStruct((n_prompt, t, d), F32),
                   jax.ShapeDtypeStruct((n_seq * rows, d), F32),
                   jax.ShapeDtypeStruct((bsz, CHUNK, sgu_w), F32),
                   jax.ShapeDtypeStruct((bsz, POOL_PAD, pool_w), F32),
                   jax.ShapeDtypeStruct((n_seq * rows, sgu_w), F32),
                   jax.ShapeDtypeStruct((POOL_STATE, n_seq, pool_w), F32)],
        scratch_shapes=[pltpu.VMEM((POOL_PAD + t, pool_w), F32),
                        pltpu.VMEM((pool_w // V7X_LANES, t, V7X_LANES), F32),
                        pltpu.VMEM((pool_w // V7X_LANES, t, V7X_LANES), F32),
                        pltpu.VMEM((t, d), F32),
                        pltpu.VMEM((t, d), BF16)]
                       + [pltpu.VMEM(w.shape, BF16) for w in big]
                       + [pltpu.VMEM((n_seq * rows, xw), F32)],
        compiler_params=pltpu.CompilerParams(
            dimension_semantics=("arbitrary",),
            vmem_limit_bytes=VMEM_LIMIT_BYTES),
        name="layer",
    )(x_prompt.reshape(n_prompt, t, d), x2d, ktb, vb, x2d, cache_kt, cache_vt, state_rows, *weights)
    return (yp.reshape(bsz, seq, d), ys.reshape(n_seq, rows, d), cv_p, ptail_p[:, POOL_PAD - POOL_STATE:, :],
            cv_s.reshape(n_seq, rows, sgu_w), jnp.transpose(ptail_s, (1, 0, 2)))


def _feature_major(kv):
    bsz, n_mem, heads, head_dim = kv.shape
    return jnp.transpose(kv, (0, 2, 3, 1)).reshape(bsz, heads * head_dim, n_mem)


def _token_major(kvt, heads):
    bsz, xw, n_mem = kvt.shape
    return jnp.transpose(kvt.reshape(bsz, heads, xw // heads, n_mem), (0, 3, 1, 2))


def kernel(x_prompt, x_sample, mem_prompt, cache_mem_k, cache_mem_v, state_pool, g_mix, w_in, g_v, b_v, w_s, b_s, w_pool, pool_scale, g_mem, w_kv, w_out_a, w_out_b, w_out_c, w_o, g_ffn, w_up, w_down, g_final):
    depth = w_in.shape[0]
    assert depth == 1, "the final rmsnorm is fused into the (single) layer kernel"
    heads = cache_mem_k.shape[-2]
    n_seq, rows, d = x_sample.shape
    lw = (g_mix[0], w_in[0], g_v[0], b_v[0], w_s[0], b_s[0], w_pool[0], pool_scale[0],
          w_out_a[0], w_out_b[0], w_out_c[0], w_o[0], g_ffn[0], w_up[0], w_down[0], g_final)
    mem_kt, mem_vt, ktb, vb = _mem_kv(mem_prompt, g_mem[0], w_kv[0])
    x2d = x_sample.reshape(n_seq * rows, d)
    y_prompt, y_sample, cv_p, ptail_p, cv_s, ptail_s = _layer(
        x_prompt, x2d, ktb, vb, _feature_major(cache_mem_k[0]), _feature_major(cache_mem_v[0]),
        state_pool[0], lw)
    return (y_prompt, y_sample, _token_major(mem_kt, heads)[None], _token_major(mem_vt, heads)[None],
            ptail_p[None], ptail_s[None], cv_p[None], cv_s[None])
```

```python
import functools
from typing import NamedTuple

import jax
import jax.numpy as jnp
from jax import lax
from jax.experimental import pallas as pl
from jax.experimental.pallas import tpu as pltpu

PAST_LEN = 16384
CHUNK = 128
POOL_WINDOWS = (2, 4, 8, 16)
POOL_STATE = max(POOL_WINDOWS) - 1
X_HEADS = 4
EPS = 1e-6

V7X_VMEM_BYTES = 64 * 1024 * 1024
V7X_LANES = 128
VMEM_RESERVE_BYTES = 6 * 1024 * 1024
VMEM_LIMIT_BYTES = V7X_VMEM_BYTES - VMEM_RESERVE_BYTES

TILE = 256
MEM_KV_SEQS = 4
FFN_SPLIT = 4
FFN_POINTS = ((("up", 0),), (("down", 0),), (("up", 1),), (("down", 1),),
              (("up", 2),), (("down", 2),), (("up", 3),), (("down", 3),))
LOAD_SLOTS = 8
LOAD_WIDE_ROWS = 32
LOAD_NARROW_ROWS = 128
POOL_PAD = POOL_STATE + 1

BF16 = jnp.bfloat16
F32 = jnp.float32


def _dot(a, b):
    return jnp.dot(a, b, preferred_element_type=F32)


def _log2(n):
    assert n > 0 and n & (n - 1) == 0, n
    return n.bit_length() - 1


def _rmsnorm(x, g):
    return x * lax.rsqrt(jnp.mean(x * x, axis=-1, keepdims=True) + EPS) * g


def _layernorm(x, g, b):
    xc = x - jnp.mean(x, axis=-1, keepdims=True)
    return xc * lax.rsqrt(jnp.mean(xc * xc, axis=-1, keepdims=True) + EPS) * g + b


def _softmax_bf16(s):
    e = jnp.exp(s - jnp.max(s, axis=-1, keepdims=True))
    return (e * (1.0 / jnp.sum(e, axis=-1, keepdims=True))).astype(BF16)


def _const_spec(shape):
    zeros = (0,) * len(shape)
    return pl.BlockSpec(shape, lambda *_: zeros, pipeline_mode=pl.Buffered(1))


def _chunk_mix(vhat_b, wmix_ref, chunk_rows):
    t, sgu_w = vhat_b.shape
    n_groups = wmix_ref.shape[0]
    group_dim = sgu_w // n_groups
    shift = _log2(chunk_rows)
    reps = t // chunk_rows
    row = lax.broadcasted_iota(jnp.int32, (t, t), 0)
    col = lax.broadcasted_iota(jnp.int32, (t, t), 1)
    same_chunk = (row >> shift) == (col >> shift)
    causal = col <= row
    parts = []
    for g in range(n_groups):
        w = jnp.concatenate([wmix_ref[g]] * (t // wmix_ref.shape[2]), axis=1)
        w = jnp.concatenate([w] * reps, axis=0)
        w = jnp.where(same_chunk, jnp.where(causal, w, 0.0), 0.0).astype(BF16)
        parts.append(_dot(w, vhat_b[:, g * group_dim:(g + 1) * group_dim]))
    return jnp.concatenate(parts, axis=1)


def _chunk_bias(bias_ref, t):
    return jnp.concatenate([bias_ref[...]] * (t // bias_ref.shape[0]), axis=0)


def _window_means_minus_self(load_shifted, p, pos):
    group_lanes = p.shape[-1] // len(POOL_WINDOWS)
    outs = []
    for lt in range(p.shape[-1] // V7X_LANES):
        p_lt = p[..., lt * V7X_LANES:(lt + 1) * V7X_LANES]
        lane = lax.broadcasted_iota(jnp.int32, p_lt.shape, p_lt.ndim - 1)
        wins = POOL_WINDOWS[lt * V7X_LANES // group_lanes:(lt + 1) * V7X_LANES // group_lanes]
        acc = p_lt
        sums = {1: p_lt}
        for k in range(1, max(wins)):
            acc = acc + load_shifted(k, lt)
            sums[k + 1] = acc
        win_sum = sums[wins[-1]]
        win = jnp.full(p_lt.shape, wins[-1], jnp.int32)
        for gi in range(len(wins) - 2, -1, -1):
            in_group = lane < (gi + 1) * group_lanes
            win_sum = jnp.where(in_group, sums[wins[gi]], win_sum)
            win = jnp.where(in_group, wins[gi], win)
        cnt = jnp.minimum(pos + 1, win).astype(F32)
        outs.append(win_sum / cnt - p_lt)
    return jnp.concatenate(outs, axis=-1)


def _mem_kv_kernel(mem_ref, g_mem_ref, w_kv_ref, kt_ref, vt_ref, ktb_ref, vb_ref):
    xw = kt_ref.shape[1]
    w_kv = w_kv_ref[...].astype(BF16)
    for i in range(mem_ref.shape[0]):
        hn = _rmsnorm(mem_ref[i], g_mem_ref[...]).astype(BF16)
        kv = _dot(hn, w_kv)
        kt = kv[:, 0:xw].T
        v = kv[:, xw:2 * xw]
        kt_ref[i] = kt
        vt_ref[i] = v.T
        ktb_ref[i] = kt.astype(BF16)
        vb_ref[i] = v.astype(BF16)


def _decode_own_head(r):
    n_seq, x_w, _ = r.ckt.shape
    rows = r.xq.shape[0] // n_seq
    stack = X_HEADS * rows
    lane_head = lax.broadcasted_iota(jnp.int32, (stack, x_w), 1) >> _log2(x_w // X_HEADS)
    row_head = lax.broadcasted_iota(jnp.int32, (stack, x_w), 0) >> _log2(rows)
    return lane_head == row_head


def _decode_scores(r, q):
    n_seq = r.ckt.shape[0]
    rows = r.xq.shape[0] // n_seq
    own_head = _decode_own_head(r)
    scores = []
    for i in range(n_seq):
        qs = jnp.concatenate([q[i * rows:(i + 1) * rows, :]] * X_HEADS, axis=0)
        qm = jnp.where(own_head, qs, 0.0).astype(BF16)
        scores.append(_dot(qm, r.ckt[i].astype(BF16)))
    return scores


def _decode_values(r, probs, step):
    n_seq = r.cvt.shape[0]
    rows = r.xq.shape[0] // n_seq
    own_head = _decode_own_head(r)
    for i in range(n_seq):
        of = _dot(probs[i], r.cvt[i].T.astype(BF16))
        of = jnp.where(own_head, of, 0.0)
        out = of[0:rows]
        for hd in range(1, X_HEADS):
            out = out + of[hd * rows:(hd + 1) * rows]
        row0 = pl.multiple_of((step * n_seq + i) * rows, rows)
        r.attn[pl.ds(row0, rows), :] = out


class _LayerRefs(NamedTuple):
    xp: object
    xs: object
    kt: object
    vb: object
    xq: object
    ckt: object
    cvt: object
    state: object
    g_mix: object
    w_in_f32: object
    g_v: object
    b_v: object
    wmix_p: object
    bias_p: object
    wmix_s: object
    bias_s: object
    wpool: object
    pool_scale: object
    w_out_a_f32: object
    w_out_b_f32: object
    w_out_c_f32: object
    w_o_f32: object
    g_ffn: object
    w_up_f32: object
    w_down_f32: object
    g_final: object
    yp: object
    ys: object
    cv_p: object
    ptail_p: object
    cv_s: object
    ptail_s: object
    ext_p: object
    pool_in: object
    pool_out: object
    x1: object
    h2: object
    w_in: object
    w_out_a: object
    w_out_b: object
    w_out_c: object
    w_o: object
    w_up: object
    w_down: object
    attn: object


def _layer_step(r, decode, j, step, n_prompt):
    t, d = r.x1.shape
    sgu_w = r.g_v.shape[-1]
    pool_w = r.pool_scale.shape[-1]
    x_w = r.attn.shape[-1]
    n_decode = r.attn.shape[0] // t
    head_dim = x_w // X_HEADS
    c_pool = 2 * sgu_w
    c_q = c_pool + pool_w
    c_gate = c_q + x_w

    ffn = _FfnSchedule(r)

    def gate(h, part):
        lo = c_gate + part * d
        return jax.nn.sigmoid(_dot(h, r.w_in[:, lo:lo + d]))

    ffn.at(0)
    x = r.xs[...] if decode else r.xp[0]
    h = _rmsnorm(x, r.g_mix[...]).astype(BF16)
    if decode:
        z = _dot(h, r.w_in[:, 0:c_q])
    else:
        h_dec = _rmsnorm(r.xq[...], r.g_mix[...]).astype(BF16)
        z = _dot(jnp.concatenate([h, h_dec], axis=0), r.w_in[:, 0:c_gate])
        dec_q = z[t:, c_q:c_gate] * (head_dim ** -0.5)
        z = z[0:t]
    u = z[:, 0:sgu_w]
    v = z[:, sgu_w:c_pool]
    p = z[:, c_pool:c_q]
    gate_a = gate(h, 0)
    ffn.at(1)

    u = jax.nn.gelu(u)
    vhat = _layernorm(jax.nn.gelu(v), r.g_v[...], r.b_v[...])
    if decode:
        rows = r.wmix_s.shape[1]
        r.cv_s[...] = vhat
        mixed = _chunk_mix(vhat.astype(BF16), r.wmix_s, rows)
        bias = _chunk_bias(r.bias_s, t)
    else:
        r.cv_p[0] = vhat[t - CHUNK:, :]
        mixed = _chunk_mix(vhat.astype(BF16), r.wmix_p, CHUNK)
        bias = _chunk_bias(r.bias_p, t)
    if not decode:
        q = z[:, c_q:c_gate] * (head_dim ** -0.5)
        dec_scores = _decode_scores(r, dec_q)
    ffn.at(2)
    a_in = (u * (mixed + bias)).astype(BF16)

    if decode:
        n_seq = t // rows
        lane_tiles = range(pool_w // V7X_LANES)
        for lt in lane_tiles:
            r.pool_in[lt] = p[:, lt * V7X_LANES:(lt + 1) * V7X_LANES]
        ext = [r.state[k] for k in range(POOL_STATE)]
        ext += [jnp.concatenate([r.pool_in[lt, pl.ds(i, n_seq, stride=rows), :]
                                 for lt in lane_tiles], axis=-1) for i in range(rows)]
        for i in range(rows):
            cur = POOL_STATE + i

            def load_shifted(k, lt, cur=cur):
                return ext[cur - k][:, lt * V7X_LANES:(lt + 1) * V7X_LANES]

            res = _window_means_minus_self(load_shifted, ext[cur], PAST_LEN + i)
            for lt in lane_tiles:
                r.pool_out[lt, pl.ds(i, n_seq, stride=rows), :] = (
                    res[:, lt * V7X_LANES:(lt + 1) * V7X_LANES])
        for k in range(POOL_STATE):
            r.ptail_s[k] = ext[rows + k]
        pooled = jnp.concatenate([r.pool_out[lt] for lt in lane_tiles], axis=-1)
    else:
        r.ext_p[POOL_PAD:POOL_PAD + t, :] = p

        def load_shifted(k, lt):
            return r.ext_p[POOL_PAD - k:POOL_PAD - k + t, lt * V7X_LANES:(lt + 1) * V7X_LANES]

        pos = j * t + lax.broadcasted_iota(jnp.int32, (t, V7X_LANES), 0)
        pooled = _window_means_minus_self(load_shifted, p, pos)
        tail = r.ext_p[t:t + POOL_PAD, :]
        r.ext_p[0:POOL_PAD, :] = tail
        r.ptail_p[0] = tail
    pooled = pooled.astype(BF16)

    a = _dot(a_in, r.w_out_a[...])

    if not decode:
        head = lax.broadcasted_iota(jnp.int32, (t, x_w), 1) >> _log2(head_dim)
        kt = r.kt[0]
        q_heads = jnp.concatenate(
            [jnp.where(head == hd, q, 0.0).astype(BF16) for hd in range(X_HEADS)], axis=0)
        scores = _dot(q_heads, kt)
    pool_mixed = _dot(pooled, r.wpool[...]) * r.pool_scale[...]
    ffn.at(3)
    if decode:
        row0 = pl.multiple_of(jnp.clip(step - n_prompt, 0, n_decode - 1) * t, t)
        o = r.attn[pl.ds(row0, t), :]
    else:
        dec_probs = [_softmax_bf16(s) for s in dec_scores]
        o_heads = _dot(_softmax_bf16(scores), r.vb[0])
        o = jnp.zeros((t, x_w), F32)
        for hd in range(X_HEADS):
            o = jnp.where(head == hd, o_heads[hd * t:(hd + 1) * t], o)
        _decode_values(r, dec_probs, step)
    b = _dot(pool_mixed.astype(BF16), r.w_out_b[...])
    gate_b = gate(h, 1)
    c = _dot(o.astype(BF16), r.w_out_c[...])
    ffn.at(4)
    gate_c = gate(h, 2)
    ffn.at(5)
    merged = (gate_a * a + gate_b * b + gate_c * c).astype(BF16)
    ffn.at(6)
    ffn.at(7)
    x2 = ffn.x2
    x1 = x + _dot(merged, r.w_o[...])
    r.x1[...] = x1
    r.h2[...] = _rmsnorm(x1, r.g_ffn[...]).astype(BF16)
    y = _rmsnorm(x2, r.g_final[...])
    if not decode:
        r.yp[0] = y
    else:
        @pl.when(step == n_prompt)
        def _():
            r.yp[0] = y

        @pl.when(step != n_prompt)
        def _():
            r.ys[...] = y


def _ffn_up(r, s):
    slab = r.w_up.shape[1] // FFN_SPLIT
    up = _dot(r.h2[...], r.w_up[:, s * slab:(s + 1) * slab])
    return jnp.square(jnp.maximum(up, 0.0)).astype(BF16)


def _ffn_down(r, act, s):
    slab = r.w_up.shape[1] // FFN_SPLIT
    return _dot(act, r.w_down[s * slab:(s + 1) * slab, :])


class _FfnSchedule:
    def __init__(self, r):
        self.r = r
        self.acts = {}
        self.x2 = None

    def at(self, point):
        for kind, s in FFN_POINTS[point]:
            if kind == "up":
                self.acts[s] = _ffn_up(self.r, s)
            else:
                base = self.r.x1[...] if self.x2 is None else self.x2
                self.x2 = base + _ffn_down(self.r, self.acts.pop(s), s)


def _drain_step(r):
    x2 = r.x1[...]
    act = _ffn_up(r, 0)
    for s in range(FFN_SPLIT):
        nxt = _ffn_up(r, s + 1) if s + 1 < FFN_SPLIT else None
        x2 = x2 + _ffn_down(r, act, s)
        act = nxt
    r.ys[...] = _rmsnorm(x2, r.g_final[...])


def _load_weights_bf16(r):
    ahead = LOAD_SLOTS - 1
    narrow_cols = r.w_o_f32.shape[1]
    wide_cols = max(r.w_in_f32.shape[1], r.w_up_f32.shape[1])

    def load(stage_w, stage_n, sem_w, sem_n):
        def stream(src, dst):
            narrow = src.shape[1] == narrow_cols
            stage, sem = (stage_n, sem_n) if narrow else (stage_w, sem_w)
            rows = stage.shape[1]
            n_chunks = src.shape[0] // rows
            assert src.shape[0] % rows == 0, (src.shape, rows)

            def copy(i, slot):
                return pltpu.make_async_copy(src.at[pl.ds(i * rows, rows), :],
                                             stage.at[slot, :, 0:src.shape[1]], sem.at[slot])

            def prime():
                for i in range(min(ahead, n_chunks)):
                    copy(i, i % LOAD_SLOTS).start()

            def drain():
                def body(i, carry):
                    @pl.when(i + ahead < n_chunks)
                    def _():
                        copy(i + ahead, lax.rem(i + ahead, LOAD_SLOTS)).start()

                    slot = lax.rem(i, LOAD_SLOTS)
                    copy(i, slot).wait()
                    row0 = pl.multiple_of(i * rows, rows)
                    dst[pl.ds(row0, rows), :] = stage[slot, :, 0:src.shape[1]].astype(BF16)
                    return carry

                lax.fori_loop(0, n_chunks, body, 0)

            return prime, drain

        order = [(r.w_in_f32, r.w_in), (r.w_down_f32, r.w_down), (r.w_up_f32, r.w_up),
                 (r.w_o_f32, r.w_o), (r.w_out_a_f32, r.w_out_a), (r.w_out_b_f32, r.w_out_b),
                 (r.w_out_c_f32, r.w_out_c)]
        streams = [stream(src, dst) for src, dst in order]
        streams[0][0]()
        for k, (_, drain) in enumerate(streams):
            nxt_uses_other_ring = (k + 1 < len(order)
                                   and (order[k + 1][0].shape[1] == narrow_cols)
                                   != (order[k][0].shape[1] == narrow_cols))
            if nxt_uses_other_ring:
                streams[k + 1][0]()
            drain()
            if k + 1 < len(order) and not nxt_uses_other_ring:
                streams[k + 1][0]()

    pl.run_scoped(load,
                  pltpu.VMEM((LOAD_SLOTS, LOAD_WIDE_ROWS, wide_cols), F32),
                  pltpu.VMEM((LOAD_SLOTS, LOAD_NARROW_ROWS, narrow_cols), F32),
                  pltpu.SemaphoreType.DMA((LOAD_SLOTS,)),
                  pltpu.SemaphoreType.DMA((LOAD_SLOTS,)))


def _layer_kernel(tiles_per_seq, n_prompt, *refs):
    r = _LayerRefs(*refs)
    step = pl.program_id(0)
    last = pl.num_programs(0) - 1
    j = jnp.minimum(step, n_prompt - 1) & (tiles_per_seq - 1)

    @pl.when(step == 0)
    def _():
        r.x1[...] = jnp.zeros(r.x1.shape, F32)
        r.h2[...] = jnp.zeros(r.h2.shape, BF16)
        _load_weights_bf16(r)

    @pl.when(jnp.logical_and(step < n_prompt, j == 0))
    def _():
        r.ext_p[0:POOL_PAD, :] = jnp.zeros((POOL_PAD, r.ext_p.shape[1]), F32)

    @pl.when(step < n_prompt)
    def _():
        _layer_step(r, False, j, step, n_prompt)

    @pl.when(jnp.logical_and(step >= n_prompt, step < last))
    def _():
        _layer_step(r, True, j, step, n_prompt)

    @pl.when(step == last)
    def _():
        _drain_step(r)


def _mem_kv(mem, g_mem, w_kv):
    bsz, n_mem, d = mem.shape
    xw = w_kv.shape[-1] // 2
    sb = MEM_KV_SEQS
    assert bsz % sb == 0, (bsz, sb)
    return pl.pallas_call(
        _mem_kv_kernel,
        grid=(bsz // sb,),
        in_specs=[pl.BlockSpec((sb, n_mem, d), lambda i: (i, 0, 0)),
                  _const_spec((1, d)), _const_spec((d, 2 * xw))],
        out_specs=[pl.BlockSpec((sb, xw, n_mem), lambda i: (i, 0, 0)),
                   pl.BlockSpec((sb, xw, n_mem), lambda i: (i, 0, 0)),
                   pl.BlockSpec((sb, xw, n_mem), lambda i: (i, 0, 0)),
                   pl.BlockSpec((sb, n_mem, xw), lambda i: (i, 0, 0))],
        out_shape=[jax.ShapeDtypeStruct((bsz, xw, n_mem), F32),
                   jax.ShapeDtypeStruct((bsz, xw, n_mem), F32),
                   jax.ShapeDtypeStruct((bsz, xw, n_mem), BF16),
                   jax.ShapeDtypeStruct((bsz, n_mem, xw), BF16)],
        compiler_params=pltpu.CompilerParams(
            dimension_semantics=("arbitrary",),
            vmem_limit_bytes=VMEM_LIMIT_BYTES),
        name="mem_kv",
    )(mem, g_mem.reshape(1, d).astype(F32), w_kv)


def _spatial_operands(w_s, b_s, group_dim, chunk_rows, tile_rows):
    wmix = jnp.tile(w_s[:, :chunk_rows, :chunk_rows], (1, 1, tile_rows // chunk_rows))
    bias = jnp.repeat(b_s[:, :chunk_rows].T, group_dim, axis=1)
    return wmix.astype(F32), bias.astype(F32)


def _layer(x_prompt, x2d, ktb, vb, cache_kt, cache_vt, state, lw):
    (g_mix, w_in, g_v, b_v, w_s, b_s, w_pool, pool_scale,
     w_out_a, w_out_b, w_out_c, w_o, g_ffn, w_up, w_down, g_final) = lw
    bsz, seq, d = x_prompt.shape
    n_seq = state.shape[0]
    rows = x2d.shape[0] // n_seq
    t = TILE
    ts = t // rows
    tps = seq // t
    n_prompt = bsz * tps
    n_decode = n_seq // ts
    assert n_seq % n_prompt == 0, (n_seq, n_prompt)
    sps = n_seq // n_prompt
    sgu_w = g_v.shape[-1]
    pool_w = pool_scale.shape[-1]
    n_mem, xw = vb.shape[1:]
    group_dim = sgu_w // w_s.shape[0]
    row2 = lambda vec: vec.reshape(1, -1).astype(F32)
    wmix_p, bias_p = _spatial_operands(w_s, b_s, group_dim, CHUNK, CHUNK)
    wmix_s, bias_s = _spatial_operands(w_s, b_s, group_dim, rows, t)
    wpool = jax.scipy.linalg.block_diag(*[w_pool[i] for i in range(w_pool.shape[0])])
    state_rows = jnp.transpose(state, (1, 0, 2))
    big = [w_in, w_out_a, w_out_b, w_out_c, w_o, w_up, w_down]
    weights = [row2(g_mix), w_in, row2(g_v), row2(b_v), wmix_p, bias_p, wmix_s, bias_s,
               wpool.astype(BF16), row2(pool_scale), w_out_a, w_out_b, w_out_c, w_o,
               row2(g_ffn), w_up, w_down, row2(g_final)]

    def prompt_tile(m):
        return jnp.clip(m, 0, n_prompt - 1)

    def prompt_seq(m):
        return lax.shift_right_logical(prompt_tile(m), _log2(tps))

    def decode_tile(m):
        return jnp.clip(m - n_prompt, 0, n_decode - 1)

    yp, ys, cv_p, ptail_p, cv_s, ptail_s = pl.pallas_call(
        functools.partial(_layer_kernel, tps, n_prompt),
        grid=(n_prompt + n_decode + 1,),
        in_specs=[pl.BlockSpec((1, t, d), lambda s: (prompt_tile(s), 0, 0)),
                  pl.BlockSpec((t, d), lambda s: (decode_tile(s), 0)),
                  pl.BlockSpec((1, xw, n_mem), lambda s: (prompt_seq(s), 0, 0)),
                  pl.BlockSpec((1, n_mem, xw), lambda s: (prompt_seq(s), 0, 0)),
                  pl.BlockSpec((sps * rows, d), lambda s: (prompt_tile(s), 0)),
                  pl.BlockSpec((sps, xw, n_mem), lambda s: (prompt_tile(s), 0, 0)),
                  pl.BlockSpec((sps, xw, n_mem), lambda s: (prompt_tile(s), 0, 0)),
                  pl.BlockSpec((POOL_STATE, ts, pool_w), lambda s: (0, decode_tile(s), 0))]
                 + [pl.BlockSpec(memory_space=pl.ANY) if any(w is b for b in big)
                    else _const_spec(w.shape) for w in weights],
        out_specs=[pl.BlockSpec((1, t, d), lambda s: (prompt_tile(s - 1), 0, 0)),
                   pl.BlockSpec((t, d), lambda s: (decode_tile(s - 1), 0)),
                   pl.BlockSpec((1, CHUNK, sgu_w), lambda s: (prompt_seq(s), 0, 0)),
                   pl.BlockSpec((1, POOL_PAD, pool_w), lambda s: (prompt_seq(s), 0, 0)),
                   pl.BlockSpec((t, sgu_w), lambda s: (decode_tile(s), 0)),
                   pl.BlockSpec((POOL_STATE, ts, pool_w), lambda s: (0, decode_tile(s), 0))],
        out_shape=[jax.ShapeDtypeStruct((n_prompt, t, d), F32),
                   jax.ShapeDtypeStruct((n_seq * rows, d), F32),
                   jax.ShapeDtypeStruct((bsz, CHUNK, sgu_w), F32),
                   jax.ShapeDtypeStruct((bsz, POOL_PAD, pool_w), F32),
                   jax.ShapeDtypeStruct((n_seq * rows, sgu_w), F32),
                   jax.ShapeDtypeStruct((POOL_STATE, n_seq, pool_w), F32)],
        scratch_shapes=[pltpu.VMEM((POOL_PAD + t, pool_w), F32),
                        pltpu.VMEM((pool_w // V7X_LANES, t, V7X_LANES), F32),
                        pltpu.VMEM((pool_w // V7X_LANES, t, V7X_LANES), F32),
                        pltpu.VMEM((t, d), F32),
                        pltpu.VMEM((t, d), BF16)]
                       + [pltpu.VMEM(w.shape, BF16) for w in big]
                       + [pltpu.VMEM((n_seq * rows, xw), F32)],
        compiler_params=pltpu.CompilerParams(
            dimension_semantics=("arbitrary",),
            vmem_limit_bytes=VMEM_LIMIT_BYTES),
        name="layer",
    )(x_prompt.reshape(n_prompt, t, d), x2d, ktb, vb, x2d, cache_kt, cache_vt, state_rows, *weights)
    return (yp.reshape(bsz, seq, d), ys.reshape(n_seq, rows, d), cv_p, ptail_p[:, POOL_PAD - POOL_STATE:, :],
            cv_s.reshape(n_seq, rows, sgu_w), jnp.transpose(ptail_s, (1, 0, 2)))


def _feature_major(kv):
    bsz, n_mem, heads, head_dim = kv.shape
    return jnp.transpose(kv, (0, 2, 3, 1)).reshape(bsz, heads * head_dim, n_mem)


def _token_major(kvt, heads):
    bsz, xw, n_mem = kvt.shape
    return jnp.transpose(kvt.reshape(bsz, heads, xw // heads, n_mem), (0, 3, 1, 2))


def kernel(x_prompt, x_sample, mem_prompt, cache_mem_k, cache_mem_v, state_pool, g_mix, w_in, g_v, b_v, w_s, b_s, w_pool, pool_scale, g_mem, w_kv, w_out_a, w_out_b, w_out_c, w_o, g_ffn, w_up, w_down, g_final):
    depth = w_in.shape[0]
    assert depth == 1, "the final rmsnorm is fused into the (single) layer kernel"
    heads = cache_mem_k.shape[-2]
    n_seq, rows, d = x_sample.shape
    lw = (g_mix[0], w_in[0], g_v[0], b_v[0], w_s[0], b_s[0], w_pool[0], pool_scale[0],
          w_out_a[0], w_out_b[0], w_out_c[0], w_o[0], g_ffn[0], w_up[0], w_down[0], g_final)
    mem_kt, mem_vt, ktb, vb = _mem_kv(mem_prompt, g_mem[0], w_kv[0])
    x2d = x_sample.reshape(n_seq * rows, d)
    y_prompt, y_sample, cv_p, ptail_p, cv_s, ptail_s = _layer(
        x_prompt, x2d, ktb, vb, _feature_major(cache_mem_k[0]), _feature_major(cache_mem_v[0]),
        state_pool[0], lw)
    return (y_prompt, y_sample, _token_major(mem_kt, heads)[None], _token_major(mem_vt, heads)[None],
            ptail_p[None], ptail_s[None], cv_p[None], cv_s[None])
```

```python
import functools
from typing import NamedTuple

import jax
import jax.numpy as jnp
from jax import lax
from jax.experimental import pallas as pl
from jax.experimental.pallas import tpu as pltpu

PAST_LEN = 16384
CHUNK = 128
POOL_WINDOWS = (2, 4, 8, 16)
POOL_STATE = max(POOL_WINDOWS) - 1
X_HEADS = 4
EPS = 1e-6

V7X_VMEM_BYTES = 64 * 1024 * 1024
V7X_LANES = 128
VMEM_RESERVE_BYTES = 6 * 1024 * 1024
VMEM_LIMIT_BYTES = V7X_VMEM_BYTES - VMEM_RESERVE_BYTES

TILE = 256
MEM_KV_SEQS = 4
FFN_SPLIT = 4
FFN_POINTS = ((("up", 0),), (("down", 0),), (("up", 1),), (("down", 1),),
              (("up", 2),), (("down", 2),), (("up", 3),), (("down", 3),))
LOAD_SLOTS = 8
LOAD_WIDE_ROWS = 32
LOAD_NARROW_ROWS = 128
POOL_PAD = POOL_STATE + 1

BF16 = jnp.bfloat16
F32 = jnp.float32


def _dot(a, b):
    return jnp.dot(a, b, preferred_element_type=F32)


def _log2(n):
    assert n > 0 and n & (n - 1) == 0, n
    return n.bit_length() - 1


def _rmsnorm(x, g):
    return x * lax.rsqrt(jnp.mean(x * x, axis=-1, keepdims=True) + EPS) * g


def _layernorm(x, g, b):
    xc = x - jnp.mean(x, axis=-1, keepdims=True)
    return xc * lax.rsqrt(jnp.mean(xc * xc, axis=-1, keepdims=True) + EPS) * g + b


def _softmax_bf16(s):
    e = jnp.exp(s - jnp.max(s, axis=-1, keepdims=True))
    return (e * (1.0 / jnp.sum(e, axis=-1, keepdims=True))).astype(BF16)


def _const_spec(shape):
    zeros = (0,) * len(shape)
    return pl.BlockSpec(shape, lambda *_: zeros, pipeline_mode=pl.Buffered(1))


def _chunk_mix(vhat_b, wmix_ref, chunk_rows):
    t, sgu_w = vhat_b.shape
    n_groups = wmix_ref.shape[0]
    group_dim = sgu_w // n_groups
    shift = _log2(chunk_rows)
    reps = t // chunk_rows
    row = lax.broadcasted_iota(jnp.int32, (t, t), 0)
    col = lax.broadcasted_iota(jnp.int32, (t, t), 1)
    same_chunk = (row >> shift) == (col >> shift)
    causal = col <= row
    parts = []
    for g in range(n_groups):
        w = jnp.concatenate([wmix_ref[g]] * (t // wmix_ref.shape[2]), axis=1)
        w = jnp.concatenate([w] * reps, axis=0)
        w = jnp.where(same_chunk, jnp.where(causal, w, 0.0), 0.0).astype(BF16)
        parts.append(_dot(w, vhat_b[:, g * group_dim:(g + 1) * group_dim]))
    return jnp.concatenate(parts, axis=1)


def _chunk_bias(bias_ref, chunk_rows, t, group_dim):
    bt = bias_ref[0:chunk_rows, :]
    cols = [jnp.broadcast_to(bt[:, g:g + 1], (chunk_rows, group_dim)) for g in range(bt.shape[1])]
    return jnp.concatenate([jnp.concatenate(cols, axis=1)] * (t // chunk_rows), axis=0)


def _window_means_minus_self(load_shifted, p, pos):
    group_lanes = p.shape[-1] // len(POOL_WINDOWS)
    outs = []
    for lt in range(p.shape[-1] // V7X_LANES):
        p_lt = p[..., lt * V7X_LANES:(lt + 1) * V7X_LANES]
        lane = lax.broadcasted_iota(jnp.int32, p_lt.shape, p_lt.ndim - 1)
        wins = POOL_WINDOWS[lt * V7X_LANES // group_lanes:(lt + 1) * V7X_LANES // group_lanes]
        acc = p_lt
        sums = {1: p_lt}
        for k in range(1, max(wins)):
            acc = acc + load_shifted(k, lt)
            sums[k + 1] = acc
        win_sum = sums[wins[-1]]
        win = jnp.full(p_lt.shape, wins[-1], jnp.int32)
        for gi in range(len(wins) - 2, -1, -1):
            in_group = lane < (gi + 1) * group_lanes
            win_sum = jnp.where(in_group, sums[wins[gi]], win_sum)
            win = jnp.where(in_group, wins[gi], win)
        cnt = jnp.minimum(pos + 1, win).astype(F32)
        outs.append(win_sum / cnt - p_lt)
    return jnp.concatenate(outs, axis=-1)


def _mem_kv_kernel(mem_ref, g_mem_ref, w_kv_ref, kt_ref, vt_ref, ktb_ref, vb_ref):
    xw = kt_ref.shape[1]
    w_kv = w_kv_ref[...].astype(BF16)
    for i in range(mem_ref.shape[0]):
        hn = _rmsnorm(mem_ref[i], g_mem_ref[...]).astype(BF16)
        kv = _dot(hn, w_kv)
        kt = kv[:, 0:xw].T
        v = kv[:, xw:2 * xw]
        kt_ref[i] = kt
        vt_ref[i] = v.T
        ktb_ref[i] = kt.astype(BF16)
        vb_ref[i] = v.astype(BF16)


def _decode_own_head(r):
    n_seq, x_w, _ = r.ckt.shape
    rows = r.xq.shape[0] // n_seq
    stack = X_HEADS * rows
    lane_head = lax.broadcasted_iota(jnp.int32, (stack, x_w), 1) >> _log2(x_w // X_HEADS)
    row_head = lax.broadcasted_iota(jnp.int32, (stack, x_w), 0) >> _log2(rows)
    return lane_head == row_head


def _decode_scores(r, q):
    n_seq = r.ckt.shape[0]
    rows = r.xq.shape[0] // n_seq
    own_head = _decode_own_head(r)
    scores = []
    for i in range(n_seq):
        qs = jnp.concatenate([q[i * rows:(i + 1) * rows, :]] * X_HEADS, axis=0)
        qm = jnp.where(own_head, qs, 0.0).astype(BF16)
        scores.append(_dot(qm, r.ckt[i].astype(BF16)))
    return scores


def _decode_values(r, probs, step):
    n_seq = r.cvt.shape[0]
    rows = r.xq.shape[0] // n_seq
    own_head = _decode_own_head(r)
    for i in range(n_seq):
        of = _dot(probs[i], r.cvt[i].T.astype(BF16))
        of = jnp.where(own_head, of, 0.0)
        out = of[0:rows]
        for hd in range(1, X_HEADS):
            out = out + of[hd * rows:(hd + 1) * rows]
        row0 = pl.multiple_of((step * n_seq + i) * rows, rows)
        r.attn[pl.ds(row0, rows), :] = out


class _LayerRefs(NamedTuple):
    xp: object
    xs: object
    kt: object
    vb: object
    xq: object
    ckt: object
    cvt: object
    state: object
    g_mix: object
    w_in_f32: object
    g_v: object
    b_v: object
    wmix_p: object
    bias_t: object
    wmix_s: object
    wpool: object
    pool_scale: object
    w_out_a_f32: object
    w_out_b_f32: object
    w_out_c_f32: object
    w_o_f32: object
    g_ffn: object
    w_up_f32: object
    w_down_f32: object
    g_final: object
    yp: object
    ys: object
    cv_p: object
    ptail_p: object
    cv_s: object
    ptail_s: object
    ext_p: object
    pool_in: object
    pool_out: object
    x1: object
    h2: object
    w_in: object
    w_out_a: object
    w_out_b: object
    w_out_c: object
    w_o: object
    w_up: object
    w_down: object
    attn: object


def _layer_step(r, decode, j, step, n_prompt):
    t, d = r.x1.shape
    sgu_w = r.g_v.shape[-1]
    pool_w = r.pool_scale.shape[-1]
    x_w = r.attn.shape[-1]
    n_decode = r.attn.shape[0] // t
    head_dim = x_w // X_HEADS
    c_pool = 2 * sgu_w
    c_q = c_pool + pool_w
    c_gate = c_q + x_w

    ffn = _FfnSchedule(r)

    def gate(h, part):
        lo = c_gate + part * d
        return jax.nn.sigmoid(_dot(h, r.w_in[:, lo:lo + d]))

    ffn.at(0)
    x = r.xs[...] if decode else r.xp[0]
    h = _rmsnorm(x, r.g_mix[...]).astype(BF16)
    if decode:
        z = _dot(h, r.w_in[:, 0:c_q])
    else:
        h_dec = _rmsnorm(r.xq[...], r.g_mix[...]).astype(BF16)
        z = _dot(jnp.concatenate([h, h_dec], axis=0), r.w_in[:, 0:c_gate])
        dec_q = z[t:, c_q:c_gate] * (head_dim ** -0.5)
        z = z[0:t]
    u = z[:, 0:sgu_w]
    v = z[:, sgu_w:c_pool]
    p = z[:, c_pool:c_q]
    gate_a = gate(h, 0)
    ffn.at(1)

    u = jax.nn.gelu(u)
    vhat = _layernorm(jax.nn.gelu(v), r.g_v[...], r.b_v[...])
    if decode:
        rows = r.wmix_s.shape[1]
        r.cv_s[...] = vhat
        mixed = _chunk_mix(vhat.astype(BF16), r.wmix_s, rows)
        bias = _chunk_bias(r.bias_t, rows, t, sgu_w // r.wmix_s.shape[0])
    else:
        r.cv_p[0] = vhat[t - CHUNK:, :]
        mixed = _chunk_mix(vhat.astype(BF16), r.wmix_p, CHUNK)
        bias = _chunk_bias(r.bias_t, CHUNK, t, sgu_w // r.wmix_p.shape[0])
    if not decode:
        q = z[:, c_q:c_gate] * (head_dim ** -0.5)
        dec_scores = _decode_scores(r, dec_q)
    ffn.at(2)
    a_in = (u * (mixed + bias)).astype(BF16)

    if decode:
        n_seq = t // rows
        lane_tiles = range(pool_w // V7X_LANES)
        for lt in lane_tiles:
            r.pool_in[lt] = p[:, lt * V7X_LANES:(lt + 1) * V7X_LANES]
        ext = [r.state[k] for k in range(POOL_STATE)]
        ext += [jnp.concatenate([r.pool_in[lt, pl.ds(i, n_seq, stride=rows), :]
                                 for lt in lane_tiles], axis=-1) for i in range(rows)]
        for i in range(rows):
            cur = POOL_STATE + i

            def load_shifted(k, lt, cur=cur):
                return ext[cur - k][:, lt * V7X_LANES:(lt + 1) * V7X_LANES]

            res = _window_means_minus_self(load_shifted, ext[cur], PAST_LEN + i)
            for lt in lane_tiles:
                r.pool_out[lt, pl.ds(i, n_seq, stride=rows), :] = (
                    res[:, lt * V7X_LANES:(lt + 1) * V7X_LANES])
        for k in range(POOL_STATE):
            r.ptail_s[k] = ext[rows + k]
        pooled = jnp.concatenate([r.pool_out[lt] for lt in lane_tiles], axis=-1)
    else:
        r.ext_p[POOL_PAD:POOL_PAD + t, :] = p

        def load_shifted(k, lt):
            return r.ext_p[POOL_PAD - k:POOL_PAD - k + t, lt * V7X_LANES:(lt + 1) * V7X_LANES]

        pos = j * t + lax.broadcasted_iota(jnp.int32, (t, V7X_LANES), 0)
        pooled = _window_means_minus_self(load_shifted, p, pos)
        tail = r.ext_p[t:t + POOL_PAD, :]
        r.ext_p[0:POOL_PAD, :] = tail
        r.ptail_p[0] = tail
    pooled = pooled.astype(BF16)

    a = _dot(a_in, r.w_out_a[...])

    if not decode:
        head = lax.broadcasted_iota(jnp.int32, (t, x_w), 1) >> _log2(head_dim)
        kt = r.kt[0]
        q_heads = jnp.concatenate(
            [jnp.where(head == hd, q, 0.0).astype(BF16) for hd in range(X_HEADS)], axis=0)
        scores = _dot(q_heads, kt)
    pool_mixed = _dot(pooled, r.wpool[...]) * r.pool_scale[...]
    ffn.at(3)
    if decode:
        row0 = pl.multiple_of(jnp.clip(step - n_prompt, 0, n_decode - 1) * t, t)
        o = r.attn[pl.ds(row0, t), :]
    else:
        dec_probs = [_softmax_bf16(s) for s in dec_scores]
        o_heads = _dot(_softmax_bf16(scores), r.vb[0])
        o = jnp.zeros((t, x_w), F32)
        for hd in range(X_HEADS):
            o = jnp.where(head == hd, o_heads[hd * t:(hd + 1) * t], o)
        _decode_values(r, dec_probs, step)
    b = _dot(pool_mixed.astype(BF16), r.w_out_b[...])
    gate_b = gate(h, 1)
    c = _dot(o.astype(BF16), r.w_out_c[...])
    ffn.at(4)
    gate_c = gate(h, 2)
    ffn.at(5)
    merged = (gate_a * a + gate_b * b + gate_c * c).astype(BF16)
    ffn.at(6)
    ffn.at(7)
    x2 = ffn.x2
    x1 = x + _dot(merged, r.w_o[...])
    r.x1[...] = x1
    r.h2[...] = _rmsnorm(x1, r.g_ffn[...]).astype(BF16)
    y = _rmsnorm(x2, r.g_final[...])
    if not decode:
        r.yp[0] = y
    else:
        @pl.when(step == n_prompt)
        def _():
            r.yp[0] = y

        @pl.when(step != n_prompt)
        def _():
            r.ys[...] = y


def _ffn_up(r, s):
    slab = r.w_up.shape[1] // FFN_SPLIT
    up = _dot(r.h2[...], r.w_up[:, s * slab:(s + 1) * slab])
    return jnp.square(jnp.maximum(up, 0.0)).astype(BF16)


def _ffn_down(r, act, s):
    slab = r.w_up.shape[1] // FFN_SPLIT
    return _dot(act, r.w_down[s * slab:(s + 1) * slab, :])


class _FfnSchedule:
    def __init__(self, r):
        self.r = r
        self.acts = {}
        self.x2 = None

    def at(self, point):
        for kind, s in FFN_POINTS[point]:
            if kind == "up":
                self.acts[s] = _ffn_up(self.r, s)
            else:
                base = self.r.x1[...] if self.x2 is None else self.x2
                self.x2 = base + _ffn_down(self.r, self.acts.pop(s), s)


def _drain_step(r):
    x2 = r.x1[...]
    act = _ffn_up(r, 0)
    for s in range(FFN_SPLIT):
        nxt = _ffn_up(r, s + 1) if s + 1 < FFN_SPLIT else None
        x2 = x2 + _ffn_down(r, act, s)
        act = nxt
    r.ys[...] = _rmsnorm(x2, r.g_final[...])


def _load_weights_bf16(r):
    ahead = LOAD_SLOTS - 1
    narrow_cols = r.w_o_f32.shape[1]
    wide_cols = max(r.w_in_f32.shape[1], r.w_up_f32.shape[1])

    def load(stage_w, stage_n, sem_w, sem_n):
        def stream(src, dst):
            narrow = src.shape[1] == narrow_cols
            stage, sem = (stage_n, sem_n) if narrow else (stage_w, sem_w)
            rows = stage.shape[1]
            n_chunks = src.shape[0] // rows
            assert src.shape[0] % rows == 0, (src.shape, rows)

            def copy(i, slot):
                return pltpu.make_async_copy(src.at[pl.ds(i * rows, rows), :],
                                             stage.at[slot, :, 0:src.shape[1]], sem.at[slot])

            def prime():
                for i in range(min(ahead, n_chunks)):
                    copy(i, i % LOAD_SLOTS).start()

            def drain():
                def body(i, carry):
                    @pl.when(i + ahead < n_chunks)
                    def _():
                        copy(i + ahead, lax.rem(i + ahead, LOAD_SLOTS)).start()

                    slot = lax.rem(i, LOAD_SLOTS)
                    copy(i, slot).wait()
                    row0 = pl.multiple_of(i * rows, rows)
                    dst[pl.ds(row0, rows), :] = stage[slot, :, 0:src.shape[1]].astype(BF16)
                    return carry

                lax.fori_loop(0, n_chunks, body, 0)

            return prime, drain

        order = [(r.w_in_f32, r.w_in), (r.w_down_f32, r.w_down), (r.w_up_f32, r.w_up),
                 (r.w_o_f32, r.w_o), (r.w_out_a_f32, r.w_out_a), (r.w_out_b_f32, r.w_out_b),
                 (r.w_out_c_f32, r.w_out_c)]
        streams = [stream(src, dst) for src, dst in order]
        streams[0][0]()
        for k, (_, drain) in enumerate(streams):
            nxt_uses_other_ring = (k + 1 < len(order)
                                   and (order[k + 1][0].shape[1] == narrow_cols)
                                   != (order[k][0].shape[1] == narrow_cols))
            if nxt_uses_other_ring:
                streams[k + 1][0]()
            drain()
            if k + 1 < len(order) and not nxt_uses_other_ring:
                streams[k + 1][0]()

    pl.run_scoped(load,
                  pltpu.VMEM((LOAD_SLOTS, LOAD_WIDE_ROWS, wide_cols), F32),
                  pltpu.VMEM((LOAD_SLOTS, LOAD_NARROW_ROWS, narrow_cols), F32),
                  pltpu.SemaphoreType.DMA((LOAD_SLOTS,)),
                  pltpu.SemaphoreType.DMA((LOAD_SLOTS,)))


def _layer_kernel(tiles_per_seq, n_prompt, *refs):
    r = _LayerRefs(*refs)
    step = pl.program_id(0)
    last = pl.num_programs(0) - 1
    j = jnp.minimum(step, n_prompt - 1) & (tiles_per_seq - 1)

    @pl.when(step == 0)
    def _():
        r.x1[...] = jnp.zeros(r.x1.shape, F32)
        r.h2[...] = jnp.zeros(r.h2.shape, BF16)
        _load_weights_bf16(r)

    @pl.when(jnp.logical_and(step < n_prompt, j == 0))
    def _():
        r.ext_p[0:POOL_PAD, :] = jnp.zeros((POOL_PAD, r.ext_p.shape[1]), F32)

    @pl.when(step < n_prompt)
    def _():
        _layer_step(r, False, j, step, n_prompt)

    @pl.when(jnp.logical_and(step >= n_prompt, step < last))
    def _():
        _layer_step(r, True, j, step, n_prompt)

    @pl.when(step == last)
    def _():
        _drain_step(r)


def _mem_kv(mem, g_mem, w_kv):
    bsz, n_mem, d = mem.shape
    xw = w_kv.shape[-1] // 2
    sb = MEM_KV_SEQS
    assert bsz % sb == 0, (bsz, sb)
    return pl.pallas_call(
        _mem_kv_kernel,
        grid=(bsz // sb,),
        in_specs=[pl.BlockSpec((sb, n_mem, d), lambda i: (i, 0, 0)),
                  _const_spec((1, d)), _const_spec((d, 2 * xw))],
        out_specs=[pl.BlockSpec((sb, xw, n_mem), lambda i: (i, 0, 0)),
                   pl.BlockSpec((sb, xw, n_mem), lambda i: (i, 0, 0)),
                   pl.BlockSpec((sb, xw, n_mem), lambda i: (i, 0, 0)),
                   pl.BlockSpec((sb, n_mem, xw), lambda i: (i, 0, 0))],
        out_shape=[jax.ShapeDtypeStruct((bsz, xw, n_mem), F32),
                   jax.ShapeDtypeStruct((bsz, xw, n_mem), F32),
                   jax.ShapeDtypeStruct((bsz, xw, n_mem), BF16),
                   jax.ShapeDtypeStruct((bsz, n_mem, xw), BF16)],
        compiler_params=pltpu.CompilerParams(
            dimension_semantics=("arbitrary",),
            vmem_limit_bytes=VMEM_LIMIT_BYTES),
        name="mem_kv",
    )(mem, g_mem.reshape(1, d).astype(F32), w_kv)


def _spatial_weights(w_s, chunk_rows, tile_rows):
    wmix = jnp.tile(w_s[:, :chunk_rows, :chunk_rows], (1, 1, tile_rows // chunk_rows))
    return wmix.astype(F32)


def _layer(x_prompt, x2d, ktb, vb, cache_kt, cache_vt, state, lw):
    (g_mix, w_in, g_v, b_v, w_s, b_s, w_pool, pool_scale,
     w_out_a, w_out_b, w_out_c, w_o, g_ffn, w_up, w_down, g_final) = lw
    bsz, seq, d = x_prompt.shape
    n_seq = state.shape[0]
    rows = x2d.shape[0] // n_seq
    t = TILE
    ts = t // rows
    tps = seq // t
    n_prompt = bsz * tps
    n_decode = n_seq // ts
    assert n_seq % n_prompt == 0, (n_seq, n_prompt)
    sps = n_seq // n_prompt
    sgu_w = g_v.shape[-1]
    pool_w = pool_scale.shape[-1]
    n_mem, xw = vb.shape[1:]
    group_dim = sgu_w // w_s.shape[0]
    row2 = lambda vec: vec.reshape(1, -1).astype(F32)
    wmix_p = _spatial_weights(w_s, CHUNK, CHUNK)
    wmix_s = _spatial_weights(w_s, rows, t)
    bias_t = b_s.T.astype(F32)
    wpool = jax.scipy.linalg.block_diag(*[w_pool[i] for i in range(w_pool.shape[0])])
    state_rows = jnp.transpose(state, (1, 0, 2))
    big = [w_in, w_out_a, w_out_b, w_out_c, w_o, w_up, w_down]
    weights = [row2(g_mix), w_in, row2(g_v), row2(b_v), wmix_p, bias_t, wmix_s,
               wpool.astype(BF16), row2(pool_scale), w_out_a, w_out_b, w_out_c, w_o,
               row2(g_ffn), w_up, w_down, row2(g_final)]

    def prompt_tile(m):
        return jnp.clip(m, 0, n_prompt - 1)

    def prompt_seq(m):
        return lax.shift_right_logical(prompt_tile(m), _log2(tps))

    def decode_tile(m):
        return jnp.clip(m - n_prompt, 0, n_decode - 1)

    yp, ys, cv_p, ptail_p, cv_s, ptail_s = pl.pallas_call(
        functools.partial(_layer_kernel, tps, n_prompt),
        grid=(n_prompt + n_decode + 1,),
        in_specs=[pl.BlockSpec((1, t, d), lambda s: (prompt_tile(s), 0, 0)),
                  pl.BlockSpec((t, d), lambda s: (decode_tile(s), 0)),
                  pl.BlockSpec((1, xw, n_mem), lambda s: (prompt_seq(s), 0, 0)),
                  pl.BlockSpec((1, n_mem, xw), lambda s: (prompt_seq(s), 0, 0)),
                  pl.BlockSpec((sps * rows, d), lambda s: (prompt_tile(s), 0)),
                  pl.BlockSpec((sps, xw, n_mem), lambda s: (prompt_tile(s), 0, 0)),
                  pl.BlockSpec((sps, xw, n_mem), lambda s: (prompt_tile(s), 0, 0)),
                  pl.BlockSpec((POOL_STATE, ts, pool_w), lambda s: (0, decode_tile(s), 0))]
                 + [pl.BlockSpec(memory_space=pl.ANY) if any(w is b for b in big)
                    else _const_spec(w.shape) for w in weights],
        out_specs=[pl.BlockSpec((1, t, d), lambda s: (prompt_tile(s - 1), 0, 0)),
                   pl.BlockSpec((t, d), lambda s: (decode_tile(s - 1), 0)),
                   pl.BlockSpec((1, CHUNK, sgu_w), lambda s: (prompt_seq(s), 0, 0)),
                   pl.BlockSpec((1, POOL_PAD, pool_w), lambda s: (prompt_seq(s), 0, 0)),
                   pl.BlockSpec((t, sgu_w), lambda s: (decode_tile(s), 0)),
                   pl.BlockSpec((POOL_STATE, ts, pool_w), lambda s: (0, decode_tile(s), 0))],
        out_shape=[jax.ShapeDtypeStruct((n_prompt, t, d), F32),
                   jax.ShapeDtypeStruct((n_seq * rows, d), F32),
                   jax.ShapeDtypeStruct((bsz, CHUNK, sgu_w), F32),
                   jax.ShapeDtypeStruct((bsz, POOL_PAD, pool_w), F32),
                   jax.ShapeDtypeStruct((n_seq * rows, sgu_w), F32),
                   jax.ShapeDtypeStruct((POOL_STATE, n_seq, pool_w), F32)],
        scratch_shapes=[pltpu.VMEM((POOL_PAD + t, pool_w), F32),
                        pltpu.VMEM((pool_w // V7X_LANES, t, V7X_LANES), F32),
                        pltpu.VMEM((pool_w // V7X_LANES, t, V7X_LANES), F32),
                        pltpu.VMEM((t, d), F32),
                        pltpu.VMEM((t, d), BF16)]
                       + [pltpu.VMEM(w.shape, BF16) for w in big]
                       + [pltpu.VMEM((n_seq * rows, xw), F32)],
        compiler_params=pltpu.CompilerParams(
            dimension_semantics=("arbitrary",),
            vmem_limit_bytes=VMEM_LIMIT_BYTES),
        name="layer",
    )(x_prompt.reshape(n_prompt, t, d), x2d, ktb, vb, x2d, cache_kt, cache_vt, state_rows, *weights)
    return (yp.reshape(bsz, seq, d), ys.reshape(n_seq, rows, d), cv_p, ptail_p[:, POOL_PAD - POOL_STATE:, :],
            cv_s.reshape(n_seq, rows, sgu_w), jnp.transpose(ptail_s, (1, 0, 2)))


def _feature_major(kv):
    bsz, n_mem, heads, head_dim = kv.shape
    return jnp.transpose(kv, (0, 2, 3, 1)).reshape(bsz, heads * head_dim, n_mem)


def _token_major(kvt, heads):
    bsz, xw, n_mem = kvt.shape
    return jnp.transpose(kvt.reshape(bsz, heads, xw // heads, n_mem), (0, 3, 1, 2))


def kernel(x_prompt, x_sample, mem_prompt, cache_mem_k, cache_mem_v, state_pool, g_mix, w_in, g_v, b_v, w_s, b_s, w_pool, pool_scale, g_mem, w_kv, w_out_a, w_out_b, w_out_c, w_o, g_ffn, w_up, w_down, g_final):
    depth = w_in.shape[0]
    assert depth == 1, "the final rmsnorm is fused into the (single) layer kernel"
    heads = cache_mem_k.shape[-2]
    n_seq, rows, d = x_sample.shape
    lw = (g_mix[0], w_in[0], g_v[0], b_v[0], w_s[0], b_s[0], w_pool[0], pool_scale[0],
          w_out_a[0], w_out_b[0], w_out_c[0], w_o[0], g_ffn[0], w_up[0], w_down[0], g_final)
    mem_kt, mem_vt, ktb, vb = _mem_kv(mem_prompt, g_mem[0], w_kv[0])
    x2d = x_sample.reshape(n_seq * rows, d)
    y_prompt, y_sample, cv_p, ptail_p, cv_s, ptail_s = _layer(
        x_prompt, x2d, ktb, vb, _feature_major(cache_mem_k[0]), _feature_major(cache_mem_v[0]),
        state_pool[0], lw)
    return (y_prompt, y_sample, _token_major(mem_kt, heads)[None], _token_major(mem_vt, heads)[None],
            ptail_p[None], ptail_s[None], cv_p[None], cv_s[None])
```

```python
import functools
from typing import NamedTuple

import jax
import jax.numpy as jnp
from jax import lax
from jax.experimental import pallas as pl
from jax.experimental.pallas import tpu as pltpu

PAST_LEN = 16384
CHUNK = 128
POOL_WINDOWS = (2, 4, 8, 16)
POOL_STATE = max(POOL_WINDOWS) - 1
X_HEADS = 4
EPS = 1e-6

V7X_VMEM_BYTES = 64 * 1024 * 1024
V7X_LANES = 128
VMEM_RESERVE_BYTES = 6 * 1024 * 1024
VMEM_LIMIT_BYTES = V7X_VMEM_BYTES - VMEM_RESERVE_BYTES

TILE = 256
MEM_KV_SEQS = 4
FFN_SPLIT = 4
FFN_POINTS = ((("up", 0),), (("down", 0),), (("up", 1),), (("down", 1),),
              (("up", 2),), (("down", 2),), (("up", 3),), (("down", 3),))
LOAD_SLOTS = 8
LOAD_WIDE_ROWS = 32
LOAD_NARROW_ROWS = 128
POOL_PAD = POOL_STATE + 1

BF16 = jnp.bfloat16
F32 = jnp.float32


def _dot(a, b):
    return jnp.dot(a, b, preferred_element_type=F32)


def _log2(n):
    assert n > 0 and n & (n - 1) == 0, n
    return n.bit_length() - 1


def _rmsnorm(x, g):
    return x * lax.rsqrt(jnp.mean(x * x, axis=-1, keepdims=True) + EPS) * g


def _layernorm(x, g, b):
    xc = x - jnp.mean(x, axis=-1, keepdims=True)
    return xc * lax.rsqrt(jnp.mean(xc * xc, axis=-1, keepdims=True) + EPS) * g + b


def _softmax_bf16(s):
    e = jnp.exp(s - jnp.max(s, axis=-1, keepdims=True))
    return (e * (1.0 / jnp.sum(e, axis=-1, keepdims=True))).astype(BF16)


def _const_spec(shape):
    zeros = (0,) * len(shape)
    return pl.BlockSpec(shape, lambda *_: zeros, pipeline_mode=pl.Buffered(1))


def _chunk_mix(vhat_b, wmix_ref, chunk_rows):
    t, sgu_w = vhat_b.shape
    n_groups = wmix_ref.shape[0]
    group_dim = sgu_w // n_groups
    shift = _log2(chunk_rows)
    reps = t // chunk_rows
    row = lax.broadcasted_iota(jnp.int32, (t, t), 0)
    col = lax.broadcasted_iota(jnp.int32, (t, t), 1)
    same_chunk = (row >> shift) == (col >> shift)
    causal = col <= row
    def masked(g):
        w = jnp.concatenate([wmix_ref[g]] * (t // wmix_ref.shape[2]), axis=1)
        w = jnp.concatenate([w] * reps, axis=0)
        return jnp.where(same_chunk, jnp.where(causal, w, 0.0), 0.0).astype(BF16)

    assert n_groups % 2 == 0
    parts = []
    for g in range(0, n_groups, 2):
        both = _dot(jnp.concatenate([masked(g), masked(g + 1)], axis=0),
                    vhat_b[:, g * group_dim:(g + 2) * group_dim])
        parts += [both[0:t, 0:group_dim], both[t:2 * t, group_dim:2 * group_dim]]
    return jnp.concatenate(parts, axis=1)


def _chunk_bias(bias_ref, t):
    return jnp.concatenate([bias_ref[...]] * (t // bias_ref.shape[0]), axis=0)


def _window_means_minus_self(load_shifted, p, pos):
    group_lanes = p.shape[-1] // len(POOL_WINDOWS)
    outs = []
    for lt in range(p.shape[-1] // V7X_LANES):
        p_lt = p[..., lt * V7X_LANES:(lt + 1) * V7X_LANES]
        lane = lax.broadcasted_iota(jnp.int32, p_lt.shape, p_lt.ndim - 1)
        wins = POOL_WINDOWS[lt * V7X_LANES // group_lanes:(lt + 1) * V7X_LANES // group_lanes]
        acc = p_lt
        sums = {1: p_lt}
        for k in range(1, max(wins)):
            acc = acc + load_shifted(k, lt)
            sums[k + 1] = acc
        win_sum = sums[wins[-1]]
        win = jnp.full(p_lt.shape, wins[-1], jnp.int32)
        for gi in range(len(wins) - 2, -1, -1):
            in_group = lane < (gi + 1) * group_lanes
            win_sum = jnp.where(in_group, sums[wins[gi]], win_sum)
            win = jnp.where(in_group, wins[gi], win)
        cnt = jnp.minimum(pos + 1, win).astype(F32)
        outs.append(win_sum / cnt - p_lt)
    return jnp.concatenate(outs, axis=-1)


def _mem_kv_kernel(mem_ref, g_mem_ref, w_kv_ref, kt_ref, vt_ref, ktb_ref, vb_ref):
    xw = kt_ref.shape[1]
    w_kv = w_kv_ref[...].astype(BF16)
    for i in range(mem_ref.shape[0]):
        hn = _rmsnorm(mem_ref[i], g_mem_ref[...]).astype(BF16)
        kv = _dot(hn, w_kv)
        kt = kv[:, 0:xw].T
        v = kv[:, xw:2 * xw]
        kt_ref[i] = kt
        vt_ref[i] = v.T
        ktb_ref[i] = kt.astype(BF16)
        vb_ref[i] = v.astype(BF16)


def _decode_own_head(r):
    n_seq, x_w, _ = r.ckt.shape
    rows = r.xq.shape[0] // n_seq
    stack = X_HEADS * rows
    lane_head = lax.broadcasted_iota(jnp.int32, (stack, x_w), 1) >> _log2(x_w // X_HEADS)
    row_head = lax.broadcasted_iota(jnp.int32, (stack, x_w), 0) >> _log2(rows)
    return lane_head == row_head


def _decode_scores(r, q):
    n_seq = r.ckt.shape[0]
    rows = r.xq.shape[0] // n_seq
    own_head = _decode_own_head(r)
    scores = []
    for i in range(n_seq):
        qs = jnp.concatenate([q[i * rows:(i + 1) * rows, :]] * X_HEADS, axis=0)
        qm = jnp.where(own_head, qs, 0.0).astype(BF16)
        scores.append(_dot(qm, r.ckt[i].astype(BF16)))
    return scores


def _decode_values(r, probs, step):
    n_seq = r.cvt.shape[0]
    rows = r.xq.shape[0] // n_seq
    own_head = _decode_own_head(r)
    for i in range(n_seq):
        of = _dot(probs[i], r.cvt[i].T.astype(BF16))
        of = jnp.where(own_head, of, 0.0)
        out = of[0:rows]
        for hd in range(1, X_HEADS):
            out = out + of[hd * rows:(hd + 1) * rows]
        row0 = pl.multiple_of((step * n_seq + i) * rows, rows)
        r.attn[pl.ds(row0, rows), :] = out


class _LayerRefs(NamedTuple):
    xp: object
    xs: object
    kt: object
    vb: object
    xq: object
    ckt: object
    cvt: object
    state: object
    g_mix: object
    w_in_f32: object
    g_v: object
    b_v: object
    wmix_p: object
    bias_p: object
    wmix_s: object
    bias_s: object
    wpool: object
    pool_scale: object
    w_out_a_f32: object
    w_out_b_f32: object
    w_out_c_f32: object
    w_o_f32: object
    g_ffn: object
    w_up_f32: object
    w_down_f32: object
    g_final: object
    yp: object
    ys: object
    cv_p: object
    ptail_p: object
    cv_s: object
    ptail_s: object
    ext_p: object
    pool_in: object
    pool_out: object
    x1: object
    h2: object
    w_in: object
    w_out_a: object
    w_out_b: object
    w_out_c: object
    w_o: object
    w_up: object
    w_down: object
    attn: object


def _layer_step(r, decode, j, step, n_prompt):
    t, d = r.x1.shape
    sgu_w = r.g_v.shape[-1]
    pool_w = r.pool_scale.shape[-1]
    x_w = r.attn.shape[-1]
    n_decode = r.attn.shape[0] // t
    head_dim = x_w // X_HEADS
    c_pool = 2 * sgu_w
    c_q = c_pool + pool_w
    c_gate = c_q + x_w

    ffn = _FfnSchedule(r)

    def gate(h, part):
        lo = c_gate + part * d
        return jax.nn.sigmoid(_dot(h, r.w_in[:, lo:lo + d]))

    ffn.at(0)
    x = r.xs[...] if decode else r.xp[0]
    h = _rmsnorm(x, r.g_mix[...]).astype(BF16)
    if decode:
        z = _dot(h, r.w_in[:, 0:c_q])
    else:
        h_dec = _rmsnorm(r.xq[...], r.g_mix[...]).astype(BF16)
        z = _dot(jnp.concatenate([h, h_dec], axis=0), r.w_in[:, 0:c_gate])
        dec_q = z[t:, c_q:c_gate] * (head_dim ** -0.5)
        z = z[0:t]
    u = z[:, 0:sgu_w]
    v = z[:, sgu_w:c_pool]
    p = z[:, c_pool:c_q]
    gate_a = gate(h, 0)
    ffn.at(1)

    u = jax.nn.gelu(u)
    vhat = _layernorm(jax.nn.gelu(v), r.g_v[...], r.b_v[...])
    if decode:
        rows = r.wmix_s.shape[1]
        r.cv_s[...] = vhat
        mixed = _chunk_mix(vhat.astype(BF16), r.wmix_s, rows)
        bias = _chunk_bias(r.bias_s, t)
    else:
        r.cv_p[0] = vhat[t - CHUNK:, :]
        mixed = _chunk_mix(vhat.astype(BF16), r.wmix_p, CHUNK)
        bias = _chunk_bias(r.bias_p, t)
    if not decode:
        q = z[:, c_q:c_gate] * (head_dim ** -0.5)
        dec_scores = _decode_scores(r, dec_q)
    ffn.at(2)
    a_in = (u * (mixed + bias)).astype(BF16)

    if decode:
        n_seq = t // rows
        lane_tiles = range(pool_w // V7X_LANES)
        for lt in lane_tiles:
            r.pool_in[lt] = p[:, lt * V7X_LANES:(lt + 1) * V7X_LANES]
        ext = [r.state[k] for k in range(POOL_STATE)]
        ext += [jnp.concatenate([r.pool_in[lt, pl.ds(i, n_seq, stride=rows), :]
                                 for lt in lane_tiles], axis=-1) for i in range(rows)]
        for i in range(rows):
            cur = POOL_STATE + i

            def load_shifted(k, lt, cur=cur):
                return ext[cur - k][:, lt * V7X_LANES:(lt + 1) * V7X_LANES]

            res = _window_means_minus_self(load_shifted, ext[cur], PAST_LEN + i)
            for lt in lane_tiles:
                r.pool_out[lt, pl.ds(i, n_seq, stride=rows), :] = (
                    res[:, lt * V7X_LANES:(lt + 1) * V7X_LANES])
        for k in range(POOL_STATE):
            r.ptail_s[k] = ext[rows + k]
        pooled = jnp.concatenate([r.pool_out[lt] for lt in lane_tiles], axis=-1)
    else:
        r.ext_p[POOL_PAD:POOL_PAD + t, :] = p

        def load_shifted(k, lt):
            return r.ext_p[POOL_PAD - k:POOL_PAD - k + t, lt * V7X_LANES:(lt + 1) * V7X_LANES]

        pos = j * t + lax.broadcasted_iota(jnp.int32, (t, V7X_LANES), 0)
        pooled = _window_means_minus_self(load_shifted, p, pos)
        tail = r.ext_p[t:t + POOL_PAD, :]
        r.ext_p[0:POOL_PAD, :] = tail
        r.ptail_p[0] = tail
    pooled = pooled.astype(BF16)

    a = _dot(a_in, r.w_out_a[...])

    if not decode:
        head = lax.broadcasted_iota(jnp.int32, (t, x_w), 1) >> _log2(head_dim)
        kt = r.kt[0]
        q_heads = jnp.concatenate(
            [jnp.where(head == hd, q, 0.0).astype(BF16) for hd in range(X_HEADS)], axis=0)
        scores = _dot(q_heads, kt)
    pool_mixed = _dot(pooled, r.wpool[...]) * r.pool_scale[...]
    ffn.at(3)
    if decode:
        row0 = pl.multiple_of(jnp.clip(step - n_prompt, 0, n_decode - 1) * t, t)
        o = r.attn[pl.ds(row0, t), :]
    else:
        dec_probs = [_softmax_bf16(s) for s in dec_scores]
        o_heads = _dot(_softmax_bf16(scores), r.vb[0])
        o = jnp.zeros((t, x_w), F32)
        for hd in range(X_HEADS):
            o = jnp.where(head == hd, o_heads[hd * t:(hd + 1) * t], o)
        _decode_values(r, dec_probs, step)
    b = _dot(pool_mixed.astype(BF16), r.w_out_b[...])
    gate_b = gate(h, 1)
    c = _dot(o.astype(BF16), r.w_out_c[...])
    ffn.at(4)
    gate_c = gate(h, 2)
    ffn.at(5)
    merged = (gate_a * a + gate_b * b + gate_c * c).astype(BF16)
    ffn.at(6)
    ffn.at(7)
    x2 = ffn.x2
    x1 = x + _dot(merged, r.w_o[...])
    r.x1[...] = x1
    r.h2[...] = _rmsnorm(x1, r.g_ffn[...]).astype(BF16)
    y = _rmsnorm(x2, r.g_final[...])
    if not decode:
        r.yp[0] = y
    else:
        @pl.when(step == n_prompt)
        def _():
            r.yp[0] = y

        @pl.when(step != n_prompt)
        def _():
            r.ys[...] = y


def _ffn_up(r, s):
    slab = r.w_up.shape[1] // FFN_SPLIT
    up = _dot(r.h2[...], r.w_up[:, s * slab:(s + 1) * slab])
    return jnp.square(jnp.maximum(up, 0.0)).astype(BF16)


def _ffn_down(r, act, s):
    slab = r.w_up.shape[1] // FFN_SPLIT
    return _dot(act, r.w_down[s * slab:(s + 1) * slab, :])


class _FfnSchedule:
    def __init__(self, r):
        self.r = r
        self.acts = {}
        self.x2 = None

    def at(self, point):
        for kind, s in FFN_POINTS[point]:
            if kind == "up":
                self.acts[s] = _ffn_up(self.r, s)
            else:
                base = self.r.x1[...] if self.x2 is None else self.x2
                self.x2 = base + _ffn_down(self.r, self.acts.pop(s), s)


def _drain_step(r):
    x2 = r.x1[...]
    act = _ffn_up(r, 0)
    for s in range(FFN_SPLIT):
        nxt = _ffn_up(r, s + 1) if s + 1 < FFN_SPLIT else None
        x2 = x2 + _ffn_down(r, act, s)
        act = nxt
    r.ys[...] = _rmsnorm(x2, r.g_final[...])


def _load_weights_bf16(r):
    ahead = LOAD_SLOTS - 1
    narrow_cols = r.w_o_f32.shape[1]
    wide_cols = max(r.w_in_f32.shape[1], r.w_up_f32.shape[1])

    def load(stage_w, stage_n, sem_w, sem_n):
        def stream(src, dst):
            narrow = src.shape[1] == narrow_cols
            stage, sem = (stage_n, sem_n) if narrow else (stage_w, sem_w)
            rows = stage.shape[1]
            n_chunks = src.shape[0] // rows
            assert src.shape[0] % rows == 0, (src.shape, rows)

            def copy(i, slot):
                return pltpu.make_async_copy(src.at[pl.ds(i * rows, rows), :],
                                             stage.at[slot, :, 0:src.shape[1]], sem.at[slot])

            def prime():
                for i in range(min(ahead, n_chunks)):
                    copy(i, i % LOAD_SLOTS).start()

            def drain():
                def body(i, carry):
                    @pl.when(i + ahead < n_chunks)
                    def _():
                        copy(i + ahead, lax.rem(i + ahead, LOAD_SLOTS)).start()

                    slot = lax.rem(i, LOAD_SLOTS)
                    copy(i, slot).wait()
                    row0 = pl.multiple_of(i * rows, rows)
                    dst[pl.ds(row0, rows), :] = stage[slot, :, 0:src.shape[1]].astype(BF16)
                    return carry

                lax.fori_loop(0, n_chunks, body, 0)

            return prime, drain

        order = [(r.w_in_f32, r.w_in), (r.w_down_f32, r.w_down), (r.w_up_f32, r.w_up),
                 (r.w_o_f32, r.w_o), (r.w_out_a_f32, r.w_out_a), (r.w_out_b_f32, r.w_out_b),
                 (r.w_out_c_f32, r.w_out_c)]
        streams = [stream(src, dst) for src, dst in order]
        streams[0][0]()
        for k, (_, drain) in enumerate(streams):
            nxt_uses_other_ring = (k + 1 < len(order)
                                   and (order[k + 1][0].shape[1] == narrow_cols)
                                   != (order[k][0].shape[1] == narrow_cols))
            if nxt_uses_other_ring:
                streams[k + 1][0]()
            drain()
            if k + 1 < len(order) and not nxt_uses_other_ring:
                streams[k + 1][0]()

    pl.run_scoped(load,
                  pltpu.VMEM((LOAD_SLOTS, LOAD_WIDE_ROWS, wide_cols), F32),
                  pltpu.VMEM((LOAD_SLOTS, LOAD_NARROW_ROWS, narrow_cols), F32),
                  pltpu.SemaphoreType.DMA((LOAD_SLOTS,)),
                  pltpu.SemaphoreType.DMA((LOAD_SLOTS,)))


def _layer_kernel(tiles_per_seq, n_prompt, *refs):
    r = _LayerRefs(*refs)
    step = pl.program_id(0)
    last = pl.num_programs(0) - 1
    j = jnp.minimum(step, n_prompt - 1) & (tiles_per_seq - 1)

    @pl.when(step == 0)
    def _():
        r.x1[...] = jnp.zeros(r.x1.shape, F32)
        r.h2[...] = jnp.zeros(r.h2.shape, BF16)
        _load_weights_bf16(r)

    @pl.when(jnp.logical_and(step < n_prompt, j == 0))
    def _():
        r.ext_p[0:POOL_PAD, :] = jnp.zeros((POOL_PAD, r.ext_p.shape[1]), F32)

    @pl.when(step < n_prompt)
    def _():
        _layer_step(r, False, j, step, n_prompt)

    @pl.when(jnp.logical_and(step >= n_prompt, step < last))
    def _():
        _layer_step(r, True, j, step, n_prompt)

    @pl.when(step == last)
    def _():
        _drain_step(r)


def _mem_kv(mem, g_mem, w_kv):
    bsz, n_mem, d = mem.shape
    xw = w_kv.shape[-1] // 2
    sb = MEM_KV_SEQS
    assert bsz % sb == 0, (bsz, sb)
    return pl.pallas_call(
        _mem_kv_kernel,
        grid=(bsz // sb,),
        in_specs=[pl.BlockSpec((sb, n_mem, d), lambda i: (i, 0, 0)),
                  _const_spec((1, d)), _const_spec((d, 2 * xw))],
        out_specs=[pl.BlockSpec((sb, xw, n_mem), lambda i: (i, 0, 0)),
                   pl.BlockSpec((sb, xw, n_mem), lambda i: (i, 0, 0)),
                   pl.BlockSpec((sb, xw, n_mem), lambda i: (i, 0, 0)),
                   pl.BlockSpec((sb, n_mem, xw), lambda i: (i, 0, 0))],
        out_shape=[jax.ShapeDtypeStruct((bsz, xw, n_mem), F32),
                   jax.ShapeDtypeStruct((bsz, xw, n_mem), F32),
                   jax.ShapeDtypeStruct((bsz, xw, n_mem), BF16),
                   jax.ShapeDtypeStruct((bsz, n_mem, xw), BF16)],
        compiler_params=pltpu.CompilerParams(
            dimension_semantics=("arbitrary",),
            vmem_limit_bytes=VMEM_LIMIT_BYTES),
        name="mem_kv",
    )(mem, g_mem.reshape(1, d).astype(F32), w_kv)


def _spatial_operands(w_s, b_s, group_dim, chunk_rows, tile_rows):
    wmix = jnp.tile(w_s[:, :chunk_rows, :chunk_rows], (1, 1, tile_rows // chunk_rows))
    bias = jnp.repeat(b_s[:, :chunk_rows].T, group_dim, axis=1)
    return wmix.astype(F32), bias.astype(F32)


def _layer(x_prompt, x2d, ktb, vb, cache_kt, cache_vt, state, lw):
    (g_mix, w_in, g_v, b_v, w_s, b_s, w_pool, pool_scale,
     w_out_a, w_out_b, w_out_c, w_o, g_ffn, w_up, w_down, g_final) = lw
    bsz, seq, d = x_prompt.shape
    n_seq = state.shape[0]
    rows = x2d.shape[0] // n_seq
    t = TILE
    ts = t // rows
    tps = seq // t
    n_prompt = bsz * tps
    n_decode = n_seq // ts
    assert n_seq % n_prompt == 0, (n_seq, n_prompt)
    sps = n_seq // n_prompt
    sgu_w = g_v.shape[-1]
    pool_w = pool_scale.shape[-1]
    n_mem, xw = vb.shape[1:]
    group_dim = sgu_w // w_s.shape[0]
    row2 = lambda vec: vec.reshape(1, -1).astype(F32)
    wmix_p, bias_p = _spatial_operands(w_s, b_s, group_dim, CHUNK, CHUNK)
    wmix_s, bias_s = _spatial_operands(w_s, b_s, group_dim, rows, t)
    wpool = jax.scipy.linalg.block_diag(*[w_pool[i] for i in range(w_pool.shape[0])])
    state_rows = jnp.transpose(state, (1, 0, 2))
    big = [w_in, w_out_a, w_out_b, w_out_c, w_o, w_up, w_down]
    weights = [row2(g_mix), w_in, row2(g_v), row2(b_v), wmix_p, bias_p, wmix_s, bias_s,
               wpool.astype(BF16), row2(pool_scale), w_out_a, w_out_b, w_out_c, w_o,
               row2(g_ffn), w_up, w_down, row2(g_final)]

    def prompt_tile(m):
        return jnp.clip(m, 0, n_prompt - 1)

    def prompt_seq(m):
        return lax.shift_right_logical(prompt_tile(m), _log2(tps))

    def decode_tile(m):
        return jnp.clip(m - n_prompt, 0, n_decode - 1)

    yp, ys, cv_p, ptail_p, cv_s, ptail_s = pl.pallas_call(
        functools.partial(_layer_kernel, tps, n_prompt),
        grid=(n_prompt + n_decode + 1,),
        in_specs=[pl.BlockSpec((1, t, d), lambda s: (prompt_tile(s), 0, 0)),
                  pl.BlockSpec((t, d), lambda s: (decode_tile(s), 0)),
                  pl.BlockSpec((1, xw, n_mem), lambda s: (prompt_seq(s), 0, 0)),
                  pl.BlockSpec((1, n_mem, xw), lambda s: (prompt_seq(s), 0, 0)),
                  pl.BlockSpec((sps * rows, d), lambda s: (prompt_tile(s), 0)),
                  pl.BlockSpec((sps, xw, n_mem), lambda s: (prompt_tile(s), 0, 0)),
                  pl.BlockSpec((sps, xw, n_mem), lambda s: (prompt_tile(s), 0, 0)),
                  pl.BlockSpec((POOL_STATE, ts, pool_w), lambda s: (0, decode_tile(s), 0))]
                 + [pl.BlockSpec(memory_space=pl.ANY) if any(w is b for b in big)
                    else _const_spec(w.shape) for w in weights],
        out_specs=[pl.BlockSpec((1, t, d), lambda s: (prompt_tile(s - 1), 0, 0)),
                   pl.BlockSpec((t, d), lambda s: (decode_tile(s - 1), 0)),
                   pl.BlockSpec((1, CHUNK, sgu_w), lambda s: (prompt_seq(s), 0, 0)),
                   pl.BlockSpec((1, POOL_PAD, pool_w), lambda s: (prompt_seq(s), 0, 0)),
                   pl.BlockSpec((t, sgu_w), lambda s: (decode_tile(s), 0)),
                   pl.BlockSpec((POOL_STATE, ts, pool_w), lambda s: (0, decode_tile(s), 0))],
        out_shape=[jax.ShapeDtypeStruct((n_prompt, t, d), F32),
                   jax.ShapeDtypeStruct((n_seq * rows, d), F32),
                   jax.ShapeDtypeStruct((bsz, CHUNK, sgu_w), F32),
                   jax.ShapeDtypeStruct((bsz, POOL_PAD, pool_w), F32),
                   jax.ShapeDtypeStruct((n_seq * rows, sgu_w), F32),
                   jax.ShapeDtypeStruct((POOL_STATE, n_seq, pool_w), F32)],
        scratch_shapes=[pltpu.VMEM((POOL_PAD + t, pool_w), F32),
                        pltpu.VMEM((pool_w // V7X_LANES, t, V7X_LANES), F32),
                        pltpu.VMEM((pool_w // V7X_LANES, t, V7X_LANES), F32),
                        pltpu.VMEM((t, d), F32),
                        pltpu.VMEM((t, d), BF16)]
                       + [pltpu.VMEM(w.shape, BF16) for w in big]
                       + [pltpu.VMEM((n_seq * rows, xw), F32)],
        compiler_params=pltpu.CompilerParams(
            dimension_semantics=("arbitrary",),
            vmem_limit_bytes=VMEM_LIMIT_BYTES),
        name="layer",
    )(x_prompt.reshape(n_prompt, t, d), x2d, ktb, vb, x2d, cache_kt, cache_vt, state_rows, *weights)
    return (yp.reshape(bsz, seq, d), ys.reshape(n_seq, rows, d), cv_p, ptail_p[:, POOL_PAD - POOL_STATE:, :],
            cv_s.reshape(n_seq, rows, sgu_w), jnp.transpose(ptail_s, (1, 0, 2)))


def _feature_major(kv):
    bsz, n_mem, heads, head_dim = kv.shape
    return jnp.transpose(kv, (0, 2, 3, 1)).reshape(bsz, heads * head_dim, n_mem)


def _token_major(kvt, heads):
    bsz, xw, n_mem = kvt.shape
    return jnp.transpose(kvt.reshape(bsz, heads, xw // heads, n_mem), (0, 3, 1, 2))


def kernel(x_prompt, x_sample, mem_prompt, cache_mem_k, cache_mem_v, state_pool, g_mix, w_in, g_v, b_v, w_s, b_s, w_pool, pool_scale, g_mem, w_kv, w_out_a, w_out_b, w_out_c, w_o, g_ffn, w_up, w_down, g_final):
    depth = w_in.shape[0]
    assert depth == 1, "the final rmsnorm is fused into the (single) layer kernel"
    heads = cache_mem_k.shape[-2]
    n_seq, rows, d = x_sample.shape
    lw = (g_mix[0], w_in[0], g_v[0], b_v[0], w_s[0], b_s[0], w_pool[0], pool_scale[0],
          w_out_a[0], w_out_b[0], w_out_c[0], w_o[0], g_ffn[0], w_up[0], w_down[0], g_final)
    mem_kt, mem_vt, ktb, vb = _mem_kv(mem_prompt, g_mem[0], w_kv[0])
    x2d = x_sample.reshape(n_seq * rows, d)
    y_prompt, y_sample, cv_p, ptail_p, cv_s, ptail_s = _layer(
        x_prompt, x2d, ktb, vb, _feature_major(cache_mem_k[0]), _feature_major(cache_mem_v[0]),
        state_pool[0], lw)
    return (y_prompt, y_sample, _token_major(mem_kt, heads)[None], _token_major(mem_vt, heads)[None],
            ptail_p[None], ptail_s[None], cv_p[None], cv_s[None])
```

```python
import functools
from typing import NamedTuple

import jax
import jax.numpy as jnp
from jax import lax
from jax.experimental import pallas as pl
from jax.experimental.pallas import tpu as pltpu

PAST_LEN = 16384
CHUNK = 128
POOL_WINDOWS = (2, 4, 8, 16)
POOL_STATE = max(POOL_WINDOWS) - 1
X_HEADS = 4
EPS = 1e-6

V7X_VMEM_BYTES = 64 * 1024 * 1024
V7X_LANES = 128
VMEM_RESERVE_BYTES = 6 * 1024 * 1024
VMEM_LIMIT_BYTES = V7X_VMEM_BYTES - VMEM_RESERVE_BYTES

TILE = 256
MEM_KV_SEQS = 4
FFN_SPLIT = 4
FFN_POINTS = ((("up", 0),), (("down", 0),), (("up", 1),), (("down", 1),),
              (("up", 2),), (("down", 2),), (("up", 3),), (("down", 3),))
LOAD_SLOTS = 8
LOAD_WIDE_ROWS = 32
LOAD_NARROW_ROWS = 128
POOL_PAD = POOL_STATE + 1

BF16 = jnp.bfloat16
F32 = jnp.float32


def _dot(a, b):
    return jnp.dot(a, b, preferred_element_type=F32)


def _log2(n):
    assert n > 0 and n & (n - 1) == 0, n
    return n.bit_length() - 1


def _rmsnorm(x, g):
    return x * lax.rsqrt(jnp.mean(x * x, axis=-1, keepdims=True) + EPS) * g


def _layernorm(x, g, b):
    xc = x - jnp.mean(x, axis=-1, keepdims=True)
    return xc * lax.rsqrt(jnp.mean(xc * xc, axis=-1, keepdims=True) + EPS) * g + b


def _softmax_bf16(s):
    e = jnp.exp(s - jnp.max(s, axis=-1, keepdims=True))
    return (e * (1.0 / jnp.sum(e, axis=-1, keepdims=True))).astype(BF16)


def _const_spec(shape):
    zeros = (0,) * len(shape)
    return pl.BlockSpec(shape, lambda *_: zeros, pipeline_mode=pl.Buffered(1))


def _chunk_mix(vhat_b, wmix_ref, chunk_rows):
    t, sgu_w = vhat_b.shape
    n_groups = wmix_ref.shape[0]
    group_dim = sgu_w // n_groups
    shift = _log2(chunk_rows)
    reps = t // chunk_rows
    row = lax.broadcasted_iota(jnp.int32, (t, t), 0)
    col = lax.broadcasted_iota(jnp.int32, (t, t), 1)
    same_chunk = (row >> shift) == (col >> shift)
    causal = col <= row
    parts = []
    for g in range(n_groups):
        w = jnp.concatenate([wmix_ref[g]] * (t // wmix_ref.shape[2]), axis=1)
        w = jnp.concatenate([w] * reps, axis=0)
        w = jnp.where(same_chunk, jnp.where(causal, w, 0.0), 0.0).astype(BF16)
        parts.append(_dot(w, vhat_b[:, g * group_dim:(g + 1) * group_dim]))
    return jnp.concatenate(parts, axis=1)


def _chunk_bias(bias_ref, t):
    return jnp.concatenate([bias_ref[...]] * (t // bias_ref.shape[0]), axis=0)


def _window_means_minus_self(load_shifted, p, pos):
    group_lanes = p.shape[-1] // len(POOL_WINDOWS)
    outs = []
    for lt in range(p.shape[-1] // V7X_LANES):
        p_lt = p[..., lt * V7X_LANES:(lt + 1) * V7X_LANES]
        lane = lax.broadcasted_iota(jnp.int32, p_lt.shape, p_lt.ndim - 1)
        wins = POOL_WINDOWS[lt * V7X_LANES // group_lanes:(lt + 1) * V7X_LANES // group_lanes]
        acc = p_lt
        sums = {1: p_lt}
        for k in range(1, max(wins)):
            acc = acc + load_shifted(k, lt)
            sums[k + 1] = acc
        win_sum = sums[wins[-1]]
        win = jnp.full(p_lt.shape, wins[-1], jnp.int32)
        for gi in range(len(wins) - 2, -1, -1):
            in_group = lane < (gi + 1) * group_lanes
            win_sum = jnp.where(in_group, sums[wins[gi]], win_sum)
            win = jnp.where(in_group, wins[gi], win)
        cnt = jnp.minimum(pos + 1, win).astype(F32)
        outs.append(win_sum / cnt - p_lt)
    return jnp.concatenate(outs, axis=-1)


def _mem_kv_kernel(mem_ref, g_mem_ref, w_kv_ref, kt_ref, vt_ref, ktb_ref, vb_ref):
    xw = kt_ref.shape[1]
    w_kv = w_kv_ref[...].astype(BF16)
    for i in range(mem_ref.shape[0]):
        hn = _rmsnorm(mem_ref[i], g_mem_ref[...]).astype(BF16)
        kv = _dot(hn, w_kv)
        kt = kv[:, 0:xw].T
        v = kv[:, xw:2 * xw]
        kt_ref[i] = kt
        vt_ref[i] = v.T
        ktb_ref[i] = kt.astype(BF16)
        vb_ref[i] = v.astype(BF16)


def _decode_own_head(r):
    n_seq, x_w, _ = r.ckt.shape
    rows = r.xq.shape[0] // n_seq
    stack = X_HEADS * rows
    lane_head = lax.broadcasted_iota(jnp.int32, (stack, x_w), 1) >> _log2(x_w // X_HEADS)
    row_head = lax.broadcasted_iota(jnp.int32, (stack, x_w), 0) >> _log2(rows)
    return lane_head == row_head


def _decode_scores(r, q):
    n_seq = r.ckt.shape[0]
    rows = r.xq.shape[0] // n_seq
    own_head = _decode_own_head(r)
    scores = []
    for i in range(n_seq):
        qs = jnp.concatenate([q[i * rows:(i + 1) * rows, :]] * X_HEADS, axis=0)
        qm = jnp.where(own_head, qs, 0.0).astype(BF16)
        scores.append(_dot(qm, r.ckt[i].astype(BF16)))
    return scores


def _decode_values(r, probs, step):
    n_seq = r.cvt.shape[0]
    rows = r.xq.shape[0] // n_seq
    own_head = _decode_own_head(r)
    for i in range(n_seq):
        of = _dot(probs[i], r.cvt[i].T.astype(BF16))
        of = jnp.where(own_head, of, 0.0)
        out = of[0:rows]
        for hd in range(1, X_HEADS):
            out = out + of[hd * rows:(hd + 1) * rows]
        row0 = pl.multiple_of((step * n_seq + i) * rows, rows)
        r.attn[pl.ds(row0, rows), :] = out


class _LayerRefs(NamedTuple):
    xp: object
    xs: object
    kt: object
    vb: object
    xq: object
    ckt: object
    cvt: object
    state: object
    g_mix: object
    w_in_f32: object
    g_v: object
    b_v: object
    wmix_p: object
    bias_p: object
    wmix_s: object
    bias_s: object
    wpool: object
    pool_scale: object
    w_out_a_f32: object
    w_out_b_f32: object
    w_out_c_f32: object
    w_o_f32: object
    g_ffn: object
    w_up_f32: object
    w_down_f32: object
    g_final: object
    yp: object
    ys: object
    cv_p: object
    ptail_p: object
    cv_s: object
    ptail_s: object
    ext_p: object
    pool_in: object
    pool_out: object
    x1: object
    h2: object
    w_in: object
    w_out_a: object
    w_out_b: object
    w_out_c: object
    w_o: object
    w_up: object
    w_down: object
    attn: object


def _layer_step(r, decode, j, step, n_prompt):
    t, d = r.x1.shape
    sgu_w = r.g_v.shape[-1]
    pool_w = r.pool_scale.shape[-1]
    x_w = r.attn.shape[-1]
    n_decode = r.attn.shape[0] // t
    head_dim = x_w // X_HEADS
    c_pool = 2 * sgu_w
    c_q = c_pool + pool_w
    c_gate = c_q + x_w

    ffn = _FfnSchedule(r)

    def gate(h, part):
        lo = c_gate + part * d
        return jax.nn.sigmoid(_dot(h, r.w_in[:, lo:lo + d]))

    ffn.at(0)
    x = r.xs[...] if decode else r.xp[0]
    h = _rmsnorm(x, r.g_mix[...]).astype(BF16)
    if decode:
        z = _dot(h, r.w_in[:, 0:c_q])
    else:
        h_dec = _rmsnorm(r.xq[...], r.g_mix[...]).astype(BF16)
        z = _dot(jnp.concatenate([h, h_dec], axis=0), r.w_in[:, 0:c_gate])
        dec_q = z[t:, c_q:c_gate] * (head_dim ** -0.5)
        z = z[0:t]
    u = z[:, 0:sgu_w]
    v = z[:, sgu_w:c_pool]
    p = z[:, c_pool:c_q]
    gate_a = gate(h, 0)
    ffn.at(1)

    u = jax.nn.gelu(u)
    vhat = _layernorm(jax.nn.gelu(v), r.g_v[...], r.b_v[...])
    if decode:
        rows = r.wmix_s.shape[1]
        r.cv_s[...] = vhat
        mixed = _chunk_mix(vhat.astype(BF16), r.wmix_s, rows)
        bias = _chunk_bias(r.bias_s, t)
    else:
        r.cv_p[0] = vhat[t - CHUNK:, :]
        mixed = _chunk_mix(vhat.astype(BF16), r.wmix_p, CHUNK)
        bias = _chunk_bias(r.bias_p, t)
    if not decode:
        q = z[:, c_q:c_gate] * (head_dim ** -0.5)
        dec_scores = _decode_scores(r, dec_q)
    ffn.at(2)
    a_in = (u * (mixed + bias)).astype(BF16)

    if decode:
        n_seq = t // rows
        lane_tiles = range(pool_w // V7X_LANES)
        for lt in lane_tiles:
            r.pool_in[lt] = p[:, lt * V7X_LANES:(lt + 1) * V7X_LANES]
        ext = [r.state[k] for k in range(POOL_STATE)]
        ext += [jnp.concatenate([r.pool_in[lt, pl.ds(i, n_seq, stride=rows), :]
                                 for lt in lane_tiles], axis=-1) for i in range(rows)]
        for i in range(rows):
            cur = POOL_STATE + i

            def load_shifted(k, lt, cur=cur):
                return ext[cur - k][:, lt * V7X_LANES:(lt + 1) * V7X_LANES]

            res = _window_means_minus_self(load_shifted, ext[cur], PAST_LEN + i)
            for lt in lane_tiles:
                r.pool_out[lt, pl.ds(i, n_seq, stride=rows), :] = (
                    res[:, lt * V7X_LANES:(lt + 1) * V7X_LANES])
        for k in range(POOL_STATE):
            r.ptail_s[k] = ext[rows + k]
        pooled = jnp.concatenate([r.pool_out[lt] for lt in lane_tiles], axis=-1)
    else:
        r.ext_p[POOL_PAD:POOL_PAD + t, :] = p

        def load_shifted(k, lt):
            return r.ext_p[POOL_PAD - k:POOL_PAD - k + t, lt * V7X_LANES:(lt + 1) * V7X_LANES]

        pos = j * t + lax.broadcasted_iota(jnp.int32, (t, V7X_LANES), 0)
        pooled = _window_means_minus_self(load_shifted, p, pos)
        tail = r.ext_p[t:t + POOL_PAD, :]
        r.ext_p[0:POOL_PAD, :] = tail
        r.ptail_p[0] = tail[POOL_PAD - POOL_STATE:]
    pooled = pooled.astype(BF16)

    a = _dot(a_in, r.w_out_a[...])

    if not decode:
        head = lax.broadcasted_iota(jnp.int32, (t, x_w), 1) >> _log2(head_dim)
        kt = r.kt[0]
        q_heads = jnp.concatenate(
            [jnp.where(head == hd, q, 0.0).astype(BF16) for hd in range(X_HEADS)], axis=0)
        scores = _dot(q_heads, kt)
    pool_mixed = _dot(pooled, r.wpool[...].astype(BF16)) * r.pool_scale[...]
    ffn.at(3)
    if decode:
        row0 = pl.multiple_of(jnp.clip(step - n_prompt, 0, n_decode - 1) * t, t)
        o = r.attn[pl.ds(row0, t), :]
    else:
        dec_probs = [_softmax_bf16(s) for s in dec_scores]
        o_heads = _dot(_softmax_bf16(scores), r.vb[0])
        o = jnp.zeros((t, x_w), F32)
        for hd in range(X_HEADS):
            o = jnp.where(head == hd, o_heads[hd * t:(hd + 1) * t], o)
        _decode_values(r, dec_probs, step)
    b = _dot(pool_mixed.astype(BF16), r.w_out_b[...])
    gate_b = gate(h, 1)
    c = _dot(o.astype(BF16), r.w_out_c[...])
    ffn.at(4)
    gate_c = gate(h, 2)
    ffn.at(5)
    merged = (gate_a * a + gate_b * b + gate_c * c).astype(BF16)
    ffn.at(6)
    ffn.at(7)
    x2 = ffn.x2
    x1 = x + _dot(merged, r.w_o[...])
    r.x1[...] = x1
    r.h2[...] = _rmsnorm(x1, r.g_ffn[...]).astype(BF16)
    y = _rmsnorm(x2, r.g_final[...])
    if not decode:
        r.yp[0] = y
    else:
        @pl.when(step == n_prompt)
        def _():
            r.yp[0] = y

        @pl.when(step != n_prompt)
        def _():
            r.ys[...] = y


def _ffn_up(r, s):
    slab = r.w_up.shape[1] // FFN_SPLIT
    up = _dot(r.h2[...], r.w_up[:, s * slab:(s + 1) * slab])
    return jnp.square(jnp.maximum(up, 0.0)).astype(BF16)


def _ffn_down(r, act, s):
    slab = r.w_up.shape[1] // FFN_SPLIT
    return _dot(act, r.w_down[s * slab:(s + 1) * slab, :])


class _FfnSchedule:
    def __init__(self, r):
        self.r = r
        self.acts = {}
        self.x2 = None

    def at(self, point):
        for kind, s in FFN_POINTS[point]:
            if kind == "up":
                self.acts[s] = _ffn_up(self.r, s)
            else:
                base = self.r.x1[...] if self.x2 is None else self.x2
                self.x2 = base + _ffn_down(self.r, self.acts.pop(s), s)


def _drain_step(r):
    x2 = r.x1[...]
    act = _ffn_up(r, 0)
    for s in range(FFN_SPLIT):
        nxt = _ffn_up(r, s + 1) if s + 1 < FFN_SPLIT else None
        x2 = x2 + _ffn_down(r, act, s)
        act = nxt
    r.ys[...] = _rmsnorm(x2, r.g_final[...])


def _load_weights_bf16(r):
    ahead = LOAD_SLOTS - 1
    narrow_cols = r.w_o_f32.shape[1]
    wide_cols = max(r.w_in_f32.shape[1], r.w_up_f32.shape[1])

    def load(stage_w, stage_n, sem_w, sem_n):
        def stream(src, dst):
            narrow = src.shape[1] == narrow_cols
            stage, sem = (stage_n, sem_n) if narrow else (stage_w, sem_w)
            rows = stage.shape[1]
            n_chunks = src.shape[0] // rows
            assert src.shape[0] % rows == 0, (src.shape, rows)

            def copy(i, slot):
                return pltpu.make_async_copy(src.at[pl.ds(i * rows, rows), :],
                                             stage.at[slot, :, 0:src.shape[1]], sem.at[slot])

            def prime():
                for i in range(min(ahead, n_chunks)):
                    copy(i, i % LOAD_SLOTS).start()

            def drain():
                def body(i, carry):
                    @pl.when(i + ahead < n_chunks)
                    def _():
                        copy(i + ahead, lax.rem(i + ahead, LOAD_SLOTS)).start()

                    slot = lax.rem(i, LOAD_SLOTS)
                    copy(i, slot).wait()
                    row0 = pl.multiple_of(i * rows, rows)
                    dst[pl.ds(row0, rows), :] = stage[slot, :, 0:src.shape[1]].astype(BF16)
                    return carry

                lax.fori_loop(0, n_chunks, body, 0)

            return prime, drain

        order = [(r.w_in_f32, r.w_in), (r.w_down_f32, r.w_down), (r.w_up_f32, r.w_up),
                 (r.w_o_f32, r.w_o), (r.w_out_a_f32, r.w_out_a), (r.w_out_b_f32, r.w_out_b),
                 (r.w_out_c_f32, r.w_out_c)]
        streams = [stream(src, dst) for src, dst in order]
        streams[0][0]()
        for k, (_, drain) in enumerate(streams):
            nxt_uses_other_ring = (k + 1 < len(order)
                                   and (order[k + 1][0].shape[1] == narrow_cols)
                                   != (order[k][0].shape[1] == narrow_cols))
            if nxt_uses_other_ring:
                streams[k + 1][0]()
            drain()
            if k + 1 < len(order) and not nxt_uses_other_ring:
                streams[k + 1][0]()

    pl.run_scoped(load,
                  pltpu.VMEM((LOAD_SLOTS, LOAD_WIDE_ROWS, wide_cols), F32),
                  pltpu.VMEM((LOAD_SLOTS, LOAD_NARROW_ROWS, narrow_cols), F32),
                  pltpu.SemaphoreType.DMA((LOAD_SLOTS,)),
                  pltpu.SemaphoreType.DMA((LOAD_SLOTS,)))


def _layer_kernel(tiles_per_seq, n_prompt, *refs):
    r = _LayerRefs(*refs)
    step = pl.program_id(0)
    last = pl.num_programs(0) - 1
    j = jnp.minimum(step, n_prompt - 1) & (tiles_per_seq - 1)

    @pl.when(step == 0)
    def _():
        r.x1[...] = jnp.zeros(r.x1.shape, F32)
        r.h2[...] = jnp.zeros(r.h2.shape, BF16)
        _load_weights_bf16(r)

    @pl.when(jnp.logical_and(step < n_prompt, j == 0))
    def _():
        r.ext_p[0:POOL_PAD, :] = jnp.zeros((POOL_PAD, r.ext_p.shape[1]), F32)

    @pl.when(step < n_prompt)
    def _():
        _layer_step(r, False, j, step, n_prompt)

    @pl.when(jnp.logical_and(step >= n_prompt, step < last))
    def _():
        _layer_step(r, True, j, step, n_prompt)

    @pl.when(step == last)
    def _():
        _drain_step(r)


def _mem_kv(mem, g_mem, w_kv):
    bsz, n_mem, d = mem.shape
    xw = w_kv.shape[-1] // 2
    sb = MEM_KV_SEQS
    assert bsz % sb == 0, (bsz, sb)
    return pl.pallas_call(
        _mem_kv_kernel,
        grid=(bsz // sb,),
        in_specs=[pl.BlockSpec((sb, n_mem, d), lambda i: (i, 0, 0)),
                  _const_spec((1, d)), _const_spec((d, 2 * xw))],
        out_specs=[pl.BlockSpec((sb, xw, n_mem), lambda i: (i, 0, 0)),
                   pl.BlockSpec((sb, xw, n_mem), lambda i: (i, 0, 0)),
                   pl.BlockSpec((sb, xw, n_mem), lambda i: (i, 0, 0)),
                   pl.BlockSpec((sb, n_mem, xw), lambda i: (i, 0, 0))],
        out_shape=[jax.ShapeDtypeStruct((bsz, xw, n_mem), F32),
                   jax.ShapeDtypeStruct((bsz, xw, n_mem), F32),
                   jax.ShapeDtypeStruct((bsz, xw, n_mem), BF16),
                   jax.ShapeDtypeStruct((bsz, n_mem, xw), BF16)],
        compiler_params=pltpu.CompilerParams(
            dimension_semantics=("arbitrary",),
            vmem_limit_bytes=VMEM_LIMIT_BYTES),
        name="mem_kv",
    )(mem, g_mem.reshape(1, d).astype(F32), w_kv)


def _spatial_operands(w_s, b_s, group_dim, chunk_rows, tile_rows):
    wmix = jnp.tile(w_s[:, :chunk_rows, :chunk_rows], (1, 1, tile_rows // chunk_rows))
    bias = jnp.repeat(b_s[:, :chunk_rows].T, group_dim, axis=1)
    return wmix.astype(F32), bias.astype(F32)


def _layer(x_prompt, x2d, ktb, vb, cache_kt, cache_vt, state, lw):
    (g_mix, w_in, g_v, b_v, w_s, b_s, w_pool, pool_scale,
     w_out_a, w_out_b, w_out_c, w_o, g_ffn, w_up, w_down, g_final) = lw
    bsz, seq, d = x_prompt.shape
    n_seq = state.shape[0]
    rows = x2d.shape[0] // n_seq
    t = TILE
    ts = t // rows
    tps = seq // t
    n_prompt = bsz * tps
    n_decode = n_seq // ts
    assert n_seq % n_prompt == 0, (n_seq, n_prompt)
    sps = n_seq // n_prompt
    sgu_w = g_v.shape[-1]
    pool_w = pool_scale.shape[-1]
    n_mem, xw = vb.shape[1:]
    group_dim = sgu_w // w_s.shape[0]
    row2 = lambda vec: vec.reshape(1, -1).astype(F32)
    wmix_p, bias_p = _spatial_operands(w_s, b_s, group_dim, CHUNK, CHUNK)
    wmix_s, bias_s = _spatial_operands(w_s, b_s, group_dim, rows, t)
    wpool = jax.scipy.linalg.block_diag(*[w_pool[i] for i in range(w_pool.shape[0])])
    state_rows = jnp.transpose(state, (1, 0, 2))
    big = [w_in, w_out_a, w_out_b, w_out_c, w_o, w_up, w_down]
    weights = [row2(g_mix), w_in, row2(g_v), row2(b_v), wmix_p, bias_p, wmix_s, bias_s,
               wpool.astype(F32), row2(pool_scale), w_out_a, w_out_b, w_out_c, w_o,
               row2(g_ffn), w_up, w_down, row2(g_final)]

    def prompt_tile(m):
        return jnp.clip(m, 0, n_prompt - 1)

    def prompt_seq(m):
        return lax.shift_right_logical(prompt_tile(m), _log2(tps))

    def decode_tile(m):
        return jnp.clip(m - n_prompt, 0, n_decode - 1)

    yp, ys, cv_p, ptail_p, cv_s, ptail_s = pl.pallas_call(
        functools.partial(_layer_kernel, tps, n_prompt),
        grid=(n_prompt + n_decode + 1,),
        in_specs=[pl.BlockSpec((1, t, d), lambda s: (prompt_tile(s), 0, 0)),
                  pl.BlockSpec((t, d), lambda s: (decode_tile(s), 0)),
                  pl.BlockSpec((1, xw, n_mem), lambda s: (prompt_seq(s), 0, 0)),
                  pl.BlockSpec((1, n_mem, xw), lambda s: (prompt_seq(s), 0, 0)),
                  pl.BlockSpec((sps * rows, d), lambda s: (prompt_tile(s), 0)),
                  pl.BlockSpec((sps, xw, n_mem), lambda s: (prompt_tile(s), 0, 0)),
                  pl.BlockSpec((sps, xw, n_mem), lambda s: (prompt_tile(s), 0, 0)),
                  pl.BlockSpec((POOL_STATE, ts, pool_w), lambda s: (0, decode_tile(s), 0))]
                 + [pl.BlockSpec(memory_space=pl.ANY) if any(w is b for b in big)
                    else _const_spec(w.shape) for w in weights],
        out_specs=[pl.BlockSpec((1, t, d), lambda s: (prompt_tile(s - 1), 0, 0)),
                   pl.BlockSpec((t, d), lambda s: (decode_tile(s - 1), 0)),
                   pl.BlockSpec((1, CHUNK, sgu_w), lambda s: (prompt_seq(s), 0, 0)),
                   pl.BlockSpec((1, POOL_STATE, pool_w), lambda s: (prompt_seq(s), 0, 0)),
                   pl.BlockSpec((t, sgu_w), lambda s: (decode_tile(s), 0)),
                   pl.BlockSpec((POOL_STATE, ts, pool_w), lambda s: (0, decode_tile(s), 0))],
        out_shape=[jax.ShapeDtypeStruct((n_prompt, t, d), F32),
                   jax.ShapeDtypeStruct((n_seq * rows, d), F32),
                   jax.ShapeDtypeStruct((bsz, CHUNK, sgu_w), F32),
                   jax.ShapeDtypeStruct((bsz, POOL_STATE, pool_w), F32),
                   jax.ShapeDtypeStruct((n_seq * rows, sgu_w), F32),
                   jax.ShapeDtypeStruct((POOL_STATE, n_seq, pool_w), F32)],
        scratch_shapes=[pltpu.VMEM((POOL_PAD + t, pool_w), F32),
                        pltpu.VMEM((pool_w // V7X_LANES, t, V7X_LANES), F32),
                        pltpu.VMEM((pool_w // V7X_LANES, t, V7X_LANES), F32),
                        pltpu.VMEM((t, d), F32),
                        pltpu.VMEM((t, d), BF16)]
                       + [pltpu.VMEM(w.shape, BF16) for w in big]
                       + [pltpu.VMEM((n_seq * rows, xw), F32)],
        compiler_params=pltpu.CompilerParams(
            dimension_semantics=("arbitrary",),
            vmem_limit_bytes=VMEM_LIMIT_BYTES),
        name="layer",
    )(x_prompt.reshape(n_prompt, t, d), x2d, ktb, vb, x2d, cache_kt, cache_vt, state_rows, *weights)
    return (yp.reshape(bsz, seq, d), ys.reshape(n_seq, rows, d), cv_p, ptail_p,
            cv_s.reshape(n_seq, rows, sgu_w), jnp.transpose(ptail_s, (1, 0, 2)))


def _feature_major(kv):
    bsz, n_mem, heads, head_dim = kv.shape
    return jnp.transpose(kv, (0, 2, 3, 1)).reshape(bsz, heads * head_dim, n_mem)


def _token_major(kvt, heads):
    bsz, xw, n_mem = kvt.shape
    return jnp.transpose(kvt.reshape(bsz, heads, xw // heads, n_mem), (0, 3, 1, 2))


def kernel(x_prompt, x_sample, mem_prompt, cache_mem_k, cache_mem_v, state_pool, g_mix, w_in, g_v, b_v, w_s, b_s, w_pool, pool_scale, g_mem, w_kv, w_out_a, w_out_b, w_out_c, w_o, g_ffn, w_up, w_down, g_final):
    depth = w_in.shape[0]
    assert depth == 1, "the final rmsnorm is fused into the (single) layer kernel"
    heads = cache_mem_k.shape[-2]
    n_seq, rows, d = x_sample.shape
    lw = (g_mix[0], w_in[0], g_v[0], b_v[0], w_s[0], b_s[0], w_pool[0], pool_scale[0],
          w_out_a[0], w_out_b[0], w_out_c[0], w_o[0], g_ffn[0], w_up[0], w_down[0], g_final)
    mem_kt, mem_vt, ktb, vb = _mem_kv(mem_prompt, g_mem[0], w_kv[0])
    x2d = x_sample.reshape(n_seq * rows, d)
    y_prompt, y_sample, cv_p, ptail_p, cv_s, ptail_s = _layer(
        x_prompt, x2d, ktb, vb, _feature_major(cache_mem_k[0]), _feature_major(cache_mem_v[0]),
        state_pool[0], lw)
    return (y_prompt, y_sample, _token_major(mem_kt, heads)[None], _token_major(mem_vt, heads)[None],
            ptail_p[None], ptail_s[None], cv_p[None], cv_s[None])
```

```python
import functools
from typing import NamedTuple

import jax
import jax.numpy as jnp
from jax import lax
from jax.experimental import pallas as pl
from jax.experimental.pallas import tpu as pltpu

PAST_LEN = 16384
CHUNK = 128
POOL_WINDOWS = (2, 4, 8, 16)
POOL_STATE = max(POOL_WINDOWS) - 1
X_HEADS = 4
EPS = 1e-6

V7X_VMEM_BYTES = 64 * 1024 * 1024
V7X_LANES = 128
VMEM_RESERVE_BYTES = 6 * 1024 * 1024
VMEM_LIMIT_BYTES = V7X_VMEM_BYTES - VMEM_RESERVE_BYTES

TILE = 256
MEM_KV_SEQS = 4
FFN_SPLIT = 4
FFN_POINTS = ((("up", 0),), (("down", 0),), (("up", 1),), (("down", 1),),
              (("up", 2),), (("down", 2),), (("up", 3),), (("down", 3),))
LOAD_SLOTS = 8
LOAD_WIDE_ROWS = 32
LOAD_NARROW_ROWS = 128
POOL_PAD = POOL_STATE + 1

BF16 = jnp.bfloat16
F32 = jnp.float32


def _dot(a, b):
    return jnp.dot(a, b, preferred_element_type=F32)


def _log2(n):
    assert n > 0 and n & (n - 1) == 0, n
    return n.bit_length() - 1


def _rmsnorm(x, g):
    return x * lax.rsqrt(jnp.mean(x * x, axis=-1, keepdims=True) + EPS) * g


def _layernorm(x, g, b):
    xc = x - jnp.mean(x, axis=-1, keepdims=True)
    return xc * lax.rsqrt(jnp.mean(xc * xc, axis=-1, keepdims=True) + EPS) * g + b


def _softmax_bf16(s):
    e = jnp.exp(s - jnp.max(s, axis=-1, keepdims=True))
    return (e * (1.0 / jnp.sum(e, axis=-1, keepdims=True))).astype(BF16)


def _const_spec(shape):
    zeros = (0,) * len(shape)
    return pl.BlockSpec(shape, lambda *_: zeros, pipeline_mode=pl.Buffered(1))


def _chunk_mix(vhat_b, wmix_ref, chunk_rows):
    t, sgu_w = vhat_b.shape
    n_groups = wmix_ref.shape[0]
    group_dim = sgu_w // n_groups
    shift = _log2(chunk_rows)
    reps = t // chunk_rows
    row = lax.broadcasted_iota(jnp.int32, (t, t), 0)
    col = lax.broadcasted_iota(jnp.int32, (t, t), 1)
    same_chunk = (row >> shift) == (col >> shift)
    causal = col <= row
    parts = []
    for g in range(n_groups):
        w = jnp.concatenate([wmix_ref[g]] * (t // wmix_ref.shape[2]), axis=1)
        w = jnp.concatenate([w] * reps, axis=0)
        w = jnp.where(same_chunk, jnp.where(causal, w, 0.0), 0.0).astype(BF16)
        parts.append(_dot(w, vhat_b[:, g * group_dim:(g + 1) * group_dim]))
    return jnp.concatenate(parts, axis=1)


def _chunk_bias(bias_ref, t):
    return jnp.concatenate([bias_ref[...]] * (t // bias_ref.shape[0]), axis=0)


def _window_means_minus_self(load_shifted, p, pos):
    group_lanes = p.shape[-1] // len(POOL_WINDOWS)
    outs = []
    for lt in range(p.shape[-1] // V7X_LANES):
        p_lt = p[..., lt * V7X_LANES:(lt + 1) * V7X_LANES]
        lane = lax.broadcasted_iota(jnp.int32, p_lt.shape, p_lt.ndim - 1)
        wins = POOL_WINDOWS[lt * V7X_LANES // group_lanes:(lt + 1) * V7X_LANES // group_lanes]
        acc = p_lt
        sums = {1: p_lt}
        for k in range(1, max(wins)):
            acc = acc + load_shifted(k, lt)
            sums[k + 1] = acc
        win_sum = sums[wins[-1]]
        win = jnp.full(p_lt.shape, wins[-1], jnp.int32)
        for gi in range(len(wins) - 2, -1, -1):
            in_group = lane < (gi + 1) * group_lanes
            win_sum = jnp.where(in_group, sums[wins[gi]], win_sum)
            win = jnp.where(in_group, wins[gi], win)
        cnt = jnp.minimum(pos + 1, win).astype(F32)
        outs.append(win_sum / cnt - p_lt)
    return jnp.concatenate(outs, axis=-1)


def _mem_kv_kernel(mem_ref, g_mem_ref, w_kv_ref, kt_ref, vt_ref, ktb_ref, vb_ref):
    xw = kt_ref.shape[1]
    w_kv = w_kv_ref[...].astype(BF16)
    for i in range(mem_ref.shape[0]):
        hn = _rmsnorm(mem_ref[i], g_mem_ref[...]).astype(BF16)
        kv = _dot(hn, w_kv)
        kt = kv[:, 0:xw].T
        v = kv[:, xw:2 * xw]
        kt_ref[i] = kt
        vt_ref[i] = v.T
        ktb_ref[i] = kt.astype(BF16)
        vb_ref[i] = v.astype(BF16)


def _decode_own_head(r):
    n_seq, x_w, _ = r.ckt.shape
    rows = r.xq.shape[0] // n_seq
    stack = X_HEADS * rows
    lane_head = lax.broadcasted_iota(jnp.int32, (stack, x_w), 1) >> _log2(x_w // X_HEADS)
    row_head = lax.broadcasted_iota(jnp.int32, (stack, x_w), 0) >> _log2(rows)
    return lane_head == row_head


def _decode_scores(r, q):
    n_seq = r.ckt.shape[0]
    rows = r.xq.shape[0] // n_seq
    own_head = _decode_own_head(r)
    scores = []
    for i in range(n_seq):
        qs = jnp.concatenate([q[i * rows:(i + 1) * rows, :]] * X_HEADS, axis=0)
        qm = jnp.where(own_head, qs, 0.0).astype(BF16)
        scores.append(_dot(qm, r.ckt[i].astype(BF16)))
    return scores


def _decode_values(r, probs, step):
    n_seq = r.cvt.shape[0]
    rows = r.xq.shape[0] // n_seq
    own_head = _decode_own_head(r)
    for i in range(n_seq):
        of = _dot(probs[i], r.cvt[i].T.astype(BF16))
        of = jnp.where(own_head, of, 0.0)
        out = of[0:rows]
        for hd in range(1, X_HEADS):
            out = out + of[hd * rows:(hd + 1) * rows]
        row0 = pl.multiple_of((step * n_seq + i) * rows, rows)
        r.attn[pl.ds(row0, rows), :] = out


class _LayerRefs(NamedTuple):
    xp: object
    xs: object
    kt: object
    vb: object
    xq: object
    ckt: object
    cvt: object
    state: object
    g_mix: object
    w_in_f32: object
    g_v: object
    b_v: object
    wmix_p: object
    bias_p: object
    wmix_s: object
    bias_s: object
    wpool: object
    pool_scale: object
    w_out_a_f32: object
    w_out_b_f32: object
    w_out_c_f32: object
    w_o_f32: object
    g_ffn: object
    w_up_f32: object
    w_down_f32: object
    g_final: object
    yp: object
    ys: object
    cv_p: object
    ptail_p: object
    cv_s: object
    ptail_s: object
    ext_p: object
    pool_in: object
    pool_out: object
    x1: object
    h2: object
    w_in: object
    w_out_a: object
    w_out_b: object
    w_out_c: object
    w_o: object
    w_up: object
    w_down: object
    attn: object


def _layer_step(r, decode, j, step, n_prompt, first=False):
    t, d = r.x1.shape
    sgu_w = r.g_v.shape[-1]
    pool_w = r.pool_scale.shape[-1]
    x_w = r.attn.shape[-1]
    n_decode = r.attn.shape[0] // t
    head_dim = x_w // X_HEADS
    c_pool = 2 * sgu_w
    c_q = c_pool + pool_w
    c_gate = c_q + x_w

    ffn = _FfnSchedule(r, enabled=not first)

    def gate(h, part):
        lo = c_gate + part * d
        return jax.nn.sigmoid(_dot(h, r.w_in[:, lo:lo + d]))

    ffn.at(0)
    x = r.xs[...] if decode else r.xp[0]
    h = _rmsnorm(x, r.g_mix[...]).astype(BF16)
    if decode:
        z = _dot(h, r.w_in[:, 0:c_q])
    else:
        h_dec = _rmsnorm(r.xq[...], r.g_mix[...]).astype(BF16)
        z = _dot(jnp.concatenate([h, h_dec], axis=0), r.w_in[:, 0:c_gate])
        dec_q = z[t:, c_q:c_gate] * (head_dim ** -0.5)
        z = z[0:t]
    u = z[:, 0:sgu_w]
    v = z[:, sgu_w:c_pool]
    p = z[:, c_pool:c_q]
    gate_a = gate(h, 0)
    ffn.at(1)

    u = jax.nn.gelu(u)
    vhat = _layernorm(jax.nn.gelu(v), r.g_v[...], r.b_v[...])
    if decode:
        rows = r.wmix_s.shape[1]
        r.cv_s[...] = vhat
        mixed = _chunk_mix(vhat.astype(BF16), r.wmix_s, rows)
        bias = _chunk_bias(r.bias_s, t)
    else:
        r.cv_p[0] = vhat[t - CHUNK:, :]
        mixed = _chunk_mix(vhat.astype(BF16), r.wmix_p, CHUNK)
        bias = _chunk_bias(r.bias_p, t)
    if not decode:
        q = z[:, c_q:c_gate] * (head_dim ** -0.5)
        dec_scores = _decode_scores(r, dec_q)
    ffn.at(2)
    a_in = (u * (mixed + bias)).astype(BF16)

    if decode:
        n_seq = t // rows
        lane_tiles = range(pool_w // V7X_LANES)
        for lt in lane_tiles:
            r.pool_in[lt] = p[:, lt * V7X_LANES:(lt + 1) * V7X_LANES]
        ext = [r.state[k] for k in range(POOL_STATE)]
        ext += [jnp.concatenate([r.pool_in[lt, pl.ds(i, n_seq, stride=rows), :]
                                 for lt in lane_tiles], axis=-1) for i in range(rows)]
        for i in range(rows):
            cur = POOL_STATE + i

            def load_shifted(k, lt, cur=cur):
                return ext[cur - k][:, lt * V7X_LANES:(lt + 1) * V7X_LANES]

            res = _window_means_minus_self(load_shifted, ext[cur], PAST_LEN + i)
            for lt in lane_tiles:
                r.pool_out[lt, pl.ds(i, n_seq, stride=rows), :] = (
                    res[:, lt * V7X_LANES:(lt + 1) * V7X_LANES])
        for k in range(POOL_STATE):
            r.ptail_s[k] = ext[rows + k]
        pooled = jnp.concatenate([r.pool_out[lt] for lt in lane_tiles], axis=-1)
    else:
        r.ext_p[POOL_PAD:POOL_PAD + t, :] = p

        def load_shifted(k, lt):
            return r.ext_p[POOL_PAD - k:POOL_PAD - k + t, lt * V7X_LANES:(lt + 1) * V7X_LANES]

        pos = j * t + lax.broadcasted_iota(jnp.int32, (t, V7X_LANES), 0)
        pooled = _window_means_minus_self(load_shifted, p, pos)
        tail = r.ext_p[t:t + POOL_PAD, :]
        r.ext_p[0:POOL_PAD, :] = tail
        r.ptail_p[0] = tail
    pooled = pooled.astype(BF16)

    a = _dot(a_in, r.w_out_a[...])

    if not decode:
        head = lax.broadcasted_iota(jnp.int32, (t, x_w), 1) >> _log2(head_dim)
        kt = r.kt[0]
        q_heads = jnp.concatenate(
            [jnp.where(head == hd, q, 0.0).astype(BF16) for hd in range(X_HEADS)], axis=0)
        scores = _dot(q_heads, kt)
    pool_mixed = _dot(pooled, r.wpool[...]) * r.pool_scale[...]
    ffn.at(3)
    if decode:
        row0 = pl.multiple_of(jnp.clip(step - n_prompt, 0, n_decode - 1) * t, t)
        o = r.attn[pl.ds(row0, t), :]
    else:
        dec_probs = [_softmax_bf16(s) for s in dec_scores]
        o_heads = _dot(_softmax_bf16(scores), r.vb[0])
        o = jnp.zeros((t, x_w), F32)
        for hd in range(X_HEADS):
            o = jnp.where(head == hd, o_heads[hd * t:(hd + 1) * t], o)
        _decode_values(r, dec_probs, step)
    b = _dot(pool_mixed.astype(BF16), r.w_out_b[...])
    gate_b = gate(h, 1)
    c = _dot(o.astype(BF16), r.w_out_c[...])
    ffn.at(4)
    gate_c = gate(h, 2)
    ffn.at(5)
    merged = (gate_a * a + gate_b * b + gate_c * c).astype(BF16)
    ffn.at(6)
    ffn.at(7)
    x2 = ffn.x2
    x1 = x + _dot(merged, r.w_o[...])
    r.x1[...] = x1
    r.h2[...] = _rmsnorm(x1, r.g_ffn[...]).astype(BF16)
    if first:
        return
    y = _rmsnorm(x2, r.g_final[...])
    if not decode:
        r.yp[0] = y
    else:
        @pl.when(step == n_prompt)
        def _():
            r.yp[0] = y

        @pl.when(step != n_prompt)
        def _():
            r.ys[...] = y


def _ffn_up(r, s):
    slab = r.w_up.shape[1] // FFN_SPLIT
    up = _dot(r.h2[...], r.w_up[:, s * slab:(s + 1) * slab])
    return jnp.square(jnp.maximum(up, 0.0)).astype(BF16)


def _ffn_down(r, act, s):
    slab = r.w_up.shape[1] // FFN_SPLIT
    return _dot(act, r.w_down[s * slab:(s + 1) * slab, :])


class _FfnSchedule:
    def __init__(self, r, enabled=True):
        self.r = r
        self.acts = {}
        self.x2 = None
        self.points = FFN_POINTS if enabled else ((),) * len(FFN_POINTS)

    def at(self, point):
        for kind, s in self.points[point]:
            if kind == "up":
                self.acts[s] = _ffn_up(self.r, s)
            else:
                base = self.r.x1[...] if self.x2 is None else self.x2
                self.x2 = base + _ffn_down(self.r, self.acts.pop(s), s)


def _drain_step(r):
    x2 = r.x1[...]
    act = _ffn_up(r, 0)
    for s in range(FFN_SPLIT):
        nxt = _ffn_up(r, s + 1) if s + 1 < FFN_SPLIT else None
        x2 = x2 + _ffn_down(r, act, s)
        act = nxt
    r.ys[...] = _rmsnorm(x2, r.g_final[...])


def _load_weights_bf16(r):
    ahead = LOAD_SLOTS - 1
    narrow_cols = r.w_o_f32.shape[1]
    wide_cols = max(r.w_in_f32.shape[1], r.w_up_f32.shape[1])

    def load(stage_w, stage_n, sem_w, sem_n):
        def stream(src, dst):
            narrow = src.shape[1] == narrow_cols
            stage, sem = (stage_n, sem_n) if narrow else (stage_w, sem_w)
            rows = stage.shape[1]
            n_chunks = src.shape[0] // rows
            assert src.shape[0] % rows == 0, (src.shape, rows)

            def copy(i, slot):
                return pltpu.make_async_copy(src.at[pl.ds(i * rows, rows), :],
                                             stage.at[slot, :, 0:src.shape[1]], sem.at[slot])

            def prime():
                for i in range(min(ahead, n_chunks)):
                    copy(i, i % LOAD_SLOTS).start()

            def drain():
                def body(i, carry):
                    @pl.when(i + ahead < n_chunks)
                    def _():
                        copy(i + ahead, lax.rem(i + ahead, LOAD_SLOTS)).start()

                    slot = lax.rem(i, LOAD_SLOTS)
                    copy(i, slot).wait()
                    row0 = pl.multiple_of(i * rows, rows)
                    dst[pl.ds(row0, rows), :] = stage[slot, :, 0:src.shape[1]].astype(BF16)
                    return carry

                lax.fori_loop(0, n_chunks, body, 0)

            return prime, drain

        order = [(r.w_in_f32, r.w_in), (r.w_down_f32, r.w_down), (r.w_up_f32, r.w_up),
                 (r.w_o_f32, r.w_o), (r.w_out_a_f32, r.w_out_a), (r.w_out_b_f32, r.w_out_b),
                 (r.w_out_c_f32, r.w_out_c)]
        streams = [stream(src, dst) for src, dst in order]
        streams[0][0]()
        for k, (_, drain) in enumerate(streams):
            nxt_uses_other_ring = (k + 1 < len(order)
                                   and (order[k + 1][0].shape[1] == narrow_cols)
                                   != (order[k][0].shape[1] == narrow_cols))
            if nxt_uses_other_ring:
                streams[k + 1][0]()
            drain()
            if k + 1 < len(order) and not nxt_uses_other_ring:
                streams[k + 1][0]()

    pl.run_scoped(load,
                  pltpu.VMEM((LOAD_SLOTS, LOAD_WIDE_ROWS, wide_cols), F32),
                  pltpu.VMEM((LOAD_SLOTS, LOAD_NARROW_ROWS, narrow_cols), F32),
                  pltpu.SemaphoreType.DMA((LOAD_SLOTS,)),
                  pltpu.SemaphoreType.DMA((LOAD_SLOTS,)))


def _layer_kernel(tiles_per_seq, n_prompt, *refs):
    r = _LayerRefs(*refs)
    step = pl.program_id(0)
    last = pl.num_programs(0) - 1
    j = jnp.minimum(step, n_prompt - 1) & (tiles_per_seq - 1)

    @pl.when(step == 0)
    def _():
        _load_weights_bf16(r)

    @pl.when(jnp.logical_and(step < n_prompt, j == 0))
    def _():
        r.ext_p[0:POOL_PAD, :] = jnp.zeros((POOL_PAD, r.ext_p.shape[1]), F32)

    @pl.when(step == 0)
    def _():
        _layer_step(r, False, j, step, n_prompt, first=True)

    @pl.when(jnp.logical_and(step > 0, step < n_prompt))
    def _():
        _layer_step(r, False, j, step, n_prompt)

    @pl.when(jnp.logical_and(step >= n_prompt, step < last))
    def _():
        _layer_step(r, True, j, step, n_prompt)

    @pl.when(step == last)
    def _():
        _drain_step(r)


def _mem_kv(mem, g_mem, w_kv):
    bsz, n_mem, d = mem.shape
    xw = w_kv.shape[-1] // 2
    sb = MEM_KV_SEQS
    assert bsz % sb == 0, (bsz, sb)
    return pl.pallas_call(
        _mem_kv_kernel,
        grid=(bsz // sb,),
        in_specs=[pl.BlockSpec((sb, n_mem, d), lambda i: (i, 0, 0)),
                  _const_spec((1, d)), _const_spec((d, 2 * xw))],
        out_specs=[pl.BlockSpec((sb, xw, n_mem), lambda i: (i, 0, 0)),
                   pl.BlockSpec((sb, xw, n_mem), lambda i: (i, 0, 0)),
                   pl.BlockSpec((sb, xw, n_mem), lambda i: (i, 0, 0)),
                   pl.BlockSpec((sb, n_mem, xw), lambda i: (i, 0, 0))],
        out_shape=[jax.ShapeDtypeStruct((bsz, xw, n_mem), F32),
                   jax.ShapeDtypeStruct((bsz, xw, n_mem), F32),
                   jax.ShapeDtypeStruct((bsz, xw, n_mem), BF16),
                   jax.ShapeDtypeStruct((bsz, n_mem, xw), BF16)],
        compiler_params=pltpu.CompilerParams(
            dimension_semantics=("arbitrary",),
            vmem_limit_bytes=VMEM_LIMIT_BYTES),
        name="mem_kv",
    )(mem, g_mem.reshape(1, d).astype(F32), w_kv)


def _spatial_operands(w_s, b_s, group_dim, chunk_rows, tile_rows):
    wmix = jnp.tile(w_s[:, :chunk_rows, :chunk_rows], (1, 1, tile_rows // chunk_rows))
    bias = jnp.repeat(b_s[:, :chunk_rows].T, group_dim, axis=1)
    return wmix.astype(F32), bias.astype(F32)


def _layer(x_prompt, x2d, ktb, vb, cache_kt, cache_vt, state, lw):
    (g_mix, w_in, g_v, b_v, w_s, b_s, w_pool, pool_scale,
     w_out_a, w_out_b, w_out_c, w_o, g_ffn, w_up, w_down, g_final) = lw
    bsz, seq, d = x_prompt.shape
    n_seq = state.shape[0]
    rows = x2d.shape[0] // n_seq
    t = TILE
    ts = t // rows
    tps = seq // t
    n_prompt = bsz * tps
    n_decode = n_seq // ts
    assert n_seq % n_prompt == 0, (n_seq, n_prompt)
    sps = n_seq // n_prompt
    sgu_w = g_v.shape[-1]
    pool_w = pool_scale.shape[-1]
    n_mem, xw = vb.shape[1:]
    group_dim = sgu_w // w_s.shape[0]
    row2 = lambda vec: vec.reshape(1, -1).astype(F32)
    wmix_p, bias_p = _spatial_operands(w_s, b_s, group_dim, CHUNK, CHUNK)
    wmix_s, bias_s = _spatial_operands(w_s, b_s, group_dim, rows, t)
    wpool = jax.scipy.linalg.block_diag(*[w_pool[i] for i in range(w_pool.shape[0])])
    state_rows = jnp.transpose(state, (1, 0, 2))
    big = [w_in, w_out_a, w_out_b, w_out_c, w_o, w_up, w_down]
    weights = [row2(g_mix), w_in, row2(g_v), row2(b_v), wmix_p, bias_p, wmix_s, bias_s,
               wpool.astype(BF16), row2(pool_scale), w_out_a, w_out_b, w_out_c, w_o,
               row2(g_ffn), w_up, w_down, row2(g_final)]

    def prompt_tile(m):
        return jnp.clip(m, 0, n_prompt - 1)

    def prompt_seq(m):
        return lax.shift_right_logical(prompt_tile(m), _log2(tps))

    def decode_tile(m):
        return jnp.clip(m - n_prompt, 0, n_decode - 1)

    yp, ys, cv_p, ptail_p, cv_s, ptail_s = pl.pallas_call(
        functools.partial(_layer_kernel, tps, n_prompt),
        grid=(n_prompt + n_decode + 1,),
        in_specs=[pl.BlockSpec((1, t, d), lambda s: (prompt_tile(s), 0, 0)),
                  pl.BlockSpec((t, d), lambda s: (decode_tile(s), 0)),
                  pl.BlockSpec((1, xw, n_mem), lambda s: (prompt_seq(s), 0, 0)),
                  pl.BlockSpec((1, n_mem, xw), lambda s: (prompt_seq(s), 0, 0)),
                  pl.BlockSpec((sps * rows, d), lambda s: (prompt_tile(s), 0)),
                  pl.BlockSpec((sps, xw, n_mem), lambda s: (prompt_tile(s), 0, 0)),
                  pl.BlockSpec((sps, xw, n_mem), lambda s: (prompt_tile(s), 0, 0)),
                  pl.BlockSpec((POOL_STATE, ts, pool_w), lambda s: (0, decode_tile(s), 0))]
                 + [pl.BlockSpec(memory_space=pl.ANY) if any(w is b for b in big)
                    else _const_spec(w.shape) for w in weights],
        out_specs=[pl.BlockSpec((1, t, d), lambda s: (prompt_tile(s - 1), 0, 0)),
                   pl.BlockSpec((t, d), lambda s: (decode_tile(s - 1), 0)),
                   pl.BlockSpec((1, CHUNK, sgu_w), lambda s: (prompt_seq(s), 0, 0)),
                   pl.BlockSpec((1, POOL_PAD, pool_w), lambda s: (prompt_seq(s), 0, 0)),
                   pl.BlockSpec((t, sgu_w), lambda s: (decode_tile(s), 0)),
                   pl.BlockSpec((POOL_STATE, ts, pool_w), lambda s: (0, decode_tile(s), 0))],
        out_shape=[jax.ShapeDtypeStruct((n_prompt, t, d), F32),
                   jax.ShapeDtypeStruct((n_seq * rows, d), F32),
                   jax.ShapeDtypeStruct((bsz, CHUNK, sgu_w), F32),
                   jax.ShapeDtypeStruct((bsz, POOL_PAD, pool_w), F32),
                   jax.ShapeDtypeStruct((n_seq * rows, sgu_w), F32),
                   jax.ShapeDtypeStruct((POOL_STATE, n_seq, pool_w), F32)],
        scratch_shapes=[pltpu.VMEM((POOL_PAD + t, pool_w), F32),
                        pltpu.VMEM((pool_w // V7X_LANES, t, V7X_LANES), F32),
                        pltpu.VMEM((pool_w // V7X_LANES, t, V7X_LANES), F32),
                        pltpu.VMEM((t, d), F32),
                        pltpu.VMEM((t, d), BF16)]
                       + [pltpu.VMEM(w.shape, BF16) for w in big]
                       + [pltpu.VMEM((n_seq * rows, xw), F32)],
        compiler_params=pltpu.CompilerParams(
            dimension_semantics=("arbitrary",),
            vmem_limit_bytes=VMEM_LIMIT_BYTES),
        name="layer",
    )(x_prompt.reshape(n_prompt, t, d), x2d, ktb, vb, x2d, cache_kt, cache_vt, state_rows, *weights)
    return (yp.reshape(bsz, seq, d), ys.reshape(n_seq, rows, d), cv_p, ptail_p[:, POOL_PAD - POOL_STATE:, :],
            cv_s.reshape(n_seq, rows, sgu_w), jnp.transpose(ptail_s, (1, 0, 2)))


def _feature_major(kv):
    bsz, n_mem, heads, head_dim = kv.shape
    return jnp.transpose(kv, (0, 2, 3, 1)).reshape(bsz, heads * head_dim, n_mem)


def _token_major(kvt, heads):
    bsz, xw, n_mem = kvt.shape
    return jnp.transpose(kvt.reshape(bsz, heads, xw // heads, n_mem), (0, 3, 1, 2))


def kernel(x_prompt, x_sample, mem_prompt, cache_mem_k, cache_mem_v, state_pool, g_mix, w_in, g_v, b_v, w_s, b_s, w_pool, pool_scale, g_mem, w_kv, w_out_a, w_out_b, w_out_c, w_o, g_ffn, w_up, w_down, g_final):
    depth = w_in.shape[0]
    assert depth == 1, "the final rmsnorm is fused into the (single) layer kernel"
    heads = cache_mem_k.shape[-2]
    n_seq, rows, d = x_sample.shape
    lw = (g_mix[0], w_in[0], g_v[0], b_v[0], w_s[0], b_s[0], w_pool[0], pool_scale[0],
          w_out_a[0], w_out_b[0], w_out_c[0], w_o[0], g_ffn[0], w_up[0], w_down[0], g_final)
    mem_kt, mem_vt, ktb, vb = _mem_kv(mem_prompt, g_mem[0], w_kv[0])
    x2d = x_sample.reshape(n_seq * rows, d)
    y_prompt, y_sample, cv_p, ptail_p, cv_s, ptail_s = _layer(
        x_prompt, x2d, ktb, vb, _feature_major(cache_mem_k[0]), _feature_major(cache_mem_v[0]),
        state_pool[0], lw)
    return (y_prompt, y_sample, _token_major(mem_kt, heads)[None], _token_major(mem_vt, heads)[None],
            ptail_p[None], ptail_s[None], cv_p[None], cv_s[None])
```

```python
import functools
from typing import NamedTuple

import jax
import jax.numpy as jnp
from jax import lax
from jax.experimental import pallas as pl
from jax.experimental.pallas import tpu as pltpu

PAST_LEN = 16384
CHUNK = 128
POOL_WINDOWS = (2, 4, 8, 16)
POOL_STATE = max(POOL_WINDOWS) - 1
X_HEADS = 4
EPS = 1e-6

V7X_VMEM_BYTES = 64 * 1024 * 1024
V7X_LANES = 128
VMEM_RESERVE_BYTES = 6 * 1024 * 1024
VMEM_LIMIT_BYTES = V7X_VMEM_BYTES - VMEM_RESERVE_BYTES

TILE = 256
MEM_KV_SEQS = 4
FFN_SPLIT = 4
FFN_POINTS = ((("up", 0),), (("down", 0),), (("up", 1),), (("down", 1),),
              (("up", 2),), (("down", 2),), (("up", 3),), (("down", 3),))
LOAD_SLOTS = 8
LOAD_WIDE_ROWS = 32
LOAD_NARROW_ROWS = 128
POOL_PAD = POOL_STATE + 1

BF16 = jnp.bfloat16
F32 = jnp.float32


def _dot(a, b):
    return jnp.dot(a, b, preferred_element_type=F32)


def _log2(n):
    assert n > 0 and n & (n - 1) == 0, n
    return n.bit_length() - 1


def _rmsnorm(x, g):
    return x * lax.rsqrt(jnp.mean(x * x, axis=-1, keepdims=True) + EPS) * g


def _layernorm(x, g, b):
    xc = x - jnp.mean(x, axis=-1, keepdims=True)
    return xc * lax.rsqrt(jnp.mean(xc * xc, axis=-1, keepdims=True) + EPS) * g + b


def _softmax_bf16(s):
    e = jnp.exp(s - jnp.max(s, axis=-1, keepdims=True))
    return (e * (1.0 / jnp.sum(e, axis=-1, keepdims=True))).astype(BF16)


def _const_spec(shape):
    zeros = (0,) * len(shape)
    return pl.BlockSpec(shape, lambda *_: zeros, pipeline_mode=pl.Buffered(1))


def _chunk_mix(vhat_b, wmix_ref, chunk_rows):
    t, sgu_w = vhat_b.shape
    n_groups = wmix_ref.shape[0]
    group_dim = sgu_w // n_groups
    shift = _log2(chunk_rows)
    reps = t // chunk_rows
    row = lax.broadcasted_iota(jnp.int32, (t, t), 0)
    col = lax.broadcasted_iota(jnp.int32, (t, t), 1)
    same_chunk = (row >> shift) == (col >> shift)
    causal = col <= row
    parts = []
    for g in range(n_groups):
        w = jnp.concatenate([wmix_ref[g]] * (t // wmix_ref.shape[2]), axis=1)
        w = jnp.concatenate([w] * reps, axis=0)
        w = jnp.where(same_chunk, jnp.where(causal, w, 0.0), 0.0).astype(BF16)
        parts.append(_dot(w, vhat_b[:, g * group_dim:(g + 1) * group_dim]))
    return jnp.concatenate(parts, axis=1)


def _chunk_bias(bias_ref, t):
    return jnp.concatenate([bias_ref[...]] * (t // bias_ref.shape[0]), axis=0)


def _window_means_minus_self(load_shifted, p, pos):
    group_lanes = p.shape[-1] // len(POOL_WINDOWS)
    outs = []
    for lt in range(p.shape[-1] // V7X_LANES):
        p_lt = p[..., lt * V7X_LANES:(lt + 1) * V7X_LANES]
        lane = lax.broadcasted_iota(jnp.int32, p_lt.shape, p_lt.ndim - 1)
        wins = POOL_WINDOWS[lt * V7X_LANES // group_lanes:(lt + 1) * V7X_LANES // group_lanes]
        acc = p_lt
        sums = {1: p_lt}
        for k in range(1, max(wins)):
            acc = acc + load_shifted(k, lt)
            sums[k + 1] = acc
        win_sum = sums[wins[-1]]
        win = jnp.full(p_lt.shape, wins[-1], jnp.int32)
        for gi in range(len(wins) - 2, -1, -1):
            in_group = lane < (gi + 1) * group_lanes
            win_sum = jnp.where(in_group, sums[wins[gi]], win_sum)
            win = jnp.where(in_group, wins[gi], win)
        cnt = jnp.minimum(pos + 1, win).astype(F32)
        outs.append(win_sum / cnt - p_lt)
    return jnp.concatenate(outs, axis=-1)


def _mem_kv_kernel(mem_ref, g_mem_ref, w_kv_ref, kt_ref, vt_ref, ktb_ref, vb_ref):
    xw = kt_ref.shape[1]
    w_kv = w_kv_ref[...].astype(BF16)
    for i in range(mem_ref.shape[0]):
        hn = _rmsnorm(mem_ref[i], g_mem_ref[...]).astype(BF16)
        kv = _dot(hn, w_kv)
        kt = kv[:, 0:xw].T
        v = kv[:, xw:2 * xw]
        kt_ref[i] = kt
        vt_ref[i] = v.T
        ktb_ref[i] = kt.astype(BF16)
        vb_ref[i] = v.astype(BF16)


def _decode_own_head(r):
    n_seq, x_w, _ = r.ckt.shape
    rows = r.xq.shape[0] // n_seq
    stack = X_HEADS * rows
    lane_head = lax.broadcasted_iota(jnp.int32, (stack, x_w), 1) >> _log2(x_w // X_HEADS)
    row_head = lax.broadcasted_iota(jnp.int32, (stack, x_w), 0) >> _log2(rows)
    return lane_head == row_head


def _decode_scores(r, q):
    n_seq, _, n_mem = r.ckt.shape
    rows = r.xq.shape[0] // n_seq
    stack = X_HEADS * rows
    own_head = _decode_own_head(r)
    qms = []
    for i in range(n_seq):
        qs = jnp.concatenate([q[i * rows:(i + 1) * rows, :]] * X_HEADS, axis=0)
        qms.append(jnp.where(own_head, qs, 0.0).astype(BF16))
    keys = jnp.concatenate([r.ckt[i].astype(BF16) for i in range(n_seq)], axis=1)
    both = _dot(jnp.concatenate(qms, axis=0), keys)
    return [both[i * stack:(i + 1) * stack, i * n_mem:(i + 1) * n_mem] for i in range(n_seq)]


def _decode_values(r, probs, step):
    n_seq = r.cvt.shape[0]
    rows = r.xq.shape[0] // n_seq
    own_head = _decode_own_head(r)
    for i in range(n_seq):
        of = _dot(probs[i], r.cvt[i].T.astype(BF16))
        of = jnp.where(own_head, of, 0.0)
        out = of[0:rows]
        for hd in range(1, X_HEADS):
            out = out + of[hd * rows:(hd + 1) * rows]
        row0 = pl.multiple_of((step * n_seq + i) * rows, rows)
        r.attn[pl.ds(row0, rows), :] = out


class _LayerRefs(NamedTuple):
    xp: object
    xs: object
    kt: object
    vb: object
    xq: object
    ckt: object
    cvt: object
    state: object
    g_mix: object
    w_in_f32: object
    g_v: object
    b_v: object
    wmix_p: object
    bias_p: object
    wmix_s: object
    bias_s: object
    wpool: object
    pool_scale: object
    w_out_a_f32: object
    w_out_b_f32: object
    w_out_c_f32: object
    w_o_f32: object
    g_ffn: object
    w_up_f32: object
    w_down_f32: object
    g_final: object
    yp: object
    ys: object
    cv_p: object
    ptail_p: object
    cv_s: object
    ptail_s: object
    ext_p: object
    pool_in: object
    pool_out: object
    x1: object
    h2: object
    w_in: object
    w_out_a: object
    w_out_b: object
    w_out_c: object
    w_o: object
    w_up: object
    w_down: object
    attn: object


def _layer_step(r, decode, j, step, n_prompt, first=False):
    t, d = r.x1.shape
    sgu_w = r.g_v.shape[-1]
    pool_w = r.pool_scale.shape[-1]
    x_w = r.attn.shape[-1]
    n_decode = r.attn.shape[0] // t
    head_dim = x_w // X_HEADS
    c_pool = 2 * sgu_w
    c_q = c_pool + pool_w
    c_gate = c_q + x_w

    ffn = _FfnSchedule(r, enabled=not first)

    def gate(h, part):
        lo = c_gate + part * d
        return jax.nn.sigmoid(_dot(h, r.w_in[:, lo:lo + d]))

    ffn.at(0)
    x = r.xs[...] if decode else r.xp[0]
    h = _rmsnorm(x, r.g_mix[...]).astype(BF16)
    if decode:
        z = _dot(h, r.w_in[:, 0:c_q])
    else:
        h_dec = _rmsnorm(r.xq[...], r.g_mix[...]).astype(BF16)
        z = _dot(jnp.concatenate([h, h_dec], axis=0), r.w_in[:, 0:c_gate])
        dec_q = z[t:, c_q:c_gate] * (head_dim ** -0.5)
        z = z[0:t]
    u = z[:, 0:sgu_w]
    v = z[:, sgu_w:c_pool]
    p = z[:, c_pool:c_q]
    gate_a = gate(h, 0)
    ffn.at(1)

    u = jax.nn.gelu(u)
    vhat = _layernorm(jax.nn.gelu(v), r.g_v[...], r.b_v[...])
    if decode:
        rows = r.wmix_s.shape[1]
        r.cv_s[...] = vhat
        mixed = _chunk_mix(vhat.astype(BF16), r.wmix_s, rows)
        bias = _chunk_bias(r.bias_s, t)
    else:
        r.cv_p[0] = vhat[t - CHUNK:, :]
        mixed = _chunk_mix(vhat.astype(BF16), r.wmix_p, CHUNK)
        bias = _chunk_bias(r.bias_p, t)
    if not decode:
        q = z[:, c_q:c_gate] * (head_dim ** -0.5)
        dec_scores = _decode_scores(r, dec_q)
    ffn.at(2)
    a_in = (u * (mixed + bias)).astype(BF16)

    if decode:
        n_seq = t // rows
        lane_tiles = range(pool_w // V7X_LANES)
        for lt in lane_tiles:
            r.pool_in[lt] = p[:, lt * V7X_LANES:(lt + 1) * V7X_LANES]
        ext = [r.state[k] for k in range(POOL_STATE)]
        ext += [jnp.concatenate([r.pool_in[lt, pl.ds(i, n_seq, stride=rows), :]
                                 for lt in lane_tiles], axis=-1) for i in range(rows)]
        for i in range(rows):
            cur = POOL_STATE + i

            def load_shifted(k, lt, cur=cur):
                return ext[cur - k][:, lt * V7X_LANES:(lt + 1) * V7X_LANES]

            res = _window_means_minus_self(load_shifted, ext[cur], PAST_LEN + i)
            for lt in lane_tiles:
                r.pool_out[lt, pl.ds(i, n_seq, stride=rows), :] = (
                    res[:, lt * V7X_LANES:(lt + 1) * V7X_LANES])
        for k in range(POOL_STATE):
            r.ptail_s[k] = ext[rows + k]
        pooled = jnp.concatenate([r.pool_out[lt] for lt in lane_tiles], axis=-1)
    else:
        r.ext_p[POOL_PAD:POOL_PAD + t, :] = p

        def load_shifted(k, lt):
            return r.ext_p[POOL_PAD - k:POOL_PAD - k + t, lt * V7X_LANES:(lt + 1) * V7X_LANES]

        pos = j * t + lax.broadcasted_iota(jnp.int32, (t, V7X_LANES), 0)
        pooled = _window_means_minus_self(load_shifted, p, pos)
        tail = r.ext_p[t:t + POOL_PAD, :]
        r.ext_p[0:POOL_PAD, :] = tail
        r.ptail_p[0] = tail
    pooled = pooled.astype(BF16)

    a = _dot(a_in, r.w_out_a[...])

    if not decode:
        head = lax.broadcasted_iota(jnp.int32, (t, x_w), 1) >> _log2(head_dim)
        kt = r.kt[0]
        q_heads = jnp.concatenate(
            [jnp.where(head == hd, q, 0.0).astype(BF16) for hd in range(X_HEADS)], axis=0)
        scores = _dot(q_heads, kt)
    pool_mixed = _dot(pooled, r.wpool[...]) * r.pool_scale[...]
    ffn.at(3)
    if decode:
        row0 = pl.multiple_of(jnp.clip(step - n_prompt, 0, n_decode - 1) * t, t)
        o = r.attn[pl.ds(row0, t), :]
    else:
        dec_probs = [_softmax_bf16(s) for s in dec_scores]
        o_heads = _dot(_softmax_bf16(scores), r.vb[0])
        o = jnp.zeros((t, x_w), F32)
        for hd in range(X_HEADS):
            o = jnp.where(head == hd, o_heads[hd * t:(hd + 1) * t], o)
        _decode_values(r, dec_probs, step)
    b = _dot(pool_mixed.astype(BF16), r.w_out_b[...])
    gate_b = gate(h, 1)
    c = _dot(o.astype(BF16), r.w_out_c[...])
    ffn.at(4)
    gate_c = gate(h, 2)
    ffn.at(5)
    merged = (gate_a * a + gate_b * b + gate_c * c).astype(BF16)
    ffn.at(6)
    ffn.at(7)
    x2 = ffn.x2
    x1 = x + _dot(merged, r.w_o[...])
    r.x1[...] = x1
    r.h2[...] = _rmsnorm(x1, r.g_ffn[...]).astype(BF16)
    if first:
        return
    y = _rmsnorm(x2, r.g_final[...])
    if not decode:
        r.yp[0] = y
    else:
        @pl.when(step == n_prompt)
        def _():
            r.yp[0] = y

        @pl.when(step != n_prompt)
        def _():
            r.ys[...] = y


def _ffn_up(r, s):
    slab = r.w_up.shape[1] // FFN_SPLIT
    up = _dot(r.h2[...], r.w_up[:, s * slab:(s + 1) * slab])
    return jnp.square(jnp.maximum(up, 0.0)).astype(BF16)


def _ffn_down(r, act, s):
    slab = r.w_up.shape[1] // FFN_SPLIT
    return _dot(act, r.w_down[s * slab:(s + 1) * slab, :])


class _FfnSchedule:
    def __init__(self, r, enabled=True):
        self.r = r
        self.acts = {}
        self.x2 = None
        self.points = FFN_POINTS if enabled else ((),) * len(FFN_POINTS)

    def at(self, point):
        for kind, s in self.points[point]:
            if kind == "up":
                self.acts[s] = _ffn_up(self.r, s)
            else:
                base = self.r.x1[...] if self.x2 is None else self.x2
                self.x2 = base + _ffn_down(self.r, self.acts.pop(s), s)


def _drain_step(r):
    x2 = r.x1[...]
    act = _ffn_up(r, 0)
    for s in range(FFN_SPLIT):
        nxt = _ffn_up(r, s + 1) if s + 1 < FFN_SPLIT else None
        x2 = x2 + _ffn_down(r, act, s)
        act = nxt
    r.ys[...] = _rmsnorm(x2, r.g_final[...])


def _load_weights_bf16(r):
    ahead = LOAD_SLOTS - 1
    narrow_cols = r.w_o_f32.shape[1]
    wide_cols = max(r.w_in_f32.shape[1], r.w_up_f32.shape[1])

    def load(stage_w, stage_n, sem_w, sem_n):
        def stream(src, dst):
            narrow = src.shape[1] == narrow_cols
            stage, sem = (stage_n, sem_n) if narrow else (stage_w, sem_w)
            rows = stage.shape[1]
            n_chunks = src.shape[0] // rows
            assert src.shape[0] % rows == 0, (src.shape, rows)

            def copy(i, slot):
                return pltpu.make_async_copy(src.at[pl.ds(i * rows, rows), :],
                                             stage.at[slot, :, 0:src.shape[1]], sem.at[slot])

            def prime():
                for i in range(min(ahead, n_chunks)):
                    copy(i, i % LOAD_SLOTS).start()

            def drain():
                def body(i, carry):
                    @pl.when(i + ahead < n_chunks)
                    def _():
                        copy(i + ahead, lax.rem(i + ahead, LOAD_SLOTS)).start()

                    slot = lax.rem(i, LOAD_SLOTS)
                    copy(i, slot).wait()
                    row0 = pl.multiple_of(i * rows, rows)
                    dst[pl.ds(row0, rows), :] = stage[slot, :, 0:src.shape[1]].astype(BF16)
                    return carry

                lax.fori_loop(0, n_chunks, body, 0)

            return prime, drain

        order = [(r.w_in_f32, r.w_in), (r.w_down_f32, r.w_down), (r.w_up_f32, r.w_up),
                 (r.w_o_f32, r.w_o), (r.w_out_a_f32, r.w_out_a), (r.w_out_b_f32, r.w_out_b),
                 (r.w_out_c_f32, r.w_out_c)]
        streams = [stream(src, dst) for src, dst in order]
        streams[0][0]()
        for k, (_, drain) in enumerate(streams):
            nxt_uses_other_ring = (k + 1 < len(order)
                                   and (order[k + 1][0].shape[1] == narrow_cols)
                                   != (order[k][0].shape[1] == narrow_cols))
            if nxt_uses_other_ring:
                streams[k + 1][0]()
            drain()
            if k + 1 < len(order) and not nxt_uses_other_ring:
                streams[k + 1][0]()

    pl.run_scoped(load,
                  pltpu.VMEM((LOAD_SLOTS, LOAD_WIDE_ROWS, wide_cols), F32),
                  pltpu.VMEM((LOAD_SLOTS, LOAD_NARROW_ROWS, narrow_cols), F32),
                  pltpu.SemaphoreType.DMA((LOAD_SLOTS,)),
                  pltpu.SemaphoreType.DMA((LOAD_SLOTS,)))


def _layer_kernel(tiles_per_seq, n_prompt, *refs):
    r = _LayerRefs(*refs)
    step = pl.program_id(0)
    last = pl.num_programs(0) - 1
    j = jnp.minimum(step, n_prompt - 1) & (tiles_per_seq - 1)

    @pl.when(step == 0)
    def _():
        _load_weights_bf16(r)

    @pl.when(jnp.logical_and(step < n_prompt, j == 0))
    def _():
        r.ext_p[0:POOL_PAD, :] = jnp.zeros((POOL_PAD, r.ext_p.shape[1]), F32)

    @pl.when(step == 0)
    def _():
        _layer_step(r, False, j, step, n_prompt, first=True)

    @pl.when(jnp.logical_and(step > 0, step < n_prompt))
    def _():
        _layer_step(r, False, j, step, n_prompt)

    @pl.when(jnp.logical_and(step >= n_prompt, step < last))
    def _():
        _layer_step(r, True, j, step, n_prompt)

    @pl.when(step == last)
    def _():
        _drain_step(r)


def _mem_kv(mem, g_mem, w_kv):
    bsz, n_mem, d = mem.shape
    xw = w_kv.shape[-1] // 2
    sb = MEM_KV_SEQS
    assert bsz % sb == 0, (bsz, sb)
    return pl.pallas_call(
        _mem_kv_kernel,
        grid=(bsz // sb,),
        in_specs=[pl.BlockSpec((sb, n_mem, d), lambda i: (i, 0, 0)),
                  _const_spec((1, d)), _const_spec((d, 2 * xw))],
        out_specs=[pl.BlockSpec((sb, xw, n_mem), lambda i: (i, 0, 0)),
                   pl.BlockSpec((sb, xw, n_mem), lambda i: (i, 0, 0)),
                   pl.BlockSpec((sb, xw, n_mem), lambda i: (i, 0, 0)),
                   pl.BlockSpec((sb, n_mem, xw), lambda i: (i, 0, 0))],
        out_shape=[jax.ShapeDtypeStruct((bsz, xw, n_mem), F32),
                   jax.ShapeDtypeStruct((bsz, xw, n_mem), F32),
                   jax.ShapeDtypeStruct((bsz, xw, n_mem), BF16),
                   jax.ShapeDtypeStruct((bsz, n_mem, xw), BF16)],
        compiler_params=pltpu.CompilerParams(
            dimension_semantics=("arbitrary",),
            vmem_limit_bytes=VMEM_LIMIT_BYTES),
        name="mem_kv",
    )(mem, g_mem.reshape(1, d).astype(F32), w_kv)


def _spatial_operands(w_s, b_s, group_dim, chunk_rows, tile_rows):
    wmix = jnp.tile(w_s[:, :chunk_rows, :chunk_rows], (1, 1, tile_rows // chunk_rows))
    bias = jnp.repeat(b_s[:, :chunk_rows].T, group_dim, axis=1)
    return wmix.astype(F32), bias.astype(F32)


def _layer(x_prompt, x2d, ktb, vb, cache_kt, cache_vt, state, lw):
    (g_mix, w_in, g_v, b_v, w_s, b_s, w_pool, pool_scale,
     w_out_a, w_out_b, w_out_c, w_o, g_ffn, w_up, w_down, g_final) = lw
    bsz, seq, d = x_prompt.shape
    n_seq = state.shape[0]
    rows = x2d.shape[0] // n_seq
    t = TILE
    ts = t // rows
    tps = seq // t
    n_prompt = bsz * tps
    n_decode = n_seq // ts
    assert n_seq % n_prompt == 0, (n_seq, n_prompt)
    sps = n_seq // n_prompt
    sgu_w = g_v.shape[-1]
    pool_w = pool_scale.shape[-1]
    n_mem, xw = vb.shape[1:]
    group_dim = sgu_w // w_s.shape[0]
    row2 = lambda vec: vec.reshape(1, -1).astype(F32)
    wmix_p, bias_p = _spatial_operands(w_s, b_s, group_dim, CHUNK, CHUNK)
    wmix_s, bias_s = _spatial_operands(w_s, b_s, group_dim, rows, t)
    wpool = jax.scipy.linalg.block_diag(*[w_pool[i] for i in range(w_pool.shape[0])])
    state_rows = jnp.transpose(state, (1, 0, 2))
    big = [w_in, w_out_a, w_out_b, w_out_c, w_o, w_up, w_down]
    weights = [row2(g_mix), w_in, row2(g_v), row2(b_v), wmix_p, bias_p, wmix_s, bias_s,
               wpool.astype(BF16), row2(pool_scale), w_out_a, w_out_b, w_out_c, w_o,
               row2(g_ffn), w_up, w_down, row2(g_final)]

    def prompt_tile(m):
        return jnp.clip(m, 0, n_prompt - 1)

    def prompt_seq(m):
        return lax.shift_right_logical(prompt_tile(m), _log2(tps))

    def decode_tile(m):
        return jnp.clip(m - n_prompt, 0, n_decode - 1)

    yp, ys, cv_p, ptail_p, cv_s, ptail_s = pl.pallas_call(
        functools.partial(_layer_kernel, tps, n_prompt),
        grid=(n_prompt + n_decode + 1,),
        in_specs=[pl.BlockSpec((1, t, d), lambda s: (prompt_tile(s), 0, 0)),
                  pl.BlockSpec((t, d), lambda s: (decode_tile(s), 0)),
                  pl.BlockSpec((1, xw, n_mem), lambda s: (prompt_seq(s), 0, 0)),
                  pl.BlockSpec((1, n_mem, xw), lambda s: (prompt_seq(s), 0, 0)),
                  pl.BlockSpec((sps * rows, d), lambda s: (prompt_tile(s), 0)),
                  pl.BlockSpec((sps, xw, n_mem), lambda s: (prompt_tile(s), 0, 0)),
                  pl.BlockSpec((sps, xw, n_mem), lambda s: (prompt_tile(s), 0, 0)),
                  pl.BlockSpec((POOL_STATE, ts, pool_w), lambda s: (0, decode_tile(s), 0))]
                 + [pl.BlockSpec(memory_space=pl.ANY) if any(w is b for b in big)
                    else _const_spec(w.shape) for w in weights],
        out_specs=[pl.BlockSpec((1, t, d), lambda s: (prompt_tile(s - 1), 0, 0)),
                   pl.BlockSpec((t, d), lambda s: (decode_tile(s - 1), 0)),
                   pl.BlockSpec((1, CHUNK, sgu_w), lambda s: (prompt_seq(s), 0, 0)),
                   pl.BlockSpec((1, POOL_PAD, pool_w), lambda s: (prompt_seq(s), 0, 0)),
                   pl.BlockSpec((t, sgu_w), lambda s: (decode_tile(s), 0)),
                   pl.BlockSpec((POOL_STATE, ts, pool_w), lambda s: (0, decode_tile(s), 0))],
        out_shape=[jax.ShapeDtypeStruct((n_prompt, t, d), F32),
                   jax.ShapeDtypeStruct((n_seq * rows, d), F32),
                   jax.ShapeDtypeStruct((bsz, CHUNK, sgu_w), F32),
                   jax.ShapeDtypeStruct((bsz, POOL_PAD, pool_w), F32),
                   jax.ShapeDtypeStruct((n_seq * rows, sgu_w), F32),
                   jax.ShapeDtypeStruct((POOL_STATE, n_seq, pool_w), F32)],
        scratch_shapes=[pltpu.VMEM((POOL_PAD + t, pool_w), F32),
                        pltpu.VMEM((pool_w // V7X_LANES, t, V7X_LANES), F32),
                        pltpu.VMEM((pool_w // V7X_LANES, t, V7X_LANES), F32),
                        pltpu.VMEM((t, d), F32),
                        pltpu.VMEM((t, d), BF16)]
                       + [pltpu.VMEM(w.shape, BF16) for w in big]
                       + [pltpu.VMEM((n_seq * rows, xw), F32)],
        compiler_params=pltpu.CompilerParams(
            dimension_semantics=("arbitrary",),
            vmem_limit_bytes=VMEM_LIMIT_BYTES),
        name="layer",
    )(x_prompt.reshape(n_prompt, t, d), x2d, ktb, vb, x2d, cache_kt, cache_vt, state_rows, *weights)
    return (yp.reshape(bsz, seq, d), ys.reshape(n_seq, rows, d), cv_p, ptail_p[:, POOL_PAD - POOL_STATE:, :],
            cv_s.reshape(n_seq, rows, sgu_w), jnp.transpose(ptail_s, (1, 0, 2)))


def _feature_major(kv):
    bsz, n_mem, heads, head_dim = kv.shape
    return jnp.transpose(kv, (0, 2, 3, 1)).reshape(bsz, heads * head_dim, n_mem)


def _token_major(kvt, heads):
    bsz, xw, n_mem = kvt.shape
    return jnp.transpose(kvt.reshape(bsz, heads, xw // heads, n_mem), (0, 3, 1, 2))


def kernel(x_prompt, x_sample, mem_prompt, cache_mem_k, cache_mem_v, state_pool, g_mix, w_in, g_v, b_v, w_s, b_s, w_pool, pool_scale, g_mem, w_kv, w_out_a, w_out_b, w_out_c, w_o, g_ffn, w_up, w_down, g_final):
    depth = w_in.shape[0]
    assert depth == 1, "the final rmsnorm is fused into the (single) layer kernel"
    heads = cache_mem_k.shape[-2]
    n_seq, rows, d = x_sample.shape
    lw = (g_mix[0], w_in[0], g_v[0], b_v[0], w_s[0], b_s[0], w_pool[0], pool_scale[0],
          w_out_a[0], w_out_b[0], w_out_c[0], w_o[0], g_ffn[0], w_up[0], w_down[0], g_final)
    mem_kt, mem_vt, ktb, vb = _mem_kv(mem_prompt, g_mem[0], w_kv[0])
    x2d = x_sample.reshape(n_seq * rows, d)
    y_prompt, y_sample, cv_p, ptail_p, cv_s, ptail_s = _layer(
        x_prompt, x2d, ktb, vb, _feature_major(cache_mem_k[0]), _feature_major(cache_mem_v[0]),
        state_pool[0], lw)
    return (y_prompt, y_sample, _token_major(mem_kt, heads)[None], _token_major(mem_vt, heads)[None],
            ptail_p[None], ptail_s[None], cv_p[None], cv_s[None])
```

```python
import functools
from typing import NamedTuple

import jax
import jax.numpy as jnp
from jax import lax
from jax.experimental import pallas as pl
from jax.experimental.pallas import tpu as pltpu

PAST_LEN = 16384
CHUNK = 128
POOL_WINDOWS = (2, 4, 8, 16)
POOL_STATE = max(POOL_WINDOWS) - 1
X_HEADS = 4
EPS = 1e-6

V7X_VMEM_BYTES = 64 * 1024 * 1024
V7X_LANES = 128
VMEM_RESERVE_BYTES = 6 * 1024 * 1024
VMEM_LIMIT_BYTES = V7X_VMEM_BYTES - VMEM_RESERVE_BYTES

TILE = 256
MEM_KV_SEQS = 4
FFN_SPLIT = 4
FFN_POINTS = ((("up", 0),), (("down", 0),), (("up", 1),), (("down", 1),),
              (("up", 2),), (("down", 2),), (("up", 3),), (("down", 3),))
LOAD_SLOTS = 8
LOAD_WIDE_ROWS = 32
LOAD_NARROW_ROWS = 128
POOL_PAD = POOL_STATE + 1

BF16 = jnp.bfloat16
F32 = jnp.float32


def _dot(a, b):
    return jnp.dot(a, b, preferred_element_type=F32)


def _log2(n):
    assert n > 0 and n & (n - 1) == 0, n
    return n.bit_length() - 1


def _rmsnorm(x, g):
    return x * lax.rsqrt(jnp.mean(x * x, axis=-1, keepdims=True) + EPS) * g


def _layernorm(x, g, b):
    xc = x - jnp.mean(x, axis=-1, keepdims=True)
    return xc * lax.rsqrt(jnp.mean(xc * xc, axis=-1, keepdims=True) + EPS) * g + b


def _softmax_bf16(s):
    e = jnp.exp(s - jnp.max(s, axis=-1, keepdims=True))
    return (e * (1.0 / jnp.sum(e, axis=-1, keepdims=True))).astype(BF16)


def _const_spec(shape):
    zeros = (0,) * len(shape)
    return pl.BlockSpec(shape, lambda *_: zeros, pipeline_mode=pl.Buffered(1))


def _chunk_mix(vhat_b, wmix_ref, chunk_rows):
    t, sgu_w = vhat_b.shape
    n_groups = wmix_ref.shape[0]
    group_dim = sgu_w // n_groups
    shift = _log2(chunk_rows)
    reps = t // chunk_rows
    row = lax.broadcasted_iota(jnp.int32, (t, t), 0)
    col = lax.broadcasted_iota(jnp.int32, (t, t), 1)
    same_chunk = (row >> shift) == (col >> shift)
    causal = col <= row
    parts = []
    for g in range(n_groups):
        w = jnp.concatenate([wmix_ref[g]] * (t // wmix_ref.shape[2]), axis=1)
        w = jnp.concatenate([w] * reps, axis=0)
        w = jnp.where(same_chunk, jnp.where(causal, w, 0.0), 0.0).astype(BF16)
        parts.append(_dot(w, vhat_b[:, g * group_dim:(g + 1) * group_dim]))
    return jnp.concatenate(parts, axis=1)


def _chunk_bias(bias_ref, t):
    return jnp.concatenate([bias_ref[...]] * (t // bias_ref.shape[0]), axis=0)


def _window_means_minus_self(load_shifted, p, pos):
    group_lanes = p.shape[-1] // len(POOL_WINDOWS)
    outs = []
    for lt in range(p.shape[-1] // V7X_LANES):
        p_lt = p[..., lt * V7X_LANES:(lt + 1) * V7X_LANES]
        lane = lax.broadcasted_iota(jnp.int32, p_lt.shape, p_lt.ndim - 1)
        wins = POOL_WINDOWS[lt * V7X_LANES // group_lanes:(lt + 1) * V7X_LANES // group_lanes]
        acc = p_lt
        sums = {1: p_lt}
        for k in range(1, max(wins)):
            acc = acc + load_shifted(k, lt)
            sums[k + 1] = acc
        win_sum = sums[wins[-1]]
        win = jnp.full(p_lt.shape, wins[-1], jnp.int32)
        for gi in range(len(wins) - 2, -1, -1):
            in_group = lane < (gi + 1) * group_lanes
            win_sum = jnp.where(in_group, sums[wins[gi]], win_sum)
            win = jnp.where(in_group, wins[gi], win)
        cnt = jnp.minimum(pos + 1, win).astype(F32)
        outs.append(win_sum / cnt - p_lt)
    return jnp.concatenate(outs, axis=-1)


def _mem_kv_kernel(mem_ref, g_mem_ref, w_kv_ref, kt_ref, vt_ref, ktb_ref, vb_ref):
    xw = kt_ref.shape[1]
    w_kv = w_kv_ref[...].astype(BF16)
    for i in range(mem_ref.shape[0]):
        hn = _rmsnorm(mem_ref[i], g_mem_ref[...]).astype(BF16)
        kv = _dot(hn, w_kv)
        kt = kv[:, 0:xw].T
        v = kv[:, xw:2 * xw]
        kt_ref[i] = kt
        vt_ref[i] = v.T
        ktb_ref[i] = kt.astype(BF16)
        vb_ref[i] = v.astype(BF16)


def _decode_own_head(r):
    n_seq, x_w, _ = r.ckt.shape
    rows = r.xq.shape[0] // n_seq
    stack = X_HEADS * rows
    lane_head = lax.broadcasted_iota(jnp.int32, (stack, x_w), 1) >> _log2(x_w // X_HEADS)
    row_head = lax.broadcasted_iota(jnp.int32, (stack, x_w), 0) >> _log2(rows)
    return lane_head == row_head


def _decode_scores(r, q):
    n_seq, _, n_mem = r.ckt.shape
    rows = r.xq.shape[0] // n_seq
    stack = X_HEADS * rows
    own_head = _decode_own_head(r)
    qms = []
    for i in range(n_seq):
        qs = jnp.concatenate([q[i * rows:(i + 1) * rows, :]] * X_HEADS, axis=0)
        qms.append(jnp.where(own_head, qs, 0.0).astype(BF16))
    keys = jnp.concatenate([r.ckt[i].astype(BF16) for i in range(n_seq)], axis=1)
    both = _dot(jnp.concatenate(qms, axis=0), keys)
    return [both[i * stack:(i + 1) * stack, i * n_mem:(i + 1) * n_mem] for i in range(n_seq)]


def _decode_values(r, probs, step):
    n_seq = r.cvt.shape[0]
    rows = r.xq.shape[0] // n_seq
    stack = X_HEADS * rows
    own_head = _decode_own_head(r)
    zeros = jnp.zeros_like(probs[0])
    blocks = jnp.concatenate(
        [jnp.concatenate([probs[i] if k == i else zeros for k in range(n_seq)], axis=1)
         for i in range(n_seq)], axis=0)
    values = jnp.concatenate([r.cvt[i].T.astype(BF16) for i in range(n_seq)], axis=0)
    both = _dot(blocks, values)
    for i in range(n_seq):
        of = both[i * stack:(i + 1) * stack]
        of = jnp.where(own_head, of, 0.0)
        out = of[0:rows]
        for hd in range(1, X_HEADS):
            out = out + of[hd * rows:(hd + 1) * rows]
        row0 = pl.multiple_of((step * n_seq + i) * rows, rows)
        r.attn[pl.ds(row0, rows), :] = out


class _LayerRefs(NamedTuple):
    xp: object
    xs: object
    kt: object
    vb: object
    xq: object
    ckt: object
    cvt: object
    state: object
    g_mix: object
    w_in_f32: object
    g_v: object
    b_v: object
    wmix_p: object
    bias_p: object
    wmix_s: object
    bias_s: object
    wpool: object
    pool_scale: object
    w_out_a_f32: object
    w_out_b_f32: object
    w_out_c_f32: object
    w_o_f32: object
    g_ffn: object
    w_up_f32: object
    w_down_f32: object
    g_final: object
    yp: object
    ys: object
    cv_p: object
    ptail_p: object
    cv_s: object
    ptail_s: object
    ext_p: object
    pool_in: object
    pool_out: object
    x1: object
    h2: object
    w_in: object
    w_out_a: object
    w_out_b: object
    w_out_c: object
    w_o: object
    w_up: object
    w_down: object
    attn: object


def _layer_step(r, decode, j, step, n_prompt, first=False):
    t, d = r.x1.shape
    sgu_w = r.g_v.shape[-1]
    pool_w = r.pool_scale.shape[-1]
    x_w = r.attn.shape[-1]
    n_decode = r.attn.shape[0] // t
    head_dim = x_w // X_HEADS
    c_pool = 2 * sgu_w
    c_q = c_pool + pool_w
    c_gate = c_q + x_w

    ffn = _FfnSchedule(r, enabled=not first)

    def gate(h, part):
        lo = c_gate + part * d
        return jax.nn.sigmoid(_dot(h, r.w_in[:, lo:lo + d]))

    ffn.at(0)
    x = r.xs[...] if decode else r.xp[0]
    h = _rmsnorm(x, r.g_mix[...]).astype(BF16)
    if decode:
        z = _dot(h, r.w_in[:, 0:c_q])
    else:
        h_dec = _rmsnorm(r.xq[...], r.g_mix[...]).astype(BF16)
        z = _dot(jnp.concatenate([h, h_dec], axis=0), r.w_in[:, 0:c_gate])
        dec_q = z[t:, c_q:c_gate] * (head_dim ** -0.5)
        z = z[0:t]
    u = z[:, 0:sgu_w]
    v = z[:, sgu_w:c_pool]
    p = z[:, c_pool:c_q]
    gate_a = gate(h, 0)
    ffn.at(1)

    u = jax.nn.gelu(u)
    vhat = _layernorm(jax.nn.gelu(v), r.g_v[...], r.b_v[...])
    if decode:
        rows = r.wmix_s.shape[1]
        r.cv_s[...] = vhat
        mixed = _chunk_mix(vhat.astype(BF16), r.wmix_s, rows)
        bias = _chunk_bias(r.bias_s, t)
    else:
        r.cv_p[0] = vhat[t - CHUNK:, :]
        mixed = _chunk_mix(vhat.astype(BF16), r.wmix_p, CHUNK)
        bias = _chunk_bias(r.bias_p, t)
    if not decode:
        q = z[:, c_q:c_gate] * (head_dim ** -0.5)
        dec_scores = _decode_scores(r, dec_q)
    ffn.at(2)
    a_in = (u * (mixed + bias)).astype(BF16)

    if decode:
        n_seq = t // rows
        lane_tiles = range(pool_w // V7X_LANES)
        for lt in lane_tiles:
            r.pool_in[lt] = p[:, lt * V7X_LANES:(lt + 1) * V7X_LANES]
        ext = [r.state[k] for k in range(POOL_STATE)]
        ext += [jnp.concatenate([r.pool_in[lt, pl.ds(i, n_seq, stride=rows), :]
                                 for lt in lane_tiles], axis=-1) for i in range(rows)]
        for i in range(rows):
            cur = POOL_STATE + i

            def load_shifted(k, lt, cur=cur):
                return ext[cur - k][:, lt * V7X_LANES:(lt + 1) * V7X_LANES]

            res = _window_means_minus_self(load_shifted, ext[cur], PAST_LEN + i)
            for lt in lane_tiles:
                r.pool_out[lt, pl.ds(i, n_seq, stride=rows), :] = (
                    res[:, lt * V7X_LANES:(lt + 1) * V7X_LANES])
        for k in range(POOL_STATE):
            r.ptail_s[k] = ext[rows + k]
        pooled = jnp.concatenate([r.pool_out[lt] for lt in lane_tiles], axis=-1)
    else:
        r.ext_p[POOL_PAD:POOL_PAD + t, :] = p

        def load_shifted(k, lt):
            return r.ext_p[POOL_PAD - k:POOL_PAD - k + t, lt * V7X_LANES:(lt + 1) * V7X_LANES]

        pos = j * t + lax.broadcasted_iota(jnp.int32, (t, V7X_LANES), 0)
        pooled = _window_means_minus_self(load_shifted, p, pos)
        tail = r.ext_p[t:t + POOL_PAD, :]
        r.ext_p[0:POOL_PAD, :] = tail
        r.ptail_p[0] = tail
    pooled = pooled.astype(BF16)

    a = _dot(a_in, r.w_out_a[...])

    if not decode:
        head = lax.broadcasted_iota(jnp.int32, (t, x_w), 1) >> _log2(head_dim)
        kt = r.kt[0]
        q_heads = jnp.concatenate(
            [jnp.where(head == hd, q, 0.0).astype(BF16) for hd in range(X_HEADS)], axis=0)
        scores = _dot(q_heads, kt)
    pool_mixed = _dot(pooled, r.wpool[...]) * r.pool_scale[...]
    ffn.at(3)
    if decode:
        row0 = pl.multiple_of(jnp.clip(step - n_prompt, 0, n_decode - 1) * t, t)
        o = r.attn[pl.ds(row0, t), :]
    else:
        dec_probs = [_softmax_bf16(s) for s in dec_scores]
        o_heads = _dot(_softmax_bf16(scores), r.vb[0])
        o = jnp.zeros((t, x_w), F32)
        for hd in range(X_HEADS):
            o = jnp.where(head == hd, o_heads[hd * t:(hd + 1) * t], o)
        _decode_values(r, dec_probs, step)
    b = _dot(pool_mixed.astype(BF16), r.w_out_b[...])
    gate_b = gate(h, 1)
    c = _dot(o.astype(BF16), r.w_out_c[...])
    ffn.at(4)
    gate_c = gate(h, 2)
    ffn.at(5)
    merged = (gate_a * a + gate_b * b + gate_c * c).astype(BF16)
    ffn.at(6)
    ffn.at(7)
    x2 = ffn.x2
    x1 = x + _dot(merged, r.w_o[...])
    r.x1[...] = x1
    r.h2[...] = _rmsnorm(x1, r.g_ffn[...]).astype(BF16)
    if first:
        return
    y = _rmsnorm(x2, r.g_final[...])
    if not decode:
        r.yp[0] = y
    else:
        @pl.when(step == n_prompt)
        def _():
            r.yp[0] = y

        @pl.when(step != n_prompt)
        def _():
            r.ys[...] = y


def _ffn_up(r, s):
    slab = r.w_up.shape[1] // FFN_SPLIT
    up = _dot(r.h2[...], r.w_up[:, s * slab:(s + 1) * slab])
    return jnp.square(jnp.maximum(up, 0.0)).astype(BF16)


def _ffn_down(r, act, s):
    slab = r.w_up.shape[1] // FFN_SPLIT
    return _dot(act, r.w_down[s * slab:(s + 1) * slab, :])


class _FfnSchedule:
    def __init__(self, r, enabled=True):
        self.r = r
        self.acts = {}
        self.x2 = None
        self.points = FFN_POINTS if enabled else ((),) * len(FFN_POINTS)

    def at(self, point):
        for kind, s in self.points[point]:
            if kind == "up":
                self.acts[s] = _ffn_up(self.r, s)
            else:
                base = self.r.x1[...] if self.x2 is None else self.x2
                self.x2 = base + _ffn_down(self.r, self.acts.pop(s), s)


def _drain_step(r):
    x2 = r.x1[...]
    act = _ffn_up(r, 0)
    for s in range(FFN_SPLIT):
        nxt = _ffn_up(r, s + 1) if s + 1 < FFN_SPLIT else None
        x2 = x2 + _ffn_down(r, act, s)
        act = nxt
    r.ys[...] = _rmsnorm(x2, r.g_final[...])


def _load_weights_bf16(r):
    ahead = LOAD_SLOTS - 1
    narrow_cols = r.w_o_f32.shape[1]
    wide_cols = max(r.w_in_f32.shape[1], r.w_up_f32.shape[1])

    def load(stage_w, stage_n, sem_w, sem_n):
        def stream(src, dst):
            narrow = src.shape[1] == narrow_cols
            stage, sem = (stage_n, sem_n) if narrow else (stage_w, sem_w)
            rows = stage.shape[1]
            n_chunks = src.shape[0] // rows
            assert src.shape[0] % rows == 0, (src.shape, rows)

            def copy(i, slot):
                return pltpu.make_async_copy(src.at[pl.ds(i * rows, rows), :],
                                             stage.at[slot, :, 0:src.shape[1]], sem.at[slot])

            def prime():
                for i in range(min(ahead, n_chunks)):
                    copy(i, i % LOAD_SLOTS).start()

            def drain():
                def body(i, carry):
                    @pl.when(i + ahead < n_chunks)
                    def _():
                        copy(i + ahead, lax.rem(i + ahead, LOAD_SLOTS)).start()

                    slot = lax.rem(i, LOAD_SLOTS)
                    copy(i, slot).wait()
                    row0 = pl.multiple_of(i * rows, rows)
                    dst[pl.ds(row0, rows), :] = stage[slot, :, 0:src.shape[1]].astype(BF16)
                    return carry

                lax.fori_loop(0, n_chunks, body, 0)

            return prime, drain

        order = [(r.w_in_f32, r.w_in), (r.w_down_f32, r.w_down), (r.w_up_f32, r.w_up),
                 (r.w_o_f32, r.w_o), (r.w_out_a_f32, r.w_out_a), (r.w_out_b_f32, r.w_out_b),
                 (r.w_out_c_f32, r.w_out_c)]
        streams = [stream(src, dst) for src, dst in order]
        streams[0][0]()
        for k, (_, drain) in enumerate(streams):
            nxt_uses_other_ring = (k + 1 < len(order)
                                   and (order[k + 1][0].shape[1] == narrow_cols)
                                   != (order[k][0].shape[1] == narrow_cols))
            if nxt_uses_other_ring:
                streams[k + 1][0]()
            drain()
            if k + 1 < len(order) and not nxt_uses_other_ring:
                streams[k + 1][0]()

    pl.run_scoped(load,
                  pltpu.VMEM((LOAD_SLOTS, LOAD_WIDE_ROWS, wide_cols), F32),
                  pltpu.VMEM((LOAD_SLOTS, LOAD_NARROW_ROWS, narrow_cols), F32),
                  pltpu.SemaphoreType.DMA((LOAD_SLOTS,)),
                  pltpu.SemaphoreType.DMA((LOAD_SLOTS,)))


def _layer_kernel(tiles_per_seq, n_prompt, *refs):
    r = _LayerRefs(*refs)
    step = pl.program_id(0)
    last = pl.num_programs(0) - 1
    j = jnp.minimum(step, n_prompt - 1) & (tiles_per_seq - 1)

    @pl.when(step == 0)
    def _():
        _load_weights_bf16(r)

    @pl.when(jnp.logical_and(step < n_prompt, j == 0))
    def _():
        r.ext_p[0:POOL_PAD, :] = jnp.zeros((POOL_PAD, r.ext_p.shape[1]), F32)

    @pl.when(step == 0)
    def _():
        _layer_step(r, False, j, step, n_prompt, first=True)

    @pl.when(jnp.logical_and(step > 0, step < n_prompt))
    def _():
        _layer_step(r, False, j, step, n_prompt)

    @pl.when(jnp.logical_and(step >= n_prompt, step < last))
    def _():
        _layer_step(r, True, j, step, n_prompt)

    @pl.when(step == last)
    def _():
        _drain_step(r)


def _mem_kv(mem, g_mem, w_kv):
    bsz, n_mem, d = mem.shape
    xw = w_kv.shape[-1] // 2
    sb = MEM_KV_SEQS
    assert bsz % sb == 0, (bsz, sb)
    return pl.pallas_call(
        _mem_kv_kernel,
        grid=(bsz // sb,),
        in_specs=[pl.BlockSpec((sb, n_mem, d), lambda i: (i, 0, 0)),
                  _const_spec((1, d)), _const_spec((d, 2 * xw))],
        out_specs=[pl.BlockSpec((sb, xw, n_mem), lambda i: (i, 0, 0)),
                   pl.BlockSpec((sb, xw, n_mem), lambda i: (i, 0, 0)),
                   pl.BlockSpec((sb, xw, n_mem), lambda i: (i, 0, 0)),
                   pl.BlockSpec((sb, n_mem, xw), lambda i: (i, 0, 0))],
        out_shape=[jax.ShapeDtypeStruct((bsz, xw, n_mem), F32),
                   jax.ShapeDtypeStruct((bsz, xw, n_mem), F32),
                   jax.ShapeDtypeStruct((bsz, xw, n_mem), BF16),
                   jax.ShapeDtypeStruct((bsz, n_mem, xw), BF16)],
        compiler_params=pltpu.CompilerParams(
            dimension_semantics=("arbitrary",),
            vmem_limit_bytes=VMEM_LIMIT_BYTES),
        name="mem_kv",
    )(mem, g_mem.reshape(1, d).astype(F32), w_kv)


def _spatial_operands(w_s, b_s, group_dim, chunk_rows, tile_rows):
    wmix = jnp.tile(w_s[:, :chunk_rows, :chunk_rows], (1, 1, tile_rows // chunk_rows))
    bias = jnp.repeat(b_s[:, :chunk_rows].T, group_dim, axis=1)
    return wmix.astype(F32), bias.astype(F32)


def _layer(x_prompt, x2d, ktb, vb, cache_kt, cache_vt, state, lw):
    (g_mix, w_in, g_v, b_v, w_s, b_s, w_pool, pool_scale,
     w_out_a, w_out_b, w_out_c, w_o, g_ffn, w_up, w_down, g_final) = lw
    bsz, seq, d = x_prompt.shape
    n_seq = state.shape[0]
    rows = x2d.shape[0] // n_seq
    t = TILE
    ts = t // rows
    tps = seq // t
    n_prompt = bsz * tps
    n_decode = n_seq // ts
    assert n_seq % n_prompt == 0, (n_seq, n_prompt)
    sps = n_seq // n_prompt
    sgu_w = g_v.shape[-1]
    pool_w = pool_scale.shape[-1]
    n_mem, xw = vb.shape[1:]
    group_dim = sgu_w // w_s.shape[0]
    row2 = lambda vec: vec.reshape(1, -1).astype(F32)
    wmix_p, bias_p = _spatial_operands(w_s, b_s, group_dim, CHUNK, CHUNK)
    wmix_s, bias_s = _spatial_operands(w_s, b_s, group_dim, rows, t)
    wpool = jax.scipy.linalg.block_diag(*[w_pool[i] for i in range(w_pool.shape[0])])
    state_rows = jnp.transpose(state, (1, 0, 2))
    big = [w_in, w_out_a, w_out_b, w_out_c, w_o, w_up, w_down]
    weights = [row2(g_mix), w_in, row2(g_v), row2(b_v), wmix_p, bias_p, wmix_s, bias_s,
               wpool.astype(BF16), row2(pool_scale), w_out_a, w_out_b, w_out_c, w_o,
               row2(g_ffn), w_up, w_down, row2(g_final)]

    def prompt_tile(m):
        return jnp.clip(m, 0, n_prompt - 1)

    def prompt_seq(m):
        return lax.shift_right_logical(prompt_tile(m), _log2(tps))

    def decode_tile(m):
        return jnp.clip(m - n_prompt, 0, n_decode - 1)

    yp, ys, cv_p, ptail_p, cv_s, ptail_s = pl.pallas_call(
        functools.partial(_layer_kernel, tps, n_prompt),
        grid=(n_prompt + n_decode + 1,),
        in_specs=[pl.BlockSpec((1, t, d), lambda s: (prompt_tile(s), 0, 0)),
                  pl.BlockSpec((t, d), lambda s: (decode_tile(s), 0)),
                  pl.BlockSpec((1, xw, n_mem), lambda s: (prompt_seq(s), 0, 0)),
                  pl.BlockSpec((1, n_mem, xw), lambda s: (prompt_seq(s), 0, 0)),
                  pl.BlockSpec((sps * rows, d), lambda s: (prompt_tile(s), 0)),
                  pl.BlockSpec((sps, xw, n_mem), lambda s: (prompt_tile(s), 0, 0)),
                  pl.BlockSpec((sps, xw, n_mem), lambda s: (prompt_tile(s), 0, 0)),
                  pl.BlockSpec((POOL_STATE, ts, pool_w), lambda s: (0, decode_tile(s), 0))]
                 + [pl.BlockSpec(memory_space=pl.ANY) if any(w is b for b in big)
                    else _const_spec(w.shape) for w in weights],
        out_specs=[pl.BlockSpec((1, t, d), lambda s: (prompt_tile(s - 1), 0, 0)),
                   pl.BlockSpec((t, d), lambda s: (decode_tile(s - 1), 0)),
                   pl.BlockSpec((1, CHUNK, sgu_w), lambda s: (prompt_seq(s), 0, 0)),
                   pl.BlockSpec((1, POOL_PAD, pool_w), lambda s: (prompt_seq(s), 0, 0)),
                   pl.BlockSpec((t, sgu_w), lambda s: (decode_tile(s), 0)),
                   pl.BlockSpec((POOL_STATE, ts, pool_w), lambda s: (0, decode_tile(s), 0))],
        out_shape=[jax.ShapeDtypeStruct((n_prompt, t, d), F32),
                   jax.ShapeDtypeStruct((n_seq * rows, d), F32),
                   jax.ShapeDtypeStruct((bsz, CHUNK, sgu_w), F32),
                   jax.ShapeDtypeStruct((bsz, POOL_PAD, pool_w), F32),
                   jax.ShapeDtypeStruct((n_seq * rows, sgu_w), F32),
                   jax.ShapeDtypeStruct((POOL_STATE, n_seq, pool_w), F32)],
        scratch_shapes=[pltpu.VMEM((POOL_PAD + t, pool_w), F32),
                        pltpu.VMEM((pool_w // V7X_LANES, t, V7X_LANES), F32),
                        pltpu.VMEM((pool_w // V7X_LANES, t, V7X_LANES), F32),
                        pltpu.VMEM((t, d), F32),
                        pltpu.VMEM((t, d), BF16)]
                       + [pltpu.VMEM(w.shape, BF16) for w in big]
                       + [pltpu.VMEM((n_seq * rows, xw), F32)],
        compiler_params=pltpu.CompilerParams(
            dimension_semantics=("arbitrary",),
            vmem_limit_bytes=VMEM_LIMIT_BYTES),
        name="layer",
    )(x_prompt.reshape(n_prompt, t, d), x2d, ktb, vb, x2d, cache_kt, cache_vt, state_rows, *weights)
    return (yp.reshape(bsz, seq, d), ys.reshape(n_seq, rows, d), cv_p, ptail_p[:, POOL_PAD - POOL_STATE:, :],
            cv_s.reshape(n_seq, rows, sgu_w), jnp.transpose(ptail_s, (1, 0, 2)))


def _feature_major(kv):
    bsz, n_mem, heads, head_dim = kv.shape
    return jnp.transpose(kv, (0, 2, 3, 1)).reshape(bsz, heads * head_dim, n_mem)


def _token_major(kvt, heads):
    bsz, xw, n_mem = kvt.shape
    return jnp.transpose(kvt.reshape(bsz, heads, xw // heads, n_mem), (0, 3, 1, 2))


def kernel(x_prompt, x_sample, mem_prompt, cache_mem_k, cache_mem_v, state_pool, g_mix, w_in, g_v, b_v, w_s, b_s, w_pool, pool_scale, g_mem, w_kv, w_out_a, w_out_b, w_out_c, w_o, g_ffn, w_up, w_down, g_final):
    depth = w_in.shape[0]
    assert depth == 1, "the final rmsnorm is fused into the (single) layer kernel"
    heads = cache_mem_k.shape[-2]
    n_seq, rows, d = x_sample.shape
    lw = (g_mix[0], w_in[0], g_v[0], b_v[0], w_s[0], b_s[0], w_pool[0], pool_scale[0],
          w_out_a[0], w_out_b[0], w_out_c[0], w_o[0], g_ffn[0], w_up[0], w_down[0], g_final)
    mem_kt, mem_vt, ktb, vb = _mem_kv(mem_prompt, g_mem[0], w_kv[0])
    x2d = x_sample.reshape(n_seq * rows, d)
    y_prompt, y_sample, cv_p, ptail_p, cv_s, ptail_s = _layer(
        x_prompt, x2d, ktb, vb, _feature_major(cache_mem_k[0]), _feature_major(cache_mem_v[0]),
        state_pool[0], lw)
    return (y_prompt, y_sample, _token_major(mem_kt, heads)[None], _token_major(mem_vt, heads)[None],
            ptail_p[None], ptail_s[None], cv_p[None], cv_s[None])
```

```python
import functools
from typing import NamedTuple

import jax
import jax.numpy as jnp
from jax import lax
from jax.experimental import pallas as pl
from jax.experimental.pallas import tpu as pltpu

PAST_LEN = 16384
CHUNK = 128
POOL_WINDOWS = (2, 4, 8, 16)
POOL_STATE = max(POOL_WINDOWS) - 1
X_HEADS = 4
EPS = 1e-6

V7X_VMEM_BYTES = 64 * 1024 * 1024
V7X_LANES = 128
VMEM_RESERVE_BYTES = 6 * 1024 * 1024
VMEM_LIMIT_BYTES = V7X_VMEM_BYTES - VMEM_RESERVE_BYTES

TILE = 256
MEM_KV_SEQS = 4
FFN_SPLIT = 4
FFN_POINTS = ((("up", 0),), (("down", 0),), (("up", 1),), (("down", 1),),
              (("up", 2),), (("down", 2),), (("up", 3),), (("down", 3),))
LOAD_SLOTS = 8
LOAD_WIDE_ROWS = 32
LOAD_NARROW_ROWS = 128
POOL_PAD = POOL_STATE + 1

BF16 = jnp.bfloat16
F32 = jnp.float32


def _dot(a, b):
    return jnp.dot(a, b, preferred_element_type=F32)


def _log2(n):
    assert n > 0 and n & (n - 1) == 0, n
    return n.bit_length() - 1


def _rmsnorm(x, g):
    return x * lax.rsqrt(jnp.mean(x * x, axis=-1, keepdims=True) + EPS) * g


def _layernorm(x, g, b):
    xc = x - jnp.mean(x, axis=-1, keepdims=True)
    return xc * lax.rsqrt(jnp.mean(xc * xc, axis=-1, keepdims=True) + EPS) * g + b


def _softmax_bf16(s):
    e = jnp.exp(s - jnp.max(s, axis=-1, keepdims=True))
    return (e * (1.0 / jnp.sum(e, axis=-1, keepdims=True))).astype(BF16)


def _const_spec(shape):
    zeros = (0,) * len(shape)
    return pl.BlockSpec(shape, lambda *_: zeros, pipeline_mode=pl.Buffered(1))


def _chunk_mix(vhat_b, wmix_ref, chunk_rows):
    t, sgu_w = vhat_b.shape
    n_groups = wmix_ref.shape[0]
    group_dim = sgu_w // n_groups
    shift = _log2(chunk_rows)
    reps = t // chunk_rows
    row = lax.broadcasted_iota(jnp.int32, (t, t), 0)
    col = lax.broadcasted_iota(jnp.int32, (t, t), 1)
    same_chunk = (row >> shift) == (col >> shift)
    causal = col <= row
    parts = []
    for g in range(n_groups):
        w = jnp.concatenate([wmix_ref[g]] * (t // wmix_ref.shape[2]), axis=1)
        w = jnp.concatenate([w] * reps, axis=0)
        w = jnp.where(same_chunk, jnp.where(causal, w, 0.0), 0.0).astype(BF16)
        parts.append(_dot(w, vhat_b[:, g * group_dim:(g + 1) * group_dim]))
    return jnp.concatenate(parts, axis=1)


def _chunk_bias(bias_ref, t):
    return jnp.concatenate([bias_ref[...]] * (t // bias_ref.shape[0]), axis=0)


def _window_means_minus_self(load_shifted, p, pos):
    group_lanes = p.shape[-1] // len(POOL_WINDOWS)
    outs = []
    for lt in range(p.shape[-1] // V7X_LANES):
        p_lt = p[..., lt * V7X_LANES:(lt + 1) * V7X_LANES]
        lane = lax.broadcasted_iota(jnp.int32, p_lt.shape, p_lt.ndim - 1)
        wins = POOL_WINDOWS[lt * V7X_LANES // group_lanes:(lt + 1) * V7X_LANES // group_lanes]
        acc = p_lt
        sums = {1: p_lt}
        for k in range(1, max(wins)):
            acc = acc + load_shifted(k, lt)
            sums[k + 1] = acc
        win_sum = sums[wins[-1]]
        win = jnp.full(p_lt.shape, wins[-1], jnp.int32)
        for gi in range(len(wins) - 2, -1, -1):
            in_group = lane < (gi + 1) * group_lanes
            win_sum = jnp.where(in_group, sums[wins[gi]], win_sum)
            win = jnp.where(in_group, wins[gi], win)
        cnt = jnp.minimum(pos + 1, win).astype(F32)
        outs.append(win_sum / cnt - p_lt)
    return jnp.concatenate(outs, axis=-1)


def _mem_kv_kernel(mem_ref, g_mem_ref, w_kv_ref, kt_ref, vt_ref, ktb_ref, vb_ref):
    xw = kt_ref.shape[1]
    w_kv = w_kv_ref[...].astype(BF16)
    for i in range(mem_ref.shape[0]):
        hn = _rmsnorm(mem_ref[i], g_mem_ref[...]).astype(BF16)
        kv = _dot(hn, w_kv)
        kt = kv[:, 0:xw].T
        v = kv[:, xw:2 * xw]
        kt_ref[i] = kt
        vt_ref[i] = v.T
        ktb_ref[i] = kt.astype(BF16)
        vb_ref[i] = v.astype(BF16)


def _decode_own_head(r):
    n_seq, x_w, _ = r.ckt.shape
    rows = r.xq.shape[0] // n_seq
    stack = X_HEADS * rows
    lane_head = lax.broadcasted_iota(jnp.int32, (stack, x_w), 1) >> _log2(x_w // X_HEADS)
    row_head = lax.broadcasted_iota(jnp.int32, (stack, x_w), 0) >> _log2(rows)
    return lane_head == row_head


def _decode_scores(r, q):
    n_seq, _, n_mem = r.ckt.shape
    rows = r.xq.shape[0] // n_seq
    stack = X_HEADS * rows
    own_head = _decode_own_head(r)
    qms = []
    for i in range(n_seq):
        qs = jnp.concatenate([q[i * rows:(i + 1) * rows, :]] * X_HEADS, axis=0)
        qms.append(jnp.where(own_head, qs, 0.0).astype(BF16))
    keys = jnp.concatenate([r.ckt[i].astype(BF16) for i in range(n_seq)], axis=1)
    both = _dot(jnp.concatenate(qms, axis=0), keys)
    return [both[i * stack:(i + 1) * stack, i * n_mem:(i + 1) * n_mem] for i in range(n_seq)]


def _decode_values(r, probs, step):
    n_seq, x_w, _ = r.cvt.shape
    rows = r.xq.shape[0] // n_seq
    stack = X_HEADS * rows
    own_head = _decode_own_head(r)
    values = jnp.concatenate([r.cvt[i].T.astype(BF16) for i in range(n_seq)], axis=1)
    both = _dot(jnp.concatenate(probs, axis=0), values)
    for i in range(n_seq):
        of = both[i * stack:(i + 1) * stack, i * x_w:(i + 1) * x_w]
        of = jnp.where(own_head, of, 0.0)
        out = of[0:rows]
        for hd in range(1, X_HEADS):
            out = out + of[hd * rows:(hd + 1) * rows]
        row0 = pl.multiple_of((step * n_seq + i) * rows, rows)
        r.attn[pl.ds(row0, rows), :] = out


class _LayerRefs(NamedTuple):
    xp: object
    xs: object
    kt: object
    vb: object
    xq: object
    ckt: object
    cvt: object
    state: object
    g_mix: object
    w_in_f32: object
    g_v: object
    b_v: object
    wmix_p: object
    bias_p: object
    wmix_s: object
    bias_s: object
    wpool: object
    pool_scale: object
    w_out_a_f32: object
    w_out_b_f32: object
    w_out_c_f32: object
    w_o_f32: object
    g_ffn: object
    w_up_f32: object
    w_down_f32: object
    g_final: object
    yp: object
    ys: object
    cv_p: object
    ptail_p: object
    cv_s: object
    ptail_s: object
    ext_p: object
    pool_in: object
    pool_out: object
    x1: object
    h2: object
    w_in: object
    w_out_a: object
    w_out_b: object
    w_out_c: object
    w_o: object
    w_up: object
    w_down: object
    attn: object


def _layer_step(r, decode, j, step, n_prompt, first=False):
    t, d = r.x1.shape
    sgu_w = r.g_v.shape[-1]
    pool_w = r.pool_scale.shape[-1]
    x_w = r.attn.shape[-1]
    n_decode = r.attn.shape[0] // t
    head_dim = x_w // X_HEADS
    c_pool = 2 * sgu_w
    c_q = c_pool + pool_w
    c_gate = c_q + x_w

    ffn = _FfnSchedule(r, enabled=not first)

    def gate(h, part):
        lo = c_gate + part * d
        return jax.nn.sigmoid(_dot(h, r.w_in[:, lo:lo + d]))

    ffn.at(0)
    x = r.xs[...] if decode else r.xp[0]
    h = _rmsnorm(x, r.g_mix[...]).astype(BF16)
    if decode:
        z = _dot(h, r.w_in[:, 0:c_q])
    else:
        h_dec = _rmsnorm(r.xq[...], r.g_mix[...]).astype(BF16)
        z = _dot(jnp.concatenate([h, h_dec], axis=0), r.w_in[:, 0:c_gate])
        dec_q = z[t:, c_q:c_gate] * (head_dim ** -0.5)
        z = z[0:t]
    u = z[:, 0:sgu_w]
    v = z[:, sgu_w:c_pool]
    p = z[:, c_pool:c_q]
    gate_a = gate(h, 0)
    ffn.at(1)

    u = jax.nn.gelu(u)
    vhat = _layernorm(jax.nn.gelu(v), r.g_v[...], r.b_v[...])
    if decode:
        rows = r.wmix_s.shape[1]
        r.cv_s[...] = vhat
        mixed = _chunk_mix(vhat.astype(BF16), r.wmix_s, rows)
        bias = _chunk_bias(r.bias_s, t)
    else:
        r.cv_p[0] = vhat[t - CHUNK:, :]
        mixed = _chunk_mix(vhat.astype(BF16), r.wmix_p, CHUNK)
        bias = _chunk_bias(r.bias_p, t)
    if not decode:
        q = z[:, c_q:c_gate] * (head_dim ** -0.5)
        dec_scores = _decode_scores(r, dec_q)
    ffn.at(2)
    a_in = (u * (mixed + bias)).astype(BF16)

    if decode:
        n_seq = t // rows
        lane_tiles = range(pool_w // V7X_LANES)
        for lt in lane_tiles:
            r.pool_in[lt] = p[:, lt * V7X_LANES:(lt + 1) * V7X_LANES]
        ext = [r.state[k] for k in range(POOL_STATE)]
        ext += [jnp.concatenate([r.pool_in[lt, pl.ds(i, n_seq, stride=rows), :]
                                 for lt in lane_tiles], axis=-1) for i in range(rows)]
        for i in range(rows):
            cur = POOL_STATE + i

            def load_shifted(k, lt, cur=cur):
                return ext[cur - k][:, lt * V7X_LANES:(lt + 1) * V7X_LANES]

            res = _window_means_minus_self(load_shifted, ext[cur], PAST_LEN + i)
            for lt in lane_tiles:
                r.pool_out[lt, pl.ds(i, n_seq, stride=rows), :] = (
                    res[:, lt * V7X_LANES:(lt + 1) * V7X_LANES])
        for k in range(POOL_STATE):
            r.ptail_s[k] = ext[rows + k]
        pooled = jnp.concatenate([r.pool_out[lt] for lt in lane_tiles], axis=-1)
    else:
        r.ext_p[POOL_PAD:POOL_PAD + t, :] = p

        def load_shifted(k, lt):
            return r.ext_p[POOL_PAD - k:POOL_PAD - k + t, lt * V7X_LANES:(lt + 1) * V7X_LANES]

        pos = j * t + lax.broadcasted_iota(jnp.int32, (t, V7X_LANES), 0)
        pooled = _window_means_minus_self(load_shifted, p, pos)
        tail = r.ext_p[t:t + POOL_PAD, :]
        r.ext_p[0:POOL_PAD, :] = tail
        r.ptail_p[0] = tail
    pooled = pooled.astype(BF16)

    a = _dot(a_in, r.w_out_a[...])

    if not decode:
        head = lax.broadcasted_iota(jnp.int32, (t, x_w), 1) >> _log2(head_dim)
        kt = r.kt[0]
        q_heads = jnp.concatenate(
            [jnp.where(head == hd, q, 0.0).astype(BF16) for hd in range(X_HEADS)], axis=0)
        scores = _dot(q_heads, kt)
    pool_mixed = _dot(pooled, r.wpool[...]) * r.pool_scale[...]
    ffn.at(3)
    if decode:
        row0 = pl.multiple_of(jnp.clip(step - n_prompt, 0, n_decode - 1) * t, t)
        o = r.attn[pl.ds(row0, t), :]
    else:
        dec_probs = [_softmax_bf16(s) for s in dec_scores]
        o_heads = _dot(_softmax_bf16(scores), r.vb[0])
        o = jnp.zeros((t, x_w), F32)
        for hd in range(X_HEADS):
            o = jnp.where(head == hd, o_heads[hd * t:(hd + 1) * t], o)
        _decode_values(r, dec_probs, step)
    b = _dot(pool_mixed.astype(BF16), r.w_out_b[...])
    gate_b = gate(h, 1)
    c = _dot(o.astype(BF16), r.w_out_c[...])
    ffn.at(4)
    gate_c = gate(h, 2)
    ffn.at(5)
    merged = (gate_a * a + gate_b * b + gate_c * c).astype(BF16)
    ffn.at(6)
    ffn.at(7)
    x2 = ffn.x2
    x1 = x + _dot(merged, r.w_o[...])
    r.x1[...] = x1
    r.h2[...] = _rmsnorm(x1, r.g_ffn[...]).astype(BF16)
    if first:
        return
    y = _rmsnorm(x2, r.g_final[...])
    if not decode:
        r.yp[0] = y
    else:
        @pl.when(step == n_prompt)
        def _():
            r.yp[0] = y

        @pl.when(step != n_prompt)
        def _():
            r.ys[...] = y


def _ffn_up(r, s):
    slab = r.w_up.shape[1] // FFN_SPLIT
    up = _dot(r.h2[...], r.w_up[:, s * slab:(s + 1) * slab])
    return jnp.square(jnp.maximum(up, 0.0)).astype(BF16)


def _ffn_down(r, act, s):
    slab = r.w_up.shape[1] // FFN_SPLIT
    return _dot(act, r.w_down[s * slab:(s + 1) * slab, :])


class _FfnSchedule:
    def __init__(self, r, enabled=True):
        self.r = r
        self.acts = {}
        self.x2 = None
        self.points = FFN_POINTS if enabled else ((),) * len(FFN_POINTS)

    def at(self, point):
        for kind, s in self.points[point]:
            if kind == "up":
                self.acts[s] = _ffn_up(self.r, s)
            else:
                base = self.r.x1[...] if self.x2 is None else self.x2
                self.x2 = base + _ffn_down(self.r, self.acts.pop(s), s)


def _drain_step(r):
    x2 = r.x1[...]
    act = _ffn_up(r, 0)
    for s in range(FFN_SPLIT):
        nxt = _ffn_up(r, s + 1) if s + 1 < FFN_SPLIT else None
        x2 = x2 + _ffn_down(r, act, s)
        act = nxt
    r.ys[...] = _rmsnorm(x2, r.g_final[...])


def _load_weights_bf16(r):
    ahead = LOAD_SLOTS - 1
    narrow_cols = r.w_o_f32.shape[1]
    wide_cols = max(r.w_in_f32.shape[1], r.w_up_f32.shape[1])

    def load(stage_w, stage_n, sem_w, sem_n):
        def stream(src, dst):
            narrow = src.shape[1] == narrow_cols
            stage, sem = (stage_n, sem_n) if narrow else (stage_w, sem_w)
            rows = stage.shape[1]
            n_chunks = src.shape[0] // rows
            assert src.shape[0] % rows == 0, (src.shape, rows)

            def copy(i, slot):
                return pltpu.make_async_copy(src.at[pl.ds(i * rows, rows), :],
                                             stage.at[slot, :, 0:src.shape[1]], sem.at[slot])

            def prime():
                for i in range(min(ahead, n_chunks)):
                    copy(i, i % LOAD_SLOTS).start()

            def drain():
                def body(i, carry):
                    @pl.when(i + ahead < n_chunks)
                    def _():
                        copy(i + ahead, lax.rem(i + ahead, LOAD_SLOTS)).start()

                    slot = lax.rem(i, LOAD_SLOTS)
                    copy(i, slot).wait()
                    row0 = pl.multiple_of(i * rows, rows)
                    dst[pl.ds(row0, rows), :] = stage[slot, :, 0:src.shape[1]].astype(BF16)
                    return carry

                lax.fori_loop(0, n_chunks, body, 0)

            return prime, drain

        order = [(r.w_in_f32, r.w_in), (r.w_down_f32, r.w_down), (r.w_up_f32, r.w_up),
                 (r.w_o_f32, r.w_o), (r.w_out_a_f32, r.w_out_a), (r.w_out_b_f32, r.w_out_b),
                 (r.w_out_c_f32, r.w_out_c)]
        streams = [stream(src, dst) for src, dst in order]
        streams[0][0]()
        for k, (_, drain) in enumerate(streams):
            nxt_uses_other_ring = (k + 1 < len(order)
                                   and (order[k + 1][0].shape[1] == narrow_cols)
                                   != (order[k][0].shape[1] == narrow_cols))
            if nxt_uses_other_ring:
                streams[k + 1][0]()
            drain()
            if k + 1 < len(order) and not nxt_uses_other_ring:
                streams[k + 1][0]()

    pl.run_scoped(load,
                  pltpu.VMEM((LOAD_SLOTS, LOAD_WIDE_ROWS, wide_cols), F32),
                  pltpu.VMEM((LOAD_SLOTS, LOAD_NARROW_ROWS, narrow_cols), F32),
                  pltpu.SemaphoreType.DMA((LOAD_SLOTS,)),
                  pltpu.SemaphoreType.DMA((LOAD_SLOTS,)))


def _layer_kernel(tiles_per_seq, n_prompt, *refs):
    r = _LayerRefs(*refs)
    step = pl.program_id(0)
    last = pl.num_programs(0) - 1
    j = jnp.minimum(step, n_prompt - 1) & (tiles_per_seq - 1)

    @pl.when(step == 0)
    def _():
        _load_weights_bf16(r)

    @pl.when(jnp.logical_and(step < n_prompt, j == 0))
    def _():
        r.ext_p[0:POOL_PAD, :] = jnp.zeros((POOL_PAD, r.ext_p.shape[1]), F32)

    @pl.when(step == 0)
    def _():
        _layer_step(r, False, j, step, n_prompt, first=True)

    @pl.when(jnp.logical_and(step > 0, step < n_prompt))
    def _():
        _layer_step(r, False, j, step, n_prompt)

    @pl.when(jnp.logical_and(step >= n_prompt, step < last))
    def _():
        _layer_step(r, True, j, step, n_prompt)

    @pl.when(step == last)
    def _():
        _drain_step(r)


def _mem_kv(mem, g_mem, w_kv):
    bsz, n_mem, d = mem.shape
    xw = w_kv.shape[-1] // 2
    sb = MEM_KV_SEQS
    assert bsz % sb == 0, (bsz, sb)
    return pl.pallas_call(
        _mem_kv_kernel,
        grid=(bsz // sb,),
        in_specs=[pl.BlockSpec((sb, n_mem, d), lambda i: (i, 0, 0)),
                  _const_spec((1, d)), _const_spec((d, 2 * xw))],
        out_specs=[pl.BlockSpec((sb, xw, n_mem), lambda i: (i, 0, 0)),
                   pl.BlockSpec((sb, xw, n_mem), lambda i: (i, 0, 0)),
                   pl.BlockSpec((sb, xw, n_mem), lambda i: (i, 0, 0)),
                   pl.BlockSpec((sb, n_mem, xw), lambda i: (i, 0, 0))],
        out_shape=[jax.ShapeDtypeStruct((bsz, xw, n_mem), F32),
                   jax.ShapeDtypeStruct((bsz, xw, n_mem), F32),
                   jax.ShapeDtypeStruct((bsz, xw, n_mem), BF16),
                   jax.ShapeDtypeStruct((bsz, n_mem, xw), BF16)],
        compiler_params=pltpu.CompilerParams(
            dimension_semantics=("arbitrary",),
            vmem_limit_bytes=VMEM_LIMIT_BYTES),
        name="mem_kv",
    )(mem, g_mem.reshape(1, d).astype(F32), w_kv)


def _spatial_operands(w_s, b_s, group_dim, chunk_rows, tile_rows):
    wmix = jnp.tile(w_s[:, :chunk_rows, :chunk_rows], (1, 1, tile_rows // chunk_rows))
    bias = jnp.repeat(b_s[:, :chunk_rows].T, group_dim, axis=1)
    return wmix.astype(F32), bias.astype(F32)


def _layer(x_prompt, x2d, ktb, vb, cache_kt, cache_vt, state, lw):
    (g_mix, w_in, g_v, b_v, w_s, b_s, w_pool, pool_scale,
     w_out_a, w_out_b, w_out_c, w_o, g_ffn, w_up, w_down, g_final) = lw
    bsz, seq, d = x_prompt.shape
    n_seq = state.shape[0]
    rows = x2d.shape[0] // n_seq
    t = TILE
    ts = t // rows
    tps = seq // t
    n_prompt = bsz * tps
    n_decode = n_seq // ts
    assert n_seq % n_prompt == 0, (n_seq, n_prompt)
    sps = n_seq // n_prompt
    sgu_w = g_v.shape[-1]
    pool_w = pool_scale.shape[-1]
    n_mem, xw = vb.shape[1:]
    group_dim = sgu_w // w_s.shape[0]
    row2 = lambda vec: vec.reshape(1, -1).astype(F32)
    wmix_p, bias_p = _spatial_operands(w_s, b_s, group_dim, CHUNK, CHUNK)
    wmix_s, bias_s = _spatial_operands(w_s, b_s, group_dim, rows, t)
    wpool = jax.scipy.linalg.block_diag(*[w_pool[i] for i in range(w_pool.shape[0])])
    state_rows = jnp.transpose(state, (1, 0, 2))
    big = [w_in, w_out_a, w_out_b, w_out_c, w_o, w_up, w_down]
    weights = [row2(g_mix), w_in, row2(g_v), row2(b_v), wmix_p, bias_p, wmix_s, bias_s,
               wpool.astype(BF16), row2(pool_scale), w_out_a, w_out_b, w_out_c, w_o,
               row2(g_ffn), w_up, w_down, row2(g_final)]

    def prompt_tile(m):
        return jnp.clip(m, 0, n_prompt - 1)

    def prompt_seq(m):
        return lax.shift_right_logical(prompt_tile(m), _log2(tps))

    def decode_tile(m):
        return jnp.clip(m - n_prompt, 0, n_decode - 1)

    yp, ys, cv_p, ptail_p, cv_s, ptail_s = pl.pallas_call(
        functools.partial(_layer_kernel, tps, n_prompt),
        grid=(n_prompt + n_decode + 1,),
        in_specs=[pl.BlockSpec((1, t, d), lambda s: (prompt_tile(s), 0, 0)),
                  pl.BlockSpec((t, d), lambda s: (decode_tile(s), 0)),
                  pl.BlockSpec((1, xw, n_mem), lambda s: (prompt_seq(s), 0, 0)),
                  pl.BlockSpec((1, n_mem, xw), lambda s: (prompt_seq(s), 0, 0)),
                  pl.BlockSpec((sps * rows, d), lambda s: (prompt_tile(s), 0)),
                  pl.BlockSpec((sps, xw, n_mem), lambda s: (prompt_tile(s), 0, 0)),
                  pl.BlockSpec((sps, xw, n_mem), lambda s: (prompt_tile(s), 0, 0)),
                  pl.BlockSpec((POOL_STATE, ts, pool_w), lambda s: (0, decode_tile(s), 0))]
                 + [pl.BlockSpec(memory_space=pl.ANY) if any(w is b for b in big)
                    else _const_spec(w.shape) for w in weights],
        out_specs=[pl.BlockSpec((1, t, d), lambda s: (prompt_tile(s - 1), 0, 0)),
                   pl.BlockSpec((t, d), lambda s: (decode_tile(s - 1), 0)),
                   pl.BlockSpec((1, CHUNK, sgu_w), lambda s: (prompt_seq(s), 0, 0)),
                   pl.BlockSpec((1, POOL_PAD, pool_w), lambda s: (prompt_seq(s), 0, 0)),
                   pl.BlockSpec((t, sgu_w), lambda s: (decode_tile(s), 0)),
                   pl.BlockSpec((POOL_STATE, ts, pool_w), lambda s: (0, decode_tile(s), 0))],
        out_shape=[jax.ShapeDtypeStruct((n_prompt, t, d), F32),
                   jax.ShapeDtypeStruct((n_seq * rows, d), F32),
                   jax.ShapeDtypeStruct((bsz, CHUNK, sgu_w), F32),
                   jax.ShapeDtypeStruct((bsz, POOL_PAD, pool_w), F32),
                   jax.ShapeDtypeStruct((n_seq * rows, sgu_w), F32),
                   jax.ShapeDtypeStruct((POOL_STATE, n_seq, pool_w), F32)],
        scratch_shapes=[pltpu.VMEM((POOL_PAD + t, pool_w), F32),
                        pltpu.VMEM((pool_w // V7X_LANES, t, V7X_LANES), F32),
                        pltpu.VMEM((pool_w // V7X_LANES, t, V7X_LANES), F32),
                        pltpu.VMEM((t, d), F32),
                        pltpu.VMEM((t, d), BF16)]
                       + [pltpu.VMEM(w.shape, BF16) for w in big]
                       + [pltpu.VMEM((n_seq * rows, xw), F32)],
        compiler_params=pltpu.CompilerParams(
            dimension_semantics=("arbitrary",),
            vmem_limit_bytes=VMEM_LIMIT_BYTES),
        name="layer",
    )(x_prompt.reshape(n_prompt, t, d), x2d, ktb, vb, x2d, cache_kt, cache_vt, state_rows, *weights)
    return (yp.reshape(bsz, seq, d), ys.reshape(n_seq, rows, d), cv_p, ptail_p[:, POOL_PAD - POOL_STATE:, :],
            cv_s.reshape(n_seq, rows, sgu_w), jnp.transpose(ptail_s, (1, 0, 2)))


def _feature_major(kv):
    bsz, n_mem, heads, head_dim = kv.shape
    return jnp.transpose(kv, (0, 2, 3, 1)).reshape(bsz, heads * head_dim, n_mem)


def _token_major(kvt, heads):
    bsz, xw, n_mem = kvt.shape
    return jnp.transpose(kvt.reshape(bsz, heads, xw // heads, n_mem), (0, 3, 1, 2))


def kernel(x_prompt, x_sample, mem_prompt, cache_mem_k, cache_mem_v, state_pool, g_mix, w_in, g_v, b_v, w_s, b_s, w_pool, pool_scale, g_mem, w_kv, w_out_a, w_out_b, w_out_c, w_o, g_ffn, w_up, w_down, g_final):
    depth = w_in.shape[0]
    assert depth == 1, "the final rmsnorm is fused into the (single) layer kernel"
    heads = cache_mem_k.shape[-2]
    n_seq, rows, d = x_sample.shape
    lw = (g_mix[0], w_in[0], g_v[0], b_v[0], w_s[0], b_s[0], w_pool[0], pool_scale[0],
          w_out_a[0], w_out_b[0], w_out_c[0], w_o[0], g_ffn[0], w_up[0], w_down[0], g_final)
    mem_kt, mem_vt, ktb, vb = _mem_kv(mem_prompt, g_mem[0], w_kv[0])
    x2d = x_sample.reshape(n_seq * rows, d)
    y_prompt, y_sample, cv_p, ptail_p, cv_s, ptail_s = _layer(
        x_prompt, x2d, ktb, vb, _feature_major(cache_mem_k[0]), _feature_major(cache_mem_v[0]),
        state_pool[0], lw)
    return (y_prompt, y_sample, _token_major(mem_kt, heads)[None], _token_major(mem_vt, heads)[None],
            ptail_p[None], ptail_s[None], cv_p[None], cv_s[None])
```
